```python
import jax, jax.numpy as jnp
from jax import lax
import numpy as np

D_MODEL = 1024
BATCH = 4
SEQ = 8192
DEPTH = 2

CHUNK = 64
N_A_LAYERS = DEPTH // 2
N_B_LAYERS = DEPTH - N_A_LAYERS
A_HEADS = 8
A_HEAD_DIM = D_MODEL // A_HEADS
B_HEADS = 16
B_HEAD_DIM = D_MODEL // B_HEADS
LEFT_CHUNKS = 8
BAND = (LEFT_CHUNKS + 1) * CHUNK
REL_MIN = -(CHUNK - 1)
REL_MAX = 256
N_REL = REL_MAX - REL_MIN + 1
ATTN_SCALE = B_HEAD_DIM ** -0.5
N_GROUPS = 4
EXPERTS_PER_GROUP = 8
N_EXPERTS = N_GROUPS * EXPERTS_PER_GROUP
TOP_K = 2
D_EXPERT = 512
MOE_BLOCK = 256
D_PLE = 256
EPS = 1e-6
NEG_INF = -1e30

kernel_name = "yoco_hgrn2_chunkattn_hiermoe"


def rmsnorm(x, g):
    xf = x.astype(jnp.float32)
    y = xf * lax.rsqrt(jnp.mean(xf * xf, axis=-1, keepdims=True) + EPS)
    return (y * g.astype(jnp.float32)).astype(x.dtype)


def hgrn2_mixer(h, lb, w_in, out_norm):
    f32 = jnp.float32
    B_, S_, _ = h.shape
    n_c = S_ // CHUNK
    proj = h @ w_in
    q, f_logit, i_val, og = jnp.split(proj, 4, axis=-1)
    f = lb + (1.0 - lb) * jax.nn.sigmoid(f_logit.astype(f32))
    k = 1.0 - f

    def heads(t):
        return t.reshape(B_, n_c, CHUNK, A_HEADS, A_HEAD_DIM).transpose(1, 0, 3, 2, 4)

    qc, kc, vc = heads(q.astype(f32)), heads(k), heads(i_val.astype(f32))
    g = jnp.cumsum(heads(jnp.log(f)), axis=3)
    g_last = g[:, :, :, -1:, :]
    q_dec = qc * jnp.exp(g)
    k_inv = kc * jnp.exp(-g)
    k_tail = kc * jnp.exp(g_last - g)
    causal = jnp.tril(jnp.ones((CHUNK, CHUNK), dtype=bool))
    att = jnp.where(causal, jnp.einsum('nbhtd,nbhsd->nbhts', q_dec, k_inv), 0.0)
    o_intra = jnp.einsum('nbhts,nbhse->nbhte', att, vc)

    def step(state, xs):
        qd, kt, v, dl = xs
        o = jnp.einsum('bhtd,bhde->bhte', qd, state)
        state = state * dl[:, :, 0, :, None] + jnp.einsum('bhsd,bhse->bhde', kt, v)
        return state, o

    s0 = jnp.zeros((B_, A_HEADS, A_HEAD_DIM, A_HEAD_DIM), f32)
    _, o_inter = lax.scan(step, s0, (q_dec, k_tail, vc, jnp.exp(g_last)))
    o = o_intra + o_inter
    o = o * lax.rsqrt(jnp.mean(o * o, axis=-1, keepdims=True) + EPS)
    o = o.transpose(1, 0, 3, 2, 4).reshape(B_, S_, D_MODEL) * out_norm.astype(f32)
    o = o * jax.nn.silu(og.astype(f32))
    return o.astype(h.dtype)


def chunk_attention(h, w_q, rel_bias, k_pad, v_pad):
    B_, S_, _ = h.shape
    n_c = S_ // CHUNK
    q = (h @ w_q).reshape(B_, S_, B_HEADS, B_HEAD_DIM)
    q_pos = jnp.arange(CHUNK)[:, None]
    k_pos = jnp.arange(BAND)[None, :] - LEFT_CHUNKS * CHUNK
    rel = jnp.clip(q_pos - k_pos, REL_MIN, REL_MAX) - REL_MIN
    bias = rel_bias[:, rel].astype(jnp.float32)
    band_idx = jnp.arange(BAND)

    def one_chunk(c):
        start = c * CHUNK
        qc = lax.dynamic_slice_in_dim(q, start, CHUNK, axis=1)
        kb = lax.dynamic_slice_in_dim(k_pad, start, BAND, axis=1)
        vb = lax.dynamic_slice_in_dim(v_pad, start, BAND, axis=1)
        s = jnp.einsum('bqhd,bkhd->bhqk', qc, kb).astype(jnp.float32) * ATTN_SCALE + bias
        valid = (start - LEFT_CHUNKS * CHUNK + band_idx) >= 0
        s = jnp.where(valid, s, NEG_INF)
        w = jax.nn.softmax(s, axis=-1).astype(vb.dtype)
        return jnp.einsum('bhqk,bkhd->bqhd', w, vb)

    o = lax.map(one_chunk, jnp.arange(n_c))
    return o.transpose(1, 0, 2, 3, 4).reshape(B_, S_, D_MODEL)


def hier_moe(h, w_group, b_group, w_expert, b_expert, w_gate, w_up, w_down):
    f32 = jnp.float32
    B_, S_, D_ = h.shape
    t = h.reshape(-1, D_)
    T = t.shape[0]
    grp_prob = jax.nn.softmax((t @ w_group).astype(f32) + b_group.astype(f32), axis=-1)
    grp_w, grp_idx = lax.top_k(grp_prob, 1)
    e_logits = ((t @ w_expert).astype(f32) + b_expert.astype(f32)).reshape(T, N_GROUPS, EXPERTS_PER_GROUP)
    e_logits = jnp.take_along_axis(e_logits, grp_idx[:, :, None], axis=1)[:, 0]
    top_logit, top_i = lax.top_k(e_logits, TOP_K)
    gate = grp_w * jax.nn.softmax(top_logit, axis=-1)
    expert = grp_idx * EXPERTS_PER_GROUP + top_i

    n_slots = T * TOP_K
    flat_e = expert.reshape(-1)
    order = jnp.argsort(flat_e)
    e_sorted = flat_e[order]
    tok_sorted = (order // TOP_K).astype(jnp.int32)
    w_sorted = gate.reshape(-1)[order]
    counts = jnp.bincount(flat_e, length=N_EXPERTS)
    padded = (counts + MOE_BLOCK - 1) // MOE_BLOCK * MOE_BLOCK
    pad_end = jnp.cumsum(padded)
    pad_start = pad_end - padded
    seg_start = jnp.cumsum(counts) - counts
    dest = pad_start[e_sorted] + jnp.arange(n_slots) - seg_start[e_sorted]
    n_blocks = -(-n_slots // MOE_BLOCK) + N_EXPERTS
    n_pad = n_blocks * MOE_BLOCK
    buf_tok = jnp.zeros((n_pad,), jnp.int32).at[dest].set(tok_sorted)
    buf_w = jnp.zeros((n_pad,), f32).at[dest].set(w_sorted)
    block_e = jnp.minimum(
        jnp.searchsorted(pad_end, jnp.arange(n_blocks) * MOE_BLOCK, side='right'), N_EXPERTS - 1)

    def expert_block(args):
        tok, e = args
        xb = t[tok]
        hid = jax.nn.silu(xb @ w_gate[e]) * (xb @ w_up[e])
        return hid @ w_down[e]

    y = lax.map(expert_block, (buf_tok.reshape(n_blocks, MOE_BLOCK), block_e))
    y = y.reshape(n_pad, D_) * buf_w[:, None].astype(y.dtype)
    out = jnp.zeros_like(t).at[buf_tok].add(y)
    return out.reshape(B_, S_, D_)


def setup_inputs(seed: int = 0) -> dict:
    key = jax.random.key(seed)
    ks = jax.random.split(key, 24)
    f32 = jnp.float32
    D, F, E, G = D_MODEL, D_EXPERT, N_EXPERTS, N_GROUPS

    def nrm(k, shape, scale):
        return jax.random.normal(k, shape, f32) * scale

    def gain(k, shape):
        return 1.0 + 0.05 * jax.random.normal(k, shape, f32)

    return {
        "x": nrm(ks[0], (BATCH, SEQ, D), 1.0),
        "p": nrm(ks[1], (DEPTH, BATCH, SEQ, D_PLE), 1.0),
        "a_w_in": nrm(ks[2], (N_A_LAYERS, D, 4 * D), D ** -0.5),
        "a_lb_logits": nrm(ks[3], (N_A_LAYERS + 1, D), 0.1),
        "a_out_norm": gain(ks[4], (N_A_LAYERS, D)),
        "a_w_o": nrm(ks[5], (N_A_LAYERS, D, D), D ** -0.5),
        "kv_norm": gain(ks[6], (D,)),
        "w_kv": nrm(ks[7], (D, 2 * D), D ** -0.5),
        "b_w_q": nrm(ks[8], (N_B_LAYERS, D, D), D ** -0.5),
        "b_rel_bias": nrm(ks[9], (N_B_LAYERS, B_HEADS, N_REL), 0.5),
        "b_w_o": nrm(ks[10], (N_B_LAYERS, D, D), D ** -0.5),
        "norm_mix": gain(ks[11], (DEPTH, D)),
        "norm_ffn": gain(ks[12], (DEPTH, D)),
        "norm_ple": gain(ks[13], (DEPTH, D)),
        "moe_w_group": nrm(ks[14], (DEPTH, D, G), D ** -0.5),
        "moe_b_group": nrm(ks[15], (DEPTH, G), 0.01),
        "moe_w_expert": nrm(ks[16], (DEPTH, D, E), D ** -0.5),
        "moe_b_expert": nrm(ks[17], (DEPTH, E), 0.01),
        "moe_w_gate": nrm(ks[18], (DEPTH, E, D, F), D ** -0.5),
        "moe_w_up": nrm(ks[19], (DEPTH, E, D, F), D ** -0.5),
        "moe_w_down": nrm(ks[20], (DEPTH, E, F, D), F ** -0.5),
        "ple_w_proj": nrm(ks[21], (DEPTH, D_PLE, D), D_PLE ** -0.5),
        "ple_w_gate": nrm(ks[22], (DEPTH, D, D), D ** -0.5),
        "final_norm": gain(ks[23], (D,)),
    }


def reference(x, p, a_w_in, a_lb_logits, a_out_norm, a_w_o, kv_norm, w_kv, b_w_q,
              b_rel_bias, b_w_o, norm_mix, norm_ffn, norm_ple, moe_w_group, moe_b_group,
              moe_w_expert, moe_b_expert, moe_w_gate, moe_w_up, moe_w_down, ple_w_proj,
              ple_w_gate, final_norm):
    B_, S_, _ = x.shape
    lower_bounds = jnp.cumsum(jax.nn.softmax(a_lb_logits.astype(jnp.float32), axis=0), axis=0)
    k_pad = v_pad = None
    for i in range(DEPTH):
        if i == N_A_LAYERS:
            kv = rmsnorm(x, kv_norm) @ w_kv
            k_sh, v_sh = jnp.split(kv, 2, axis=-1)
            pad = ((0, 0), (LEFT_CHUNKS * CHUNK, 0), (0, 0), (0, 0))
            k_pad = jnp.pad(k_sh.reshape(B_, S_, B_HEADS, B_HEAD_DIM), pad)
            v_pad = jnp.pad(v_sh.reshape(B_, S_, B_HEADS, B_HEAD_DIM), pad)
        h = rmsnorm(x, norm_mix[i])
        if i < N_A_LAYERS:
            mix = hgrn2_mixer(h, lower_bounds[i], a_w_in[i], a_out_norm[i]) @ a_w_o[i]
        else:
            j = i - N_A_LAYERS
            mix = chunk_attention(h, b_w_q[j], b_rel_bias[j], k_pad, v_pad) @ b_w_o[j]
        x = x + mix
        x = x + hier_moe(rmsnorm(x, norm_ffn[i]), moe_w_group[i], moe_b_group[i],
                         moe_w_expert[i], moe_b_expert[i], moe_w_gate[i], moe_w_up[i],
                         moe_w_down[i])
        x = x + (p[i] @ ple_w_proj[i]) * jax.nn.sigmoid(rmsnorm(x, norm_ple[i]) @ ple_w_gate[i])
    return rmsnorm(x, final_norm)
```

```python
import functools

import jax
import jax.numpy as jnp
from jax import lax
from jax.experimental import pallas as pl
from jax.experimental.pallas import tpu as pltpu

F32 = jnp.float32
BF16 = jnp.bfloat16

D = 1024
CHUNK = 64
A_HEADS = 8
A_HEAD_DIM = 128
B_HEADS = 16
B_HEAD_DIM = 64
LEFT_CHUNKS = 8
BAND = (LEFT_CHUNKS + 1) * CHUNK
REL_MIN = -(CHUNK - 1)
REL_MAX = 256
ATTN_SCALE = B_HEAD_DIM ** -0.5
N_GROUPS = 4
EXPERTS_PER_GROUP = 8
N_EXPERTS = 32
TOP_K = 2
D_EXPERT = 512
MOE_BLOCK = 256
D_PLE = 256
EPS = 1e-6
NEG_INF = -1e30

SEQ_TILE = 512
ROUTER_TILE = 512
COMBINE_TILE = 256
ROUTER_ROWS = 40
VMEM_LIMIT = 56 * 1024 * 1024


def _params(n_axes, vmem=VMEM_LIMIT):
    return pltpu.CompilerParams(dimension_semantics=("arbitrary",) * n_axes,
                                vmem_limit_bytes=vmem)


def _rms_scale(x):
    return lax.rsqrt(jnp.mean(x * x, axis=-1, keepdims=True) + EPS)


def _sigmoid(x):
    return 1.0 / (1.0 + jnp.exp(-x))


def _mixer_a_kernel(x_ref, g_ref, win_ref, lb_ref, onorm_ref, wo_ref, out_ref,
                    proj_scr, o_scr, state_scr):
    @pl.when(pl.program_id(1) == 0)
    def _():
        state_scr[...] = jnp.zeros_like(state_scr)

    x = x_ref[0]
    h = (x * _rms_scale(x) * g_ref[...]).astype(BF16)
    proj_scr[...] = jnp.dot(h, win_ref[...], preferred_element_type=F32)

    row = lax.broadcasted_iota(jnp.int32, (CHUNK, CHUNK), 0)
    col = lax.broadcasted_iota(jnp.int32, (CHUNK, CHUNK), 1)
    causal = row >= col
    tril = causal.astype(BF16)
    lb = lb_ref[...]
    onorm = onorm_ref[...]

    def chunk_body(c, carry):
        r0 = pl.multiple_of(c * CHUNK, CHUNK)
        rows = pl.ds(r0, CHUNK)
        f = lb + (1.0 - lb) * _sigmoid(proj_scr[rows, D:2 * D])
        logf = jnp.log(f)
        hi = logf.astype(BF16)
        lo = (logf - hi.astype(F32)).astype(BF16)
        g = (jnp.dot(tril, hi, preferred_element_type=F32)
             + jnp.dot(tril, lo, preferred_element_type=F32))
        for hd in range(A_HEADS):
            sl = slice(hd * A_HEAD_DIM, (hd + 1) * A_HEAD_DIM)
            gh = g[:, sl]
            g_last = gh[CHUNK - 1:CHUNK, :]
            k = 1.0 - f[:, sl]
            q_dec = (proj_scr[rows, sl] * jnp.exp(gh)).astype(BF16)
            k_inv = (k * jnp.exp(-gh)).astype(BF16)
            k_tail = (k * jnp.exp(g_last - gh)).astype(BF16)
            v = proj_scr[rows, 2 * D + hd * A_HEAD_DIM:2 * D + (hd + 1) * A_HEAD_DIM]
            v_b = v.astype(BF16)
            att = lax.dot_general(q_dec, k_inv, (((1,), (1,)), ((), ())),
                                  preferred_element_type=F32)
            att = jnp.where(causal, att, 0.0).astype(BF16)
            st = state_scr[hd]
            o = (jnp.dot(att, v_b, preferred_element_type=F32)
                 + lax.dot_general(q_dec, st.astype(BF16), (((1,), (1,)), ((), ())),
                                   preferred_element_type=F32))
            v_t = v.T.astype(BF16)
            state_scr[hd] = st * jnp.exp(g_last) + jnp.dot(v_t, k_tail,
                                                           preferred_element_type=F32)
            o = o * _rms_scale(o)
            og = proj_scr[rows, 3 * D + hd * A_HEAD_DIM:3 * D + (hd + 1) * A_HEAD_DIM]
            o = o * onorm[:, sl] * (og * _sigmoid(og))
            o_scr[rows, sl] = o.astype(BF16)
        return carry

    lax.fori_loop(0, SEQ_TILE // CHUNK, chunk_body, 0)
    out_ref[0] = x + jnp.dot(o_scr[...], wo_ref[...], preferred_element_type=F32)


def _mixer_a(x, g, w_in, lb, out_norm, w_o):
    b, s, _ = x.shape
    const = lambda bi, si: (0, 0)
    return pl.pallas_call(
        _mixer_a_kernel,
        name="mixer_a",
        grid=(b, s // SEQ_TILE),
        in_specs=[
            pl.BlockSpec((1, SEQ_TILE, D), lambda bi, si: (bi, si, 0)),
            pl.BlockSpec((1, D), const),
            pl.BlockSpec((D, 4 * D), const, pipeline_mode=pl.Buffered(1)),
            pl.BlockSpec((1, D), const),
            pl.BlockSpec((1, D), const),
            pl.BlockSpec((D, D), const, pipeline_mode=pl.Buffered(1)),
        ],
        out_specs=pl.BlockSpec((1, SEQ_TILE, D), lambda bi, si: (bi, si, 0)),
        out_shape=jax.ShapeDtypeStruct(x.shape, F32),
        scratch_shapes=[
            pltpu.VMEM((SEQ_TILE, 4 * D), F32),
            pltpu.VMEM((SEQ_TILE, D), BF16),
            pltpu.VMEM((A_HEADS, A_HEAD_DIM, A_HEAD_DIM), F32),
        ],
        compiler_params=_params(2),
    )(x, g.reshape(1, D), w_in.astype(BF16), lb.reshape(1, D), out_norm.reshape(1, D),
      w_o.astype(BF16))


def _router_kernel(x_ref, g_ref, wr_ref, br_ref, ids_ref, gates_ref, counts_ref, cnt_scr):
    @pl.when(pl.program_id(0) == 0)
    def _():
        cnt_scr[...] = jnp.zeros_like(cnt_scr)

    tm = ROUTER_TILE
    x = x_ref[...]
    h = x * _rms_scale(x) * g_ref[...]
    logits = lax.dot_general(wr_ref[...], h, (((1,), (1,)), ((), ())),
                             precision=lax.Precision.HIGHEST,
                             preferred_element_type=F32) + br_ref[...]
    el = logits[0:N_EXPERTS]
    gl = logits[N_EXPERTS:ROUTER_ROWS]
    grow = lax.broadcasted_iota(jnp.int32, gl.shape, 0)
    gl = jnp.where(grow < N_GROUPS, gl, -jnp.inf)
    gmax = jnp.max(gl, axis=0, keepdims=True)
    gsum = jnp.sum(jnp.exp(gl - gmax), axis=0, keepdims=True)
    grp_w = 1.0 / gsum
    gidx = jnp.min(jnp.where(gl == gmax, grow, N_GROUPS), axis=0, keepdims=True)

    erow = lax.broadcasted_iota(jnp.int32, el.shape, 0)
    masked = jnp.where((erow // EXPERTS_PER_GROUP) == gidx, el, -jnp.inf)
    top1 = jnp.max(masked, axis=0, keepdims=True)
    i1 = jnp.min(jnp.where(masked == top1, erow, N_EXPERTS), axis=0, keepdims=True)
    masked2 = jnp.where(erow == i1, -jnp.inf, masked)
    top2 = jnp.max(masked2, axis=0, keepdims=True)
    i2 = jnp.min(jnp.where(masked2 == top2, erow, N_EXPERTS), axis=0, keepdims=True)
    e2 = jnp.exp(top2 - top1)
    denom = 1.0 + e2
    g1 = grp_w * (1.0 / denom)
    g2 = grp_w * (e2 / denom)

    sel1 = erow == i1
    sel2 = erow == i2
    onehot = (sel1 | sel2).astype(BF16)
    tr = lax.broadcasted_iota(jnp.int32, (tm, tm), 0)
    tc = lax.broadcasted_iota(jnp.int32, (tm, tm), 1)
    before = (tr < tc).astype(BF16)
    prefix = jnp.dot(onehot, before, preferred_element_type=F32) + cnt_scr[...]
    r1 = jnp.sum(jnp.where(sel1, prefix, 0.0), axis=0, keepdims=True)
    r2 = jnp.sum(jnp.where(sel2, prefix, 0.0), axis=0, keepdims=True)
    cnt_scr[...] += jnp.sum(onehot.astype(F32), axis=1, keepdims=True)

    zi = jnp.zeros((4, tm), jnp.int32)
    ids_ref[0] = jnp.concatenate(
        [i1, i2, r1.astype(jnp.int32), r2.astype(jnp.int32), zi], axis=0)
    gates_ref[0] = jnp.concatenate([g1, g2, jnp.zeros((6, tm), F32)], axis=0)
    counts_ref[...] = jnp.broadcast_to(cnt_scr[...], counts_ref.shape).astype(jnp.int32)


def _router(x2d, g, w_group, b_group, w_expert, b_expert):
    t = x2d.shape[0]
    nt = t // ROUTER_TILE
    pad = ROUTER_ROWS - N_EXPERTS - N_GROUPS
    wr = jnp.concatenate([w_expert.T, w_group.T, jnp.zeros((pad, D), F32)], axis=0)
    br = jnp.concatenate([b_expert, b_group, jnp.zeros((pad,), F32)]).reshape(ROUTER_ROWS, 1)
    const = lambda i: (0, 0)
    return pl.pallas_call(
        _router_kernel,
        name="router",
        grid=(nt,),
        in_specs=[
            pl.BlockSpec((ROUTER_TILE, D), lambda i: (i, 0)),
            pl.BlockSpec((1, D), const),
            pl.BlockSpec((ROUTER_ROWS, D), const),
            pl.BlockSpec((ROUTER_ROWS, 1), const),
        ],
        out_specs=[
            pl.BlockSpec((1, 8, ROUTER_TILE), lambda i: (i, 0, 0)),
            pl.BlockSpec((1, 8, ROUTER_TILE), lambda i: (i, 0, 0)),
            pl.BlockSpec((N_EXPERTS, 128), const),
        ],
        out_shape=[
            jax.ShapeDtypeStruct((nt, 8, ROUTER_TILE), jnp.int32),
            jax.ShapeDtypeStruct((nt, 8, ROUTER_TILE), F32),
            jax.ShapeDtypeStruct((N_EXPERTS, 128), jnp.int32),
        ],
        scratch_shapes=[pltpu.VMEM((N_EXPERTS, 1), F32)],
        compiler_params=_params(1),
    )(x2d, g.reshape(1, D), wr, br)


LANES = 128
ROW_TILES = D // LANES


def _row_copy(src_ref, src_row, dst_ref, dst_row, sem):
    return pltpu.make_async_copy(src_ref.at[src_row], dst_ref.at[dst_row], sem)


def _to_slabs(slab_ref, x):
    for s in range(ROW_TILES):
        slab_ref[:, s, :] = x[:, s * LANES:(s + 1) * LANES]


def _from_slabs(slab_ref):
    return jnp.concatenate([slab_ref[:, s, :] for s in range(ROW_TILES)], axis=-1)


def _dispatch_kernel(zrow_ref, dest_ref, x_ref, xs_ref, xrow, zbuf, zsem, sem):
    @pl.when(pl.program_id(0) == 0)
    def _():
        zbuf[...] = jnp.zeros_like(zbuf)

        def zcopy(e):
            return pltpu.make_async_copy(zbuf, xs_ref.at[pl.ds(zrow_ref[e], MOE_BLOCK)], zsem)

        def zstart(e, c):
            @pl.when(zrow_ref[e] >= 0)
            def _():
                zcopy(e).start()
            return c

        def zwait(e, c):
            @pl.when(zrow_ref[e] >= 0)
            def _():
                zcopy(e).wait()
            return c

        lax.fori_loop(0, 2 * N_EXPERTS, zstart, 0)
        lax.fori_loop(0, 2 * N_EXPERTS, zwait, 0)

    _to_slabs(xrow, x_ref[...])

    def start(t, c):
        for k in range(TOP_K):
            _row_copy(xrow, t, xs_ref, dest_ref[0, k, t], sem).start()
        return c

    def wait(t, c):
        for k in range(TOP_K):
            _row_copy(xrow, t, xs_ref, dest_ref[0, k, t], sem).wait()
        return c

    lax.fori_loop(0, ROUTER_TILE, start, 0)
    lax.fori_loop(0, ROUTER_TILE, wait, 0)


def _dispatch(x2d, dest, zrow, n_pad):
    t = x2d.shape[0]
    nt = t // ROUTER_TILE
    return pl.pallas_call(
        _dispatch_kernel,
        name="dispatch",
        grid_spec=pltpu.PrefetchScalarGridSpec(
            num_scalar_prefetch=1,
            grid=(nt,),
            in_specs=[
                pl.BlockSpec((1, TOP_K, ROUTER_TILE), lambda i, z: (i, 0, 0),
                             memory_space=pltpu.SMEM),
                pl.BlockSpec((ROUTER_TILE, D), lambda i, z: (i, 0)),
            ],
            out_specs=pl.BlockSpec(memory_space=pl.ANY),
            scratch_shapes=[
                pltpu.VMEM((ROUTER_TILE, ROW_TILES, LANES), F32),
                pltpu.VMEM((MOE_BLOCK, ROW_TILES, LANES), F32),
                pltpu.SemaphoreType.DMA(()),
                pltpu.SemaphoreType.DMA(()),
            ],
        ),
        out_shape=jax.ShapeDtypeStruct((n_pad, ROW_TILES, LANES), F32),
        compiler_params=_params(1),
    )(zrow, dest, x2d)


def _experts_kernel(be_ref, nu_ref, xs_ref, g_ref, wg_ref, wu_ref, wd_ref, y_ref,
                    wg_b, wu_b, wd_b):
    i = pl.program_id(0)
    prev = be_ref[jnp.maximum(i - 1, 0)]

    @pl.when((i == 0) | (be_ref[i] != prev))
    def _():
        wg_b[...] = wg_ref[0].astype(BF16)
        wu_b[...] = wu_ref[0].astype(BF16)
        wd_b[...] = wd_ref[0].astype(BF16)

    @pl.when(i < nu_ref[0])
    def _():
        x = _from_slabs(xs_ref)
        h = (x * _rms_scale(x) * g_ref[...]).astype(BF16)
        a = jnp.dot(h, wg_b[...], preferred_element_type=F32)
        u = jnp.dot(h, wu_b[...], preferred_element_type=F32)
        hid = (a * _sigmoid(a) * u).astype(BF16)
        _to_slabs(y_ref, jnp.dot(hid, wd_b[...], preferred_element_type=F32))

    @pl.when(i >= nu_ref[0])
    def _():
        y_ref[...] = jnp.zeros_like(y_ref)


def _experts(xs, g, block_e, n_used, w_gate, w_up, w_down):
    n_pad = xs.shape[0]
    n_blocks = n_pad // MOE_BLOCK

    def blk(i, be, nu):
        return jnp.minimum(i, nu[0] - 1)

    return pl.pallas_call(
        _experts_kernel,
        name="experts",
        grid_spec=pltpu.PrefetchScalarGridSpec(
            num_scalar_prefetch=2,
            grid=(n_blocks,),
            in_specs=[
                pl.BlockSpec((MOE_BLOCK, ROW_TILES, LANES),
                             lambda i, be, nu: (blk(i, be, nu), 0, 0)),
                pl.BlockSpec((1, D), lambda i, be, nu: (0, 0)),
                pl.BlockSpec((1, D, D_EXPERT), lambda i, be, nu: (be[i], 0, 0)),
                pl.BlockSpec((1, D, D_EXPERT), lambda i, be, nu: (be[i], 0, 0)),
                pl.BlockSpec((1, D_EXPERT, D), lambda i, be, nu: (be[i], 0, 0)),
            ],
            out_specs=pl.BlockSpec((MOE_BLOCK, ROW_TILES, LANES), lambda i, be, nu: (i, 0, 0)),
            scratch_shapes=[
                pltpu.VMEM((D, D_EXPERT), BF16),
                pltpu.VMEM((D, D_EXPERT), BF16),
                pltpu.VMEM((D_EXPERT, D), BF16),
            ],
        ),
        out_shape=jax.ShapeDtypeStruct(xs.shape, F32),
        compiler_params=_params(1),
    )(block_e, n_used, xs, g.reshape(1, D), w_gate, w_up, w_down)


def _combine_kernel(dest_ref, x_ref, gate_ref, y_ref, p_ref, gple_ref, wp_ref, wg_ref,
                    gfin_ref, out_ref, ybuf, sem, *, final):
    tc = COMBINE_TILE

    def start(t, c):
        for k in range(TOP_K):
            _row_copy(y_ref, dest_ref[0, k, t], ybuf.at[k], t, sem).start()
        return c

    def wait(t, c):
        for k in range(TOP_K):
            _row_copy(y_ref, dest_ref[0, k, t], ybuf.at[k], t, sem).wait()
        return c

    lax.fori_loop(0, tc, start, 0)
    lax.fori_loop(0, tc, wait, 0)

    gates = gate_ref[...]
    x = (x_ref[...] + gates[:, 0:1] * _from_slabs(ybuf.at[0])
         + gates[:, 1:2] * _from_slabs(ybuf.at[1]))
    h = (x * _rms_scale(x) * gple_ref[...]).astype(BF16)
    gate = _sigmoid(jnp.dot(h, wg_ref[...], preferred_element_type=F32))
    proj = jnp.dot(p_ref[...].astype(BF16), wp_ref[...], preferred_element_type=F32)
    x = x + proj * gate
    if final:
        x = x * _rms_scale(x) * gfin_ref[...]
    out_ref[...] = x


def _combine_ple(x2d, dest, gates, y, p2d, g_ple, w_proj, w_gate, g_final, final):
    t = x2d.shape[0]
    nt = t // COMBINE_TILE
    const = lambda i: (0, 0)
    return pl.pallas_call(
        functools.partial(_combine_kernel, final=final),
        name="combine_final" if final else "combine",
        grid=(nt,),
        in_specs=[
            pl.BlockSpec((1, TOP_K, COMBINE_TILE), lambda i: (i, 0, 0), memory_space=pltpu.SMEM),
            pl.BlockSpec((COMBINE_TILE, D), lambda i: (i, 0)),
            pl.BlockSpec((COMBINE_TILE, TOP_K), lambda i: (i, 0)),
            pl.BlockSpec(memory_space=pl.ANY),
            pl.BlockSpec((COMBINE_TILE, D_PLE), lambda i: (i, 0)),
            pl.BlockSpec((1, D), const),
            pl.BlockSpec((D_PLE, D), const),
            pl.BlockSpec((D, D), const),
            pl.BlockSpec((1, D), const),
        ],
        out_specs=pl.BlockSpec((COMBINE_TILE, D), lambda i: (i, 0)),
        out_shape=jax.ShapeDtypeStruct((t, D), F32),
        scratch_shapes=[
            pltpu.VMEM((TOP_K, COMBINE_TILE, ROW_TILES, LANES), F32),
            pltpu.SemaphoreType.DMA(()),
        ],
        compiler_params=_params(1),
    )(dest, x2d, gates, y, p2d, g_ple.reshape(1, D), w_proj.astype(BF16),
      w_gate.astype(BF16), g_final.reshape(1, D))


def _moe_ple(x, p_i, norm_ffn, w_group, b_group, w_expert, b_expert, w_gate, w_up, w_down,
             norm_ple, ple_w_proj, ple_w_gate, final_norm, final):
    b, s, _ = x.shape
    t = b * s
    x2d = x.reshape(t, D)
    ids, gates, counts = _router(x2d, norm_ffn, w_group, b_group, w_expert, b_expert)

    counts = counts[:, 0]
    padded = (counts + MOE_BLOCK - 1) // MOE_BLOCK * MOE_BLOCK
    pad_end = jnp.cumsum(padded)
    pad_start = pad_end - padded
    n_blocks = t * TOP_K // MOE_BLOCK + N_EXPERTS
    n_pad = n_blocks * MOE_BLOCK
    e = ids[:, 0:TOP_K, :].transpose(1, 0, 2).reshape(TOP_K, t)
    r = ids[:, TOP_K:2 * TOP_K, :].transpose(1, 0, 2).reshape(TOP_K, t)
    dest = pad_start[e] + r
    gate_cols = gates[:, 0:TOP_K, :].transpose(0, 2, 1).reshape(t, TOP_K)
    block_e = jnp.minimum(
        jnp.searchsorted(pad_end, jnp.arange(n_blocks, dtype=jnp.int32) * MOE_BLOCK,
                         side='right'), N_EXPERTS - 1).astype(jnp.int32)
    n_used = (pad_end[-1:] // MOE_BLOCK).astype(jnp.int32)
    tail = (n_used[0] + jnp.arange(N_EXPERTS, dtype=jnp.int32)) * MOE_BLOCK
    zrow = jnp.concatenate([jnp.where(padded > 0, pad_end - MOE_BLOCK, -1),
                            jnp.where(tail < n_pad, tail, -1)]).astype(jnp.int32)

    def tiles(a, tile):
        return a.reshape(TOP_K, t // tile, tile).transpose(1, 0, 2).astype(jnp.int32)

    xs = _dispatch(x2d, tiles(dest, ROUTER_TILE), zrow, n_pad)
    y = _experts(xs, norm_ffn, block_e, n_used, w_gate, w_up, w_down)
    out = _combine_ple(x2d, tiles(dest, COMBINE_TILE), gate_cols, y, p_i.reshape(t, D_PLE),
                       norm_ple, ple_w_proj, ple_w_gate, final_norm, final)
    return out.reshape(b, s, D)


def _qkv_kernel(x_ref, gq_ref, gkv_ref, wq_ref, wkv_ref, q_ref, k_ref, v_ref):
    x = x_ref[...]
    xn = x * _rms_scale(x)
    hq = (xn * gq_ref[...]).astype(BF16)
    hkv = (xn * gkv_ref[...]).astype(BF16)
    q_ref[...] = (jnp.dot(hq, wq_ref[...], preferred_element_type=F32) * ATTN_SCALE).astype(BF16)
    kv = jnp.dot(hkv, wkv_ref[...], preferred_element_type=F32)
    k_ref[...] = kv[:, :D].astype(BF16)
    v_ref[...] = kv[:, D:].astype(BF16)


def _qkv(x2d, g_q, g_kv, w_q, w_kv):
    t = x2d.shape[0]
    const = lambda i: (0, 0)
    tile = pl.BlockSpec((SEQ_TILE, D), lambda i: (i, 0))
    return pl.pallas_call(
        _qkv_kernel,
        name="qkv",
        grid=(t // SEQ_TILE,),
        in_specs=[
            tile,
            pl.BlockSpec((1, D), const),
            pl.BlockSpec((1, D), const),
            pl.BlockSpec((D, D), const),
            pl.BlockSpec((D, 2 * D), const),
        ],
        out_specs=[tile, tile, tile],
        out_shape=[jax.ShapeDtypeStruct((t, D), BF16)] * 3,
        compiler_params=_params(1),
    )(x2d, g_q.reshape(1, D), g_kv.reshape(1, D), w_q.astype(BF16), w_kv.astype(BF16))


def _attn_kernel(q_ref, kp_ref, kc_ref, vp_ref, vc_ref, bias_ref, x_ref, wo_ref, out_ref,
                 kwin, vwin, o_scr):
    si = pl.program_id(1)
    kwin[0:SEQ_TILE, :] = kp_ref[0]
    kwin[SEQ_TILE:, :] = kc_ref[0]
    vwin[0:SEQ_TILE, :] = vp_ref[0]
    vwin[SEQ_TILE:, :] = vc_ref[0]
    lane = lax.broadcasted_iota(jnp.int32, (CHUNK, 2 * B_HEAD_DIM), 1)
    first_head = lane < B_HEAD_DIM
    kpos = lax.broadcasted_iota(jnp.int32, (CHUNK, BAND), 1)

    def chunk_body(c, carry):
        r0 = pl.multiple_of(c * CHUNK, CHUNK)
        rows = pl.ds(r0, CHUNK)
        band = pl.ds(r0, BAND)
        valid = (kpos + r0 + (si - 1) * SEQ_TILE) >= 0
        for pr in range(B_HEADS // 2):
            sl = slice(pr * 2 * B_HEAD_DIM, (pr + 1) * 2 * B_HEAD_DIM)
            q2 = q_ref[0, rows, sl]
            kb = kwin[band, sl]
            vb = vwin[band, sl]
            outs = []
            for hh in range(2):
                keep = first_head if hh == 0 else jnp.logical_not(first_head)
                qh = jnp.where(keep, q2, jnp.zeros_like(q2))
                s = lax.dot_general(qh, kb, (((1,), (1,)), ((), ())),
                                    preferred_element_type=F32)
                s = jnp.where(valid, s + bias_ref[2 * pr + hh], NEG_INF)
                m = jnp.max(s, axis=-1, keepdims=True)
                w = jnp.exp(s - m)
                denom = jnp.sum(w, axis=-1, keepdims=True)
                o = jnp.dot(w.astype(BF16), vb, preferred_element_type=F32)
                outs.append(o / denom)
            o_scr[rows, sl] = jnp.where(first_head, outs[0], outs[1]).astype(BF16)
        return carry

    lax.fori_loop(0, SEQ_TILE // CHUNK, chunk_body, 0)
    out_ref[0] = x_ref[0] + jnp.dot(o_scr[...], wo_ref[...], preferred_element_type=F32)


def _attn(x, q, k, v, bias, w_o):
    b, s, _ = x.shape
    cur = lambda bi, si: (bi, si, 0)
    prev = lambda bi, si: (bi, jnp.maximum(si - 1, 0), 0)
    blk = (1, SEQ_TILE, D)
    return pl.pallas_call(
        _attn_kernel,
        name="attn",
        grid=(b, s // SEQ_TILE),
        in_specs=[
            pl.BlockSpec(blk, cur),
            pl.BlockSpec(blk, prev),
            pl.BlockSpec(blk, cur),
            pl.BlockSpec(blk, prev),
            pl.BlockSpec(blk, cur),
            pl.BlockSpec((B_HEADS, CHUNK, BAND), lambda bi, si: (0, 0, 0)),
            pl.BlockSpec(blk, cur),
            pl.BlockSpec((D, D), lambda bi, si: (0, 0)),
        ],
        out_specs=pl.BlockSpec(blk, cur),
        out_shape=jax.ShapeDtypeStruct(x.shape, F32),
        scratch_shapes=[
            pltpu.VMEM((2 * SEQ_TILE, D), BF16),
            pltpu.VMEM((2 * SEQ_TILE, D), BF16),
            pltpu.VMEM((SEQ_TILE, D), BF16),
        ],
        compiler_params=_params(2),
    )(q, k, k, v, v, bias, x, w_o.astype(BF16))


def kernel(x, p, a_w_in, a_lb_logits, a_out_norm, a_w_o, kv_norm, w_kv, b_w_q, b_rel_bias, b_w_o,
           norm_mix, norm_ffn, norm_ple, moe_w_group, moe_b_group, moe_w_expert, moe_b_expert,
           moe_w_gate, moe_w_up, moe_w_down, ple_w_proj, ple_w_gate, final_norm):
    b, s, _ = x.shape
    lower_bounds = jnp.cumsum(jax.nn.softmax(a_lb_logits.astype(F32), axis=0), axis=0)

    def moe(xi, i, final):
        return _moe_ple(xi, p[i], norm_ffn[i], moe_w_group[i], moe_b_group[i], moe_w_expert[i],
                        moe_b_expert[i], moe_w_gate[i], moe_w_up[i], moe_w_down[i], norm_ple[i],
                        ple_w_proj[i], ple_w_gate[i], final_norm, final)

    x = _mixer_a(x, norm_mix[0], a_w_in[0], lower_bounds[0], a_out_norm[0], a_w_o[0])
    x = moe(x, 0, False)

    q, k, v = _qkv(x.reshape(b * s, D), norm_mix[1], kv_norm, b_w_q[0], w_kv)
    q_pos = jnp.arange(CHUNK)[:, None]
    k_pos = jnp.arange(BAND)[None, :] - LEFT_CHUNKS * CHUNK
    rel = jnp.clip(q_pos - k_pos, REL_MIN, REL_MAX) - REL_MIN
    bias = b_rel_bias[0][:, rel].astype(F32)
    shp = (b, s, D)
    x = _attn(x, q.reshape(shp), k.reshape(shp), v.reshape(shp), bias, b_w_o[0])
    x = moe(x, 1, True)
    return x
```

```python
import functools

import jax
import jax.numpy as jnp
from jax import lax
from jax.experimental import pallas as pl
from jax.experimental.pallas import tpu as pltpu

F32 = jnp.float32
BF16 = jnp.bfloat16

D = 1024
CHUNK = 64
A_HEADS = 8
A_HEAD_DIM = 128
B_HEADS = 16
B_HEAD_DIM = 64
LEFT_CHUNKS = 8
BAND = (LEFT_CHUNKS + 1) * CHUNK
REL_MIN = -(CHUNK - 1)
REL_MAX = 256
ATTN_SCALE = B_HEAD_DIM ** -0.5
N_GROUPS = 4
EXPERTS_PER_GROUP = 8
N_EXPERTS = 32
TOP_K = 2
D_EXPERT = 512
MOE_BLOCK = 256
D_PLE = 256
EPS = 1e-6
NEG_INF = -1e30

SEQ_TILE = 512
ROUTER_TILE = 512
COMBINE_TILE = 256
ROUTER_ROWS = 40
VMEM_LIMIT = 56 * 1024 * 1024


def _params(n_axes, vmem=VMEM_LIMIT):
    return pltpu.CompilerParams(dimension_semantics=("arbitrary",) * n_axes,
                                vmem_limit_bytes=vmem)


def _rms_scale(x):
    return lax.rsqrt(jnp.mean(x * x, axis=-1, keepdims=True) + EPS)


def _sigmoid(x):
    return 1.0 / (1.0 + jnp.exp(-x))


def _mixer_a_kernel(x_ref, g_ref, win_ref, lb_ref, onorm_ref, wo_ref, out_ref,
                    proj_scr, o_scr, state_scr):
    @pl.when(pl.program_id(1) == 0)
    def _():
        state_scr[...] = jnp.zeros_like(state_scr)

    x = x_ref[0]
    h = (x * _rms_scale(x) * g_ref[...]).astype(BF16)
    proj_scr[...] = jnp.dot(h, win_ref[...], preferred_element_type=F32)

    row = lax.broadcasted_iota(jnp.int32, (CHUNK, CHUNK), 0)
    col = lax.broadcasted_iota(jnp.int32, (CHUNK, CHUNK), 1)
    causal = row >= col
    tril = causal.astype(BF16)
    lb = lb_ref[...]
    onorm = onorm_ref[...]

    def chunk_body(c, carry):
        r0 = pl.multiple_of(c * CHUNK, CHUNK)
        rows = pl.ds(r0, CHUNK)
        f = lb + (1.0 - lb) * _sigmoid(proj_scr[rows, D:2 * D])
        logf = jnp.log(f)
        hi = logf.astype(BF16)
        lo = (logf - hi.astype(F32)).astype(BF16)
        g = (jnp.dot(tril, hi, preferred_element_type=F32)
             + jnp.dot(tril, lo, preferred_element_type=F32))
        for hd in range(A_HEADS):
            sl = slice(hd * A_HEAD_DIM, (hd + 1) * A_HEAD_DIM)
            gh = g[:, sl]
            g_last = gh[CHUNK - 1:CHUNK, :]
            k = 1.0 - f[:, sl]
            q_dec = (proj_scr[rows, sl] * jnp.exp(gh)).astype(BF16)
            k_inv = (k * jnp.exp(-gh)).astype(BF16)
            k_tail = (k * jnp.exp(g_last - gh)).astype(BF16)
            v = proj_scr[rows, 2 * D + hd * A_HEAD_DIM:2 * D + (hd + 1) * A_HEAD_DIM]
            v_b = v.astype(BF16)
            att = lax.dot_general(q_dec, k_inv, (((1,), (1,)), ((), ())),
                                  preferred_element_type=F32)
            att = jnp.where(causal, att, 0.0).astype(BF16)
            st = state_scr[hd]
            o = (jnp.dot(att, v_b, preferred_element_type=F32)
                 + lax.dot_general(q_dec, st.astype(BF16), (((1,), (1,)), ((), ())),
                                   preferred_element_type=F32))
            v_t = v.T.astype(BF16)
            state_scr[hd] = st * jnp.exp(g_last) + jnp.dot(v_t, k_tail,
                                                           preferred_element_type=F32)
            o = o * _rms_scale(o)
            og = proj_scr[rows, 3 * D + hd * A_HEAD_DIM:3 * D + (hd + 1) * A_HEAD_DIM]
            o = o * onorm[:, sl] * (og * _sigmoid(og))
            o_scr[rows, sl] = o.astype(BF16)
        return carry

    lax.fori_loop(0, SEQ_TILE // CHUNK, chunk_body, 0)
    out_ref[0] = x + jnp.dot(o_scr[...], wo_ref[...], preferred_element_type=F32)


def _mixer_a(x, g, w_in, lb, out_norm, w_o):
    b, s, _ = x.shape
    const = lambda bi, si: (0, 0)
    return pl.pallas_call(
        _mixer_a_kernel,
        name="mixer_a",
        grid=(b, s // SEQ_TILE),
        in_specs=[
            pl.BlockSpec((1, SEQ_TILE, D), lambda bi, si: (bi, si, 0)),
            pl.BlockSpec((1, D), const),
            pl.BlockSpec((D, 4 * D), const, pipeline_mode=pl.Buffered(1)),
            pl.BlockSpec((1, D), const),
            pl.BlockSpec((1, D), const),
            pl.BlockSpec((D, D), const, pipeline_mode=pl.Buffered(1)),
        ],
        out_specs=pl.BlockSpec((1, SEQ_TILE, D), lambda bi, si: (bi, si, 0)),
        out_shape=jax.ShapeDtypeStruct(x.shape, F32),
        scratch_shapes=[
            pltpu.VMEM((SEQ_TILE, 4 * D), F32),
            pltpu.VMEM((SEQ_TILE, D), BF16),
            pltpu.VMEM((A_HEADS, A_HEAD_DIM, A_HEAD_DIM), F32),
        ],
        compiler_params=_params(2),
    )(x, g.reshape(1, D), w_in.astype(BF16), lb.reshape(1, D), out_norm.reshape(1, D),
      w_o.astype(BF16))


def _router_kernel(x_ref, g_ref, wr_ref, br_ref, ids_ref, gates_ref, counts_ref, cnt_scr):
    @pl.when(pl.program_id(0) == 0)
    def _():
        cnt_scr[...] = jnp.zeros_like(cnt_scr)

    tm = ROUTER_TILE
    x = x_ref[...]
    h = x * _rms_scale(x) * g_ref[...]
    logits = lax.dot_general(wr_ref[...], h, (((1,), (1,)), ((), ())),
                             precision=lax.Precision.HIGHEST,
                             preferred_element_type=F32) + br_ref[...]
    el = logits[0:N_EXPERTS]
    gl = logits[N_EXPERTS:ROUTER_ROWS]
    grow = lax.broadcasted_iota(jnp.int32, gl.shape, 0)
    gl = jnp.where(grow < N_GROUPS, gl, -jnp.inf)
    gmax = jnp.max(gl, axis=0, keepdims=True)
    gsum = jnp.sum(jnp.exp(gl - gmax), axis=0, keepdims=True)
    grp_w = 1.0 / gsum
    gidx = jnp.min(jnp.where(gl == gmax, grow, N_GROUPS), axis=0, keepdims=True)

    erow = lax.broadcasted_iota(jnp.int32, el.shape, 0)
    masked = jnp.where((erow // EXPERTS_PER_GROUP) == gidx, el, -jnp.inf)
    top1 = jnp.max(masked, axis=0, keepdims=True)
    i1 = jnp.min(jnp.where(masked == top1, erow, N_EXPERTS), axis=0, keepdims=True)
    masked2 = jnp.where(erow == i1, -jnp.inf, masked)
    top2 = jnp.max(masked2, axis=0, keepdims=True)
    i2 = jnp.min(jnp.where(masked2 == top2, erow, N_EXPERTS), axis=0, keepdims=True)
    e2 = jnp.exp(top2 - top1)
    denom = 1.0 + e2
    g1 = grp_w * (1.0 / denom)
    g2 = grp_w * (e2 / denom)

    sel1 = erow == i1
    sel2 = erow == i2
    onehot = (sel1 | sel2).astype(BF16)
    tr = lax.broadcasted_iota(jnp.int32, (tm, tm), 0)
    tc = lax.broadcasted_iota(jnp.int32, (tm, tm), 1)
    before = (tr < tc).astype(BF16)
    prefix = jnp.dot(onehot, before, preferred_element_type=F32) + cnt_scr[...]
    r1 = jnp.sum(jnp.where(sel1, prefix, 0.0), axis=0, keepdims=True)
    r2 = jnp.sum(jnp.where(sel2, prefix, 0.0), axis=0, keepdims=True)
    cnt_scr[...] += jnp.sum(onehot.astype(F32), axis=1, keepdims=True)

    zi = jnp.zeros((4, tm), jnp.int32)
    ids_ref[0] = jnp.concatenate(
        [i1, i2, r1.astype(jnp.int32), r2.astype(jnp.int32), zi], axis=0)
    gates_ref[0] = jnp.concatenate([g1, g2, jnp.zeros((6, tm), F32)], axis=0)
    counts_ref[...] = jnp.broadcast_to(cnt_scr[...], counts_ref.shape).astype(jnp.int32)


def _router(x2d, g, w_group, b_group, w_expert, b_expert):
    t = x2d.shape[0]
    nt = t // ROUTER_TILE
    pad = ROUTER_ROWS - N_EXPERTS - N_GROUPS
    wr = jnp.concatenate([w_expert.T, w_group.T, jnp.zeros((pad, D), F32)], axis=0)
    br = jnp.concatenate([b_expert, b_group, jnp.zeros((pad,), F32)]).reshape(ROUTER_ROWS, 1)
    const = lambda i: (0, 0)
    return pl.pallas_call(
        _router_kernel,
        name="router",
        grid=(nt,),
        in_specs=[
            pl.BlockSpec((ROUTER_TILE, D), lambda i: (i, 0)),
            pl.BlockSpec((1, D), const),
            pl.BlockSpec((ROUTER_ROWS, D), const),
            pl.BlockSpec((ROUTER_ROWS, 1), const),
        ],
        out_specs=[
            pl.BlockSpec((1, 8, ROUTER_TILE), lambda i: (i, 0, 0)),
            pl.BlockSpec((1, 8, ROUTER_TILE), lambda i: (i, 0, 0)),
            pl.BlockSpec((N_EXPERTS, 128), const),
        ],
        out_shape=[
            jax.ShapeDtypeStruct((nt, 8, ROUTER_TILE), jnp.int32),
            jax.ShapeDtypeStruct((nt, 8, ROUTER_TILE), F32),
            jax.ShapeDtypeStruct((N_EXPERTS, 128), jnp.int32),
        ],
        scratch_shapes=[pltpu.VMEM((N_EXPERTS, 1), F32)],
        compiler_params=_params(1),
    )(x2d, g.reshape(1, D), wr, br)


LANES = 128
ROW_TILES = D // LANES


def _row_copy(src_ref, src_row, dst_ref, dst_row, sem):
    return pltpu.make_async_copy(src_ref.at[src_row], dst_ref.at[dst_row], sem)


def _to_slabs(slab_ref, x):
    for s in range(ROW_TILES):
        slab_ref[:, s, :] = x[:, s * LANES:(s + 1) * LANES]


def _from_slabs(slab_ref):
    return jnp.concatenate([slab_ref[:, s, :] for s in range(ROW_TILES)], axis=-1)


def _dispatch_kernel(zrow_ref, dest_ref, x_ref, xs_ref, xrow, zbuf, zsem, sem):
    @pl.when(pl.program_id(0) == 0)
    def _():
        zbuf[...] = jnp.zeros_like(zbuf)

        def zcopy(e):
            return pltpu.make_async_copy(zbuf, xs_ref.at[pl.ds(zrow_ref[e], MOE_BLOCK)], zsem)

        def zstart(e, c):
            @pl.when(zrow_ref[e] >= 0)
            def _():
                zcopy(e).start()
            return c

        def zwait(e, c):
            @pl.when(zrow_ref[e] >= 0)
            def _():
                zcopy(e).wait()
            return c

        lax.fori_loop(0, 2 * N_EXPERTS, zstart, 0)
        lax.fori_loop(0, 2 * N_EXPERTS, zwait, 0)

    _to_slabs(xrow, x_ref[...])

    def start(t, c):
        for k in range(TOP_K):
            _row_copy(xrow, t, xs_ref, dest_ref[0, k, t], sem).start()
        return c

    def wait(t, c):
        for k in range(TOP_K):
            _row_copy(xrow, t, xs_ref, dest_ref[0, k, t], sem).wait()
        return c

    lax.fori_loop(0, ROUTER_TILE, start, 0)
    lax.fori_loop(0, ROUTER_TILE, wait, 0)


def _dispatch(x2d, dest, zrow, n_pad):
    t = x2d.shape[0]
    nt = t // ROUTER_TILE
    return pl.pallas_call(
        _dispatch_kernel,
        name="dispatch",
        grid_spec=pltpu.PrefetchScalarGridSpec(
            num_scalar_prefetch=1,
            grid=(nt,),
            in_specs=[
                pl.BlockSpec((1, TOP_K, ROUTER_TILE), lambda i, z: (i, 0, 0),
                             memory_space=pltpu.SMEM),
                pl.BlockSpec((ROUTER_TILE, D), lambda i, z: (i, 0)),
            ],
            out_specs=pl.BlockSpec(memory_space=pl.ANY),
            scratch_shapes=[
                pltpu.VMEM((ROUTER_TILE, ROW_TILES, LANES), F32),
                pltpu.VMEM((MOE_BLOCK, ROW_TILES, LANES), F32),
                pltpu.SemaphoreType.DMA(()),
                pltpu.SemaphoreType.DMA(()),
            ],
        ),
        out_shape=jax.ShapeDtypeStruct((n_pad, ROW_TILES, LANES), F32),
        compiler_params=_params(1),
    )(zrow, dest, x2d)


def _experts_kernel(be_ref, nu_ref, xs_ref, g_ref, wg_ref, wu_ref, wd_ref, y_ref,
                    wg_b, wu_b, wd_b):
    i = pl.program_id(0)
    prev = be_ref[jnp.maximum(i - 1, 0)]

    @pl.when((i == 0) | (be_ref[i] != prev))
    def _():
        wg_b[...] = wg_ref[0, 0].astype(BF16)
        wu_b[...] = wu_ref[0, 0].astype(BF16)
        wd_b[...] = wd_ref[0, 0].astype(BF16)

    @pl.when(i < nu_ref[0])
    def _():
        x = _from_slabs(xs_ref)
        h = (x * _rms_scale(x) * g_ref[...]).astype(BF16)
        a = jnp.dot(h, wg_b[...], preferred_element_type=F32)
        u = jnp.dot(h, wu_b[...], preferred_element_type=F32)
        hid = (a * _sigmoid(a) * u).astype(BF16)
        _to_slabs(y_ref, jnp.dot(hid, wd_b[...], preferred_element_type=F32))

    @pl.when(i >= nu_ref[0])
    def _():
        y_ref[...] = jnp.zeros_like(y_ref)


def _experts(xs, g, block_e, n_used, layer, w_gate, w_up, w_down):
    n_pad = xs.shape[0]
    n_blocks = n_pad // MOE_BLOCK

    def blk(i, be, nu):
        return jnp.minimum(i, nu[0] - 1)

    return pl.pallas_call(
        _experts_kernel,
        name="experts",
        grid_spec=pltpu.PrefetchScalarGridSpec(
            num_scalar_prefetch=2,
            grid=(n_blocks,),
            in_specs=[
                pl.BlockSpec((MOE_BLOCK, ROW_TILES, LANES),
                             lambda i, be, nu: (blk(i, be, nu), 0, 0)),
                pl.BlockSpec((1, D), lambda i, be, nu: (0, 0)),
                pl.BlockSpec((1, 1, D, D_EXPERT), lambda i, be, nu: (layer, be[i], 0, 0)),
                pl.BlockSpec((1, 1, D, D_EXPERT), lambda i, be, nu: (layer, be[i], 0, 0)),
                pl.BlockSpec((1, 1, D_EXPERT, D), lambda i, be, nu: (layer, be[i], 0, 0)),
            ],
            out_specs=pl.BlockSpec((MOE_BLOCK, ROW_TILES, LANES), lambda i, be, nu: (i, 0, 0)),
            scratch_shapes=[
                pltpu.VMEM((D, D_EXPERT), BF16),
                pltpu.VMEM((D, D_EXPERT), BF16),
                pltpu.VMEM((D_EXPERT, D), BF16),
            ],
        ),
        out_shape=jax.ShapeDtypeStruct(xs.shape, F32),
        compiler_params=_params(1),
    )(block_e, n_used, xs, g.reshape(1, D), w_gate, w_up, w_down)


def _combine_kernel(dest_ref, x_ref, gate_ref, y_ref, p_ref, gple_ref, wp_ref, wg_ref,
                    gfin_ref, out_ref, ybuf, sem, *, final):
    tc = COMBINE_TILE

    def start(t, c):
        for k in range(TOP_K):
            _row_copy(y_ref, dest_ref[0, k, t], ybuf.at[k], t, sem).start()
        return c

    def wait(t, c):
        for k in range(TOP_K):
            _row_copy(y_ref, dest_ref[0, k, t], ybuf.at[k], t, sem).wait()
        return c

    lax.fori_loop(0, tc, start, 0)
    lax.fori_loop(0, tc, wait, 0)

    gates = gate_ref[...]
    x = (x_ref[...] + gates[:, 0:1] * _from_slabs(ybuf.at[0])
         + gates[:, 1:2] * _from_slabs(ybuf.at[1]))
    h = (x * _rms_scale(x) * gple_ref[...]).astype(BF16)
    gate = _sigmoid(jnp.dot(h, wg_ref[...], preferred_element_type=F32))
    proj = jnp.dot(p_ref[0].astype(BF16), wp_ref[...], preferred_element_type=F32)
    x = x + proj * gate
    if final:
        x = x * _rms_scale(x) * gfin_ref[...]
    out_ref[...] = x


def _combine_ple(x2d, dest, gates, y, p3d, layer, g_ple, w_proj, w_gate, g_final, final):
    t = x2d.shape[0]
    nt = t // COMBINE_TILE
    const = lambda i: (0, 0)
    return pl.pallas_call(
        functools.partial(_combine_kernel, final=final),
        name="combine_final" if final else "combine",
        grid=(nt,),
        in_specs=[
            pl.BlockSpec((1, TOP_K, COMBINE_TILE), lambda i: (i, 0, 0), memory_space=pltpu.SMEM),
            pl.BlockSpec((COMBINE_TILE, D), lambda i: (i, 0)),
            pl.BlockSpec((COMBINE_TILE, TOP_K), lambda i: (i, 0)),
            pl.BlockSpec(memory_space=pl.ANY),
            pl.BlockSpec((1, COMBINE_TILE, D_PLE), lambda i: (layer, i, 0)),
            pl.BlockSpec((1, D), const),
            pl.BlockSpec((D_PLE, D), const),
            pl.BlockSpec((D, D), const),
            pl.BlockSpec((1, D), const),
        ],
        out_specs=pl.BlockSpec((COMBINE_TILE, D), lambda i: (i, 0)),
        out_shape=jax.ShapeDtypeStruct((t, D), F32),
        scratch_shapes=[
            pltpu.VMEM((TOP_K, COMBINE_TILE, ROW_TILES, LANES), F32),
            pltpu.SemaphoreType.DMA(()),
        ],
        compiler_params=_params(1),
    )(dest, x2d, gates, y, p3d, g_ple.reshape(1, D), w_proj.astype(BF16),
      w_gate.astype(BF16), g_final.reshape(1, D))


def _moe_ple(x, p, layer, norm_ffn, w_group, b_group, w_expert, b_expert, w_gate, w_up, w_down,
             norm_ple, ple_w_proj, ple_w_gate, final_norm, final):
    b, s, _ = x.shape
    t = b * s
    x2d = x.reshape(t, D)
    ids, gates, counts = _router(x2d, norm_ffn, w_group, b_group, w_expert, b_expert)

    counts = counts[:, 0]
    padded = (counts + MOE_BLOCK - 1) // MOE_BLOCK * MOE_BLOCK
    pad_end = jnp.cumsum(padded)
    pad_start = pad_end - padded
    n_blocks = t * TOP_K // MOE_BLOCK + N_EXPERTS
    n_pad = n_blocks * MOE_BLOCK
    e = ids[:, 0:TOP_K, :].transpose(1, 0, 2).reshape(TOP_K, t)
    r = ids[:, TOP_K:2 * TOP_K, :].transpose(1, 0, 2).reshape(TOP_K, t)
    experts = jnp.arange(N_EXPERTS, dtype=jnp.int32)
    dest = r + jnp.sum(jnp.where(e[..., None] == experts, pad_start, 0), axis=-1)
    gate_cols = gates[:, 0:TOP_K, :].transpose(0, 2, 1).reshape(t, TOP_K)
    block_row = jnp.arange(n_blocks, dtype=jnp.int32) * MOE_BLOCK
    block_e = jnp.minimum(jnp.sum(pad_end[None, :] <= block_row[:, None], axis=-1),
                          N_EXPERTS - 1).astype(jnp.int32)
    n_used = (pad_end[-1:] // MOE_BLOCK).astype(jnp.int32)
    tail = (n_used[0] + jnp.arange(N_EXPERTS, dtype=jnp.int32)) * MOE_BLOCK
    zrow = jnp.concatenate([jnp.where(padded > 0, pad_end - MOE_BLOCK, -1),
                            jnp.where(tail < n_pad, tail, -1)]).astype(jnp.int32)

    def tiles(a, tile):
        return a.reshape(TOP_K, t // tile, tile).transpose(1, 0, 2).astype(jnp.int32)

    xs = _dispatch(x2d, tiles(dest, ROUTER_TILE), zrow, n_pad)
    y = _experts(xs, norm_ffn, block_e, n_used, layer, w_gate, w_up, w_down)
    out = _combine_ple(x2d, tiles(dest, COMBINE_TILE), gate_cols, y, p.reshape(-1, t, D_PLE), layer,
                       norm_ple, ple_w_proj, ple_w_gate, final_norm, final)
    return out.reshape(b, s, D)


def _qkv_kernel(x_ref, gq_ref, gkv_ref, wq_ref, wkv_ref, qt_ref, k_ref, vt_ref):
    x = x_ref[0]
    xn = x * _rms_scale(x)
    hq = (xn * gq_ref[...]).astype(BF16)
    hkv = (xn * gkv_ref[...]).astype(BF16)
    q = jnp.dot(hq, wq_ref[...], preferred_element_type=F32) * ATTN_SCALE
    kv = jnp.dot(hkv, wkv_ref[...], preferred_element_type=F32)
    qt_ref[0] = q.T.astype(BF16)
    k_ref[0] = kv[:, :D].astype(BF16)
    vt_ref[0] = kv[:, D:].T.astype(BF16)


def _qkv(x, g_q, g_kv, w_q, w_kv):
    b, s, _ = x.shape
    const = lambda bi, si: (0, 0)
    row_major = pl.BlockSpec((1, SEQ_TILE, D), lambda bi, si: (bi, si, 0))
    feat_major = pl.BlockSpec((1, D, SEQ_TILE), lambda bi, si: (bi, 0, si))
    return pl.pallas_call(
        _qkv_kernel,
        name="qkv",
        grid=(b, s // SEQ_TILE),
        in_specs=[
            row_major,
            pl.BlockSpec((1, D), const),
            pl.BlockSpec((1, D), const),
            pl.BlockSpec((D, D), const),
            pl.BlockSpec((D, 2 * D), const),
        ],
        out_specs=[feat_major, row_major, feat_major],
        out_shape=[jax.ShapeDtypeStruct((b, D, s), BF16),
                   jax.ShapeDtypeStruct((b, s, D), BF16),
                   jax.ShapeDtypeStruct((b, D, s), BF16)],
        compiler_params=_params(2),
    )(x, g_q.reshape(1, D), g_kv.reshape(1, D), w_q.astype(BF16), w_kv.astype(BF16))


Q_GROUP = 2 * CHUNK
G_BAND = BAND + CHUNK
PAIR = 2 * B_HEAD_DIM


def _attn_kernel(qt_ref, kp_ref, kc_ref, vtp_ref, vtc_ref, bias_ref, x_ref, wo_ref, out_ref, o_scr):
    si = pl.program_id(1)
    drow = lax.broadcasted_iota(jnp.int32, (PAIR, Q_GROUP), 0)
    first_head = drow < B_HEAD_DIM
    kidx = lax.broadcasted_iota(jnp.int32, (G_BAND, 1), 0)
    for g in range(SEQ_TILE // Q_GROUP):
        w0 = g * Q_GROUP
        n_prev = SEQ_TILE - w0
        n_cur = G_BAND - n_prev
        neg = jnp.where((si - 1) * SEQ_TILE + w0 + kidx >= 0, 0.0, NEG_INF)
        for pr in range(B_HEADS // 2):
            feat = slice(pr * PAIR, (pr + 1) * PAIR)
            qt = qt_ref[0, feat, w0:w0 + Q_GROUP]
            zero = jnp.zeros_like(qt)
            qblk = jnp.concatenate([jnp.where(first_head, qt, zero),
                                    jnp.where(first_head, zero, qt)], axis=1)
            kb = jnp.concatenate([kp_ref[0, w0:, feat], kc_ref[0, :n_cur, feat]], axis=0)
            s = jnp.dot(kb, qblk, preferred_element_type=F32) + bias_ref[pr] + neg
            m = jnp.max(s, axis=0, keepdims=True)
            p = jnp.exp(s - m)
            inv = 1.0 / jnp.sum(p, axis=0, keepdims=True)
            vt = jnp.concatenate([vtp_ref[0, feat, w0:], vtc_ref[0, feat, :n_cur]], axis=1)
            ot = jnp.dot(vt, p.astype(BF16), preferred_element_type=F32)
            ot = jnp.where(first_head, ot[:, :Q_GROUP] * inv[:, :Q_GROUP],
                           ot[:, Q_GROUP:] * inv[:, Q_GROUP:])
            o_scr[w0:w0 + Q_GROUP, feat] = ot.T.astype(BF16)
    out_ref[0] = x_ref[0] + jnp.dot(o_scr[...], wo_ref[...], preferred_element_type=F32)


def _attn(x, qt, k, vt, bias_t, w_o):
    b, s, _ = x.shape
    cur = lambda bi, si: (bi, si, 0)
    prev = lambda bi, si: (bi, jnp.maximum(si - 1, 0), 0)
    cur_t = lambda bi, si: (bi, 0, si)
    prev_t = lambda bi, si: (bi, 0, jnp.maximum(si - 1, 0))
    blk = (1, SEQ_TILE, D)
    blk_t = (1, D, SEQ_TILE)
    return pl.pallas_call(
        _attn_kernel,
        name="attn",
        grid=(b, s // SEQ_TILE),
        in_specs=[
            pl.BlockSpec(blk_t, cur_t),
            pl.BlockSpec(blk, prev),
            pl.BlockSpec(blk, cur),
            pl.BlockSpec(blk_t, prev_t),
            pl.BlockSpec(blk_t, cur_t),
            pl.BlockSpec((B_HEADS // 2, G_BAND, 2 * Q_GROUP), lambda bi, si: (0, 0, 0),
                         pipeline_mode=pl.Buffered(1)),
            pl.BlockSpec(blk, cur),
            pl.BlockSpec((D, D), lambda bi, si: (0, 0), pipeline_mode=pl.Buffered(1)),
        ],
        out_specs=pl.BlockSpec(blk, cur),
        out_shape=jax.ShapeDtypeStruct(x.shape, F32),
        scratch_shapes=[pltpu.VMEM((SEQ_TILE, D), BF16)],
        compiler_params=_params(2),
    )(qt, k, k, vt, vt, bias_t, x, w_o.astype(BF16))


def _group_bias(table):
    band = _band_bias(table)
    pad = lambda lo, hi: jnp.pad(band, ((0, 0), (0, 0), (lo, hi)), constant_values=NEG_INF)
    both = jnp.concatenate([pad(0, CHUNK), pad(CHUNK, 0)], axis=1)
    both = both.reshape(B_HEADS // 2, 2, Q_GROUP, G_BAND)
    return both.transpose(0, 3, 1, 2).reshape(B_HEADS // 2, G_BAND, 2 * Q_GROUP)


def _band_bias(table):
    n_rel = REL_MAX - REL_MIN + 1
    span = BAND + CHUNK - 1
    head = jnp.broadcast_to(table[:, n_rel - 1:], (table.shape[0], span - n_rel))
    ext = jnp.concatenate([head, table[:, ::-1]], axis=1)
    rows = [ext[:, CHUNK - 1 - q:CHUNK - 1 - q + BAND] for q in range(CHUNK)]
    return jnp.stack(rows, axis=1)


def kernel(x, p, a_w_in, a_lb_logits, a_out_norm, a_w_o, kv_norm, w_kv, b_w_q, b_rel_bias, b_w_o,
           norm_mix, norm_ffn, norm_ple, moe_w_group, moe_b_group, moe_w_expert, moe_b_expert,
           moe_w_gate, moe_w_up, moe_w_down, ple_w_proj, ple_w_gate, final_norm):
    b, s, _ = x.shape
    lower_bounds = jnp.cumsum(jax.nn.softmax(a_lb_logits.astype(F32), axis=0), axis=0)

    def moe(xi, i, final):
        return _moe_ple(xi, p, i, norm_ffn[i], moe_w_group[i], moe_b_group[i], moe_w_expert[i],
                        moe_b_expert[i], moe_w_gate, moe_w_up, moe_w_down, norm_ple[i],
                        ple_w_proj[i], ple_w_gate[i], final_norm, final)

    x = _mixer_a(x, norm_mix[0], a_w_in[0], lower_bounds[0], a_out_norm[0], a_w_o[0])
    x = moe(x, 0, False)

    qt, k, vt = _qkv(x, norm_mix[1], kv_norm, b_w_q[0], w_kv)
    x = _attn(x, qt, k, vt, _group_bias(b_rel_bias[0].astype(F32)), b_w_o[0])
    x = moe(x, 1, True)
    return x
```

```python
import functools

import jax
import jax.numpy as jnp
from jax import lax
from jax.experimental import pallas as pl
from jax.experimental.pallas import tpu as pltpu

F32 = jnp.float32
BF16 = jnp.bfloat16

D = 1024
CHUNK = 64
A_HEADS = 8
A_HEAD_DIM = 128
B_HEADS = 16
B_HEAD_DIM = 64
LEFT_CHUNKS = 8
BAND = (LEFT_CHUNKS + 1) * CHUNK
REL_MIN = -(CHUNK - 1)
REL_MAX = 256
ATTN_SCALE = B_HEAD_DIM ** -0.5
N_GROUPS = 4
EXPERTS_PER_GROUP = 8
N_EXPERTS = 32
TOP_K = 2
D_EXPERT = 512
MOE_BLOCK = 256
D_PLE = 256
EPS = 1e-6
NEG_INF = -1e30

SEQ_TILE = 512
ROUTER_TILE = 512
COMBINE_TILE = 512
ROUTER_ROWS = 40
VMEM_LIMIT = 56 * 1024 * 1024


def _params(n_axes, vmem=VMEM_LIMIT):
    return pltpu.CompilerParams(dimension_semantics=("arbitrary",) * n_axes,
                                vmem_limit_bytes=vmem)


def _rms_scale(x):
    return lax.rsqrt(jnp.mean(x * x, axis=-1, keepdims=True) + EPS)


def _sigmoid(x):
    return 1.0 / (1.0 + jnp.exp(-x))


def _mixer_a_kernel(x_ref, g_ref, win_ref, lb_ref, onorm_ref, wo_ref, out_ref,
                    proj_scr, o_scr, state_scr):
    @pl.when(pl.program_id(1) == 0)
    def _():
        state_scr[...] = jnp.zeros_like(state_scr)

    x = x_ref[0]
    h = (x * _rms_scale(x) * g_ref[...]).astype(BF16)
    proj_scr[...] = jnp.dot(h, win_ref[...], preferred_element_type=F32)

    row = lax.broadcasted_iota(jnp.int32, (CHUNK, CHUNK), 0)
    col = lax.broadcasted_iota(jnp.int32, (CHUNK, CHUNK), 1)
    causal = row >= col
    tril = causal.astype(BF16)
    lb = lb_ref[...]
    onorm = onorm_ref[...]

    def chunk_body(c, carry):
        r0 = pl.multiple_of(c * CHUNK, CHUNK)
        rows = pl.ds(r0, CHUNK)
        f = lb + (1.0 - lb) * _sigmoid(proj_scr[rows, D:2 * D])
        logf = jnp.log(f)
        hi = logf.astype(BF16)
        lo = (logf - hi.astype(F32)).astype(BF16)
        g = (jnp.dot(tril, hi, preferred_element_type=F32)
             + jnp.dot(tril, lo, preferred_element_type=F32))
        for hd in range(A_HEADS):
            sl = slice(hd * A_HEAD_DIM, (hd + 1) * A_HEAD_DIM)
            gh = g[:, sl]
            g_last = gh[CHUNK - 1:CHUNK, :]
            k = 1.0 - f[:, sl]
            q_dec = (proj_scr[rows, sl] * jnp.exp(gh)).astype(BF16)
            k_inv = (k * jnp.exp(-gh)).astype(BF16)
            k_tail = (k * jnp.exp(g_last - gh)).astype(BF16)
            v = proj_scr[rows, 2 * D + hd * A_HEAD_DIM:2 * D + (hd + 1) * A_HEAD_DIM]
            v_b = v.astype(BF16)
            att = lax.dot_general(q_dec, k_inv, (((1,), (1,)), ((), ())),
                                  preferred_element_type=F32)
            att = jnp.where(causal, att, 0.0).astype(BF16)
            st = state_scr[hd]
            o = (jnp.dot(att, v_b, preferred_element_type=F32)
                 + lax.dot_general(q_dec, st.astype(BF16), (((1,), (1,)), ((), ())),
                                   preferred_element_type=F32))
            v_t = v.T.astype(BF16)
            state_scr[hd] = st * jnp.exp(g_last) + jnp.dot(v_t, k_tail,
                                                           preferred_element_type=F32)
            o = o * _rms_scale(o)
            og = proj_scr[rows, 3 * D + hd * A_HEAD_DIM:3 * D + (hd + 1) * A_HEAD_DIM]
            o = o * onorm[:, sl] * (og * _sigmoid(og))
            o_scr[rows, sl] = o.astype(BF16)
        return carry

    lax.fori_loop(0, SEQ_TILE // CHUNK, chunk_body, 0)
    out_ref[0] = x + jnp.dot(o_scr[...], wo_ref[...], preferred_element_type=F32)


def _mixer_a(x, g, w_in, lb, out_norm, w_o):
    b, s, _ = x.shape
    const = lambda bi, si: (0, 0)
    return pl.pallas_call(
        _mixer_a_kernel,
        name="mixer_a",
        grid=(b, s // SEQ_TILE),
        in_specs=[
            pl.BlockSpec((1, SEQ_TILE, D), lambda bi, si: (bi, si, 0)),
            pl.BlockSpec((1, D), const),
            pl.BlockSpec((D, 4 * D), const, pipeline_mode=pl.Buffered(1)),
            pl.BlockSpec((1, D), const),
            pl.BlockSpec((1, D), const),
            pl.BlockSpec((D, D), const, pipeline_mode=pl.Buffered(1)),
        ],
        out_specs=pl.BlockSpec((1, SEQ_TILE, D), lambda bi, si: (bi, si, 0)),
        out_shape=jax.ShapeDtypeStruct(x.shape, F32),
        scratch_shapes=[
            pltpu.VMEM((SEQ_TILE, 4 * D), F32),
            pltpu.VMEM((SEQ_TILE, D), BF16),
            pltpu.VMEM((A_HEADS, A_HEAD_DIM, A_HEAD_DIM), F32),
        ],
        compiler_params=_params(2),
    )(x, g.reshape(1, D), w_in.astype(BF16), lb.reshape(1, D), out_norm.reshape(1, D),
      w_o.astype(BF16))


def _router_kernel(x_ref, g_ref, wr_ref, br_ref, ids_ref, gates_ref, counts_ref, cnt_scr):
    @pl.when(pl.program_id(0) == 0)
    def _():
        cnt_scr[...] = jnp.zeros_like(cnt_scr)

    tm = ROUTER_TILE
    x = x_ref[...]
    h = x * _rms_scale(x) * g_ref[...]
    logits = lax.dot_general(wr_ref[...], h, (((1,), (1,)), ((), ())),
                             precision=lax.Precision.HIGHEST,
                             preferred_element_type=F32) + br_ref[...]
    el = logits[0:N_EXPERTS]
    gl = logits[N_EXPERTS:ROUTER_ROWS]
    grow = lax.broadcasted_iota(jnp.int32, gl.shape, 0)
    gl = jnp.where(grow < N_GROUPS, gl, -jnp.inf)
    gmax = jnp.max(gl, axis=0, keepdims=True)
    gsum = jnp.sum(jnp.exp(gl - gmax), axis=0, keepdims=True)
    grp_w = 1.0 / gsum
    gidx = jnp.min(jnp.where(gl == gmax, grow, N_GROUPS), axis=0, keepdims=True)

    erow = lax.broadcasted_iota(jnp.int32, el.shape, 0)
    masked = jnp.where((erow // EXPERTS_PER_GROUP) == gidx, el, -jnp.inf)
    top1 = jnp.max(masked, axis=0, keepdims=True)
    i1 = jnp.min(jnp.where(masked == top1, erow, N_EXPERTS), axis=0, keepdims=True)
    masked2 = jnp.where(erow == i1, -jnp.inf, masked)
    top2 = jnp.max(masked2, axis=0, keepdims=True)
    i2 = jnp.min(jnp.where(masked2 == top2, erow, N_EXPERTS), axis=0, keepdims=True)
    e2 = jnp.exp(top2 - top1)
    denom = 1.0 + e2
    g1 = grp_w * (1.0 / denom)
    g2 = grp_w * (e2 / denom)

    sel1 = erow == i1
    sel2 = erow == i2
    onehot = (sel1 | sel2).astype(BF16)
    tr = lax.broadcasted_iota(jnp.int32, (tm, tm), 0)
    tc = lax.broadcasted_iota(jnp.int32, (tm, tm), 1)
    before = (tr < tc).astype(BF16)
    prefix = jnp.dot(onehot, before, preferred_element_type=F32) + cnt_scr[...]
    r1 = jnp.sum(jnp.where(sel1, prefix, 0.0), axis=0, keepdims=True)
    r2 = jnp.sum(jnp.where(sel2, prefix, 0.0), axis=0, keepdims=True)
    cnt_scr[...] += jnp.sum(onehot.astype(F32), axis=1, keepdims=True)

    zi = jnp.zeros((4, tm), jnp.int32)
    ids_ref[0] = jnp.concatenate(
        [i1, i2, r1.astype(jnp.int32), r2.astype(jnp.int32), zi], axis=0)
    gates_ref[0] = jnp.concatenate([g1, g2, jnp.zeros((6, tm), F32)], axis=0)
    counts_ref[...] = jnp.broadcast_to(cnt_scr[...], counts_ref.shape).astype(jnp.int32)


def _router(x2d, g, w_group, b_group, w_expert, b_expert):
    t = x2d.shape[0]
    nt = t // ROUTER_TILE
    pad = ROUTER_ROWS - N_EXPERTS - N_GROUPS
    wr = jnp.concatenate([w_expert.T, w_group.T, jnp.zeros((pad, D), F32)], axis=0)
    br = jnp.concatenate([b_expert, b_group, jnp.zeros((pad,), F32)]).reshape(ROUTER_ROWS, 1)
    const = lambda i: (0, 0)
    return pl.pallas_call(
        _router_kernel,
        name="router",
        grid=(nt,),
        in_specs=[
            pl.BlockSpec((ROUTER_TILE, D), lambda i: (i, 0)),
            pl.BlockSpec((1, D), const),
            pl.BlockSpec((ROUTER_ROWS, D), const),
            pl.BlockSpec((ROUTER_ROWS, 1), const),
        ],
        out_specs=[
            pl.BlockSpec((1, 8, ROUTER_TILE), lambda i: (i, 0, 0)),
            pl.BlockSpec((1, 8, ROUTER_TILE), lambda i: (i, 0, 0)),
            pl.BlockSpec((N_EXPERTS, 128), const),
        ],
        out_shape=[
            jax.ShapeDtypeStruct((nt, 8, ROUTER_TILE), jnp.int32),
            jax.ShapeDtypeStruct((nt, 8, ROUTER_TILE), F32),
            jax.ShapeDtypeStruct((N_EXPERTS, 128), jnp.int32),
        ],
        scratch_shapes=[pltpu.VMEM((N_EXPERTS, 1), F32)],
        compiler_params=_params(1),
    )(x2d, g.reshape(1, D), wr, br)


ROW_UNROLL = 8


def _issue_rows(n_rows, make_copy):
    def body(j, c):
        for u in range(ROW_UNROLL):
            for k in range(TOP_K):
                make_copy(j * ROW_UNROLL + u, k).start(priority=(u * TOP_K + k) % 2)
        return c

    lax.fori_loop(0, n_rows // ROW_UNROLL, body, 0)


def _wait_rows(buf_ref, sem):
    pltpu.make_async_copy(buf_ref, buf_ref, sem).wait()


def _dispatch_kernel(zrow_ref, dest_ref, x_ref, g_ref, xs_ref, hbuf, zbuf, zsem, sem):
    @pl.when(pl.program_id(0) == 0)
    def _():
        zbuf[...] = jnp.zeros_like(zbuf)

        def zcopy(e):
            return pltpu.make_async_copy(zbuf, xs_ref.at[pl.ds(zrow_ref[e], MOE_BLOCK)], zsem)

        def zstart(e, c):
            @pl.when(zrow_ref[e] >= 0)
            def _():
                zcopy(e).start()
            return c

        def zwait(e, c):
            @pl.when(zrow_ref[e] >= 0)
            def _():
                zcopy(e).wait()
            return c

        lax.fori_loop(0, 2 * N_EXPERTS, zstart, 0)
        lax.fori_loop(0, 2 * N_EXPERTS, zwait, 0)

    i = pl.program_id(0)
    slot = i % 2
    x = x_ref[...]
    hbuf[slot] = x * _rms_scale(x) * g_ref[...]

    def row_copy(t, k):
        return pltpu.make_async_copy(hbuf.at[slot, pl.ds(t, 1), :], xs_ref.at[dest_ref[0, k, t]],
                                     sem.at[slot])

    _issue_rows(ROUTER_TILE, row_copy)

    def drain(which):
        for _ in range(TOP_K):
            _wait_rows(hbuf.at[which], sem.at[which])

    @pl.when(i > 0)
    def _():
        drain(1 - slot)

    @pl.when(i == pl.num_programs(0) - 1)
    def _():
        drain(slot)


def _dispatch(x2d, g, dest, zrow, n_pad):
    t = x2d.shape[0]
    nt = t // ROUTER_TILE
    return pl.pallas_call(
        _dispatch_kernel,
        name="dispatch",
        grid_spec=pltpu.PrefetchScalarGridSpec(
            num_scalar_prefetch=1,
            grid=(nt,),
            in_specs=[
                pl.BlockSpec((1, TOP_K, ROUTER_TILE), lambda i, z: (i, 0, 0),
                             memory_space=pltpu.SMEM),
                pl.BlockSpec((ROUTER_TILE, D), lambda i, z: (i, 0)),
                pl.BlockSpec((1, D), lambda i, z: (0, 0)),
            ],
            out_specs=pl.BlockSpec(memory_space=pl.ANY),
            scratch_shapes=[
                pltpu.VMEM((2, ROUTER_TILE, D), F32),
                pltpu.VMEM((MOE_BLOCK, 1, D), F32),
                pltpu.SemaphoreType.DMA(()),
                pltpu.SemaphoreType.DMA((2,)),
            ],
        ),
        out_shape=jax.ShapeDtypeStruct((n_pad, 1, D), F32),
        compiler_params=_params(1),
    )(zrow, dest, x2d, g.reshape(1, D))


def _experts_kernel(be_ref, nu_ref, xs_ref, wg_ref, wu_ref, wd_ref, y_ref,
                    wg_b, wu_b, wd_b, x_scr):
    i = pl.program_id(0)
    prev = be_ref[jnp.maximum(i - 1, 0)]

    @pl.when((i == 0) | (be_ref[i] != prev))
    def _():
        wg_b[...] = wg_ref[0, 0].astype(BF16)
        wu_b[...] = wu_ref[0, 0].astype(BF16)
        wd_b[...] = wd_ref[0, 0].astype(BF16)

    @pl.when(i < nu_ref[0])
    def _():
        x_scr[...] = xs_ref[:, 0, :]
        h = x_scr[...].astype(BF16)
        a = jnp.dot(h, wg_b[...], preferred_element_type=F32)
        u = jnp.dot(h, wu_b[...], preferred_element_type=F32)
        hid = (a * _sigmoid(a) * u).astype(BF16)
        y_ref[:, 0, :] = jnp.dot(hid, wd_b[...], preferred_element_type=F32)

    @pl.when(i >= nu_ref[0])
    def _():
        y_ref[...] = jnp.zeros_like(y_ref)


def _experts(xs, block_e, n_used, layer, w_gate, w_up, w_down):
    n_pad = xs.shape[0]
    n_blocks = n_pad // MOE_BLOCK

    def blk(i, be, nu):
        return jnp.minimum(i, nu[0] - 1)

    return pl.pallas_call(
        _experts_kernel,
        name="experts",
        grid_spec=pltpu.PrefetchScalarGridSpec(
            num_scalar_prefetch=2,
            grid=(n_blocks,),
            in_specs=[
                pl.BlockSpec((MOE_BLOCK, 1, D), lambda i, be, nu: (blk(i, be, nu), 0, 0)),
                pl.BlockSpec((1, 1, D, D_EXPERT), lambda i, be, nu: (layer, be[i], 0, 0)),
                pl.BlockSpec((1, 1, D, D_EXPERT), lambda i, be, nu: (layer, be[i], 0, 0)),
                pl.BlockSpec((1, 1, D_EXPERT, D), lambda i, be, nu: (layer, be[i], 0, 0)),
            ],
            out_specs=pl.BlockSpec((MOE_BLOCK, 1, D), lambda i, be, nu: (i, 0, 0)),
            scratch_shapes=[
                pltpu.VMEM((D, D_EXPERT), BF16),
                pltpu.VMEM((D, D_EXPERT), BF16),
                pltpu.VMEM((D_EXPERT, D), BF16),
                pltpu.VMEM((MOE_BLOCK, D), F32),
            ],
        ),
        out_shape=jax.ShapeDtypeStruct(xs.shape, F32),
        compiler_params=_params(1),
    )(block_e, n_used, xs, w_gate, w_up, w_down)


def _combine_kernel(dest_ref, dest_next_ref, x_ref, gate_ref, y_ref, p_ref, gple_ref, wp_ref,
                    wg_ref, gfin_ref, out_ref, ybuf, sem, *, final):
    i = pl.program_id(0)
    slot = i % 2

    def gather(idx_ref, into):
        def row_copy(t, k):
            return pltpu.make_async_copy(y_ref.at[idx_ref[0, k, t]],
                                         ybuf.at[into, k, pl.ds(t, 1), :], sem.at[into])
        _issue_rows(COMBINE_TILE, row_copy)

    @pl.when(i == 0)
    def _():
        gather(dest_ref, slot)

    @pl.when(i + 1 < pl.num_programs(0))
    def _():
        gather(dest_next_ref, 1 - slot)

    for k in range(TOP_K):
        _wait_rows(ybuf.at[slot, k], sem.at[slot])

    gates = gate_ref[...]
    x = x_ref[...] + gates[:, 0:1] * ybuf[slot, 0] + gates[:, 1:2] * ybuf[slot, 1]
    h = (x * _rms_scale(x) * gple_ref[...]).astype(BF16)
    gate = _sigmoid(jnp.dot(h, wg_ref[...], preferred_element_type=F32))
    proj = jnp.dot(p_ref[0].astype(BF16), wp_ref[...], preferred_element_type=F32)
    x = x + proj * gate
    if final:
        x = x * _rms_scale(x) * gfin_ref[...]
    out_ref[...] = x


def _combine_ple(x2d, dest, gates, y, p3d, layer, g_ple, w_proj, w_gate, g_final, final):
    t = x2d.shape[0]
    nt = t // COMBINE_TILE
    const = lambda i: (0, 0)
    return pl.pallas_call(
        functools.partial(_combine_kernel, final=final),
        name="combine_final" if final else "combine",
        grid=(nt,),
        in_specs=[
            pl.BlockSpec((1, TOP_K, COMBINE_TILE), lambda i: (i, 0, 0), memory_space=pltpu.SMEM),
            pl.BlockSpec((1, TOP_K, COMBINE_TILE), lambda i: (jnp.minimum(i + 1, nt - 1), 0, 0),
                         memory_space=pltpu.SMEM),
            pl.BlockSpec((COMBINE_TILE, D), lambda i: (i, 0)),
            pl.BlockSpec((COMBINE_TILE, TOP_K), lambda i: (i, 0)),
            pl.BlockSpec(memory_space=pl.ANY),
            pl.BlockSpec((1, COMBINE_TILE, D_PLE), lambda i: (layer, i, 0)),
            pl.BlockSpec((1, D), const),
            pl.BlockSpec((D_PLE, D), const, pipeline_mode=pl.Buffered(1)),
            pl.BlockSpec((D, D), const, pipeline_mode=pl.Buffered(1)),
            pl.BlockSpec((1, D), const),
        ],
        out_specs=pl.BlockSpec((COMBINE_TILE, D), lambda i: (i, 0)),
        out_shape=jax.ShapeDtypeStruct((t, D), F32),
        scratch_shapes=[
            pltpu.VMEM((2, TOP_K, COMBINE_TILE, D), F32),
            pltpu.SemaphoreType.DMA((2,)),
        ],
        compiler_params=_params(1),
    )(dest, dest, x2d, gates, y, p3d, g_ple.reshape(1, D), w_proj.astype(BF16),
      w_gate.astype(BF16), g_final.reshape(1, D))


def _moe_ple(x, p, layer, norm_ffn, w_group, b_group, w_expert, b_expert, w_gate, w_up, w_down,
             norm_ple, ple_w_proj, ple_w_gate, final_norm, final):
    b, s, _ = x.shape
    t = b * s
    x2d = x.reshape(t, D)
    ids, gates, counts = _router(x2d, norm_ffn, w_group, b_group, w_expert, b_expert)

    counts = counts[:, 0]
    padded = (counts + MOE_BLOCK - 1) // MOE_BLOCK * MOE_BLOCK
    pad_end = jnp.cumsum(padded)
    pad_start = pad_end - padded
    n_blocks = t * TOP_K // MOE_BLOCK + N_EXPERTS
    n_pad = n_blocks * MOE_BLOCK
    e = ids[:, 0:TOP_K, :].transpose(1, 0, 2).reshape(TOP_K, t)
    r = ids[:, TOP_K:2 * TOP_K, :].transpose(1, 0, 2).reshape(TOP_K, t)
    experts = jnp.arange(N_EXPERTS, dtype=jnp.int32)
    dest = r + jnp.sum(jnp.where(e[..., None] == experts, pad_start, 0), axis=-1)
    gate_cols = gates[:, 0:TOP_K, :].transpose(0, 2, 1).reshape(t, TOP_K)
    block_row = jnp.arange(n_blocks, dtype=jnp.int32) * MOE_BLOCK
    block_e = jnp.minimum(jnp.sum(pad_end[None, :] <= block_row[:, None], axis=-1),
                          N_EXPERTS - 1).astype(jnp.int32)
    n_used = (pad_end[-1:] // MOE_BLOCK).astype(jnp.int32)
    tail = (n_used[0] + jnp.arange(N_EXPERTS, dtype=jnp.int32)) * MOE_BLOCK
    zrow = jnp.concatenate([jnp.where(padded > 0, pad_end - MOE_BLOCK, -1),
                            jnp.where(tail < n_pad, tail, -1)]).astype(jnp.int32)

    def tiles(a, tile):
        return a.reshape(TOP_K, t // tile, tile).transpose(1, 0, 2).astype(jnp.int32)

    xs = _dispatch(x2d, norm_ffn, tiles(dest, ROUTER_TILE), zrow, n_pad)
    y = _experts(xs, block_e, n_used, layer, w_gate, w_up, w_down)
    out = _combine_ple(x2d, tiles(dest, COMBINE_TILE), gate_cols, y, p.reshape(-1, t, D_PLE), layer,
                       norm_ple, ple_w_proj, ple_w_gate, final_norm, final)
    return out.reshape(b, s, D)


def _qkv_kernel(x_ref, gq_ref, gkv_ref, wq_ref, wkv_ref, qt_ref, k_ref, vt_ref):
    x = x_ref[0]
    xn = x * _rms_scale(x)
    hq = (xn * gq_ref[...]).astype(BF16)
    hkv = (xn * gkv_ref[...]).astype(BF16)
    q = jnp.dot(hq, wq_ref[...], preferred_element_type=F32) * ATTN_SCALE
    kv = jnp.dot(hkv, wkv_ref[...], preferred_element_type=F32)
    qt_ref[0] = q.T.astype(BF16)
    k_ref[0] = kv[:, :D].astype(BF16)
    vt_ref[0] = kv[:, D:].T.astype(BF16)


def _qkv(x, g_q, g_kv, w_q, w_kv):
    b, s, _ = x.shape
    const = lambda bi, si: (0, 0)
    row_major = pl.BlockSpec((1, SEQ_TILE, D), lambda bi, si: (bi, si, 0))
    feat_major = pl.BlockSpec((1, D, SEQ_TILE), lambda bi, si: (bi, 0, si))
    return pl.pallas_call(
        _qkv_kernel,
        name="qkv",
        grid=(b, s // SEQ_TILE),
        in_specs=[
            row_major,
            pl.BlockSpec((1, D), const),
            pl.BlockSpec((1, D), const),
            pl.BlockSpec((D, D), const),
            pl.BlockSpec((D, 2 * D), const),
        ],
        out_specs=[feat_major, row_major, feat_major],
        out_shape=[jax.ShapeDtypeStruct((b, D, s), BF16),
                   jax.ShapeDtypeStruct((b, s, D), BF16),
                   jax.ShapeDtypeStruct((b, D, s), BF16)],
        compiler_params=_params(2),
    )(x, g_q.reshape(1, D), g_kv.reshape(1, D), w_q.astype(BF16), w_kv.astype(BF16))


Q_GROUP = 2 * CHUNK
G_BAND = BAND + CHUNK
PAIR = 2 * B_HEAD_DIM


def _attn_kernel(qt_ref, kp_ref, kc_ref, vtp_ref, vtc_ref, bias_ref, x_ref, wo_ref, out_ref, o_scr):
    si = pl.program_id(1)
    drow = lax.broadcasted_iota(jnp.int32, (PAIR, Q_GROUP), 0)
    first_head = drow < B_HEAD_DIM
    kidx = lax.broadcasted_iota(jnp.int32, (G_BAND, 1), 0)
    for g in range(SEQ_TILE // Q_GROUP):
        w0 = g * Q_GROUP
        n_prev = SEQ_TILE - w0
        n_cur = G_BAND - n_prev
        neg = jnp.where((si - 1) * SEQ_TILE + w0 + kidx >= 0, 0.0, NEG_INF)
        for pr in range(B_HEADS // 2):
            feat = slice(pr * PAIR, (pr + 1) * PAIR)
            qt = qt_ref[0, feat, w0:w0 + Q_GROUP]
            zero = jnp.zeros_like(qt)
            qblk = jnp.concatenate([jnp.where(first_head, qt, zero),
                                    jnp.where(first_head, zero, qt)], axis=1)
            kb = jnp.concatenate([kp_ref[0, w0:, feat], kc_ref[0, :n_cur, feat]], axis=0)
            s = jnp.dot(kb, qblk, preferred_element_type=F32) + bias_ref[pr] + neg
            m = jnp.max(s, axis=0, keepdims=True)
            p = jnp.exp(s - m)
            inv = 1.0 / jnp.sum(p, axis=0, keepdims=True)
            vt = jnp.concatenate([vtp_ref[0, feat, w0:], vtc_ref[0, feat, :n_cur]], axis=1)
            ot = jnp.dot(vt, p.astype(BF16), preferred_element_type=F32)
            ot = jnp.where(first_head, ot[:, :Q_GROUP] * inv[:, :Q_GROUP],
                           ot[:, Q_GROUP:] * inv[:, Q_GROUP:])
            o_scr[w0:w0 + Q_GROUP, feat] = ot.T.astype(BF16)
    out_ref[0] = x_ref[0] + jnp.dot(o_scr[...], wo_ref[...], preferred_element_type=F32)


def _attn(x, qt, k, vt, bias_t, w_o):
    b, s, _ = x.shape
    cur = lambda bi, si: (bi, si, 0)
    prev = lambda bi, si: (bi, jnp.maximum(si - 1, 0), 0)
    cur_t = lambda bi, si: (bi, 0, si)
    prev_t = lambda bi, si: (bi, 0, jnp.maximum(si - 1, 0))
    blk = (1, SEQ_TILE, D)
    blk_t = (1, D, SEQ_TILE)
    return pl.pallas_call(
        _attn_kernel,
        name="attn",
        grid=(b, s // SEQ_TILE),
        in_specs=[
            pl.BlockSpec(blk_t, cur_t),
            pl.BlockSpec(blk, prev),
            pl.BlockSpec(blk, cur),
            pl.BlockSpec(blk_t, prev_t),
            pl.BlockSpec(blk_t, cur_t),
            pl.BlockSpec((B_HEADS // 2, G_BAND, 2 * Q_GROUP), lambda bi, si: (0, 0, 0),
                         pipeline_mode=pl.Buffered(1)),
            pl.BlockSpec(blk, cur),
            pl.BlockSpec((D, D), lambda bi, si: (0, 0), pipeline_mode=pl.Buffered(1)),
        ],
        out_specs=pl.BlockSpec(blk, cur),
        out_shape=jax.ShapeDtypeStruct(x.shape, F32),
        scratch_shapes=[pltpu.VMEM((SEQ_TILE, D), BF16)],
        compiler_params=_params(2),
    )(qt, k, k, vt, vt, bias_t, x, w_o.astype(BF16))


def _group_bias(table):
    band = _band_bias(table)
    pad = lambda lo, hi: jnp.pad(band, ((0, 0), (0, 0), (lo, hi)), constant_values=NEG_INF)
    both = jnp.concatenate([pad(0, CHUNK), pad(CHUNK, 0)], axis=1)
    both = both.reshape(B_HEADS // 2, 2, Q_GROUP, G_BAND)
    return both.transpose(0, 3, 1, 2).reshape(B_HEADS // 2, G_BAND, 2 * Q_GROUP)


def _band_bias(table):
    n_rel = REL_MAX - REL_MIN + 1
    span = BAND + CHUNK - 1
    head = jnp.broadcast_to(table[:, n_rel - 1:], (table.shape[0], span - n_rel))
    ext = jnp.concatenate([head, table[:, ::-1]], axis=1)
    rows = [ext[:, CHUNK - 1 - q:CHUNK - 1 - q + BAND] for q in range(CHUNK)]
    return jnp.stack(rows, axis=1)


def kernel(x, p, a_w_in, a_lb_logits, a_out_norm, a_w_o, kv_norm, w_kv, b_w_q, b_rel_bias, b_w_o,
           norm_mix, norm_ffn, norm_ple, moe_w_group, moe_b_group, moe_w_expert, moe_b_expert,
           moe_w_gate, moe_w_up, moe_w_down, ple_w_proj, ple_w_gate, final_norm):
    b, s, _ = x.shape
    lower_bounds = jnp.cumsum(jax.nn.softmax(a_lb_logits.astype(F32), axis=0), axis=0)

    def moe(xi, i, final):
        return _moe_ple(xi, p, i, norm_ffn[i], moe_w_group[i], moe_b_group[i], moe_w_expert[i],
                        moe_b_expert[i], moe_w_gate, moe_w_up, moe_w_down, norm_ple[i],
                        ple_w_proj[i], ple_w_gate[i], final_norm, final)

    x = _mixer_a(x, norm_mix[0], a_w_in[0], lower_bounds[0], a_out_norm[0], a_w_o[0])
    x = moe(x, 0, False)

    qt, k, vt = _qkv(x, norm_mix[1], kv_norm, b_w_q[0], w_kv)
    x = _attn(x, qt, k, vt, _group_bias(b_rel_bias[0].astype(F32)), b_w_o[0])
    x = moe(x, 1, True)
    return x
```

```python
import functools

import jax
import jax.numpy as jnp
from jax import lax
from jax.experimental import pallas as pl
from jax.experimental.pallas import tpu as pltpu

F32 = jnp.float32
BF16 = jnp.bfloat16

D = 1024
CHUNK = 64
A_HEADS = 8
A_HEAD_DIM = 128
B_HEADS = 16
B_HEAD_DIM = 64
LEFT_CHUNKS = 8
BAND = (LEFT_CHUNKS + 1) * CHUNK
REL_MIN = -(CHUNK - 1)
REL_MAX = 256
ATTN_SCALE = B_HEAD_DIM ** -0.5
N_GROUPS = 4
EXPERTS_PER_GROUP = 8
N_EXPERTS = 32
TOP_K = 2
D_EXPERT = 512
MOE_BLOCK = 256
D_PLE = 256
EPS = 1e-6
NEG_INF = -1e30

SEQ_TILE = 512
ROUTER_TILE = 512
COMBINE_TILE = 512
ROUTER_ROWS = 40
VMEM_LIMIT = 56 * 1024 * 1024


def _params(n_axes, vmem=VMEM_LIMIT):
    return pltpu.CompilerParams(dimension_semantics=("arbitrary",) * n_axes,
                                vmem_limit_bytes=vmem)


def _rms_scale(x):
    return lax.rsqrt(jnp.mean(x * x, axis=-1, keepdims=True) + EPS)


def _sigmoid(x):
    return 1.0 / (1.0 + jnp.exp(-x))


def _mixer_a_kernel(x_ref, g_ref, win_ref, lb_ref, onorm_ref, wo_ref, out_ref,
                    proj_scr, o_scr, state_scr):
    @pl.when(pl.program_id(1) == 0)
    def _():
        state_scr[...] = jnp.zeros_like(state_scr)

    x = x_ref[0]
    h = (x * _rms_scale(x) * g_ref[...]).astype(BF16)
    proj_scr[...] = jnp.dot(h, win_ref[...], preferred_element_type=F32)

    row = lax.broadcasted_iota(jnp.int32, (CHUNK, CHUNK), 0)
    col = lax.broadcasted_iota(jnp.int32, (CHUNK, CHUNK), 1)
    causal = row >= col
    tril = causal.astype(BF16)
    lb = lb_ref[...]
    onorm = onorm_ref[...]

    def chunk_body(c, carry):
        r0 = pl.multiple_of(c * CHUNK, CHUNK)
        rows = pl.ds(r0, CHUNK)
        f = lb + (1.0 - lb) * _sigmoid(proj_scr[rows, D:2 * D])
        logf = jnp.log(f)
        hi = logf.astype(BF16)
        lo = (logf - hi.astype(F32)).astype(BF16)
        g = (jnp.dot(tril, hi, preferred_element_type=F32)
             + jnp.dot(tril, lo, preferred_element_type=F32))
        for hd in range(A_HEADS):
            sl = slice(hd * A_HEAD_DIM, (hd + 1) * A_HEAD_DIM)
            gh = g[:, sl]
            g_last = gh[CHUNK - 1:CHUNK, :]
            k = 1.0 - f[:, sl]
            q_dec = (proj_scr[rows, sl] * jnp.exp(gh)).astype(BF16)
            k_inv = (k * jnp.exp(-gh)).astype(BF16)
            k_tail = (k * jnp.exp(g_last - gh)).astype(BF16)
            v = proj_scr[rows, 2 * D + hd * A_HEAD_DIM:2 * D + (hd + 1) * A_HEAD_DIM]
            v_b = v.astype(BF16)
            att = lax.dot_general(q_dec, k_inv, (((1,), (1,)), ((), ())),
                                  preferred_element_type=F32)
            att = jnp.where(causal, att, 0.0).astype(BF16)
            st = state_scr[hd]
            o = (jnp.dot(att, v_b, preferred_element_type=F32)
                 + lax.dot_general(q_dec, st.astype(BF16), (((1,), (1,)), ((), ())),
                                   preferred_element_type=F32))
            v_t = v.T.astype(BF16)
            state_scr[hd] = st * jnp.exp(g_last) + jnp.dot(v_t, k_tail,
                                                           preferred_element_type=F32)
            o = o * _rms_scale(o)
            og = proj_scr[rows, 3 * D + hd * A_HEAD_DIM:3 * D + (hd + 1) * A_HEAD_DIM]
            o = o * onorm[:, sl] * (og * _sigmoid(og))
            o_scr[rows, sl] = o.astype(BF16)
        return carry

    lax.fori_loop(0, SEQ_TILE // CHUNK, chunk_body, 0, unroll=True)
    out_ref[0] = x + jnp.dot(o_scr[...], wo_ref[...], preferred_element_type=F32)


def _mixer_a(x, g, w_in, lb, out_norm, w_o):
    b, s, _ = x.shape
    const = lambda bi, si: (0, 0)
    return pl.pallas_call(
        _mixer_a_kernel,
        name="mixer_a",
        grid=(b, s // SEQ_TILE),
        in_specs=[
            pl.BlockSpec((1, SEQ_TILE, D), lambda bi, si: (bi, si, 0)),
            pl.BlockSpec((1, D), const),
            pl.BlockSpec((D, 4 * D), const, pipeline_mode=pl.Buffered(1)),
            pl.BlockSpec((1, D), const),
            pl.BlockSpec((1, D), const),
            pl.BlockSpec((D, D), const, pipeline_mode=pl.Buffered(1)),
        ],
        out_specs=pl.BlockSpec((1, SEQ_TILE, D), lambda bi, si: (bi, si, 0)),
        out_shape=jax.ShapeDtypeStruct(x.shape, F32),
        scratch_shapes=[
            pltpu.VMEM((SEQ_TILE, 4 * D), F32),
            pltpu.VMEM((SEQ_TILE, D), BF16),
            pltpu.VMEM((A_HEADS, A_HEAD_DIM, A_HEAD_DIM), F32),
        ],
        compiler_params=_params(2),
    )(x, g.reshape(1, D), w_in.astype(BF16), lb.reshape(1, D), out_norm.reshape(1, D),
      w_o.astype(BF16))


def _router_kernel(x_ref, g_ref, wr_ref, br_ref, ids_ref, gates_ref, counts_ref, cnt_scr):
    @pl.when(pl.program_id(0) == 0)
    def _():
        cnt_scr[...] = jnp.zeros_like(cnt_scr)

    tm = ROUTER_TILE
    x = x_ref[...]
    h = x * _rms_scale(x) * g_ref[...]
    logits = lax.dot_general(wr_ref[...], h, (((1,), (1,)), ((), ())),
                             precision=lax.Precision.HIGHEST,
                             preferred_element_type=F32) + br_ref[...]
    el = logits[0:N_EXPERTS]
    gl = logits[N_EXPERTS:ROUTER_ROWS]
    grow = lax.broadcasted_iota(jnp.int32, gl.shape, 0)
    gl = jnp.where(grow < N_GROUPS, gl, -jnp.inf)
    gmax = jnp.max(gl, axis=0, keepdims=True)
    gsum = jnp.sum(jnp.exp(gl - gmax), axis=0, keepdims=True)
    grp_w = 1.0 / gsum
    gidx = jnp.min(jnp.where(gl == gmax, grow, N_GROUPS), axis=0, keepdims=True)

    erow = lax.broadcasted_iota(jnp.int32, el.shape, 0)
    masked = jnp.where((erow // EXPERTS_PER_GROUP) == gidx, el, -jnp.inf)
    top1 = jnp.max(masked, axis=0, keepdims=True)
    i1 = jnp.min(jnp.where(masked == top1, erow, N_EXPERTS), axis=0, keepdims=True)
    masked2 = jnp.where(erow == i1, -jnp.inf, masked)
    top2 = jnp.max(masked2, axis=0, keepdims=True)
    i2 = jnp.min(jnp.where(masked2 == top2, erow, N_EXPERTS), axis=0, keepdims=True)
    e2 = jnp.exp(top2 - top1)
    denom = 1.0 + e2
    g1 = grp_w * (1.0 / denom)
    g2 = grp_w * (e2 / denom)

    sel1 = erow == i1
    sel2 = erow == i2
    onehot = (sel1 | sel2).astype(BF16)
    tr = lax.broadcasted_iota(jnp.int32, (tm, tm), 0)
    tc = lax.broadcasted_iota(jnp.int32, (tm, tm), 1)
    before = (tr < tc).astype(BF16)
    prefix = jnp.dot(onehot, before, preferred_element_type=F32) + cnt_scr[...]
    r1 = jnp.sum(jnp.where(sel1, prefix, 0.0), axis=0, keepdims=True)
    r2 = jnp.sum(jnp.where(sel2, prefix, 0.0), axis=0, keepdims=True)
    cnt_scr[...] += jnp.sum(onehot.astype(F32), axis=1, keepdims=True)

    zi = jnp.zeros((4, tm), jnp.int32)
    ids_ref[0] = jnp.concatenate(
        [i1, i2, r1.astype(jnp.int32), r2.astype(jnp.int32), zi], axis=0)
    gates_ref[0] = jnp.concatenate([g1, g2, jnp.zeros((6, tm), F32)], axis=0)
    counts_ref[...] = jnp.broadcast_to(cnt_scr[...], counts_ref.shape).astype(jnp.int32)


def _router(x2d, g, w_group, b_group, w_expert, b_expert):
    t = x2d.shape[0]
    nt = t // ROUTER_TILE
    pad = ROUTER_ROWS - N_EXPERTS - N_GROUPS
    wr = jnp.concatenate([w_expert.T, w_group.T, jnp.zeros((pad, D), F32)], axis=0)
    br = jnp.concatenate([b_expert, b_group, jnp.zeros((pad,), F32)]).reshape(ROUTER_ROWS, 1)
    const = lambda i: (0, 0)
    return pl.pallas_call(
        _router_kernel,
        name="router",
        grid=(nt,),
        in_specs=[
            pl.BlockSpec((ROUTER_TILE, D), lambda i: (i, 0)),
            pl.BlockSpec((1, D), const),
            pl.BlockSpec((ROUTER_ROWS, D), const),
            pl.BlockSpec((ROUTER_ROWS, 1), const),
        ],
        out_specs=[
            pl.BlockSpec((1, 8, ROUTER_TILE), lambda i: (i, 0, 0)),
            pl.BlockSpec((1, 8, ROUTER_TILE), lambda i: (i, 0, 0)),
            pl.BlockSpec((N_EXPERTS, 128), const),
        ],
        out_shape=[
            jax.ShapeDtypeStruct((nt, 8, ROUTER_TILE), jnp.int32),
            jax.ShapeDtypeStruct((nt, 8, ROUTER_TILE), F32),
            jax.ShapeDtypeStruct((N_EXPERTS, 128), jnp.int32),
        ],
        scratch_shapes=[pltpu.VMEM((N_EXPERTS, 1), F32)],
        compiler_params=_params(1),
    )(x2d, g.reshape(1, D), wr, br)


SUBLANES = 8
SLOTS_PER_TILE = SUBLANES * TOP_K


def _issue_rows(n_rows, dest_ref, make_copy):
    def body(j, c):
        for u in range(SUBLANES):
            for k in range(TOP_K):
                slot = u * TOP_K + k
                make_copy(j, u, k, dest_ref[0, j, slot]).start(priority=slot % 2)
        return c

    lax.fori_loop(0, n_rows // SUBLANES, body, 0)


def _tile_rows(x):
    return x.reshape(x.shape[0] // SUBLANES, SUBLANES, x.shape[1])


def _wait_rows(buf_ref, sem):
    pltpu.make_async_copy(buf_ref, buf_ref, sem).wait()


def _dispatch_kernel(zrow_ref, dest_ref, x_ref, g_ref, xs_ref, hbuf, zbuf, zsem, sem):
    @pl.when(pl.program_id(0) == 0)
    def _():
        zbuf[...] = jnp.zeros_like(zbuf)

        def zcopy(e):
            return pltpu.make_async_copy(zbuf, xs_ref.at[pl.ds(zrow_ref[e], MOE_BLOCK)], zsem)

        def zstart(e, c):
            @pl.when(zrow_ref[e] >= 0)
            def _():
                zcopy(e).start()
            return c

        def zwait(e, c):
            @pl.when(zrow_ref[e] >= 0)
            def _():
                zcopy(e).wait()
            return c

        lax.fori_loop(0, 2 * N_EXPERTS, zstart, 0)
        lax.fori_loop(0, 2 * N_EXPERTS, zwait, 0)

    i = pl.program_id(0)
    slot = i % 2
    x = x_ref[...]
    hbuf[slot] = _tile_rows(x * _rms_scale(x) * g_ref[...])

    def row_copy(tile, sub, k, dest):
        return pltpu.make_async_copy(hbuf.at[slot, tile, pl.ds(sub, 1), :], xs_ref.at[dest],
                                     sem.at[slot])

    _issue_rows(ROUTER_TILE, dest_ref, row_copy)

    def drain(which):
        for _ in range(TOP_K):
            _wait_rows(hbuf.at[which], sem.at[which])

    @pl.when(i > 0)
    def _():
        drain(1 - slot)

    @pl.when(i == pl.num_programs(0) - 1)
    def _():
        drain(slot)


def _dispatch(x2d, g, dest, zrow, n_pad):
    t = x2d.shape[0]
    nt = t // ROUTER_TILE
    return pl.pallas_call(
        _dispatch_kernel,
        name="dispatch",
        grid_spec=pltpu.PrefetchScalarGridSpec(
            num_scalar_prefetch=1,
            grid=(nt,),
            in_specs=[
                pl.BlockSpec((1, ROUTER_TILE // SUBLANES, SLOTS_PER_TILE),
                             lambda i, z: (i, 0, 0), memory_space=pltpu.SMEM),
                pl.BlockSpec((ROUTER_TILE, D), lambda i, z: (i, 0)),
                pl.BlockSpec((1, D), lambda i, z: (0, 0)),
            ],
            out_specs=pl.BlockSpec(memory_space=pl.ANY),
            scratch_shapes=[
                pltpu.VMEM((2, ROUTER_TILE // SUBLANES, SUBLANES, D), F32),
                pltpu.VMEM((MOE_BLOCK, 1, D), F32),
                pltpu.SemaphoreType.DMA(()),
                pltpu.SemaphoreType.DMA((2,)),
            ],
        ),
        out_shape=jax.ShapeDtypeStruct((n_pad, 1, D), F32),
        compiler_params=_params(1),
    )(zrow, dest, x2d, g.reshape(1, D))


def _experts_kernel(be_ref, nu_ref, xs_ref, wg_ref, wu_ref, wd_ref, y_ref,
                    wg_b, wu_b, wd_b, x_scr):
    i = pl.program_id(0)
    prev = be_ref[jnp.maximum(i - 1, 0)]

    @pl.when((i == 0) | (be_ref[i] != prev))
    def _():
        wg_b[...] = wg_ref[0, 0].astype(BF16)
        wu_b[...] = wu_ref[0, 0].astype(BF16)
        wd_b[...] = wd_ref[0, 0].astype(BF16)

    @pl.when(i < nu_ref[0])
    def _():
        x_scr[...] = xs_ref[:, 0, :]
        h = x_scr[...].astype(BF16)
        a = jnp.dot(h, wg_b[...], preferred_element_type=F32)
        u = jnp.dot(h, wu_b[...], preferred_element_type=F32)
        hid = (a * _sigmoid(a) * u).astype(BF16)
        y_ref[:, 0, :] = jnp.dot(hid, wd_b[...], preferred_element_type=F32)

    @pl.when(i >= nu_ref[0])
    def _():
        y_ref[...] = jnp.zeros_like(y_ref)


def _experts(xs, block_e, n_used, layer, w_gate, w_up, w_down):
    n_pad = xs.shape[0]
    n_blocks = n_pad // MOE_BLOCK

    def blk(i, be, nu):
        return jnp.minimum(i, nu[0] - 1)

    return pl.pallas_call(
        _experts_kernel,
        name="experts",
        grid_spec=pltpu.PrefetchScalarGridSpec(
            num_scalar_prefetch=2,
            grid=(n_blocks,),
            in_specs=[
                pl.BlockSpec((MOE_BLOCK, 1, D), lambda i, be, nu: (blk(i, be, nu), 0, 0)),
                pl.BlockSpec((1, 1, D, D_EXPERT), lambda i, be, nu: (layer, be[i], 0, 0)),
                pl.BlockSpec((1, 1, D, D_EXPERT), lambda i, be, nu: (layer, be[i], 0, 0)),
                pl.BlockSpec((1, 1, D_EXPERT, D), lambda i, be, nu: (layer, be[i], 0, 0)),
            ],
            out_specs=pl.BlockSpec((MOE_BLOCK, 1, D), lambda i, be, nu: (i, 0, 0)),
            scratch_shapes=[
                pltpu.VMEM((D, D_EXPERT), BF16),
                pltpu.VMEM((D, D_EXPERT), BF16),
                pltpu.VMEM((D_EXPERT, D), BF16),
                pltpu.VMEM((MOE_BLOCK, D), F32),
            ],
        ),
        out_shape=jax.ShapeDtypeStruct(xs.shape, F32),
        compiler_params=_params(1),
    )(block_e, n_used, xs, w_gate, w_up, w_down)


def _combine_kernel(dest_ref, dest_next_ref, x_ref, gate_ref, y_ref, p_ref, gple_ref, wp_ref,
                    wg_ref, gfin_ref, out_ref, ybuf, sem, *, final):
    i = pl.program_id(0)
    slot = i % 2

    def gather(idx_ref, into):
        def row_copy(tile, sub, k, dest):
            return pltpu.make_async_copy(y_ref.at[dest], ybuf.at[into, k, tile, pl.ds(sub, 1), :],
                                         sem.at[into])
        _issue_rows(COMBINE_TILE, idx_ref, row_copy)

    @pl.when(i == 0)
    def _():
        gather(dest_ref, slot)

    @pl.when(i + 1 < pl.num_programs(0))
    def _():
        gather(dest_next_ref, 1 - slot)

    for k in range(TOP_K):
        _wait_rows(ybuf.at[slot, k], sem.at[slot])

    gates = gate_ref[...]
    y0 = ybuf[slot, 0].reshape(COMBINE_TILE, D)
    y1 = ybuf[slot, 1].reshape(COMBINE_TILE, D)
    x = x_ref[...] + gates[:, 0:1] * y0 + gates[:, 1:2] * y1
    h = (x * _rms_scale(x) * gple_ref[...]).astype(BF16)
    gate = _sigmoid(jnp.dot(h, wg_ref[...], preferred_element_type=F32))
    proj = jnp.dot(p_ref[0].astype(BF16), wp_ref[...], preferred_element_type=F32)
    x = x + proj * gate
    if final:
        x = x * _rms_scale(x) * gfin_ref[...]
    out_ref[...] = x


def _combine_ple(x2d, dest, gates, y, p3d, layer, g_ple, w_proj, w_gate, g_final, final):
    t = x2d.shape[0]
    nt = t // COMBINE_TILE
    dest_blk = (1, COMBINE_TILE // SUBLANES, SLOTS_PER_TILE)
    const = lambda i: (0, 0)
    return pl.pallas_call(
        functools.partial(_combine_kernel, final=final),
        name="combine_final" if final else "combine",
        grid=(nt,),
        in_specs=[
            pl.BlockSpec(dest_blk, lambda i: (i, 0, 0), memory_space=pltpu.SMEM),
            pl.BlockSpec(dest_blk, lambda i: (jnp.minimum(i + 1, nt - 1), 0, 0),
                         memory_space=pltpu.SMEM),
            pl.BlockSpec((COMBINE_TILE, D), lambda i: (i, 0)),
            pl.BlockSpec((COMBINE_TILE, TOP_K), lambda i: (i, 0)),
            pl.BlockSpec(memory_space=pl.ANY),
            pl.BlockSpec((1, COMBINE_TILE, D_PLE), lambda i: (layer, i, 0)),
            pl.BlockSpec((1, D), const),
            pl.BlockSpec((D_PLE, D), const, pipeline_mode=pl.Buffered(1)),
            pl.BlockSpec((D, D), const, pipeline_mode=pl.Buffered(1)),
            pl.BlockSpec((1, D), const),
        ],
        out_specs=pl.BlockSpec((COMBINE_TILE, D), lambda i: (i, 0)),
        out_shape=jax.ShapeDtypeStruct((t, D), F32),
        scratch_shapes=[
            pltpu.VMEM((2, TOP_K, COMBINE_TILE // SUBLANES, SUBLANES, D), F32),
            pltpu.SemaphoreType.DMA((2,)),
        ],
        compiler_params=_params(1),
    )(dest, dest, x2d, gates, y, p3d, g_ple.reshape(1, D), w_proj.astype(BF16),
      w_gate.astype(BF16), g_final.reshape(1, D))


def _moe_ple(x, p, layer, norm_ffn, w_group, b_group, w_expert, b_expert, w_gate, w_up, w_down,
             norm_ple, ple_w_proj, ple_w_gate, final_norm, final):
    b, s, _ = x.shape
    t = b * s
    x2d = x.reshape(t, D)
    ids, gates, counts = _router(x2d, norm_ffn, w_group, b_group, w_expert, b_expert)

    counts = counts[:, 0]
    padded = (counts + MOE_BLOCK - 1) // MOE_BLOCK * MOE_BLOCK
    pad_end = jnp.cumsum(padded)
    pad_start = pad_end - padded
    n_blocks = t * TOP_K // MOE_BLOCK + N_EXPERTS
    n_pad = n_blocks * MOE_BLOCK
    e = ids[:, 0:TOP_K, :].transpose(1, 0, 2).reshape(TOP_K, t)
    r = ids[:, TOP_K:2 * TOP_K, :].transpose(1, 0, 2).reshape(TOP_K, t)
    experts = jnp.arange(N_EXPERTS, dtype=jnp.int32)
    dest = r + jnp.sum(jnp.where(e[..., None] == experts, pad_start, 0), axis=-1)
    gate_cols = gates[:, 0:TOP_K, :].transpose(0, 2, 1).reshape(t, TOP_K)
    block_row = jnp.arange(n_blocks, dtype=jnp.int32) * MOE_BLOCK
    block_e = jnp.minimum(jnp.sum(pad_end[None, :] <= block_row[:, None], axis=-1),
                          N_EXPERTS - 1).astype(jnp.int32)
    n_used = (pad_end[-1:] // MOE_BLOCK).astype(jnp.int32)
    tail = (n_used[0] + jnp.arange(N_EXPERTS, dtype=jnp.int32)) * MOE_BLOCK
    zrow = jnp.concatenate([jnp.where(padded > 0, pad_end - MOE_BLOCK, -1),
                            jnp.where(tail < n_pad, tail, -1)]).astype(jnp.int32)

    def tiles(a, tile):
        return a.T.reshape(t // tile, tile // SUBLANES, SLOTS_PER_TILE).astype(jnp.int32)

    xs = _dispatch(x2d, norm_ffn, tiles(dest, ROUTER_TILE), zrow, n_pad)
    y = _experts(xs, block_e, n_used, layer, w_gate, w_up, w_down)
    out = _combine_ple(x2d, tiles(dest, COMBINE_TILE), gate_cols, y, p.reshape(-1, t, D_PLE), layer,
                       norm_ple, ple_w_proj, ple_w_gate, final_norm, final)
    return out.reshape(b, s, D)


def _qkv_kernel(x_ref, gq_ref, gkv_ref, wq_ref, wkv_ref, qt_ref, k_ref, vt_ref):
    x = x_ref[0]
    xn = x * _rms_scale(x)
    hq = (xn * gq_ref[...]).astype(BF16)
    hkv = (xn * gkv_ref[...]).astype(BF16)
    q = jnp.dot(hq, wq_ref[...], preferred_element_type=F32) * ATTN_SCALE
    kv = jnp.dot(hkv, wkv_ref[...], preferred_element_type=F32)
    qt_ref[0] = q.T.astype(BF16)
    k_ref[0] = kv[:, :D].astype(BF16)
    vt_ref[0] = kv[:, D:].T.astype(BF16)


def _qkv(x, g_q, g_kv, w_q, w_kv):
    b, s, _ = x.shape
    const = lambda bi, si: (0, 0)
    row_major = pl.BlockSpec((1, SEQ_TILE, D), lambda bi, si: (bi, si, 0))
    feat_major = pl.BlockSpec((1, D, SEQ_TILE), lambda bi, si: (bi, 0, si))
    return pl.pallas_call(
        _qkv_kernel,
        name="qkv",
        grid=(b, s // SEQ_TILE),
        in_specs=[
            row_major,
            pl.BlockSpec((1, D), const),
            pl.BlockSpec((1, D), const),
            pl.BlockSpec((D, D), const),
            pl.BlockSpec((D, 2 * D), const),
        ],
        out_specs=[feat_major, row_major, feat_major],
        out_shape=[jax.ShapeDtypeStruct((b, D, s), BF16),
                   jax.ShapeDtypeStruct((b, s, D), BF16),
                   jax.ShapeDtypeStruct((b, D, s), BF16)],
        compiler_params=_params(2),
    )(x, g_q.reshape(1, D), g_kv.reshape(1, D), w_q.astype(BF16), w_kv.astype(BF16))


Q_GROUP = 2 * CHUNK
G_BAND = BAND + CHUNK
PAIR = 2 * B_HEAD_DIM


def _attn_kernel(qt_ref, kp_ref, kc_ref, vtp_ref, vtc_ref, bias_ref, x_ref, wo_ref, out_ref, o_scr):
    si = pl.program_id(1)
    drow = lax.broadcasted_iota(jnp.int32, (PAIR, Q_GROUP), 0)
    first_head = drow < B_HEAD_DIM
    kidx = lax.broadcasted_iota(jnp.int32, (G_BAND, 1), 0)
    for g in range(SEQ_TILE // Q_GROUP):
        w0 = g * Q_GROUP
        n_prev = SEQ_TILE - w0
        n_cur = G_BAND - n_prev
        neg = jnp.where((si - 1) * SEQ_TILE + w0 + kidx >= 0, 0.0, NEG_INF)
        for pr in range(B_HEADS // 2):
            feat = slice(pr * PAIR, (pr + 1) * PAIR)
            qt = qt_ref[0, feat, w0:w0 + Q_GROUP]
            zero = jnp.zeros_like(qt)
            qblk = jnp.concatenate([jnp.where(first_head, qt, zero),
                                    jnp.where(first_head, zero, qt)], axis=1)
            kb = jnp.concatenate([kp_ref[0, w0:, feat], kc_ref[0, :n_cur, feat]], axis=0)
            s = jnp.dot(kb, qblk, preferred_element_type=F32) + bias_ref[pr] + neg
            m = jnp.max(s, axis=0, keepdims=True)
            p = jnp.exp(s - m)
            inv = 1.0 / jnp.sum(p, axis=0, keepdims=True)
            vt = jnp.concatenate([vtp_ref[0, feat, w0:], vtc_ref[0, feat, :n_cur]], axis=1)
            ot = jnp.dot(vt, p.astype(BF16), preferred_element_type=F32)
            ot = jnp.where(first_head, ot[:, :Q_GROUP] * inv[:, :Q_GROUP],
                           ot[:, Q_GROUP:] * inv[:, Q_GROUP:])
            o_scr[w0:w0 + Q_GROUP, feat] = ot.T.astype(BF16)
    out_ref[0] = x_ref[0] + jnp.dot(o_scr[...], wo_ref[...], preferred_element_type=F32)


def _attn(x, qt, k, vt, bias_t, w_o):
    b, s, _ = x.shape
    cur = lambda bi, si: (bi, si, 0)
    prev = lambda bi, si: (bi, jnp.maximum(si - 1, 0), 0)
    cur_t = lambda bi, si: (bi, 0, si)
    prev_t = lambda bi, si: (bi, 0, jnp.maximum(si - 1, 0))
    blk = (1, SEQ_TILE, D)
    blk_t = (1, D, SEQ_TILE)
    return pl.pallas_call(
        _attn_kernel,
        name="attn",
        grid=(b, s // SEQ_TILE),
        in_specs=[
            pl.BlockSpec(blk_t, cur_t),
            pl.BlockSpec(blk, prev),
            pl.BlockSpec(blk, cur),
            pl.BlockSpec(blk_t, prev_t),
            pl.BlockSpec(blk_t, cur_t),
            pl.BlockSpec((B_HEADS // 2, G_BAND, 2 * Q_GROUP), lambda bi, si: (0, 0, 0),
                         pipeline_mode=pl.Buffered(1)),
            pl.BlockSpec(blk, cur),
            pl.BlockSpec((D, D), lambda bi, si: (0, 0), pipeline_mode=pl.Buffered(1)),
        ],
        out_specs=pl.BlockSpec(blk, cur),
        out_shape=jax.ShapeDtypeStruct(x.shape, F32),
        scratch_shapes=[pltpu.VMEM((SEQ_TILE, D), BF16)],
        compiler_params=_params(2),
    )(qt, k, k, vt, vt, bias_t, x, w_o.astype(BF16))


def _group_bias(table):
    band = _band_bias(table)
    pad = lambda lo, hi: jnp.pad(band, ((0, 0), (0, 0), (lo, hi)), constant_values=NEG_INF)
    both = jnp.concatenate([pad(0, CHUNK), pad(CHUNK, 0)], axis=1)
    both = both.reshape(B_HEADS // 2, 2, Q_GROUP, G_BAND)
    return both.transpose(0, 3, 1, 2).reshape(B_HEADS // 2, G_BAND, 2 * Q_GROUP)


def _band_bias(table):
    n_rel = REL_MAX - REL_MIN + 1
    span = BAND + CHUNK - 1
    head = jnp.broadcast_to(table[:, n_rel - 1:], (table.shape[0], span - n_rel))
    ext = jnp.concatenate([head, table[:, ::-1]], axis=1)
    rows = [ext[:, CHUNK - 1 - q:CHUNK - 1 - q + BAND] for q in range(CHUNK)]
    return jnp.stack(rows, axis=1)


def kernel(x, p, a_w_in, a_lb_logits, a_out_norm, a_w_o, kv_norm, w_kv, b_w_q, b_rel_bias, b_w_o,
           norm_mix, norm_ffn, norm_ple, moe_w_group, moe_b_group, moe_w_expert, moe_b_expert,
           moe_w_gate, moe_w_up, moe_w_down, ple_w_proj, ple_w_gate, final_norm):
    b, s, _ = x.shape
    lower_bounds = jnp.cumsum(jax.nn.softmax(a_lb_logits.astype(F32), axis=0), axis=0)

    def moe(xi, i, final):
        return _moe_ple(xi, p, i, norm_ffn[i], moe_w_group[i], moe_b_group[i], moe_w_expert[i],
                        moe_b_expert[i], moe_w_gate, moe_w_up, moe_w_down, norm_ple[i],
                        ple_w_proj[i], ple_w_gate[i], final_norm, final)

    x = _mixer_a(x, norm_mix[0], a_w_in[0], lower_bounds[0], a_out_norm[0], a_w_o[0])
    x = moe(x, 0, False)

    qt, k, vt = _qkv(x, norm_mix[1], kv_norm, b_w_q[0], w_kv)
    x = _attn(x, qt, k, vt, _group_bias(b_rel_bias[0].astype(F32)), b_w_o[0])
    x = moe(x, 1, True)
    return x
```

```python
import functools

import jax
import jax.numpy as jnp
from jax import lax
from jax.experimental import pallas as pl
from jax.experimental.pallas import tpu as pltpu

F32 = jnp.float32
BF16 = jnp.bfloat16

D = 1024
CHUNK = 64
A_HEADS = 8
A_HEAD_DIM = 128
B_HEADS = 16
B_HEAD_DIM = 64
LEFT_CHUNKS = 8
BAND = (LEFT_CHUNKS + 1) * CHUNK
REL_MIN = -(CHUNK - 1)
REL_MAX = 256
ATTN_SCALE = B_HEAD_DIM ** -0.5
N_GROUPS = 4
EXPERTS_PER_GROUP = 8
N_EXPERTS = 32
TOP_K = 2
D_EXPERT = 512
MOE_BLOCK = 256
D_PLE = 256
EPS = 1e-6
NEG_INF = -1e30

SEQ_TILE = 512
ROUTER_TILE = 512
COMBINE_TILE = 512
ROUTER_ROWS = 40
VMEM_LIMIT = 56 * 1024 * 1024


def _params(n_axes, vmem=VMEM_LIMIT):
    return pltpu.CompilerParams(dimension_semantics=("arbitrary",) * n_axes,
                                vmem_limit_bytes=vmem)


def _rms_scale(x):
    return lax.rsqrt(jnp.mean(x * x, axis=-1, keepdims=True) + EPS)


def _sigmoid(x):
    return 1.0 / (1.0 + jnp.exp(-x))


def _mixer_a_kernel(x_ref, g_ref, win_ref, lb_ref, onorm_ref, wo_ref, out_ref,
                    proj_scr, o_scr, state_scr):
    @pl.when(pl.program_id(1) == 0)
    def _():
        state_scr[...] = jnp.zeros_like(state_scr)

    x = x_ref[0]
    h = (x * _rms_scale(x) * g_ref[...]).astype(BF16)
    proj_scr[...] = jnp.dot(h, win_ref[...], preferred_element_type=F32)

    row = lax.broadcasted_iota(jnp.int32, (CHUNK, CHUNK), 0)
    col = lax.broadcasted_iota(jnp.int32, (CHUNK, CHUNK), 1)
    causal = row >= col
    tril = causal.astype(BF16)
    lb = lb_ref[...]
    onorm = onorm_ref[...]

    def chunk_body(c, carry):
        r0 = pl.multiple_of(c * CHUNK, CHUNK)
        rows = pl.ds(r0, CHUNK)
        f = lb + (1.0 - lb) * _sigmoid(proj_scr[rows, D:2 * D])
        logf = jnp.log(f)
        hi = logf.astype(BF16)
        lo = (logf - hi.astype(F32)).astype(BF16)
        g = (jnp.dot(tril, hi, preferred_element_type=F32)
             + jnp.dot(tril, lo, preferred_element_type=F32))
        for hd in range(A_HEADS):
            sl = slice(hd * A_HEAD_DIM, (hd + 1) * A_HEAD_DIM)
            gh = g[:, sl]
            g_last = gh[CHUNK - 1:CHUNK, :]
            k = 1.0 - f[:, sl]
            q_dec = (proj_scr[rows, sl] * jnp.exp(gh)).astype(BF16)
            k_inv = (k * jnp.exp(-gh)).astype(BF16)
            k_tail = (k * jnp.exp(g_last - gh)).astype(BF16)
            v = proj_scr[rows, 2 * D + hd * A_HEAD_DIM:2 * D + (hd + 1) * A_HEAD_DIM]
            v_b = v.astype(BF16)
            att = lax.dot_general(q_dec, k_inv, (((1,), (1,)), ((), ())),
                                  preferred_element_type=F32)
            att = jnp.where(causal, att, 0.0).astype(BF16)
            st = state_scr[hd]
            o = (jnp.dot(att, v_b, preferred_element_type=F32)
                 + lax.dot_general(q_dec, st.astype(BF16), (((1,), (1,)), ((), ())),
                                   preferred_element_type=F32))
            v_t = v.T.astype(BF16)
            state_scr[hd] = st * jnp.exp(g_last) + jnp.dot(v_t, k_tail,
                                                           preferred_element_type=F32)
            o = o * _rms_scale(o)
            og = proj_scr[rows, 3 * D + hd * A_HEAD_DIM:3 * D + (hd + 1) * A_HEAD_DIM]
            o = o * onorm[:, sl] * (og * _sigmoid(og))
            o_scr[rows, sl] = o.astype(BF16)
        return carry

    lax.fori_loop(0, SEQ_TILE // CHUNK, chunk_body, 0, unroll=True)
    out_ref[0] = x + jnp.dot(o_scr[...], wo_ref[...], preferred_element_type=F32)


def _mixer_a(x, g, w_in, lb, out_norm, w_o):
    b, s, _ = x.shape
    const = lambda bi, si: (0, 0)
    return pl.pallas_call(
        _mixer_a_kernel,
        name="mixer_a",
        grid=(b, s // SEQ_TILE),
        in_specs=[
            pl.BlockSpec((1, SEQ_TILE, D), lambda bi, si: (bi, si, 0)),
            pl.BlockSpec((1, D), const),
            pl.BlockSpec((D, 4 * D), const, pipeline_mode=pl.Buffered(1)),
            pl.BlockSpec((1, D), const),
            pl.BlockSpec((1, D), const),
            pl.BlockSpec((D, D), const, pipeline_mode=pl.Buffered(1)),
        ],
        out_specs=pl.BlockSpec((1, SEQ_TILE, D), lambda bi, si: (bi, si, 0)),
        out_shape=jax.ShapeDtypeStruct(x.shape, F32),
        scratch_shapes=[
            pltpu.VMEM((SEQ_TILE, 4 * D), F32),
            pltpu.VMEM((SEQ_TILE, D), BF16),
            pltpu.VMEM((A_HEADS, A_HEAD_DIM, A_HEAD_DIM), F32),
        ],
        compiler_params=_params(2),
    )(x, g.reshape(1, D), w_in.astype(BF16), lb.reshape(1, D), out_norm.reshape(1, D),
      w_o.astype(BF16))


def _router_kernel(x_ref, g_ref, wr_ref, br_ref, ids_ref, gates_ref, counts_ref, cnt_scr):
    @pl.when(pl.program_id(0) == 0)
    def _():
        cnt_scr[...] = jnp.zeros_like(cnt_scr)

    tm = ROUTER_TILE
    x = x_ref[...]
    h = x * _rms_scale(x) * g_ref[...]
    logits = lax.dot_general(wr_ref[...], h, (((1,), (1,)), ((), ())),
                             precision=lax.Precision.HIGHEST,
                             preferred_element_type=F32) + br_ref[...]
    el = logits[0:N_EXPERTS]
    gl = logits[N_EXPERTS:ROUTER_ROWS]
    grow = lax.broadcasted_iota(jnp.int32, gl.shape, 0)
    gl = jnp.where(grow < N_GROUPS, gl, -jnp.inf)
    gmax = jnp.max(gl, axis=0, keepdims=True)
    gsum = jnp.sum(jnp.exp(gl - gmax), axis=0, keepdims=True)
    grp_w = 1.0 / gsum
    gidx = jnp.min(jnp.where(gl == gmax, grow, N_GROUPS), axis=0, keepdims=True)

    erow = lax.broadcasted_iota(jnp.int32, el.shape, 0)
    masked = jnp.where((erow // EXPERTS_PER_GROUP) == gidx, el, -jnp.inf)
    top1 = jnp.max(masked, axis=0, keepdims=True)
    i1 = jnp.min(jnp.where(masked == top1, erow, N_EXPERTS), axis=0, keepdims=True)
    masked2 = jnp.where(erow == i1, -jnp.inf, masked)
    top2 = jnp.max(masked2, axis=0, keepdims=True)
    i2 = jnp.min(jnp.where(masked2 == top2, erow, N_EXPERTS), axis=0, keepdims=True)
    e2 = jnp.exp(top2 - top1)
    denom = 1.0 + e2
    g1 = grp_w * (1.0 / denom)
    g2 = grp_w * (e2 / denom)

    sel1 = erow == i1
    sel2 = erow == i2
    onehot = (sel1 | sel2).astype(BF16)
    tr = lax.broadcasted_iota(jnp.int32, (tm, tm), 0)
    tc = lax.broadcasted_iota(jnp.int32, (tm, tm), 1)
    before = (tr < tc).astype(BF16)
    prefix = jnp.dot(onehot, before, preferred_element_type=F32) + cnt_scr[...]
    r1 = jnp.sum(jnp.where(sel1, prefix, 0.0), axis=0, keepdims=True)
    r2 = jnp.sum(jnp.where(sel2, prefix, 0.0), axis=0, keepdims=True)
    cnt_scr[...] += jnp.sum(onehot.astype(F32), axis=1, keepdims=True)

    zi = jnp.zeros((4, tm), jnp.int32)
    ids_ref[0] = jnp.concatenate(
        [i1, i2, r1.astype(jnp.int32), r2.astype(jnp.int32), zi], axis=0)
    gates_ref[0] = jnp.concatenate([g1, g2, jnp.zeros((6, tm), F32)], axis=0)
    counts_ref[...] = jnp.broadcast_to(cnt_scr[...], counts_ref.shape).astype(jnp.int32)


def _router(x2d, g, w_group, b_group, w_expert, b_expert):
    t = x2d.shape[0]
    nt = t // ROUTER_TILE
    pad = ROUTER_ROWS - N_EXPERTS - N_GROUPS
    wr = jnp.concatenate([w_expert.T, w_group.T, jnp.zeros((pad, D), F32)], axis=0)
    br = jnp.concatenate([b_expert, b_group, jnp.zeros((pad,), F32)]).reshape(ROUTER_ROWS, 1)
    const = lambda i: (0, 0)
    return pl.pallas_call(
        _router_kernel,
        name="router",
        grid=(nt,),
        in_specs=[
            pl.BlockSpec((ROUTER_TILE, D), lambda i: (i, 0)),
            pl.BlockSpec((1, D), const),
            pl.BlockSpec((ROUTER_ROWS, D), const),
            pl.BlockSpec((ROUTER_ROWS, 1), const),
        ],
        out_specs=[
            pl.BlockSpec((1, 8, ROUTER_TILE), lambda i: (i, 0, 0)),
            pl.BlockSpec((1, 8, ROUTER_TILE), lambda i: (i, 0, 0)),
            pl.BlockSpec((N_EXPERTS, 128), const),
        ],
        out_shape=[
            jax.ShapeDtypeStruct((nt, 8, ROUTER_TILE), jnp.int32),
            jax.ShapeDtypeStruct((nt, 8, ROUTER_TILE), F32),
            jax.ShapeDtypeStruct((N_EXPERTS, 128), jnp.int32),
        ],
        scratch_shapes=[pltpu.VMEM((N_EXPERTS, 1), F32)],
        compiler_params=_params(1),
    )(x2d, g.reshape(1, D), wr, br)


SUBLANES = 8
SLOTS_PER_TILE = SUBLANES * TOP_K


def _issue_rows(n_rows, dest_ref, make_copy):
    def body(j, c):
        for u in range(SUBLANES):
            for k in range(TOP_K):
                slot = u * TOP_K + k
                make_copy(j, u, k, dest_ref[0, j, slot]).start(priority=slot % 2)
        return c

    lax.fori_loop(0, n_rows // SUBLANES, body, 0)


def _tile_rows(x):
    return x.reshape(x.shape[0] // SUBLANES, SUBLANES, x.shape[1])


def _wait_rows(buf_ref, sem):
    pltpu.make_async_copy(buf_ref, buf_ref, sem).wait()


def _dispatch_kernel(zrow_ref, dest_ref, x_ref, g_ref, xs_ref, hbuf, zbuf, zsem, sem):
    @pl.when(pl.program_id(0) == 0)
    def _():
        zbuf[...] = jnp.zeros_like(zbuf)

        def zcopy(e):
            return pltpu.make_async_copy(zbuf, xs_ref.at[pl.ds(zrow_ref[e], MOE_BLOCK)], zsem)

        def zstart(e, c):
            @pl.when(zrow_ref[e] >= 0)
            def _():
                zcopy(e).start()
            return c

        def zwait(e, c):
            @pl.when(zrow_ref[e] >= 0)
            def _():
                zcopy(e).wait()
            return c

        lax.fori_loop(0, 2 * N_EXPERTS, zstart, 0)
        lax.fori_loop(0, 2 * N_EXPERTS, zwait, 0)

    i = pl.program_id(0)
    slot = i % 2
    x = x_ref[...]
    hbuf[slot] = _tile_rows(x * _rms_scale(x) * g_ref[...])

    def row_copy(tile, sub, k, dest):
        return pltpu.make_async_copy(hbuf.at[slot, tile, pl.ds(sub, 1), :], xs_ref.at[dest],
                                     sem.at[slot])

    _issue_rows(ROUTER_TILE, dest_ref, row_copy)

    def drain(which):
        for _ in range(TOP_K):
            _wait_rows(hbuf.at[which], sem.at[which])

    @pl.when(i > 0)
    def _():
        drain(1 - slot)

    @pl.when(i == pl.num_programs(0) - 1)
    def _():
        drain(slot)


def _dispatch(x2d, g, dest, zrow, n_pad):
    t = x2d.shape[0]
    nt = t // ROUTER_TILE
    return pl.pallas_call(
        _dispatch_kernel,
        name="dispatch",
        grid_spec=pltpu.PrefetchScalarGridSpec(
            num_scalar_prefetch=1,
            grid=(nt,),
            in_specs=[
                pl.BlockSpec((1, ROUTER_TILE // SUBLANES, SLOTS_PER_TILE),
                             lambda i, z: (i, 0, 0), memory_space=pltpu.SMEM),
                pl.BlockSpec((ROUTER_TILE, D), lambda i, z: (i, 0)),
                pl.BlockSpec((1, D), lambda i, z: (0, 0)),
            ],
            out_specs=pl.BlockSpec(memory_space=pl.ANY),
            scratch_shapes=[
                pltpu.VMEM((2, ROUTER_TILE // SUBLANES, SUBLANES, D), F32),
                pltpu.VMEM((MOE_BLOCK, 1, D), F32),
                pltpu.SemaphoreType.DMA(()),
                pltpu.SemaphoreType.DMA((2,)),
            ],
        ),
        out_shape=jax.ShapeDtypeStruct((n_pad, 1, D), F32),
        compiler_params=_params(1),
    )(zrow, dest, x2d, g.reshape(1, D))


def _experts_kernel(be_ref, nu_ref, xs_ref, wg_ref, wu_ref, wd_ref, y_ref,
                    wg_b, wu_b, wd_b, xbuf, ybuf, in_sem, out_sem):
    i = pl.program_id(0)
    n_used = nu_ref[0]
    slot = i % 2

    def block_rows(ref, blk):
        return ref.at[pl.ds(pl.multiple_of(blk * MOE_BLOCK, MOE_BLOCK), MOE_BLOCK), 0]

    def fetch(blk, into):
        return pltpu.make_async_copy(block_rows(xs_ref, blk), xbuf.at[into], in_sem.at[into])

    def write_back(blk, from_):
        return pltpu.make_async_copy(ybuf.at[from_], block_rows(y_ref, blk), out_sem.at[from_])

    @pl.when(i == 0)
    def _():
        fetch(0, 0).start()

    @pl.when(i + 1 < n_used)
    def _():
        fetch(i + 1, 1 - slot).start()

    prev = be_ref[jnp.maximum(i - 1, 0)]

    @pl.when((i == 0) | (be_ref[i] != prev))
    def _():
        wg_b[...] = wg_ref[0, 0].astype(BF16)
        wu_b[...] = wu_ref[0, 0].astype(BF16)
        wd_b[...] = wd_ref[0, 0].astype(BF16)

    @pl.when(i >= 2)
    def _():
        write_back(i - 2, slot).wait()

    @pl.when(i < n_used)
    def _():
        fetch(i, slot).wait()
        h = xbuf[slot].astype(BF16)
        a = jnp.dot(h, wg_b[...], preferred_element_type=F32)
        u = jnp.dot(h, wu_b[...], preferred_element_type=F32)
        hid = (a * _sigmoid(a) * u).astype(BF16)
        ybuf[slot] = jnp.dot(hid, wd_b[...], preferred_element_type=F32)

    @pl.when(i >= n_used)
    def _():
        ybuf[slot] = jnp.zeros((MOE_BLOCK, D), F32)

    write_back(i, slot).start()

    @pl.when(i == pl.num_programs(0) - 1)
    def _():
        write_back(i - 1, 1 - slot).wait()
        write_back(i, slot).wait()


def _experts(xs, block_e, n_used, layer, w_gate, w_up, w_down):
    n_pad = xs.shape[0]
    n_blocks = n_pad // MOE_BLOCK
    return pl.pallas_call(
        _experts_kernel,
        name="experts",
        grid_spec=pltpu.PrefetchScalarGridSpec(
            num_scalar_prefetch=2,
            grid=(n_blocks,),
            in_specs=[
                pl.BlockSpec(memory_space=pl.ANY),
                pl.BlockSpec((1, 1, D, D_EXPERT), lambda i, be, nu: (layer, be[i], 0, 0)),
                pl.BlockSpec((1, 1, D, D_EXPERT), lambda i, be, nu: (layer, be[i], 0, 0)),
                pl.BlockSpec((1, 1, D_EXPERT, D), lambda i, be, nu: (layer, be[i], 0, 0)),
            ],
            out_specs=pl.BlockSpec(memory_space=pl.ANY),
            scratch_shapes=[
                pltpu.VMEM((D, D_EXPERT), BF16),
                pltpu.VMEM((D, D_EXPERT), BF16),
                pltpu.VMEM((D_EXPERT, D), BF16),
                pltpu.VMEM((2, MOE_BLOCK, D), F32),
                pltpu.VMEM((2, MOE_BLOCK, D), F32),
                pltpu.SemaphoreType.DMA((2,)),
                pltpu.SemaphoreType.DMA((2,)),
            ],
        ),
        out_shape=jax.ShapeDtypeStruct(xs.shape, F32),
        compiler_params=_params(1),
    )(block_e, n_used, xs, w_gate, w_up, w_down)


def _combine_kernel(dest_ref, dest_next_ref, x_ref, gate_ref, y_ref, p_ref, gple_ref, wp_ref,
                    wg_ref, gfin_ref, out_ref, ybuf, sem, *, final):
    i = pl.program_id(0)
    slot = i % 2

    def gather(idx_ref, into):
        def row_copy(tile, sub, k, dest):
            return pltpu.make_async_copy(y_ref.at[dest], ybuf.at[into, k, tile, pl.ds(sub, 1), :],
                                         sem.at[into])
        _issue_rows(COMBINE_TILE, idx_ref, row_copy)

    @pl.when(i == 0)
    def _():
        gather(dest_ref, slot)

    @pl.when(i + 1 < pl.num_programs(0))
    def _():
        gather(dest_next_ref, 1 - slot)

    for k in range(TOP_K):
        _wait_rows(ybuf.at[slot, k], sem.at[slot])

    gates = gate_ref[...]
    y0 = ybuf[slot, 0].reshape(COMBINE_TILE, D)
    y1 = ybuf[slot, 1].reshape(COMBINE_TILE, D)
    x = x_ref[...] + gates[:, 0:1] * y0 + gates[:, 1:2] * y1
    h = (x * _rms_scale(x) * gple_ref[...]).astype(BF16)
    gate = _sigmoid(jnp.dot(h, wg_ref[...], preferred_element_type=F32))
    proj = jnp.dot(p_ref[0].astype(BF16), wp_ref[...], preferred_element_type=F32)
    x = x + proj * gate
    if final:
        x = x * _rms_scale(x) * gfin_ref[...]
    out_ref[...] = x


def _combine_ple(x2d, dest, gates, y, p3d, layer, g_ple, w_proj, w_gate, g_final, final):
    t = x2d.shape[0]
    nt = t // COMBINE_TILE
    dest_blk = (1, COMBINE_TILE // SUBLANES, SLOTS_PER_TILE)
    const = lambda i: (0, 0)
    return pl.pallas_call(
        functools.partial(_combine_kernel, final=final),
        name="combine_final" if final else "combine",
        grid=(nt,),
        in_specs=[
            pl.BlockSpec(dest_blk, lambda i: (i, 0, 0), memory_space=pltpu.SMEM),
            pl.BlockSpec(dest_blk, lambda i: (jnp.minimum(i + 1, nt - 1), 0, 0),
                         memory_space=pltpu.SMEM),
            pl.BlockSpec((COMBINE_TILE, D), lambda i: (i, 0)),
            pl.BlockSpec((COMBINE_TILE, TOP_K), lambda i: (i, 0)),
            pl.BlockSpec(memory_space=pl.ANY),
            pl.BlockSpec((1, COMBINE_TILE, D_PLE), lambda i: (layer, i, 0)),
            pl.BlockSpec((1, D), const),
            pl.BlockSpec((D_PLE, D), const, pipeline_mode=pl.Buffered(1)),
            pl.BlockSpec((D, D), const, pipeline_mode=pl.Buffered(1)),
            pl.BlockSpec((1, D), const),
        ],
        out_specs=pl.BlockSpec((COMBINE_TILE, D), lambda i: (i, 0)),
        out_shape=jax.ShapeDtypeStruct((t, D), F32),
        scratch_shapes=[
            pltpu.VMEM((2, TOP_K, COMBINE_TILE // SUBLANES, SUBLANES, D), F32),
            pltpu.SemaphoreType.DMA((2,)),
        ],
        compiler_params=_params(1),
    )(dest, dest, x2d, gates, y, p3d, g_ple.reshape(1, D), w_proj.astype(BF16),
      w_gate.astype(BF16), g_final.reshape(1, D))


def _moe_ple(x, p, layer, norm_ffn, w_group, b_group, w_expert, b_expert, w_gate, w_up, w_down,
             norm_ple, ple_w_proj, ple_w_gate, final_norm, final):
    b, s, _ = x.shape
    t = b * s
    x2d = x.reshape(t, D)
    ids, gates, counts = _router(x2d, norm_ffn, w_group, b_group, w_expert, b_expert)

    counts = counts[:, 0]
    padded = (counts + MOE_BLOCK - 1) // MOE_BLOCK * MOE_BLOCK
    pad_end = jnp.cumsum(padded)
    pad_start = pad_end - padded
    n_blocks = t * TOP_K // MOE_BLOCK + N_EXPERTS
    n_pad = n_blocks * MOE_BLOCK
    e = ids[:, 0:TOP_K, :].transpose(1, 0, 2).reshape(TOP_K, t)
    r = ids[:, TOP_K:2 * TOP_K, :].transpose(1, 0, 2).reshape(TOP_K, t)
    experts = jnp.arange(N_EXPERTS, dtype=jnp.int32)
    dest = r + jnp.sum(jnp.where(e[..., None] == experts, pad_start, 0), axis=-1)
    gate_cols = gates[:, 0:TOP_K, :].transpose(0, 2, 1).reshape(t, TOP_K)
    block_row = jnp.arange(n_blocks, dtype=jnp.int32) * MOE_BLOCK
    block_e = jnp.minimum(jnp.sum(pad_end[None, :] <= block_row[:, None], axis=-1),
                          N_EXPERTS - 1).astype(jnp.int32)
    n_used = (pad_end[-1:] // MOE_BLOCK).astype(jnp.int32)
    tail = (n_used[0] + jnp.arange(N_EXPERTS, dtype=jnp.int32)) * MOE_BLOCK
    zrow = jnp.concatenate([jnp.where(padded > 0, pad_end - MOE_BLOCK, -1),
                            jnp.where(tail < n_pad, tail, -1)]).astype(jnp.int32)

    def tiles(a, tile):
        return a.T.reshape(t // tile, tile // SUBLANES, SLOTS_PER_TILE).astype(jnp.int32)

    xs = _dispatch(x2d, norm_ffn, tiles(dest, ROUTER_TILE), zrow, n_pad)
    y = _experts(xs, block_e, n_used, layer, w_gate, w_up, w_down)
    out = _combine_ple(x2d, tiles(dest, COMBINE_TILE), gate_cols, y, p.reshape(-1, t, D_PLE), layer,
                       norm_ple, ple_w_proj, ple_w_gate, final_norm, final)
    return out.reshape(b, s, D)


def _qkv_kernel(x_ref, gq_ref, gkv_ref, wq_ref, wkv_ref, qt_ref, k_ref, vt_ref):
    x = x_ref[0]
    xn = x * _rms_scale(x)
    hq = (xn * gq_ref[...]).astype(BF16)
    hkv = (xn * gkv_ref[...]).astype(BF16)
    q = jnp.dot(hq, wq_ref[...], preferred_element_type=F32) * ATTN_SCALE
    kv = jnp.dot(hkv, wkv_ref[...], preferred_element_type=F32)
    qt_ref[0] = q.T.astype(BF16)
    k_ref[0] = kv[:, :D].astype(BF16)
    vt_ref[0] = kv[:, D:].T.astype(BF16)


def _qkv(x, g_q, g_kv, w_q, w_kv):
    b, s, _ = x.shape
    const = lambda bi, si: (0, 0)
    row_major = pl.BlockSpec((1, SEQ_TILE, D), lambda bi, si: (bi, si, 0))
    feat_major = pl.BlockSpec((1, D, SEQ_TILE), lambda bi, si: (bi, 0, si))
    return pl.pallas_call(
        _qkv_kernel,
        name="qkv",
        grid=(b, s // SEQ_TILE),
        in_specs=[
            row_major,
            pl.BlockSpec((1, D), const),
            pl.BlockSpec((1, D), const),
            pl.BlockSpec((D, D), const),
            pl.BlockSpec((D, 2 * D), const),
        ],
        out_specs=[feat_major, row_major, feat_major],
        out_shape=[jax.ShapeDtypeStruct((b, D, s), BF16),
                   jax.ShapeDtypeStruct((b, s, D), BF16),
                   jax.ShapeDtypeStruct((b, D, s), BF16)],
        compiler_params=_params(2),
    )(x, g_q.reshape(1, D), g_kv.reshape(1, D), w_q.astype(BF16), w_kv.astype(BF16))


Q_GROUP = 2 * CHUNK
G_BAND = BAND + CHUNK
PAIR = 2 * B_HEAD_DIM


def _attn_kernel(qt_ref, kp_ref, kc_ref, vtp_ref, vtc_ref, bias_ref, x_ref, wo_ref, out_ref, o_scr,
                 s_scr):
    drow = lax.broadcasted_iota(jnp.int32, (PAIR, Q_GROUP), 0)
    first_head = drow < B_HEAD_DIM

    def attend(first_tile):
        units = [(g, pr) for g in range(SEQ_TILE // Q_GROUP) for pr in range(B_HEADS // 2)]

        def geometry(g):
            w0 = g * Q_GROUP
            n_prev = SEQ_TILE - w0
            return w0, n_prev, G_BAND - n_prev

        def scores(unit, slot):
            g, pr = unit
            w0, n_prev, n_cur = geometry(g)
            feat = slice(pr * PAIR, (pr + 1) * PAIR)
            qt = qt_ref[0, feat, w0:w0 + Q_GROUP]
            zero = jnp.zeros_like(qt)
            qblk = jnp.concatenate([jnp.where(first_head, qt, zero),
                                    jnp.where(first_head, zero, qt)], axis=1)
            if first_tile:
                s_scr[slot, n_prev:, :] = (jnp.dot(kc_ref[0, :n_cur, feat], qblk,
                                                   preferred_element_type=F32)
                                           + bias_ref[pr, n_prev:, :])
            else:
                kb = jnp.concatenate([kp_ref[0, w0:, feat], kc_ref[0, :n_cur, feat]], axis=0)
                s_scr[slot] = jnp.dot(kb, qblk, preferred_element_type=F32) + bias_ref[pr]

        def values(unit, slot):
            g, pr = unit
            w0, n_prev, n_cur = geometry(g)
            feat = slice(pr * PAIR, (pr + 1) * PAIR)
            if first_tile:
                s = s_scr[slot, n_prev:, :]
                vt = vtc_ref[0, feat, :n_cur]
            else:
                s = s_scr[slot]
                vt = jnp.concatenate([vtp_ref[0, feat, w0:], vtc_ref[0, feat, :n_cur]], axis=1)
            m = jnp.max(s, axis=0, keepdims=True)
            p = jnp.exp(s - m)
            inv = 1.0 / jnp.sum(p, axis=0, keepdims=True)
            ot = jnp.dot(vt, p.astype(BF16), preferred_element_type=F32)
            ot = jnp.where(first_head, ot[:, :Q_GROUP] * inv[:, :Q_GROUP],
                           ot[:, Q_GROUP:] * inv[:, Q_GROUP:])
            o_scr[w0:w0 + Q_GROUP, feat] = ot.T.astype(BF16)

        scores(units[0], 0)
        for n, unit in enumerate(units):
            if n + 1 < len(units):
                scores(units[n + 1], (n + 1) % 2)
            values(unit, n % 2)

    @pl.when(pl.program_id(1) == 0)
    def _():
        attend(True)

    @pl.when(pl.program_id(1) > 0)
    def _():
        attend(False)

    out_ref[0] = x_ref[0] + jnp.dot(o_scr[...], wo_ref[...], preferred_element_type=F32)


def _attn(x, qt, k, vt, bias_t, w_o):
    b, s, _ = x.shape
    cur = lambda bi, si: (bi, si, 0)
    prev = lambda bi, si: (bi, jnp.maximum(si - 1, 0), 0)
    cur_t = lambda bi, si: (bi, 0, si)
    prev_t = lambda bi, si: (bi, 0, jnp.maximum(si - 1, 0))
    blk = (1, SEQ_TILE, D)
    blk_t = (1, D, SEQ_TILE)
    return pl.pallas_call(
        _attn_kernel,
        name="attn",
        grid=(b, s // SEQ_TILE),
        in_specs=[
            pl.BlockSpec(blk_t, cur_t),
            pl.BlockSpec(blk, prev),
            pl.BlockSpec(blk, cur),
            pl.BlockSpec(blk_t, prev_t),
            pl.BlockSpec(blk_t, cur_t),
            pl.BlockSpec((B_HEADS // 2, G_BAND, 2 * Q_GROUP), lambda bi, si: (0, 0, 0),
                         pipeline_mode=pl.Buffered(1)),
            pl.BlockSpec(blk, cur),
            pl.BlockSpec((D, D), lambda bi, si: (0, 0), pipeline_mode=pl.Buffered(1)),
        ],
        out_specs=pl.BlockSpec(blk, cur),
        out_shape=jax.ShapeDtypeStruct(x.shape, F32),
        scratch_shapes=[pltpu.VMEM((SEQ_TILE, D), BF16),
                        pltpu.VMEM((2, G_BAND, 2 * Q_GROUP), F32)],
        compiler_params=_params(2),
    )(qt, k, k, vt, vt, bias_t, x, w_o.astype(BF16))


def _group_bias(table):
    band = _band_bias(table)
    pad = lambda lo, hi: jnp.pad(band, ((0, 0), (0, 0), (lo, hi)), constant_values=NEG_INF)
    both = jnp.concatenate([pad(0, CHUNK), pad(CHUNK, 0)], axis=1)
    both = both.reshape(B_HEADS // 2, 2, Q_GROUP, G_BAND)
    return both.transpose(0, 3, 1, 2).reshape(B_HEADS // 2, G_BAND, 2 * Q_GROUP)


def _band_bias(table):
    n_rel = REL_MAX - REL_MIN + 1
    span = BAND + CHUNK - 1
    head = jnp.broadcast_to(table[:, n_rel - 1:], (table.shape[0], span - n_rel))
    ext = jnp.concatenate([head, table[:, ::-1]], axis=1)
    rows = [ext[:, CHUNK - 1 - q:CHUNK - 1 - q + BAND] for q in range(CHUNK)]
    return jnp.stack(rows, axis=1)


def kernel(x, p, a_w_in, a_lb_logits, a_out_norm, a_w_o, kv_norm, w_kv, b_w_q, b_rel_bias, b_w_o,
           norm_mix, norm_ffn, norm_ple, moe_w_group, moe_b_group, moe_w_expert, moe_b_expert,
           moe_w_gate, moe_w_up, moe_w_down, ple_w_proj, ple_w_gate, final_norm):
    b, s, _ = x.shape
    lower_bounds = jnp.cumsum(jax.nn.softmax(a_lb_logits.astype(F32), axis=0), axis=0)

    def moe(xi, i, final):
        return _moe_ple(xi, p, i, norm_ffn[i], moe_w_group[i], moe_b_group[i], moe_w_expert[i],
                        moe_b_expert[i], moe_w_gate, moe_w_up, moe_w_down, norm_ple[i],
                        ple_w_proj[i], ple_w_gate[i], final_norm, final)

    x = _mixer_a(x, norm_mix[0], a_w_in[0], lower_bounds[0], a_out_norm[0], a_w_o[0])
    x = moe(x, 0, False)

    qt, k, vt = _qkv(x, norm_mix[1], kv_norm, b_w_q[0], w_kv)
    x = _attn(x, qt, k, vt, _group_bias(b_rel_bias[0].astype(F32)), b_w_o[0])
    x = moe(x, 1, True)
    return x
```

```python
import functools

import jax
import jax.numpy as jnp
from jax import lax
from jax.experimental import pallas as pl
from jax.experimental.pallas import tpu as pltpu

F32 = jnp.float32
BF16 = jnp.bfloat16

D = 1024
CHUNK = 64
A_HEADS = 8
A_HEAD_DIM = 128
B_HEADS = 16
B_HEAD_DIM = 64
LEFT_CHUNKS = 8
BAND = (LEFT_CHUNKS + 1) * CHUNK
REL_MIN = -(CHUNK - 1)
REL_MAX = 256
ATTN_SCALE = B_HEAD_DIM ** -0.5
N_GROUPS = 4
EXPERTS_PER_GROUP = 8
N_EXPERTS = 32
TOP_K = 2
D_EXPERT = 512
MOE_BLOCK = 512
D_PLE = 256
EPS = 1e-6
NEG_INF = -1e30

SEQ_TILE = 512
ROUTER_TILE = 512
COMBINE_TILE = 512
ROUTER_ROWS = 40
VMEM_LIMIT = 56 * 1024 * 1024


def _params(n_axes, vmem=VMEM_LIMIT):
    return pltpu.CompilerParams(dimension_semantics=("arbitrary",) * n_axes,
                                vmem_limit_bytes=vmem)


def _rms_scale(x):
    return lax.rsqrt(jnp.mean(x * x, axis=-1, keepdims=True) + EPS)


def _sigmoid(x):
    return 1.0 / (1.0 + jnp.exp(-x))


def _mixer_a_kernel(x_ref, g_ref, win_ref, lb_ref, onorm_ref, wo_ref, out_ref,
                    proj_scr, o_scr, state_scr):
    @pl.when(pl.program_id(1) == 0)
    def _():
        state_scr[...] = jnp.zeros_like(state_scr)

    x = x_ref[0]
    h = (x * _rms_scale(x) * g_ref[...]).astype(BF16)
    proj_scr[...] = jnp.dot(h, win_ref[...], preferred_element_type=F32)

    row = lax.broadcasted_iota(jnp.int32, (CHUNK, CHUNK), 0)
    col = lax.broadcasted_iota(jnp.int32, (CHUNK, CHUNK), 1)
    causal = row >= col
    tril = causal.astype(BF16)
    lb = lb_ref[...]
    onorm = onorm_ref[...]

    def chunk_body(c, carry):
        r0 = pl.multiple_of(c * CHUNK, CHUNK)
        rows = pl.ds(r0, CHUNK)
        f = lb + (1.0 - lb) * _sigmoid(proj_scr[rows, D:2 * D])
        logf = jnp.log(f)
        hi = logf.astype(BF16)
        lo = (logf - hi.astype(F32)).astype(BF16)
        g = (jnp.dot(tril, hi, preferred_element_type=F32)
             + jnp.dot(tril, lo, preferred_element_type=F32))
        for hd in range(A_HEADS):
            sl = slice(hd * A_HEAD_DIM, (hd + 1) * A_HEAD_DIM)
            gh = g[:, sl]
            g_last = gh[CHUNK - 1:CHUNK, :]
            k = 1.0 - f[:, sl]
            q_dec = (proj_scr[rows, sl] * jnp.exp(gh)).astype(BF16)
            k_inv = (k * jnp.exp(-gh)).astype(BF16)
            k_tail = (k * jnp.exp(g_last - gh)).astype(BF16)
            v = proj_scr[rows, 2 * D + hd * A_HEAD_DIM:2 * D + (hd + 1) * A_HEAD_DIM]
            v_b = v.astype(BF16)
            att = lax.dot_general(q_dec, k_inv, (((1,), (1,)), ((), ())),
                                  preferred_element_type=F32)
            att = jnp.where(causal, att, 0.0).astype(BF16)
            st = state_scr[hd]
            o = (jnp.dot(att, v_b, preferred_element_type=F32)
                 + lax.dot_general(q_dec, st.astype(BF16), (((1,), (1,)), ((), ())),
                                   preferred_element_type=F32))
            v_t = v.T.astype(BF16)
            state_scr[hd] = st * jnp.exp(g_last) + jnp.dot(v_t, k_tail,
                                                           preferred_element_type=F32)
            o = o * _rms_scale(o)
            og = proj_scr[rows, 3 * D + hd * A_HEAD_DIM:3 * D + (hd + 1) * A_HEAD_DIM]
            o = o * onorm[:, sl] * (og * _sigmoid(og))
            o_scr[rows, sl] = o.astype(BF16)
        return carry

    lax.fori_loop(0, SEQ_TILE // CHUNK, chunk_body, 0, unroll=True)
    out_ref[0] = x + jnp.dot(o_scr[...], wo_ref[...], preferred_element_type=F32)


def _mixer_a(x, g, w_in, lb, out_norm, w_o):
    b, s, _ = x.shape
    const = lambda bi, si: (0, 0)
    return pl.pallas_call(
        _mixer_a_kernel,
        name="mixer_a",
        grid=(b, s // SEQ_TILE),
        in_specs=[
            pl.BlockSpec((1, SEQ_TILE, D), lambda bi, si: (bi, si, 0)),
            pl.BlockSpec((1, D), const),
            pl.BlockSpec((D, 4 * D), const, pipeline_mode=pl.Buffered(1)),
            pl.BlockSpec((1, D), const),
            pl.BlockSpec((1, D), const),
            pl.BlockSpec((D, D), const, pipeline_mode=pl.Buffered(1)),
        ],
        out_specs=pl.BlockSpec((1, SEQ_TILE, D), lambda bi, si: (bi, si, 0)),
        out_shape=jax.ShapeDtypeStruct(x.shape, F32),
        scratch_shapes=[
            pltpu.VMEM((SEQ_TILE, 4 * D), F32),
            pltpu.VMEM((SEQ_TILE, D), BF16),
            pltpu.VMEM((A_HEADS, A_HEAD_DIM, A_HEAD_DIM), F32),
        ],
        compiler_params=_params(2),
    )(x, g.reshape(1, D), w_in.astype(BF16), lb.reshape(1, D), out_norm.reshape(1, D),
      w_o.astype(BF16))


def _router_kernel(x_ref, g_ref, wr_ref, br_ref, ids_ref, gates_ref, counts_ref, cnt_scr):
    @pl.when(pl.program_id(0) == 0)
    def _():
        cnt_scr[...] = jnp.zeros_like(cnt_scr)

    tm = ROUTER_TILE
    x = x_ref[...]
    h = x * _rms_scale(x) * g_ref[...]
    logits = lax.dot_general(wr_ref[...], h, (((1,), (1,)), ((), ())),
                             precision=lax.Precision.HIGHEST,
                             preferred_element_type=F32) + br_ref[...]
    el = logits[0:N_EXPERTS]
    gl = logits[N_EXPERTS:ROUTER_ROWS]
    grow = lax.broadcasted_iota(jnp.int32, gl.shape, 0)
    gl = jnp.where(grow < N_GROUPS, gl, -jnp.inf)
    gmax = jnp.max(gl, axis=0, keepdims=True)
    gsum = jnp.sum(jnp.exp(gl - gmax), axis=0, keepdims=True)
    grp_w = 1.0 / gsum
    gidx = jnp.min(jnp.where(gl == gmax, grow, N_GROUPS), axis=0, keepdims=True)

    erow = lax.broadcasted_iota(jnp.int32, el.shape, 0)
    masked = jnp.where((erow // EXPERTS_PER_GROUP) == gidx, el, -jnp.inf)
    top1 = jnp.max(masked, axis=0, keepdims=True)
    i1 = jnp.min(jnp.where(masked == top1, erow, N_EXPERTS), axis=0, keepdims=True)
    masked2 = jnp.where(erow == i1, -jnp.inf, masked)
    top2 = jnp.max(masked2, axis=0, keepdims=True)
    i2 = jnp.min(jnp.where(masked2 == top2, erow, N_EXPERTS), axis=0, keepdims=True)
    e2 = jnp.exp(top2 - top1)
    denom = 1.0 + e2
    g1 = grp_w * (1.0 / denom)
    g2 = grp_w * (e2 / denom)

    sel1 = erow == i1
    sel2 = erow == i2
    onehot = (sel1 | sel2).astype(BF16)
    tr = lax.broadcasted_iota(jnp.int32, (tm, tm), 0)
    tc = lax.broadcasted_iota(jnp.int32, (tm, tm), 1)
    before = (tr < tc).astype(BF16)
    prefix = jnp.dot(onehot, before, preferred_element_type=F32) + cnt_scr[...]
    r1 = jnp.sum(jnp.where(sel1, prefix, 0.0), axis=0, keepdims=True)
    r2 = jnp.sum(jnp.where(sel2, prefix, 0.0), axis=0, keepdims=True)
    cnt_scr[...] += jnp.sum(onehot.astype(F32), axis=1, keepdims=True)

    zi = jnp.zeros((4, tm), jnp.int32)
    ids_ref[0] = jnp.concatenate(
        [i1, i2, r1.astype(jnp.int32), r2.astype(jnp.int32), zi], axis=0)
    gates_ref[0] = jnp.concatenate([g1, g2, jnp.zeros((6, tm), F32)], axis=0)
    counts_ref[...] = jnp.broadcast_to(cnt_scr[...], counts_ref.shape).astype(jnp.int32)


def _router(x2d, g, w_group, b_group, w_expert, b_expert):
    t = x2d.shape[0]
    nt = t // ROUTER_TILE
    pad = ROUTER_ROWS - N_EXPERTS - N_GROUPS
    wr = jnp.concatenate([w_expert.T, w_group.T, jnp.zeros((pad, D), F32)], axis=0)
    br = jnp.concatenate([b_expert, b_group, jnp.zeros((pad,), F32)]).reshape(ROUTER_ROWS, 1)
    const = lambda i: (0, 0)
    return pl.pallas_call(
        _router_kernel,
        name="router",
        grid=(nt,),
        in_specs=[
            pl.BlockSpec((ROUTER_TILE, D), lambda i: (i, 0)),
            pl.BlockSpec((1, D), const),
            pl.BlockSpec((ROUTER_ROWS, D), const),
            pl.BlockSpec((ROUTER_ROWS, 1), const),
        ],
        out_specs=[
            pl.BlockSpec((1, 8, ROUTER_TILE), lambda i: (i, 0, 0)),
            pl.BlockSpec((1, 8, ROUTER_TILE), lambda i: (i, 0, 0)),
            pl.BlockSpec((N_EXPERTS, 128), const),
        ],
        out_shape=[
            jax.ShapeDtypeStruct((nt, 8, ROUTER_TILE), jnp.int32),
            jax.ShapeDtypeStruct((nt, 8, ROUTER_TILE), F32),
            jax.ShapeDtypeStruct((N_EXPERTS, 128), jnp.int32),
        ],
        scratch_shapes=[pltpu.VMEM((N_EXPERTS, 1), F32)],
        compiler_params=_params(1),
    )(x2d, g.reshape(1, D), wr, br)


SUBLANES = 8
SLOTS_PER_TILE = SUBLANES * TOP_K


def _issue_rows(n_rows, dest_ref, make_copy):
    def body(j, c):
        for u in range(SUBLANES):
            for k in range(TOP_K):
                slot = u * TOP_K + k
                make_copy(j, u, k, dest_ref[0, j, slot]).start(priority=slot % 2)
        return c

    lax.fori_loop(0, n_rows // SUBLANES, body, 0)


def _tile_rows(x):
    return x.reshape(x.shape[0] // SUBLANES, SUBLANES, x.shape[1])


def _wait_rows(buf_ref, sem):
    pltpu.make_async_copy(buf_ref, buf_ref, sem).wait()


def _dispatch_kernel(zrow_ref, dest_ref, x_ref, g_ref, xs_ref, hbuf, zbuf, zsem, sem):
    @pl.when(pl.program_id(0) == 0)
    def _():
        zbuf[...] = jnp.zeros_like(zbuf)

        def zcopy(e):
            return pltpu.make_async_copy(zbuf, xs_ref.at[pl.ds(zrow_ref[e], MOE_BLOCK)], zsem)

        def zstart(e, c):
            @pl.when(zrow_ref[e] >= 0)
            def _():
                zcopy(e).start()
            return c

        def zwait(e, c):
            @pl.when(zrow_ref[e] >= 0)
            def _():
                zcopy(e).wait()
            return c

        lax.fori_loop(0, 2 * N_EXPERTS, zstart, 0)
        lax.fori_loop(0, 2 * N_EXPERTS, zwait, 0)

    i = pl.program_id(0)
    slot = i % 2
    x = x_ref[...]
    hbuf[slot] = _tile_rows(x * _rms_scale(x) * g_ref[...])

    def row_copy(tile, sub, k, dest):
        return pltpu.make_async_copy(hbuf.at[slot, tile, pl.ds(sub, 1), :], xs_ref.at[dest],
                                     sem.at[slot])

    _issue_rows(ROUTER_TILE, dest_ref, row_copy)

    def drain(which):
        for _ in range(TOP_K):
            _wait_rows(hbuf.at[which], sem.at[which])

    @pl.when(i > 0)
    def _():
        drain(1 - slot)

    @pl.when(i == pl.num_programs(0) - 1)
    def _():
        drain(slot)


def _dispatch(x2d, g, dest, zrow, n_pad):
    t = x2d.shape[0]
    nt = t // ROUTER_TILE
    return pl.pallas_call(
        _dispatch_kernel,
        name="dispatch",
        grid_spec=pltpu.PrefetchScalarGridSpec(
            num_scalar_prefetch=1,
            grid=(nt,),
            in_specs=[
                pl.BlockSpec((1, ROUTER_TILE // SUBLANES, SLOTS_PER_TILE),
                             lambda i, z: (i, 0, 0), memory_space=pltpu.SMEM),
                pl.BlockSpec((ROUTER_TILE, D), lambda i, z: (i, 0)),
                pl.BlockSpec((1, D), lambda i, z: (0, 0)),
            ],
            out_specs=pl.BlockSpec(memory_space=pl.ANY),
            scratch_shapes=[
                pltpu.VMEM((2, ROUTER_TILE // SUBLANES, SUBLANES, D), F32),
                pltpu.VMEM((MOE_BLOCK, 1, D), F32),
                pltpu.SemaphoreType.DMA(()),
                pltpu.SemaphoreType.DMA((2,)),
            ],
        ),
        out_shape=jax.ShapeDtypeStruct((n_pad, 1, D), F32),
        compiler_params=_params(1),
    )(zrow, dest, x2d, g.reshape(1, D))


def _experts_kernel(be_ref, nxt_ref, nu_ref, xs_ref, wg_ref, wu_ref, wd_ref, y_ref,
                    wg_f, wu_f, wd_f, wg_b, wu_b, wd_b, xbuf, ybuf, w_sem, in_sem, out_sem, *, layer):
    i = pl.program_id(0)
    n_used = nu_ref[0]
    slot = i % 2

    def block_rows(ref, blk):
        return ref.at[pl.ds(pl.multiple_of(blk * MOE_BLOCK, MOE_BLOCK), MOE_BLOCK), 0]

    def fetch(blk, into):
        return pltpu.make_async_copy(block_rows(xs_ref, blk), xbuf.at[into], in_sem.at[into])

    def write_back(blk, from_):
        return pltpu.make_async_copy(ybuf.at[from_], block_rows(y_ref, blk), out_sem.at[from_])

    def weight_copies(e):
        return [pltpu.make_async_copy(src.at[layer, e], dst, w_sem.at[n])
                for n, (src, dst) in enumerate(((wg_ref, wg_f), (wu_ref, wu_f), (wd_ref, wd_f)))]

    @pl.when(i == 0)
    def _():
        for c in weight_copies(be_ref[0]):
            c.start()
        fetch(0, 0).start()

    @pl.when(i + 1 < n_used)
    def _():
        fetch(i + 1, 1 - slot).start()

    prev = be_ref[jnp.maximum(i - 1, 0)]

    @pl.when((i < n_used) & ((i == 0) | (be_ref[i] != prev)))
    def _():
        for c in weight_copies(be_ref[i]):
            c.wait()
        wg_b[...] = wg_f[...].astype(BF16)
        wu_b[...] = wu_f[...].astype(BF16)
        wd_b[...] = wd_f[...].astype(BF16)

        @pl.when(nxt_ref[i] >= 0)
        def _():
            for c in weight_copies(nxt_ref[i]):
                c.start()

    @pl.when(i >= 2)
    def _():
        write_back(i - 2, slot).wait()

    @pl.when(i < n_used)
    def _():
        fetch(i, slot).wait()
        h = xbuf[slot].astype(BF16)
        a = jnp.dot(h, wg_b[...], preferred_element_type=F32)
        u = jnp.dot(h, wu_b[...], preferred_element_type=F32)
        hid = (a * _sigmoid(a) * u).astype(BF16)
        ybuf[slot] = jnp.dot(hid, wd_b[...], preferred_element_type=F32)

    @pl.when(i >= n_used)
    def _():
        ybuf[slot] = jnp.zeros((MOE_BLOCK, D), F32)

    write_back(i, slot).start()

    @pl.when(i == pl.num_programs(0) - 1)
    def _():
        write_back(i - 1, 1 - slot).wait()
        write_back(i, slot).wait()


def _experts(xs, block_e, next_e, n_used, layer, w_gate, w_up, w_down):
    n_pad = xs.shape[0]
    n_blocks = n_pad // MOE_BLOCK
    hbm = pl.BlockSpec(memory_space=pl.ANY)
    return pl.pallas_call(
        functools.partial(_experts_kernel, layer=layer),
        name="experts",
        grid_spec=pltpu.PrefetchScalarGridSpec(
            num_scalar_prefetch=3,
            grid=(n_blocks,),
            in_specs=[hbm, hbm, hbm, hbm],
            out_specs=hbm,
            scratch_shapes=[
                pltpu.VMEM((D, D_EXPERT), F32),
                pltpu.VMEM((D, D_EXPERT), F32),
                pltpu.VMEM((D_EXPERT, D), F32),
                pltpu.VMEM((D, D_EXPERT), BF16),
                pltpu.VMEM((D, D_EXPERT), BF16),
                pltpu.VMEM((D_EXPERT, D), BF16),
                pltpu.VMEM((2, MOE_BLOCK, D), F32),
                pltpu.VMEM((2, MOE_BLOCK, D), F32),
                pltpu.SemaphoreType.DMA((3,)),
                pltpu.SemaphoreType.DMA((2,)),
                pltpu.SemaphoreType.DMA((2,)),
            ],
        ),
        out_shape=jax.ShapeDtypeStruct(xs.shape, F32),
        compiler_params=_params(1),
    )(block_e, next_e, n_used, xs, w_gate, w_up, w_down)


def _combine_kernel(dest_ref, dest_next_ref, x_ref, gate_ref, y_ref, p_ref, gple_ref, wp_ref,
                    wg_ref, gfin_ref, out_ref, ybuf, sem, *, final):
    i = pl.program_id(0)
    slot = i % 2

    def gather(idx_ref, into):
        def row_copy(tile, sub, k, dest):
            return pltpu.make_async_copy(y_ref.at[dest], ybuf.at[into, k, tile, pl.ds(sub, 1), :],
                                         sem.at[into])
        _issue_rows(COMBINE_TILE, idx_ref, row_copy)

    @pl.when(i == 0)
    def _():
        gather(dest_ref, slot)

    @pl.when(i + 1 < pl.num_programs(0))
    def _():
        gather(dest_next_ref, 1 - slot)

    for k in range(TOP_K):
        _wait_rows(ybuf.at[slot, k], sem.at[slot])

    gates = gate_ref[...]
    y0 = ybuf[slot, 0].reshape(COMBINE_TILE, D)
    y1 = ybuf[slot, 1].reshape(COMBINE_TILE, D)
    x = x_ref[...] + gates[:, 0:1] * y0 + gates[:, 1:2] * y1
    h = (x * _rms_scale(x) * gple_ref[...]).astype(BF16)
    gate = _sigmoid(jnp.dot(h, wg_ref[...], preferred_element_type=F32))
    proj = jnp.dot(p_ref[0].astype(BF16), wp_ref[...], preferred_element_type=F32)
    x = x + proj * gate
    if final:
        x = x * _rms_scale(x) * gfin_ref[...]
    out_ref[...] = x


def _combine_ple(x2d, dest, gates, y, p3d, layer, g_ple, w_proj, w_gate, g_final, final):
    t = x2d.shape[0]
    nt = t // COMBINE_TILE
    dest_blk = (1, COMBINE_TILE // SUBLANES, SLOTS_PER_TILE)
    const = lambda i: (0, 0)
    return pl.pallas_call(
        functools.partial(_combine_kernel, final=final),
        name="combine_final" if final else "combine",
        grid=(nt,),
        in_specs=[
            pl.BlockSpec(dest_blk, lambda i: (i, 0, 0), memory_space=pltpu.SMEM),
            pl.BlockSpec(dest_blk, lambda i: (jnp.minimum(i + 1, nt - 1), 0, 0),
                         memory_space=pltpu.SMEM),
            pl.BlockSpec((COMBINE_TILE, D), lambda i: (i, 0)),
            pl.BlockSpec((COMBINE_TILE, TOP_K), lambda i: (i, 0)),
            pl.BlockSpec(memory_space=pl.ANY),
            pl.BlockSpec((1, COMBINE_TILE, D_PLE), lambda i: (layer, i, 0)),
            pl.BlockSpec((1, D), const),
            pl.BlockSpec((D_PLE, D), const, pipeline_mode=pl.Buffered(1)),
            pl.BlockSpec((D, D), const, pipeline_mode=pl.Buffered(1)),
            pl.BlockSpec((1, D), const),
        ],
        out_specs=pl.BlockSpec((COMBINE_TILE, D), lambda i: (i, 0)),
        out_shape=jax.ShapeDtypeStruct((t, D), F32),
        scratch_shapes=[
            pltpu.VMEM((2, TOP_K, COMBINE_TILE // SUBLANES, SUBLANES, D), F32),
            pltpu.SemaphoreType.DMA((2,)),
        ],
        compiler_params=_params(1),
    )(dest, dest, x2d, gates, y, p3d, g_ple.reshape(1, D), w_proj.astype(BF16),
      w_gate.astype(BF16), g_final.reshape(1, D))


def _moe_ple(x, p, layer, norm_ffn, w_group, b_group, w_expert, b_expert, w_gate, w_up, w_down,
             norm_ple, ple_w_proj, ple_w_gate, final_norm, final):
    b, s, _ = x.shape
    t = b * s
    x2d = x.reshape(t, D)
    ids, gates, counts = _router(x2d, norm_ffn, w_group, b_group, w_expert, b_expert)

    counts = counts[:, 0]
    padded = (counts + MOE_BLOCK - 1) // MOE_BLOCK * MOE_BLOCK
    pad_end = jnp.cumsum(padded)
    pad_start = pad_end - padded
    n_blocks = t * TOP_K // MOE_BLOCK + N_EXPERTS
    n_pad = n_blocks * MOE_BLOCK
    e = ids[:, 0:TOP_K, :].transpose(1, 0, 2).reshape(TOP_K, t)
    r = ids[:, TOP_K:2 * TOP_K, :].transpose(1, 0, 2).reshape(TOP_K, t)
    experts = jnp.arange(N_EXPERTS, dtype=jnp.int32)
    dest = r + jnp.sum(jnp.where(e[..., None] == experts, pad_start, 0), axis=-1)
    gate_cols = gates[:, 0:TOP_K, :].transpose(0, 2, 1).reshape(t, TOP_K)
    block_row = jnp.arange(n_blocks, dtype=jnp.int32) * MOE_BLOCK
    block_e = jnp.minimum(jnp.sum(pad_end[None, :] <= block_row[:, None], axis=-1),
                          N_EXPERTS - 1).astype(jnp.int32)
    n_used = (pad_end[-1:] // MOE_BLOCK).astype(jnp.int32)
    tail = (n_used[0] + jnp.arange(N_EXPERTS, dtype=jnp.int32)) * MOE_BLOCK
    zrow = jnp.concatenate([jnp.where(padded > 0, pad_end - MOE_BLOCK, -1),
                            jnp.where(tail < n_pad, tail, -1)]).astype(jnp.int32)

    def tiles(a, tile):
        return a.T.reshape(t // tile, tile // SUBLANES, SLOTS_PER_TILE).astype(jnp.int32)

    xs = _dispatch(x2d, norm_ffn, tiles(dest, ROUTER_TILE), zrow, n_pad)
    later = jnp.where((experts[None, :] > block_e[:, None]) & (padded[None, :] > 0), experts[None, :],
                      N_EXPERTS)
    next_e = jnp.min(later, axis=-1)
    next_e = jnp.where(next_e < N_EXPERTS, next_e, -1).astype(jnp.int32)
    y = _experts(xs, block_e, next_e, n_used, layer, w_gate, w_up, w_down)
    out = _combine_ple(x2d, tiles(dest, COMBINE_TILE), gate_cols, y, p.reshape(-1, t, D_PLE), layer,
                       norm_ple, ple_w_proj, ple_w_gate, final_norm, final)
    return out.reshape(b, s, D)


def _qkv_kernel(x_ref, gq_ref, gkv_ref, wq_ref, wkv_ref, qt_ref, k_ref, vt_ref):
    x = x_ref[0]
    xn = x * _rms_scale(x)
    hq = (xn * gq_ref[...]).astype(BF16)
    hkv = (xn * gkv_ref[...]).astype(BF16)
    q = jnp.dot(hq, wq_ref[...], preferred_element_type=F32) * ATTN_SCALE
    kv = jnp.dot(hkv, wkv_ref[...], preferred_element_type=F32)
    qt_ref[0] = q.T.astype(BF16)
    k_ref[0] = kv[:, :D].astype(BF16)
    vt_ref[0] = kv[:, D:].T.astype(BF16)


def _qkv(x, g_q, g_kv, w_q, w_kv):
    b, s, _ = x.shape
    const = lambda bi, si: (0, 0)
    row_major = pl.BlockSpec((1, SEQ_TILE, D), lambda bi, si: (bi, si, 0))
    feat_major = pl.BlockSpec((1, D, SEQ_TILE), lambda bi, si: (bi, 0, si))
    return pl.pallas_call(
        _qkv_kernel,
        name="qkv",
        grid=(b, s // SEQ_TILE),
        in_specs=[
            row_major,
            pl.BlockSpec((1, D), const),
            pl.BlockSpec((1, D), const),
            pl.BlockSpec((D, D), const),
            pl.BlockSpec((D, 2 * D), const),
        ],
        out_specs=[feat_major, row_major, feat_major],
        out_shape=[jax.ShapeDtypeStruct((b, D, s), BF16),
                   jax.ShapeDtypeStruct((b, s, D), BF16),
                   jax.ShapeDtypeStruct((b, D, s), BF16)],
        compiler_params=_params(2),
    )(x, g_q.reshape(1, D), g_kv.reshape(1, D), w_q.astype(BF16), w_kv.astype(BF16))


Q_GROUP = 2 * CHUNK
G_BAND = BAND + CHUNK
PAIR = 2 * B_HEAD_DIM


def _attn_kernel(qt_ref, kp_ref, kc_ref, vtp_ref, vtc_ref, bias_ref, x_ref, wo_ref, out_ref, o_scr,
                 s_scr):
    drow = lax.broadcasted_iota(jnp.int32, (PAIR, Q_GROUP), 0)
    first_head = drow < B_HEAD_DIM

    def attend(first_tile):
        units = [(g, pr) for g in range(SEQ_TILE // Q_GROUP) for pr in range(B_HEADS // 2)]

        def geometry(g):
            w0 = g * Q_GROUP
            n_prev = SEQ_TILE - w0
            return w0, n_prev, G_BAND - n_prev

        def scores(unit, slot):
            g, pr = unit
            w0, n_prev, n_cur = geometry(g)
            feat = slice(pr * PAIR, (pr + 1) * PAIR)
            qt = qt_ref[0, feat, w0:w0 + Q_GROUP]
            zero = jnp.zeros_like(qt)
            qblk = jnp.concatenate([jnp.where(first_head, qt, zero),
                                    jnp.where(first_head, zero, qt)], axis=1)
            if first_tile:
                s_scr[slot, n_prev:, :] = (jnp.dot(kc_ref[0, :n_cur, feat], qblk,
                                                   preferred_element_type=F32)
                                           + bias_ref[pr, n_prev:, :])
            else:
                kb = jnp.concatenate([kp_ref[0, w0:, feat], kc_ref[0, :n_cur, feat]], axis=0)
                s_scr[slot] = jnp.dot(kb, qblk, preferred_element_type=F32) + bias_ref[pr]

        def values(unit, slot):
            g, pr = unit
            w0, n_prev, n_cur = geometry(g)
            feat = slice(pr * PAIR, (pr + 1) * PAIR)
            if first_tile:
                s = s_scr[slot, n_prev:, :]
                vt = vtc_ref[0, feat, :n_cur]
            else:
                s = s_scr[slot]
                vt = jnp.concatenate([vtp_ref[0, feat, w0:], vtc_ref[0, feat, :n_cur]], axis=1)
            m = jnp.max(s, axis=0, keepdims=True)
            p = jnp.exp(s - m)
            inv = 1.0 / jnp.sum(p, axis=0, keepdims=True)
            ot = jnp.dot(vt, p.astype(BF16), preferred_element_type=F32)
            ot = jnp.where(first_head, ot[:, :Q_GROUP] * inv[:, :Q_GROUP],
                           ot[:, Q_GROUP:] * inv[:, Q_GROUP:])
            o_scr[w0:w0 + Q_GROUP, feat] = ot.T.astype(BF16)

        scores(units[0], 0)
        for n, unit in enumerate(units):
            if n + 1 < len(units):
                scores(units[n + 1], (n + 1) % 2)
            values(unit, n % 2)

    @pl.when(pl.program_id(1) == 0)
    def _():
        attend(True)

    @pl.when(pl.program_id(1) > 0)
    def _():
        attend(False)

    out_ref[0] = x_ref[0] + jnp.dot(o_scr[...], wo_ref[...], preferred_element_type=F32)


def _attn(x, qt, k, vt, bias_t, w_o):
    b, s, _ = x.shape
    cur = lambda bi, si: (bi, si, 0)
    prev = lambda bi, si: (bi, jnp.maximum(si - 1, 0), 0)
    cur_t = lambda bi, si: (bi, 0, si)
    prev_t = lambda bi, si: (bi, 0, jnp.maximum(si - 1, 0))
    blk = (1, SEQ_TILE, D)
    blk_t = (1, D, SEQ_TILE)
    return pl.pallas_call(
        _attn_kernel,
        name="attn",
        grid=(b, s // SEQ_TILE),
        in_specs=[
            pl.BlockSpec(blk_t, cur_t),
            pl.BlockSpec(blk, prev),
            pl.BlockSpec(blk, cur),
            pl.BlockSpec(blk_t, prev_t),
            pl.BlockSpec(blk_t, cur_t),
            pl.BlockSpec((B_HEADS // 2, G_BAND, 2 * Q_GROUP), lambda bi, si: (0, 0, 0),
                         pipeline_mode=pl.Buffered(1)),
            pl.BlockSpec(blk, cur),
            pl.BlockSpec((D, D), lambda bi, si: (0, 0), pipeline_mode=pl.Buffered(1)),
        ],
        out_specs=pl.BlockSpec(blk, cur),
        out_shape=jax.ShapeDtypeStruct(x.shape, F32),
        scratch_shapes=[pltpu.VMEM((SEQ_TILE, D), BF16),
                        pltpu.VMEM((2, G_BAND, 2 * Q_GROUP), F32)],
        compiler_params=_params(2),
    )(qt, k, k, vt, vt, bias_t, x, w_o.astype(BF16))


def _group_bias(table):
    band = _band_bias(table)
    pad = lambda lo, hi: jnp.pad(band, ((0, 0), (0, 0), (lo, hi)), constant_values=NEG_INF)
    both = jnp.concatenate([pad(0, CHUNK), pad(CHUNK, 0)], axis=1)
    both = both.reshape(B_HEADS // 2, 2, Q_GROUP, G_BAND)
    return both.transpose(0, 3, 1, 2).reshape(B_HEADS // 2, G_BAND, 2 * Q_GROUP)


def _band_bias(table):
    n_rel = REL_MAX - REL_MIN + 1
    span = BAND + CHUNK - 1
    head = jnp.broadcast_to(table[:, n_rel - 1:], (table.shape[0], span - n_rel))
    ext = jnp.concatenate([head, table[:, ::-1]], axis=1)
    rows = [ext[:, CHUNK - 1 - q:CHUNK - 1 - q + BAND] for q in range(CHUNK)]
    return jnp.stack(rows, axis=1)


def kernel(x, p, a_w_in, a_lb_logits, a_out_norm, a_w_o, kv_norm, w_kv, b_w_q, b_rel_bias, b_w_o,
           norm_mix, norm_ffn, norm_ple, moe_w_group, moe_b_group, moe_w_expert, moe_b_expert,
           moe_w_gate, moe_w_up, moe_w_down, ple_w_proj, ple_w_gate, final_norm):
    b, s, _ = x.shape
    lower_bounds = jnp.cumsum(jax.nn.softmax(a_lb_logits.astype(F32), axis=0), axis=0)

    def moe(xi, i, final):
        return _moe_ple(xi, p, i, norm_ffn[i], moe_w_group[i], moe_b_group[i], moe_w_expert[i],
                        moe_b_expert[i], moe_w_gate, moe_w_up, moe_w_down, norm_ple[i],
                        ple_w_proj[i], ple_w_gate[i], final_norm, final)

    x = _mixer_a(x, norm_mix[0], a_w_in[0], lower_bounds[0], a_out_norm[0], a_w_o[0])
    x = moe(x, 0, False)

    qt, k, vt = _qkv(x, norm_mix[1], kv_norm, b_w_q[0], w_kv)
    x = _attn(x, qt, k, vt, _group_bias(b_rel_bias[0].astype(F32)), b_w_o[0])
    x = moe(x, 1, True)
    return x
```

```python
import functools

import jax
import jax.numpy as jnp
from jax import lax
from jax.experimental import pallas as pl
from jax.experimental.pallas import tpu as pltpu

F32 = jnp.float32
BF16 = jnp.bfloat16

D = 1024
CHUNK = 64
A_HEADS = 8
A_HEAD_DIM = 128
B_HEADS = 16
B_HEAD_DIM = 64
LEFT_CHUNKS = 8
BAND = (LEFT_CHUNKS + 1) * CHUNK
REL_MIN = -(CHUNK - 1)
REL_MAX = 256
ATTN_SCALE = B_HEAD_DIM ** -0.5
N_GROUPS = 4
EXPERTS_PER_GROUP = 8
N_EXPERTS = 32
TOP_K = 2
D_EXPERT = 512
MOE_BLOCK = 512
D_PLE = 256
EPS = 1e-6
NEG_INF = -1e30

SEQ_TILE = 512
ROUTER_TILE = 512
COMBINE_TILE = 512
ROUTER_ROWS = 40
VMEM_LIMIT = 56 * 1024 * 1024


def _params(n_axes, vmem=VMEM_LIMIT):
    return pltpu.CompilerParams(dimension_semantics=("arbitrary",) * n_axes,
                                vmem_limit_bytes=vmem)


def _rms_scale(x):
    return lax.rsqrt(jnp.mean(x * x, axis=-1, keepdims=True) + EPS)


def _sigmoid(x):
    return 1.0 / (1.0 + jnp.exp(-x))


def _mixer_a_kernel(x_ref, g_ref, win_ref, lb_ref, onorm_ref, wo_ref, out_ref,
                    proj_scr, o_scr, state_scr):
    @pl.when(pl.program_id(1) == 0)
    def _():
        state_scr[...] = jnp.zeros_like(state_scr)

    x = x_ref[0]
    h = (x * _rms_scale(x) * g_ref[...]).astype(BF16)
    proj_scr[...] = jnp.dot(h, win_ref[...], preferred_element_type=F32)

    row = lax.broadcasted_iota(jnp.int32, (CHUNK, CHUNK), 0)
    col = lax.broadcasted_iota(jnp.int32, (CHUNK, CHUNK), 1)
    causal = row >= col
    tril = causal.astype(BF16)
    lb = lb_ref[...]
    onorm = onorm_ref[...]

    def chunk_body(c, carry):
        r0 = pl.multiple_of(c * CHUNK, CHUNK)
        rows = pl.ds(r0, CHUNK)
        f = lb + (1.0 - lb) * _sigmoid(proj_scr[rows, D:2 * D])
        logf = jnp.log(f)
        hi = logf.astype(BF16)
        lo = (logf - hi.astype(F32)).astype(BF16)
        g = (jnp.dot(tril, hi, preferred_element_type=F32)
             + jnp.dot(tril, lo, preferred_element_type=F32))
        for hd in range(A_HEADS):
            sl = slice(hd * A_HEAD_DIM, (hd + 1) * A_HEAD_DIM)
            gh = g[:, sl]
            g_last = gh[CHUNK - 1:CHUNK, :]
            k = 1.0 - f[:, sl]
            q_dec = (proj_scr[rows, sl] * jnp.exp(gh)).astype(BF16)
            k_inv = (k * jnp.exp(-gh)).astype(BF16)
            k_tail = (k * jnp.exp(g_last - gh)).astype(BF16)
            v = proj_scr[rows, 2 * D + hd * A_HEAD_DIM:2 * D + (hd + 1) * A_HEAD_DIM]
            v_b = v.astype(BF16)
            att = lax.dot_general(q_dec, k_inv, (((1,), (1,)), ((), ())),
                                  preferred_element_type=F32)
            att = jnp.where(causal, att, 0.0).astype(BF16)
            st = state_scr[hd]
            o = (jnp.dot(att, v_b, preferred_element_type=F32)
                 + lax.dot_general(q_dec, st.astype(BF16), (((1,), (1,)), ((), ())),
                                   preferred_element_type=F32))
            v_t = v.T.astype(BF16)
            state_scr[hd] = st * jnp.exp(g_last) + jnp.dot(v_t, k_tail,
                                                           preferred_element_type=F32)
            o = o * _rms_scale(o)
            og = proj_scr[rows, 3 * D + hd * A_HEAD_DIM:3 * D + (hd + 1) * A_HEAD_DIM]
            o = o * onorm[:, sl] * (og * _sigmoid(og))
            o_scr[rows, sl] = o.astype(BF16)
        return carry

    lax.fori_loop(0, SEQ_TILE // CHUNK, chunk_body, 0, unroll=True)
    out_ref[0] = x + jnp.dot(o_scr[...], wo_ref[...], preferred_element_type=F32)


def _mixer_a(x, g, w_in, lb, out_norm, w_o):
    b, s, _ = x.shape
    const = lambda bi, si: (0, 0)
    return pl.pallas_call(
        _mixer_a_kernel,
        name="mixer_a",
        grid=(b, s // SEQ_TILE),
        in_specs=[
            pl.BlockSpec((1, SEQ_TILE, D), lambda bi, si: (bi, si, 0)),
            pl.BlockSpec((1, D), const),
            pl.BlockSpec((D, 4 * D), const, pipeline_mode=pl.Buffered(1)),
            pl.BlockSpec((1, D), const),
            pl.BlockSpec((1, D), const),
            pl.BlockSpec((D, D), const, pipeline_mode=pl.Buffered(1)),
        ],
        out_specs=pl.BlockSpec((1, SEQ_TILE, D), lambda bi, si: (bi, si, 0)),
        out_shape=jax.ShapeDtypeStruct(x.shape, F32),
        scratch_shapes=[
            pltpu.VMEM((SEQ_TILE, 4 * D), F32),
            pltpu.VMEM((SEQ_TILE, D), BF16),
            pltpu.VMEM((A_HEADS, A_HEAD_DIM, A_HEAD_DIM), F32),
        ],
        compiler_params=_params(2),
    )(x, g.reshape(1, D), w_in.astype(BF16), lb.reshape(1, D), out_norm.reshape(1, D),
      w_o.astype(BF16))


def _router_kernel(x_ref, g_ref, wr_ref, br_ref, ids_ref, gates_ref, counts_ref, cnt_scr):
    @pl.when(pl.program_id(0) == 0)
    def _():
        cnt_scr[...] = jnp.zeros_like(cnt_scr)

    tm = ROUTER_TILE
    x = x_ref[...]
    h = x * _rms_scale(x) * g_ref[...]
    logits = lax.dot_general(wr_ref[...], h, (((1,), (1,)), ((), ())),
                             precision=lax.Precision.HIGHEST,
                             preferred_element_type=F32) + br_ref[...]
    el = logits[0:N_EXPERTS]
    gl = logits[N_EXPERTS:ROUTER_ROWS]
    grow = lax.broadcasted_iota(jnp.int32, gl.shape, 0)
    gl = jnp.where(grow < N_GROUPS, gl, -jnp.inf)
    gmax = jnp.max(gl, axis=0, keepdims=True)
    gsum = jnp.sum(jnp.exp(gl - gmax), axis=0, keepdims=True)
    grp_w = 1.0 / gsum
    gidx = jnp.min(jnp.where(gl == gmax, grow, N_GROUPS), axis=0, keepdims=True)

    erow = lax.broadcasted_iota(jnp.int32, el.shape, 0)
    masked = jnp.where((erow // EXPERTS_PER_GROUP) == gidx, el, -jnp.inf)
    top1 = jnp.max(masked, axis=0, keepdims=True)
    i1 = jnp.min(jnp.where(masked == top1, erow, N_EXPERTS), axis=0, keepdims=True)
    masked2 = jnp.where(erow == i1, -jnp.inf, masked)
    top2 = jnp.max(masked2, axis=0, keepdims=True)
    i2 = jnp.min(jnp.where(masked2 == top2, erow, N_EXPERTS), axis=0, keepdims=True)
    e2 = jnp.exp(top2 - top1)
    denom = 1.0 + e2
    g1 = grp_w * (1.0 / denom)
    g2 = grp_w * (e2 / denom)

    sel1 = erow == i1
    sel2 = erow == i2
    onehot = (sel1 | sel2).astype(BF16)
    tr = lax.broadcasted_iota(jnp.int32, (tm, tm), 0)
    tc = lax.broadcasted_iota(jnp.int32, (tm, tm), 1)
    before = (tr < tc).astype(BF16)
    prefix = jnp.dot(onehot, before, preferred_element_type=F32) + cnt_scr[...]
    r1 = jnp.sum(jnp.where(sel1, prefix, 0.0), axis=0, keepdims=True)
    r2 = jnp.sum(jnp.where(sel2, prefix, 0.0), axis=0, keepdims=True)
    cnt_scr[...] += jnp.sum(onehot.astype(F32), axis=1, keepdims=True)

    zi = jnp.zeros((4, tm), jnp.int32)
    ids_ref[0] = jnp.concatenate(
        [i1, i2, r1.astype(jnp.int32), r2.astype(jnp.int32), zi], axis=0)
    gates_ref[0] = jnp.concatenate([g1, g2, jnp.zeros((6, tm), F32)], axis=0)
    counts_ref[...] = jnp.broadcast_to(cnt_scr[...], counts_ref.shape).astype(jnp.int32)


def _router(x2d, g, w_group, b_group, w_expert, b_expert):
    t = x2d.shape[0]
    nt = t // ROUTER_TILE
    pad = ROUTER_ROWS - N_EXPERTS - N_GROUPS
    wr = jnp.concatenate([w_expert.T, w_group.T, jnp.zeros((pad, D), F32)], axis=0)
    br = jnp.concatenate([b_expert, b_group, jnp.zeros((pad,), F32)]).reshape(ROUTER_ROWS, 1)
    const = lambda i: (0, 0)
    return pl.pallas_call(
        _router_kernel,
        name="router",
        grid=(nt,),
        in_specs=[
            pl.BlockSpec((ROUTER_TILE, D), lambda i: (i, 0)),
            pl.BlockSpec((1, D), const),
            pl.BlockSpec((ROUTER_ROWS, D), const),
            pl.BlockSpec((ROUTER_ROWS, 1), const),
        ],
        out_specs=[
            pl.BlockSpec((1, 8, ROUTER_TILE), lambda i: (i, 0, 0)),
            pl.BlockSpec((1, 8, ROUTER_TILE), lambda i: (i, 0, 0)),
            pl.BlockSpec((N_EXPERTS, 128), const),
        ],
        out_shape=[
            jax.ShapeDtypeStruct((nt, 8, ROUTER_TILE), jnp.int32),
            jax.ShapeDtypeStruct((nt, 8, ROUTER_TILE), F32),
            jax.ShapeDtypeStruct((N_EXPERTS, 128), jnp.int32),
        ],
        scratch_shapes=[pltpu.VMEM((N_EXPERTS, 1), F32)],
        compiler_params=_params(1),
    )(x2d, g.reshape(1, D), wr, br)


SUBLANES = 8
SLOTS_PER_TILE = SUBLANES * TOP_K


def _issue_rows(n_rows, dest_ref, first_tile, make_copy):
    def body(j, c):
        for u in range(SUBLANES):
            for k in range(TOP_K):
                slot = u * TOP_K + k
                make_copy(j, u, k, dest_ref[0, first_tile + j, slot]).start(priority=slot % 2)
        return c

    lax.fori_loop(0, n_rows // SUBLANES, body, 0)


def _issue_rows_inline(n_rows, dest_ref, first_tile, make_copy):
    for j in range(n_rows // SUBLANES):
        for u in range(SUBLANES):
            for k in range(TOP_K):
                slot = u * TOP_K + k
                make_copy(j, u, k, dest_ref[0, first_tile + j, slot]).start(priority=slot % 2)


def _tile_rows(x):
    return x.reshape(x.shape[0] // SUBLANES, SUBLANES, x.shape[1])


def _wait_rows(buf_ref, sem):
    pltpu.make_async_copy(buf_ref, buf_ref, sem).wait()


def _dispatch_kernel(zrow_ref, dest_ref, x_ref, g_ref, xs_ref, hbuf, zbuf, zsem, sem):
    @pl.when(pl.program_id(0) == 0)
    def _():
        zbuf[...] = jnp.zeros_like(zbuf)

        def zcopy(e):
            return pltpu.make_async_copy(zbuf, xs_ref.at[pl.ds(zrow_ref[e], MOE_BLOCK)], zsem)

        def zstart(e, c):
            @pl.when(zrow_ref[e] >= 0)
            def _():
                zcopy(e).start()
            return c

        def zwait(e, c):
            @pl.when(zrow_ref[e] >= 0)
            def _():
                zcopy(e).wait()
            return c

        lax.fori_loop(0, 2 * N_EXPERTS, zstart, 0)
        lax.fori_loop(0, 2 * N_EXPERTS, zwait, 0)

    i = pl.program_id(0)
    slot = i % 2
    x = x_ref[...]
    hbuf[slot] = _tile_rows(x * _rms_scale(x) * g_ref[...])

    def row_copy(tile, sub, k, dest):
        return pltpu.make_async_copy(hbuf.at[slot, tile, pl.ds(sub, 1), :], xs_ref.at[dest],
                                     sem.at[slot])

    _issue_rows(ROUTER_TILE, dest_ref, 0, row_copy)

    def drain(which):
        for _ in range(TOP_K):
            _wait_rows(hbuf.at[which], sem.at[which])

    @pl.when(i > 0)
    def _():
        drain(1 - slot)

    @pl.when(i == pl.num_programs(0) - 1)
    def _():
        drain(slot)


def _dispatch(x2d, g, dest, zrow, n_pad):
    t = x2d.shape[0]
    nt = t // ROUTER_TILE
    return pl.pallas_call(
        _dispatch_kernel,
        name="dispatch",
        grid_spec=pltpu.PrefetchScalarGridSpec(
            num_scalar_prefetch=1,
            grid=(nt,),
            in_specs=[
                pl.BlockSpec((1, ROUTER_TILE // SUBLANES, SLOTS_PER_TILE),
                             lambda i, z: (i, 0, 0), memory_space=pltpu.SMEM),
                pl.BlockSpec((ROUTER_TILE, D), lambda i, z: (i, 0)),
                pl.BlockSpec((1, D), lambda i, z: (0, 0)),
            ],
            out_specs=pl.BlockSpec(memory_space=pl.ANY),
            scratch_shapes=[
                pltpu.VMEM((2, ROUTER_TILE // SUBLANES, SUBLANES, D), F32),
                pltpu.VMEM((MOE_BLOCK, 1, D), F32),
                pltpu.SemaphoreType.DMA(()),
                pltpu.SemaphoreType.DMA((2,)),
            ],
        ),
        out_shape=jax.ShapeDtypeStruct((n_pad, 1, D), F32),
        compiler_params=_params(1),
    )(zrow, dest, x2d, g.reshape(1, D))


def _experts_kernel(be_ref, nxt_ref, nu_ref, xs_ref, wg_ref, wu_ref, wd_ref, y_ref,
                    wg_f, wu_f, wd_f, wg_b, wu_b, wd_b, xbuf, ybuf, w_sem, in_sem, out_sem, *, layer):
    i = pl.program_id(0)
    n_used = nu_ref[0]
    slot = i % 2

    def block_rows(ref, blk):
        return ref.at[pl.ds(pl.multiple_of(blk * MOE_BLOCK, MOE_BLOCK), MOE_BLOCK), 0]

    def fetch(blk, into):
        return pltpu.make_async_copy(block_rows(xs_ref, blk), xbuf.at[into], in_sem.at[into])

    def write_back(blk, from_):
        return pltpu.make_async_copy(ybuf.at[from_], block_rows(y_ref, blk), out_sem.at[from_])

    def weight_copies(e):
        return [pltpu.make_async_copy(src.at[layer, e], dst, w_sem.at[n])
                for n, (src, dst) in enumerate(((wg_ref, wg_f), (wu_ref, wu_f), (wd_ref, wd_f)))]

    @pl.when(i == 0)
    def _():
        for c in weight_copies(be_ref[0]):
            c.start()
        fetch(0, 0).start()

    @pl.when(i + 1 < n_used)
    def _():
        fetch(i + 1, 1 - slot).start()

    prev = be_ref[jnp.maximum(i - 1, 0)]

    @pl.when((i < n_used) & ((i == 0) | (be_ref[i] != prev)))
    def _():
        for c in weight_copies(be_ref[i]):
            c.wait()
        wg_b[...] = wg_f[...].astype(BF16)
        wu_b[...] = wu_f[...].astype(BF16)
        wd_b[...] = wd_f[...].astype(BF16)

        @pl.when(nxt_ref[i] >= 0)
        def _():
            for c in weight_copies(nxt_ref[i]):
                c.start()

    @pl.when(i >= 2)
    def _():
        write_back(i - 2, slot).wait()

    @pl.when(i < n_used)
    def _():
        fetch(i, slot).wait()
        h = xbuf[slot].astype(BF16)
        a = jnp.dot(h, wg_b[...], preferred_element_type=F32)
        u = jnp.dot(h, wu_b[...], preferred_element_type=F32)
        hid = (a * _sigmoid(a) * u).astype(BF16)
        ybuf[slot] = jnp.dot(hid, wd_b[...], preferred_element_type=F32)

    @pl.when(i >= n_used)
    def _():
        ybuf[slot] = jnp.zeros((MOE_BLOCK, D), F32)

    write_back(i, slot).start()

    @pl.when(i == pl.num_programs(0) - 1)
    def _():
        write_back(i - 1, 1 - slot).wait()
        write_back(i, slot).wait()


def _experts(xs, block_e, next_e, n_used, layer, w_gate, w_up, w_down):
    n_pad = xs.shape[0]
    n_blocks = n_pad // MOE_BLOCK
    hbm = pl.BlockSpec(memory_space=pl.ANY)
    return pl.pallas_call(
        functools.partial(_experts_kernel, layer=layer),
        name="experts",
        grid_spec=pltpu.PrefetchScalarGridSpec(
            num_scalar_prefetch=3,
            grid=(n_blocks,),
            in_specs=[hbm, hbm, hbm, hbm],
            out_specs=hbm,
            scratch_shapes=[
                pltpu.VMEM((D, D_EXPERT), F32),
                pltpu.VMEM((D, D_EXPERT), F32),
                pltpu.VMEM((D_EXPERT, D), F32),
                pltpu.VMEM((D, D_EXPERT), BF16),
                pltpu.VMEM((D, D_EXPERT), BF16),
                pltpu.VMEM((D_EXPERT, D), BF16),
                pltpu.VMEM((2, MOE_BLOCK, D), F32),
                pltpu.VMEM((2, MOE_BLOCK, D), F32),
                pltpu.SemaphoreType.DMA((3,)),
                pltpu.SemaphoreType.DMA((2,)),
                pltpu.SemaphoreType.DMA((2,)),
            ],
        ),
        out_shape=jax.ShapeDtypeStruct(xs.shape, F32),
        compiler_params=_params(1),
    )(block_e, next_e, n_used, xs, w_gate, w_up, w_down)


def _combine_kernel(dest_ref, dest_next_ref, x_ref, gate_ref, y_ref, p_ref, gple_ref, wp_ref,
                    wg_ref, gfin_ref, out_ref, ybuf0, ybuf1, sem, *, final):
    i = pl.program_id(0)
    tiles = COMBINE_TILE // SUBLANES
    ybuf = (ybuf0, ybuf1)

    def row_copy(into):
        def make(tile, sub, k, dest):
            return pltpu.make_async_copy(y_ref.at[dest], ybuf[into].at[k, tile, pl.ds(sub, 1), :],
                                         sem.at[into])
        return make

    def combine(half, slot):
        rows = slice(half * COMBINE_TILE, (half + 1) * COMBINE_TILE)
        gates = gate_ref[rows, :]
        y0 = ybuf[slot][0].reshape(COMBINE_TILE, D)
        y1 = ybuf[slot][1].reshape(COMBINE_TILE, D)
        x = x_ref[rows, :] + gates[:, 0:1] * y0 + gates[:, 1:2] * y1
        h = (x * _rms_scale(x) * gple_ref[...]).astype(BF16)
        gate = _sigmoid(jnp.dot(h, wg_ref[...], preferred_element_type=F32))
        proj = jnp.dot(p_ref[0, rows, :].astype(BF16), wp_ref[...], preferred_element_type=F32)
        x = x + proj * gate
        if final:
            x = x * _rms_scale(x) * gfin_ref[...]
        out_ref[rows, :] = x

    def wait(slot):
        for k in range(TOP_K):
            _wait_rows(ybuf[slot].at[k], sem.at[slot])

    @pl.when(i == 0)
    def _():
        _issue_rows(COMBINE_TILE, dest_ref, 0, row_copy(0))

    wait(0)
    _issue_rows_inline(COMBINE_TILE, dest_ref, tiles, row_copy(1))
    combine(0, 0)
    wait(1)
    _issue_rows_inline(COMBINE_TILE, dest_next_ref, 0, row_copy(0))
    combine(1, 1)

    @pl.when(i == pl.num_programs(0) - 1)
    def _():
        wait(0)


def _combine_ple(x2d, dest, gates, y, p3d, layer, g_ple, w_proj, w_gate, g_final, final):
    t = x2d.shape[0]
    step = 2 * COMBINE_TILE
    nt = t // step
    dest_blk = (1, step // SUBLANES, SLOTS_PER_TILE)
    const = lambda i: (0, 0)
    return pl.pallas_call(
        functools.partial(_combine_kernel, final=final),
        name="combine_final" if final else "combine",
        grid=(nt,),
        in_specs=[
            pl.BlockSpec(dest_blk, lambda i: (i, 0, 0), memory_space=pltpu.SMEM),
            pl.BlockSpec(dest_blk, lambda i: (jnp.minimum(i + 1, nt - 1), 0, 0),
                         memory_space=pltpu.SMEM),
            pl.BlockSpec((step, D), lambda i: (i, 0)),
            pl.BlockSpec((step, TOP_K), lambda i: (i, 0)),
            pl.BlockSpec(memory_space=pl.ANY),
            pl.BlockSpec((1, step, D_PLE), lambda i: (layer, i, 0)),
            pl.BlockSpec((1, D), const),
            pl.BlockSpec((D_PLE, D), const, pipeline_mode=pl.Buffered(1)),
            pl.BlockSpec((D, D), const, pipeline_mode=pl.Buffered(1)),
            pl.BlockSpec((1, D), const),
        ],
        out_specs=pl.BlockSpec((step, D), lambda i: (i, 0)),
        out_shape=jax.ShapeDtypeStruct((t, D), F32),
        scratch_shapes=[
            pltpu.VMEM((TOP_K, COMBINE_TILE // SUBLANES, SUBLANES, D), F32),
            pltpu.VMEM((TOP_K, COMBINE_TILE // SUBLANES, SUBLANES, D), F32),
            pltpu.SemaphoreType.DMA((2,)),
        ],
        compiler_params=_params(1),
    )(dest, dest, x2d, gates, y, p3d, g_ple.reshape(1, D), w_proj.astype(BF16),
      w_gate.astype(BF16), g_final.reshape(1, D))


def _moe_ple(x, p, layer, norm_ffn, w_group, b_group, w_expert, b_expert, w_gate, w_up, w_down,
             norm_ple, ple_w_proj, ple_w_gate, final_norm, final):
    b, s, _ = x.shape
    t = b * s
    x2d = x.reshape(t, D)
    ids, gates, counts = _router(x2d, norm_ffn, w_group, b_group, w_expert, b_expert)

    counts = counts[:, 0]
    padded = (counts + MOE_BLOCK - 1) // MOE_BLOCK * MOE_BLOCK
    pad_end = jnp.cumsum(padded)
    pad_start = pad_end - padded
    n_blocks = t * TOP_K // MOE_BLOCK + N_EXPERTS
    n_pad = n_blocks * MOE_BLOCK
    e = ids[:, 0:TOP_K, :].transpose(1, 0, 2).reshape(TOP_K, t)
    r = ids[:, TOP_K:2 * TOP_K, :].transpose(1, 0, 2).reshape(TOP_K, t)
    experts = jnp.arange(N_EXPERTS, dtype=jnp.int32)
    dest = r + jnp.sum(jnp.where(e[..., None] == experts, pad_start, 0), axis=-1)
    gate_cols = gates[:, 0:TOP_K, :].transpose(0, 2, 1).reshape(t, TOP_K)
    block_row = jnp.arange(n_blocks, dtype=jnp.int32) * MOE_BLOCK
    block_e = jnp.minimum(jnp.sum(pad_end[None, :] <= block_row[:, None], axis=-1),
                          N_EXPERTS - 1).astype(jnp.int32)
    n_used = (pad_end[-1:] // MOE_BLOCK).astype(jnp.int32)
    tail = (n_used[0] + jnp.arange(N_EXPERTS, dtype=jnp.int32)) * MOE_BLOCK
    zrow = jnp.concatenate([jnp.where(padded > 0, pad_end - MOE_BLOCK, -1),
                            jnp.where(tail < n_pad, tail, -1)]).astype(jnp.int32)

    def tiles(a, tile):
        return a.T.reshape(t // tile, tile // SUBLANES, SLOTS_PER_TILE).astype(jnp.int32)

    xs = _dispatch(x2d, norm_ffn, tiles(dest, ROUTER_TILE), zrow, n_pad)
    later = jnp.where((experts[None, :] > block_e[:, None]) & (padded[None, :] > 0), experts[None, :],
                      N_EXPERTS)
    next_e = jnp.min(later, axis=-1)
    next_e = jnp.where(next_e < N_EXPERTS, next_e, -1).astype(jnp.int32)
    y = _experts(xs, block_e, next_e, n_used, layer, w_gate, w_up, w_down)
    out = _combine_ple(x2d, tiles(dest, 2 * COMBINE_TILE), gate_cols, y, p.reshape(-1, t, D_PLE), layer,
                       norm_ple, ple_w_proj, ple_w_gate, final_norm, final)
    return out.reshape(b, s, D)


def _qkv_kernel(x_ref, gq_ref, gkv_ref, wq_ref, wkv_ref, qt_ref, k_ref, vt_ref):
    x = x_ref[0]
    xn = x * _rms_scale(x)
    hq = (xn * gq_ref[...]).astype(BF16)
    hkv = (xn * gkv_ref[...]).astype(BF16)
    q = jnp.dot(hq, wq_ref[...], preferred_element_type=F32) * ATTN_SCALE
    kv = jnp.dot(hkv, wkv_ref[...], preferred_element_type=F32)
    qt_ref[0] = q.T.astype(BF16)
    k_ref[0] = kv[:, :D].astype(BF16)
    vt_ref[0] = kv[:, D:].T.astype(BF16)


def _qkv(x, g_q, g_kv, w_q, w_kv):
    b, s, _ = x.shape
    const = lambda bi, si: (0, 0)
    row_major = pl.BlockSpec((1, SEQ_TILE, D), lambda bi, si: (bi, si, 0))
    feat_major = pl.BlockSpec((1, D, SEQ_TILE), lambda bi, si: (bi, 0, si))
    return pl.pallas_call(
        _qkv_kernel,
        name="qkv",
        grid=(b, s // SEQ_TILE),
        in_specs=[
            row_major,
            pl.BlockSpec((1, D), const),
            pl.BlockSpec((1, D), const),
            pl.BlockSpec((D, D), const),
            pl.BlockSpec((D, 2 * D), const),
        ],
        out_specs=[feat_major, row_major, feat_major],
        out_shape=[jax.ShapeDtypeStruct((b, D, s), BF16),
                   jax.ShapeDtypeStruct((b, s, D), BF16),
                   jax.ShapeDtypeStruct((b, D, s), BF16)],
        compiler_params=_params(2),
    )(x, g_q.reshape(1, D), g_kv.reshape(1, D), w_q.astype(BF16), w_kv.astype(BF16))


Q_GROUP = 2 * CHUNK
G_BAND = BAND + CHUNK
PAIR = 2 * B_HEAD_DIM


def _attn_kernel(qt_ref, kp_ref, kc_ref, vtp_ref, vtc_ref, bias_ref, x_ref, wo_ref, out_ref, o_scr,
                 s_scr):
    drow = lax.broadcasted_iota(jnp.int32, (PAIR, Q_GROUP), 0)
    first_head = drow < B_HEAD_DIM

    def attend(first_tile):
        units = [(g, pr) for g in range(SEQ_TILE // Q_GROUP) for pr in range(B_HEADS // 2)]

        def geometry(g):
            w0 = g * Q_GROUP
            n_prev = SEQ_TILE - w0
            return w0, n_prev, G_BAND - n_prev

        def scores(unit, slot):
            g, pr = unit
            w0, n_prev, n_cur = geometry(g)
            feat = slice(pr * PAIR, (pr + 1) * PAIR)
            qt = qt_ref[0, feat, w0:w0 + Q_GROUP]
            zero = jnp.zeros_like(qt)
            qblk = jnp.concatenate([jnp.where(first_head, qt, zero),
                                    jnp.where(first_head, zero, qt)], axis=1)
            if first_tile:
                s_scr[slot, n_prev:, :] = (jnp.dot(kc_ref[0, :n_cur, feat], qblk,
                                                   preferred_element_type=F32)
                                           + bias_ref[pr, n_prev:, :])
            else:
                kb = jnp.concatenate([kp_ref[0, w0:, feat], kc_ref[0, :n_cur, feat]], axis=0)
                s_scr[slot] = jnp.dot(kb, qblk, preferred_element_type=F32) + bias_ref[pr]

        def values(unit, slot):
            g, pr = unit
            w0, n_prev, n_cur = geometry(g)
            feat = slice(pr * PAIR, (pr + 1) * PAIR)
            if first_tile:
                s = s_scr[slot, n_prev:, :]
                vt = vtc_ref[0, feat, :n_cur]
            else:
                s = s_scr[slot]
                vt = jnp.concatenate([vtp_ref[0, feat, w0:], vtc_ref[0, feat, :n_cur]], axis=1)
            m = jnp.max(s, axis=0, keepdims=True)
            p = jnp.exp(s - m)
            inv = 1.0 / jnp.sum(p, axis=0, keepdims=True)
            ot = jnp.dot(vt, p.astype(BF16), preferred_element_type=F32)
            ot = jnp.where(first_head, ot[:, :Q_GROUP] * inv[:, :Q_GROUP],
                           ot[:, Q_GROUP:] * inv[:, Q_GROUP:])
            o_scr[w0:w0 + Q_GROUP, feat] = ot.T.astype(BF16)

        scores(units[0], 0)
        for n, unit in enumerate(units):
            if n + 1 < len(units):
                scores(units[n + 1], (n + 1) % 2)
            values(unit, n % 2)

    @pl.when(pl.program_id(1) == 0)
    def _():
        attend(True)

    @pl.when(pl.program_id(1) > 0)
    def _():
        attend(False)

    out_ref[0] = x_ref[0] + jnp.dot(o_scr[...], wo_ref[...], preferred_element_type=F32)


def _attn(x, qt, k, vt, bias_t, w_o):
    b, s, _ = x.shape
    cur = lambda bi, si: (bi, si, 0)
    prev = lambda bi, si: (bi, jnp.maximum(si - 1, 0), 0)
    cur_t = lambda bi, si: (bi, 0, si)
    prev_t = lambda bi, si: (bi, 0, jnp.maximum(si - 1, 0))
    blk = (1, SEQ_TILE, D)
    blk_t = (1, D, SEQ_TILE)
    return pl.pallas_call(
        _attn_kernel,
        name="attn",
        grid=(b, s // SEQ_TILE),
        in_specs=[
            pl.BlockSpec(blk_t, cur_t),
            pl.BlockSpec(blk, prev),
            pl.BlockSpec(blk, cur),
            pl.BlockSpec(blk_t, prev_t),
            pl.BlockSpec(blk_t, cur_t),
            pl.BlockSpec((B_HEADS // 2, G_BAND, 2 * Q_GROUP), lambda bi, si: (0, 0, 0),
                         pipeline_mode=pl.Buffered(1)),
            pl.BlockSpec(blk, cur),
            pl.BlockSpec((D, D), lambda bi, si: (0, 0), pipeline_mode=pl.Buffered(1)),
        ],
        out_specs=pl.BlockSpec(blk, cur),
        out_shape=jax.ShapeDtypeStruct(x.shape, F32),
        scratch_shapes=[pltpu.VMEM((SEQ_TILE, D), BF16),
                        pltpu.VMEM((2, G_BAND, 2 * Q_GROUP), F32)],
        compiler_params=_params(2),
    )(qt, k, k, vt, vt, bias_t, x, w_o.astype(BF16))


def _group_bias(table):
    band = _band_bias(table)
    pad = lambda lo, hi: jnp.pad(band, ((0, 0), (0, 0), (lo, hi)), constant_values=NEG_INF)
    both = jnp.concatenate([pad(0, CHUNK), pad(CHUNK, 0)], axis=1)
    both = both.reshape(B_HEADS // 2, 2, Q_GROUP, G_BAND)
    return both.transpose(0, 3, 1, 2).reshape(B_HEADS // 2, G_BAND, 2 * Q_GROUP)


def _band_bias(table):
    n_rel = REL_MAX - REL_MIN + 1
    span = BAND + CHUNK - 1
    head = jnp.broadcast_to(table[:, n_rel - 1:], (table.shape[0], span - n_rel))
    ext = jnp.concatenate([head, table[:, ::-1]], axis=1)
    rows = [ext[:, CHUNK - 1 - q:CHUNK - 1 - q + BAND] for q in range(CHUNK)]
    return jnp.stack(rows, axis=1)


def kernel(x, p, a_w_in, a_lb_logits, a_out_norm, a_w_o, kv_norm, w_kv, b_w_q, b_rel_bias, b_w_o,
           norm_mix, norm_ffn, norm_ple, moe_w_group, moe_b_group, moe_w_expert, moe_b_expert,
           moe_w_gate, moe_w_up, moe_w_down, ple_w_proj, ple_w_gate, final_norm):
    b, s, _ = x.shape
    lower_bounds = jnp.cumsum(jax.nn.softmax(a_lb_logits.astype(F32), axis=0), axis=0)

    def moe(xi, i, final):
        return _moe_ple(xi, p, i, norm_ffn[i], moe_w_group[i], moe_b_group[i], moe_w_expert[i],
                        moe_b_expert[i], moe_w_gate, moe_w_up, moe_w_down, norm_ple[i],
                        ple_w_proj[i], ple_w_gate[i], final_norm, final)

    x = _mixer_a(x, norm_mix[0], a_w_in[0], lower_bounds[0], a_out_norm[0], a_w_o[0])
    x = moe(x, 0, False)

    qt, k, vt = _qkv(x, norm_mix[1], kv_norm, b_w_q[0], w_kv)
    x = _attn(x, qt, k, vt, _group_bias(b_rel_bias[0].astype(F32)), b_w_o[0])
    x = moe(x, 1, True)
    return x
```

```python
import functools

import jax
import jax.numpy as jnp
from jax import lax
from jax.experimental import pallas as pl
from jax.experimental.pallas import tpu as pltpu

F32 = jnp.float32
BF16 = jnp.bfloat16

D = 1024
CHUNK = 64
A_HEADS = 8
A_HEAD_DIM = 128
B_HEADS = 16
B_HEAD_DIM = 64
LEFT_CHUNKS = 8
BAND = (LEFT_CHUNKS + 1) * CHUNK
REL_MIN = -(CHUNK - 1)
REL_MAX = 256
ATTN_SCALE = B_HEAD_DIM ** -0.5
N_GROUPS = 4
EXPERTS_PER_GROUP = 8
N_EXPERTS = 32
TOP_K = 2
D_EXPERT = 512
MOE_BLOCK = 512
D_PLE = 256
EPS = 1e-6
NEG_INF = -1e30
LOG2E = 1.4426950408889634

SEQ_TILE = 512
ROUTER_TILE = 512
COMBINE_TILE = 512
ROUTER_ROWS = 48
VMEM_LIMIT = 56 * 1024 * 1024


def _params(n_axes, vmem=VMEM_LIMIT):
    return pltpu.CompilerParams(dimension_semantics=("arbitrary",) * n_axes,
                                vmem_limit_bytes=vmem)


def _rms_scale(x):
    return lax.rsqrt(jnp.mean(x * x, axis=-1, keepdims=True) + EPS)


def _sigmoid(x):
    return 1.0 / (1.0 + jnp.exp(-x))


def _mixer_a_kernel(x_ref, g_ref, win_ref, lb_ref, onorm_ref, wo_ref, out_ref,
                    proj_scr, o_scr, state_scr):
    @pl.when(pl.program_id(1) == 0)
    def _():
        state_scr[...] = jnp.zeros_like(state_scr)

    x = x_ref[0]
    h = (x * _rms_scale(x) * g_ref[...]).astype(BF16)
    proj_scr[...] = jnp.dot(h, win_ref[...], preferred_element_type=F32)

    row = lax.broadcasted_iota(jnp.int32, (CHUNK, CHUNK), 0)
    col = lax.broadcasted_iota(jnp.int32, (CHUNK, CHUNK), 1)
    causal = row >= col
    tril = causal.astype(BF16)
    lb = lb_ref[...]
    onorm = onorm_ref[...]

    def chunk_body(c, carry):
        r0 = pl.multiple_of(c * CHUNK, CHUNK)
        rows = pl.ds(r0, CHUNK)
        f = lb + (1.0 - lb) * _sigmoid(proj_scr[rows, D:2 * D])
        logf = jnp.log(f)
        hi = logf.astype(BF16)
        lo = (logf - hi.astype(F32)).astype(BF16)
        g = (jnp.dot(tril, hi, preferred_element_type=F32)
             + jnp.dot(tril, lo, preferred_element_type=F32))
        for hd in range(A_HEADS):
            sl = slice(hd * A_HEAD_DIM, (hd + 1) * A_HEAD_DIM)
            gh = g[:, sl]
            g_last = gh[CHUNK - 1:CHUNK, :]
            k = 1.0 - f[:, sl]
            q_dec = (proj_scr[rows, sl] * jnp.exp(gh)).astype(BF16)
            k_inv = (k * jnp.exp(-gh)).astype(BF16)
            k_tail = (k * jnp.exp(g_last - gh)).astype(BF16)
            v = proj_scr[rows, 2 * D + hd * A_HEAD_DIM:2 * D + (hd + 1) * A_HEAD_DIM]
            v_b = v.astype(BF16)
            att = lax.dot_general(q_dec, k_inv, (((1,), (1,)), ((), ())),
                                  preferred_element_type=F32)
            att = jnp.where(causal, att, 0.0).astype(BF16)
            st = state_scr[hd]
            o = (jnp.dot(att, v_b, preferred_element_type=F32)
                 + lax.dot_general(q_dec, st.astype(BF16), (((1,), (1,)), ((), ())),
                                   preferred_element_type=F32))
            v_t = v.T.astype(BF16)
            state_scr[hd] = st * jnp.exp(g_last) + jnp.dot(v_t, k_tail,
                                                           preferred_element_type=F32)
            o = o * _rms_scale(o)
            og = proj_scr[rows, 3 * D + hd * A_HEAD_DIM:3 * D + (hd + 1) * A_HEAD_DIM]
            o = o * onorm[:, sl] * (og * _sigmoid(og))
            o_scr[rows, sl] = o.astype(BF16)
        return carry

    lax.fori_loop(0, SEQ_TILE // CHUNK, chunk_body, 0, unroll=True)
    out_ref[0] = x + jnp.dot(o_scr[...], wo_ref[...], preferred_element_type=F32)


def _mixer_a(x, g, w_in, lb, out_norm, w_o):
    b, s, _ = x.shape
    const = lambda bi, si: (0, 0)
    return pl.pallas_call(
        _mixer_a_kernel,
        name="mixer_a",
        grid=(b, s // SEQ_TILE),
        in_specs=[
            pl.BlockSpec((1, SEQ_TILE, D), lambda bi, si: (bi, si, 0)),
            pl.BlockSpec((1, D), const),
            pl.BlockSpec((D, 4 * D), const, pipeline_mode=pl.Buffered(1)),
            pl.BlockSpec((1, D), const),
            pl.BlockSpec((1, D), const),
            pl.BlockSpec((D, D), const, pipeline_mode=pl.Buffered(1)),
        ],
        out_specs=pl.BlockSpec((1, SEQ_TILE, D), lambda bi, si: (bi, si, 0)),
        out_shape=jax.ShapeDtypeStruct(x.shape, F32),
        scratch_shapes=[
            pltpu.VMEM((SEQ_TILE, 4 * D), F32),
            pltpu.VMEM((SEQ_TILE, D), BF16),
            pltpu.VMEM((A_HEADS, A_HEAD_DIM, A_HEAD_DIM), F32),
        ],
        compiler_params=_params(2),
    )(x, g.reshape(1, D), w_in.astype(BF16), lb.reshape(1, D), out_norm.reshape(1, D),
      w_o.astype(BF16))


def _router_kernel(x_ref, g_ref, wr_ref, br_ref, ids_ref, gates_ref, counts_ref, cnt_scr,
                   before_scr):
    tm = ROUTER_TILE

    @pl.when(pl.program_id(0) == 0)
    def _():
        cnt_scr[...] = jnp.zeros_like(cnt_scr)
        tr = lax.broadcasted_iota(jnp.int32, (tm, tm), 0)
        tc = lax.broadcasted_iota(jnp.int32, (tm, tm), 1)
        before_scr[...] = (tr < tc).astype(BF16)

    x = x_ref[...]
    h = x * _rms_scale(x) * g_ref[...]
    h_hi = h.astype(BF16)
    h_lo = (h - h_hi.astype(F32)).astype(BF16)
    nt = (((1,), (1,)), ((), ()))
    both = lax.dot_general(wr_ref[...], h_hi, nt, preferred_element_type=F32)
    cross = lax.dot_general(wr_ref[0:ROUTER_ROWS], h_lo, nt, preferred_element_type=F32)
    logits = both[0:ROUTER_ROWS] + both[ROUTER_ROWS:] + cross + br_ref[...]
    el = logits[0:N_EXPERTS]
    gl = logits[N_EXPERTS:ROUTER_ROWS]
    grow = lax.broadcasted_iota(jnp.int32, gl.shape, 0)
    gl = jnp.where(grow < N_GROUPS, gl, -jnp.inf)
    gmax = jnp.max(gl, axis=0, keepdims=True)
    gsum = jnp.sum(jnp.exp(gl - gmax), axis=0, keepdims=True)
    grp_w = 1.0 / gsum
    gidx = jnp.min(jnp.where(gl == gmax, grow, N_GROUPS), axis=0, keepdims=True)

    erow = lax.broadcasted_iota(jnp.int32, el.shape, 0)
    masked = jnp.where((erow // EXPERTS_PER_GROUP) == gidx, el, -jnp.inf)
    top1 = jnp.max(masked, axis=0, keepdims=True)
    i1 = jnp.min(jnp.where(masked == top1, erow, N_EXPERTS), axis=0, keepdims=True)
    masked2 = jnp.where(erow == i1, -jnp.inf, masked)
    top2 = jnp.max(masked2, axis=0, keepdims=True)
    i2 = jnp.min(jnp.where(masked2 == top2, erow, N_EXPERTS), axis=0, keepdims=True)
    e2 = jnp.exp(top2 - top1)
    denom = 1.0 + e2
    g1 = grp_w * (1.0 / denom)
    g2 = grp_w * (e2 / denom)

    sel1 = erow == i1
    sel2 = erow == i2
    onehot = (sel1 | sel2).astype(BF16)
    prefix = jnp.dot(onehot, before_scr[...], preferred_element_type=F32) + cnt_scr[...]
    r1 = jnp.sum(jnp.where(sel1, prefix, 0.0), axis=0, keepdims=True)
    r2 = jnp.sum(jnp.where(sel2, prefix, 0.0), axis=0, keepdims=True)
    cnt_scr[...] += jnp.sum(onehot.astype(F32), axis=1, keepdims=True)

    zi = jnp.zeros((4, tm), jnp.int32)
    ids_ref[0] = jnp.concatenate(
        [i1, i2, r1.astype(jnp.int32), r2.astype(jnp.int32), zi], axis=0)
    gates_ref[0] = jnp.concatenate([g1, g2, jnp.zeros((6, tm), F32)], axis=0)
    counts_ref[...] = jnp.broadcast_to(cnt_scr[...], counts_ref.shape).astype(jnp.int32)


def _router(x2d, g, w_group, b_group, w_expert, b_expert):
    t = x2d.shape[0]
    nt = t // ROUTER_TILE
    pad = ROUTER_ROWS - N_EXPERTS - N_GROUPS
    wr = jnp.concatenate([w_expert.T, w_group.T, jnp.zeros((pad, D), F32)], axis=0)
    wr_hi = wr.astype(BF16)
    wr_lo = (wr - wr_hi.astype(F32)).astype(BF16)
    br = jnp.concatenate([b_expert, b_group, jnp.zeros((pad,), F32)]).reshape(ROUTER_ROWS, 1)
    const = lambda i: (0, 0)
    return pl.pallas_call(
        _router_kernel,
        name="router",
        grid=(nt,),
        in_specs=[
            pl.BlockSpec((ROUTER_TILE, D), lambda i: (i, 0)),
            pl.BlockSpec((1, D), const),
            pl.BlockSpec((2 * ROUTER_ROWS, D), const),
            pl.BlockSpec((ROUTER_ROWS, 1), const),
        ],
        out_specs=[
            pl.BlockSpec((1, 8, ROUTER_TILE), lambda i: (i, 0, 0)),
            pl.BlockSpec((1, 8, ROUTER_TILE), lambda i: (i, 0, 0)),
            pl.BlockSpec((N_EXPERTS, 128), const),
        ],
        out_shape=[
            jax.ShapeDtypeStruct((nt, 8, ROUTER_TILE), jnp.int32),
            jax.ShapeDtypeStruct((nt, 8, ROUTER_TILE), F32),
            jax.ShapeDtypeStruct((N_EXPERTS, 128), jnp.int32),
        ],
        scratch_shapes=[pltpu.VMEM((N_EXPERTS, 1), F32),
                        pltpu.VMEM((ROUTER_TILE, ROUTER_TILE), BF16)],
        compiler_params=_params(1),
    )(x2d, g.reshape(1, D), jnp.concatenate([wr_hi, wr_lo], axis=0), br)


SUBLANES = 8
SLOTS_PER_TILE = SUBLANES * TOP_K


def _issue_rows(n_rows, dest_ref, first_tile, make_copy):
    def body(j, c):
        for u in range(SUBLANES):
            for k in range(TOP_K):
                slot = u * TOP_K + k
                make_copy(j, u, k, dest_ref[0, first_tile + j, slot]).start(priority=slot % 2)
        return c

    lax.fori_loop(0, n_rows // SUBLANES, body, 0)


def _issue_rows_inline(n_rows, dest_ref, first_tile, make_copy):
    for j in range(n_rows // SUBLANES):
        for u in range(SUBLANES):
            for k in range(TOP_K):
                slot = u * TOP_K + k
                make_copy(j, u, k, dest_ref[0, first_tile + j, slot]).start(priority=slot % 2)


def _tile_rows(x):
    return x.reshape(x.shape[0] // SUBLANES, SUBLANES, x.shape[1])


def _wait_rows(buf_ref, sem):
    pltpu.make_async_copy(buf_ref, buf_ref, sem).wait()


def _dispatch_kernel(zrow_ref, dest_ref, x_ref, g_ref, xs_ref, hbuf, zbuf, zsem, sem):
    @pl.when(pl.program_id(0) == 0)
    def _():
        zbuf[...] = jnp.zeros_like(zbuf)

        def zcopy(e):
            return pltpu.make_async_copy(zbuf, xs_ref.at[pl.ds(zrow_ref[e], MOE_BLOCK)], zsem)

        def zstart(e, c):
            @pl.when(zrow_ref[e] >= 0)
            def _():
                zcopy(e).start()
            return c

        def zwait(e, c):
            @pl.when(zrow_ref[e] >= 0)
            def _():
                zcopy(e).wait()
            return c

        lax.fori_loop(0, 2 * N_EXPERTS, zstart, 0)
        lax.fori_loop(0, 2 * N_EXPERTS, zwait, 0)

    i = pl.program_id(0)
    slot = i % 2
    x = x_ref[...]
    hbuf[slot] = _tile_rows(x * _rms_scale(x) * g_ref[...])

    def row_copy(tile, sub, k, dest):
        return pltpu.make_async_copy(hbuf.at[slot, tile, pl.ds(sub, 1), :], xs_ref.at[dest],
                                     sem.at[slot])

    _issue_rows(ROUTER_TILE, dest_ref, 0, row_copy)

    def drain(which):
        for _ in range(TOP_K):
            _wait_rows(hbuf.at[which], sem.at[which])

    @pl.when(i > 0)
    def _():
        drain(1 - slot)

    @pl.when(i == pl.num_programs(0) - 1)
    def _():
        drain(slot)


def _dispatch(x2d, g, dest, zrow, n_pad):
    t = x2d.shape[0]
    nt = t // ROUTER_TILE
    return pl.pallas_call(
        _dispatch_kernel,
        name="dispatch",
        grid_spec=pltpu.PrefetchScalarGridSpec(
            num_scalar_prefetch=1,
            grid=(nt,),
            in_specs=[
                pl.BlockSpec((1, ROUTER_TILE // SUBLANES, SLOTS_PER_TILE),
                             lambda i, z: (i, 0, 0), memory_space=pltpu.SMEM),
                pl.BlockSpec((ROUTER_TILE, D), lambda i, z: (i, 0)),
                pl.BlockSpec((1, D), lambda i, z: (0, 0)),
            ],
            out_specs=pl.BlockSpec(memory_space=pl.ANY),
            scratch_shapes=[
                pltpu.VMEM((2, ROUTER_TILE // SUBLANES, SUBLANES, D), F32),
                pltpu.VMEM((MOE_BLOCK, 1, D), F32),
                pltpu.SemaphoreType.DMA(()),
                pltpu.SemaphoreType.DMA((2,)),
            ],
        ),
        out_shape=jax.ShapeDtypeStruct((n_pad, 1, D), F32),
        compiler_params=_params(1),
    )(zrow, dest, x2d, g.reshape(1, D))


def _experts_kernel(be_ref, nxt_ref, nu_ref, xs_ref, wg_ref, wu_ref, wd_ref, y_ref,
                    wg_f, wu_f, wd_f, wg_b, wu_b, wd_b, xbuf, ybuf, w_sem, in_sem, out_sem, *, layer):
    i = pl.program_id(0)
    n_used = nu_ref[0]
    slot = i % 2

    def block_rows(ref, blk):
        return ref.at[pl.ds(pl.multiple_of(blk * MOE_BLOCK, MOE_BLOCK), MOE_BLOCK), 0]

    def fetch(blk, into):
        return pltpu.make_async_copy(block_rows(xs_ref, blk), xbuf.at[into], in_sem.at[into])

    def write_back(blk, from_):
        return pltpu.make_async_copy(ybuf.at[from_], block_rows(y_ref, blk), out_sem.at[from_])

    def weight_copies(e):
        return [pltpu.make_async_copy(src.at[layer, e], dst, w_sem.at[n])
                for n, (src, dst) in enumerate(((wg_ref, wg_f), (wu_ref, wu_f), (wd_ref, wd_f)))]

    @pl.when(i == 0)
    def _():
        for c in weight_copies(be_ref[0]):
            c.start()
        fetch(0, 0).start()

    @pl.when(i + 1 < n_used)
    def _():
        fetch(i + 1, 1 - slot).start()

    prev = be_ref[jnp.maximum(i - 1, 0)]

    @pl.when((i < n_used) & ((i == 0) | (be_ref[i] != prev)))
    def _():
        for c in weight_copies(be_ref[i]):
            c.wait()
        wg_b[...] = wg_f[...].astype(BF16)
        wu_b[...] = wu_f[...].astype(BF16)
        wd_b[...] = wd_f[...].astype(BF16)

        @pl.when(nxt_ref[i] >= 0)
        def _():
            for c in weight_copies(nxt_ref[i]):
                c.start()

    @pl.when(i >= 2)
    def _():
        write_back(i - 2, slot).wait()

    @pl.when(i < n_used)
    def _():
        fetch(i, slot).wait()
        h = xbuf[slot].astype(BF16)
        a = jnp.dot(h, wg_b[...], preferred_element_type=F32)
        u = jnp.dot(h, wu_b[...], preferred_element_type=F32)
        hid = (a * _sigmoid(a) * u).astype(BF16)
        ybuf[slot] = jnp.dot(hid, wd_b[...], preferred_element_type=F32)

    @pl.when(i >= n_used)
    def _():
        ybuf[slot] = jnp.zeros((MOE_BLOCK, D), F32)

    write_back(i, slot).start()

    @pl.when(i == pl.num_programs(0) - 1)
    def _():
        write_back(i - 1, 1 - slot).wait()
        write_back(i, slot).wait()


def _experts(xs, block_e, next_e, n_used, layer, w_gate, w_up, w_down):
    n_pad = xs.shape[0]
    n_blocks = n_pad // MOE_BLOCK
    hbm = pl.BlockSpec(memory_space=pl.ANY)
    return pl.pallas_call(
        functools.partial(_experts_kernel, layer=layer),
        name="experts",
        grid_spec=pltpu.PrefetchScalarGridSpec(
            num_scalar_prefetch=3,
            grid=(n_blocks,),
            in_specs=[hbm, hbm, hbm, hbm],
            out_specs=hbm,
            scratch_shapes=[
                pltpu.VMEM((D, D_EXPERT), F32),
                pltpu.VMEM((D, D_EXPERT), F32),
                pltpu.VMEM((D_EXPERT, D), F32),
                pltpu.VMEM((D, D_EXPERT), BF16),
                pltpu.VMEM((D, D_EXPERT), BF16),
                pltpu.VMEM((D_EXPERT, D), BF16),
                pltpu.VMEM((2, MOE_BLOCK, D), F32),
                pltpu.VMEM((2, MOE_BLOCK, D), F32),
                pltpu.SemaphoreType.DMA((3,)),
                pltpu.SemaphoreType.DMA((2,)),
                pltpu.SemaphoreType.DMA((2,)),
            ],
        ),
        out_shape=jax.ShapeDtypeStruct(xs.shape, F32),
        compiler_params=_params(1),
    )(block_e, next_e, n_used, xs, w_gate, w_up, w_down)


def _combine_kernel(dest_ref, dest_next_ref, x_ref, gate_ref, y_ref, p_ref, gple_ref, wp_ref,
                    wg_ref, gfin_ref, out_ref, ybuf0, ybuf1, sem, *, final):
    i = pl.program_id(0)
    tiles = COMBINE_TILE // SUBLANES
    ybuf = (ybuf0, ybuf1)

    def row_copy(into):
        def make(tile, sub, k, dest):
            return pltpu.make_async_copy(y_ref.at[dest], ybuf[into].at[k, tile, pl.ds(sub, 1), :],
                                         sem.at[into])
        return make

    def combine(half, slot):
        rows = slice(half * COMBINE_TILE, (half + 1) * COMBINE_TILE)
        gates = gate_ref[rows, :]
        y0 = ybuf[slot][0].reshape(COMBINE_TILE, D)
        y1 = ybuf[slot][1].reshape(COMBINE_TILE, D)
        x = x_ref[rows, :] + gates[:, 0:1] * y0 + gates[:, 1:2] * y1
        h = (x * _rms_scale(x) * gple_ref[...]).astype(BF16)
        gate = _sigmoid(jnp.dot(h, wg_ref[...], preferred_element_type=F32))
        proj = jnp.dot(p_ref[0, rows, :].astype(BF16), wp_ref[...], preferred_element_type=F32)
        x = x + proj * gate
        if final:
            x = x * _rms_scale(x) * gfin_ref[...]
        out_ref[rows, :] = x

    def wait(slot):
        for k in range(TOP_K):
            _wait_rows(ybuf[slot].at[k], sem.at[slot])

    @pl.when(i == 0)
    def _():
        _issue_rows(COMBINE_TILE, dest_ref, 0, row_copy(0))

    wait(0)
    _issue_rows_inline(COMBINE_TILE, dest_ref, tiles, row_copy(1))
    combine(0, 0)
    wait(1)
    _issue_rows_inline(COMBINE_TILE, dest_next_ref, 0, row_copy(0))
    combine(1, 1)

    @pl.when(i == pl.num_programs(0) - 1)
    def _():
        wait(0)


def _combine_ple(x2d, dest, gates, y, p3d, layer, g_ple, w_proj, w_gate, g_final, final):
    t = x2d.shape[0]
    step = 2 * COMBINE_TILE
    nt = t // step
    dest_blk = (1, step // SUBLANES, SLOTS_PER_TILE)
    const = lambda i: (0, 0)
    return pl.pallas_call(
        functools.partial(_combine_kernel, final=final),
        name="combine_final" if final else "combine",
        grid=(nt,),
        in_specs=[
            pl.BlockSpec(dest_blk, lambda i: (i, 0, 0), memory_space=pltpu.SMEM),
            pl.BlockSpec(dest_blk, lambda i: (jnp.minimum(i + 1, nt - 1), 0, 0),
                         memory_space=pltpu.SMEM),
            pl.BlockSpec((step, D), lambda i: (i, 0)),
            pl.BlockSpec((step, TOP_K), lambda i: (i, 0)),
            pl.BlockSpec(memory_space=pl.ANY),
            pl.BlockSpec((1, step, D_PLE), lambda i: (layer, i, 0)),
            pl.BlockSpec((1, D), const),
            pl.BlockSpec((D_PLE, D), const, pipeline_mode=pl.Buffered(1)),
            pl.BlockSpec((D, D), const, pipeline_mode=pl.Buffered(1)),
            pl.BlockSpec((1, D), const),
        ],
        out_specs=pl.BlockSpec((step, D), lambda i: (i, 0)),
        out_shape=jax.ShapeDtypeStruct((t, D), F32),
        scratch_shapes=[
            pltpu.VMEM((TOP_K, COMBINE_TILE // SUBLANES, SUBLANES, D), F32),
            pltpu.VMEM((TOP_K, COMBINE_TILE // SUBLANES, SUBLANES, D), F32),
            pltpu.SemaphoreType.DMA((2,)),
        ],
        compiler_params=_params(1),
    )(dest, dest, x2d, gates, y, p3d, g_ple.reshape(1, D), w_proj.astype(BF16),
      w_gate.astype(BF16), g_final.reshape(1, D))


def _moe_ple(x, p, layer, norm_ffn, w_group, b_group, w_expert, b_expert, w_gate, w_up, w_down,
             norm_ple, ple_w_proj, ple_w_gate, final_norm, final):
    b, s, _ = x.shape
    t = b * s
    x2d = x.reshape(t, D)
    ids, gates, counts = _router(x2d, norm_ffn, w_group, b_group, w_expert, b_expert)

    counts = counts[:, 0]
    padded = (counts + MOE_BLOCK - 1) // MOE_BLOCK * MOE_BLOCK
    pad_end = jnp.cumsum(padded)
    pad_start = pad_end - padded
    n_blocks = t * TOP_K // MOE_BLOCK + N_EXPERTS
    n_pad = n_blocks * MOE_BLOCK
    e = ids[:, 0:TOP_K, :].transpose(1, 0, 2).reshape(TOP_K, t)
    r = ids[:, TOP_K:2 * TOP_K, :].transpose(1, 0, 2).reshape(TOP_K, t)
    experts = jnp.arange(N_EXPERTS, dtype=jnp.int32)
    dest = r + jnp.sum(jnp.where(e[..., None] == experts, pad_start, 0), axis=-1)
    gate_cols = gates[:, 0:TOP_K, :].transpose(0, 2, 1).reshape(t, TOP_K)
    block_row = jnp.arange(n_blocks, dtype=jnp.int32) * MOE_BLOCK
    block_e = jnp.minimum(jnp.sum(pad_end[None, :] <= block_row[:, None], axis=-1),
                          N_EXPERTS - 1).astype(jnp.int32)
    n_used = (pad_end[-1:] // MOE_BLOCK).astype(jnp.int32)
    tail = (n_used[0] + jnp.arange(N_EXPERTS, dtype=jnp.int32)) * MOE_BLOCK
    zrow = jnp.concatenate([jnp.where(padded > 0, pad_end - MOE_BLOCK, -1),
                            jnp.where(tail < n_pad, tail, -1)]).astype(jnp.int32)

    def tiles(a, tile):
        return a.T.reshape(t // tile, tile // SUBLANES, SLOTS_PER_TILE).astype(jnp.int32)

    xs = _dispatch(x2d, norm_ffn, tiles(dest, ROUTER_TILE), zrow, n_pad)
    later = jnp.where((experts[None, :] > block_e[:, None]) & (padded[None, :] > 0), experts[None, :],
                      N_EXPERTS)
    next_e = jnp.min(later, axis=-1)
    next_e = jnp.where(next_e < N_EXPERTS, next_e, -1).astype(jnp.int32)
    y = _experts(xs, block_e, next_e, n_used, layer, w_gate, w_up, w_down)
    out = _combine_ple(x2d, tiles(dest, 2 * COMBINE_TILE), gate_cols, y, p.reshape(-1, t, D_PLE), layer,
                       norm_ple, ple_w_proj, ple_w_gate, final_norm, final)
    return out.reshape(b, s, D)


def _qkv_kernel(x_ref, gq_ref, gkv_ref, wq_ref, wkv_ref, qt_ref, k_ref, vt_ref):
    x = x_ref[0]
    xn = x * _rms_scale(x)
    hq = (xn * gq_ref[...]).astype(BF16)
    hkv = (xn * gkv_ref[...]).astype(BF16)
    q = jnp.dot(hq, wq_ref[...], preferred_element_type=F32) * (ATTN_SCALE * LOG2E)
    kv = jnp.dot(hkv, wkv_ref[...], preferred_element_type=F32)
    qt_ref[0] = q.T.astype(BF16)
    k_ref[0] = kv[:, :D].astype(BF16)
    vt_ref[0] = kv[:, D:].T.astype(BF16)


def _qkv(x, g_q, g_kv, w_q, w_kv):
    b, s, _ = x.shape
    const = lambda bi, si: (0, 0)
    row_major = pl.BlockSpec((1, SEQ_TILE, D), lambda bi, si: (bi, si, 0))
    feat_major = pl.BlockSpec((1, D, SEQ_TILE), lambda bi, si: (bi, 0, si))
    return pl.pallas_call(
        _qkv_kernel,
        name="qkv",
        grid=(b, s // SEQ_TILE),
        in_specs=[
            row_major,
            pl.BlockSpec((1, D), const),
            pl.BlockSpec((1, D), const),
            pl.BlockSpec((D, D), const),
            pl.BlockSpec((D, 2 * D), const),
        ],
        out_specs=[feat_major, row_major, feat_major],
        out_shape=[jax.ShapeDtypeStruct((b, D, s), BF16),
                   jax.ShapeDtypeStruct((b, s, D), BF16),
                   jax.ShapeDtypeStruct((b, D, s), BF16)],
        compiler_params=_params(2),
    )(x, g_q.reshape(1, D), g_kv.reshape(1, D), w_q.astype(BF16), w_kv.astype(BF16))


Q_GROUP = 2 * CHUNK
G_BAND = BAND + CHUNK
PAIR = 2 * B_HEAD_DIM
ONES_ROWS = 16


def _attn_kernel(qt_ref, kp_ref, kc_ref, vtp_ref, vtc_ref, bias_ref, x_ref, wo_ref, out_ref, o_scr,
                 s_scr0, s_scr1, p_scr0, p_scr1):
    drow = lax.broadcasted_iota(jnp.int32, (PAIR, Q_GROUP), 0)
    first_head = drow < B_HEAD_DIM
    s_scr = (s_scr0, s_scr1)
    p_scr = (p_scr0, p_scr1)

    def attend(first_tile):
        units = [(g, pr) for g in range(SEQ_TILE // Q_GROUP) for pr in range(B_HEADS // 2)]

        def geometry(g):
            w0 = g * Q_GROUP
            n_prev = SEQ_TILE - w0
            return w0, n_prev, G_BAND - n_prev

        def keys(g):
            return slice(geometry(g)[1], None) if first_tile else slice(None)

        def scores(unit, s_ref):
            g, pr = unit
            w0, n_prev, n_cur = geometry(g)
            feat = slice(pr * PAIR, (pr + 1) * PAIR)
            qt = qt_ref[0, feat, w0:w0 + Q_GROUP]
            zero = jnp.zeros_like(qt)
            qblk = jnp.concatenate([jnp.where(first_head, qt, zero),
                                    jnp.where(first_head, zero, qt)], axis=1)
            if first_tile:
                kb = kc_ref[0, :n_cur, feat]
            else:
                kb = jnp.concatenate([kp_ref[0, w0:, feat], kc_ref[0, :n_cur, feat]], axis=0)
            s_ref[keys(g), :] = (jnp.dot(kb, qblk, preferred_element_type=F32)
                                 + bias_ref[pr, keys(g), :])

        def weights(unit, s_ref, p_ref):
            g, _ = unit
            s = s_ref[keys(g), :]
            m = jnp.max(s, axis=0, keepdims=True)
            p_ref[keys(g), :] = jnp.exp2(s - m).astype(BF16)

        def values(unit, p_ref):
            g, pr = unit
            w0, n_prev, n_cur = geometry(g)
            feat = slice(pr * PAIR, (pr + 1) * PAIR)
            if first_tile:
                vt = vtc_ref[0, feat, :n_cur]
            else:
                vt = jnp.concatenate([vtp_ref[0, feat, w0:], vtc_ref[0, feat, :n_cur]], axis=1)
            ones = jnp.ones((ONES_ROWS, vt.shape[1]), BF16)
            ot = jnp.dot(jnp.concatenate([vt, ones], axis=0), p_ref[keys(g), :],
                         preferred_element_type=F32)
            inv = 1.0 / ot[PAIR:PAIR + 1, :]
            ot = jnp.where(first_head, ot[:PAIR, :Q_GROUP] * inv[:, :Q_GROUP],
                           ot[:PAIR, Q_GROUP:] * inv[:, Q_GROUP:])
            o_scr[w0:w0 + Q_GROUP, feat] = ot.T.astype(BF16)

        n_units = len(units)
        scores(units[0], s_scr[0])
        scores(units[1], s_scr[1])
        weights(units[0], s_scr[0], p_scr[0])
        for n, unit in enumerate(units):
            if n + 2 < n_units:
                scores(units[n + 2], s_scr[n % 2])
            if n + 1 < n_units:
                weights(units[n + 1], s_scr[(n + 1) % 2], p_scr[(n + 1) % 2])
            values(unit, p_scr[n % 2])

    @pl.when(pl.program_id(1) == 0)
    def _():
        attend(True)

    @pl.when(pl.program_id(1) > 0)
    def _():
        attend(False)

    out_ref[0] = x_ref[0] + jnp.dot(o_scr[...], wo_ref[...], preferred_element_type=F32)


def _attn(x, qt, k, vt, bias_t, w_o):
    b, s, _ = x.shape
    cur = lambda bi, si: (bi, si, 0)
    prev = lambda bi, si: (bi, jnp.maximum(si - 1, 0), 0)
    cur_t = lambda bi, si: (bi, 0, si)
    prev_t = lambda bi, si: (bi, 0, jnp.maximum(si - 1, 0))
    blk = (1, SEQ_TILE, D)
    blk_t = (1, D, SEQ_TILE)
    return pl.pallas_call(
        _attn_kernel,
        name="attn",
        grid=(b, s // SEQ_TILE),
        in_specs=[
            pl.BlockSpec(blk_t, cur_t),
            pl.BlockSpec(blk, prev),
            pl.BlockSpec(blk, cur),
            pl.BlockSpec(blk_t, prev_t),
            pl.BlockSpec(blk_t, cur_t),
            pl.BlockSpec((B_HEADS // 2, G_BAND, 2 * Q_GROUP), lambda bi, si: (0, 0, 0),
                         pipeline_mode=pl.Buffered(1)),
            pl.BlockSpec(blk, cur),
            pl.BlockSpec((D, D), lambda bi, si: (0, 0), pipeline_mode=pl.Buffered(1)),
        ],
        out_specs=pl.BlockSpec(blk, cur),
        out_shape=jax.ShapeDtypeStruct(x.shape, F32),
        scratch_shapes=[pltpu.VMEM((SEQ_TILE, D), BF16)]
        + [pltpu.VMEM((G_BAND, 2 * Q_GROUP), F32)] * 2
        + [pltpu.VMEM((G_BAND, 2 * Q_GROUP), BF16)] * 2,
        compiler_params=_params(2),
    )(qt, k, k, vt, vt, bias_t, x, w_o.astype(BF16))


def _group_bias(table):
    band = _band_bias(table) * LOG2E
    pad = lambda lo, hi: jnp.pad(band, ((0, 0), (0, 0), (lo, hi)), constant_values=NEG_INF)
    both = jnp.concatenate([pad(0, CHUNK), pad(CHUNK, 0)], axis=1)
    both = both.reshape(B_HEADS // 2, 2, Q_GROUP, G_BAND)
    return both.transpose(0, 3, 1, 2).reshape(B_HEADS // 2, G_BAND, 2 * Q_GROUP)


def _band_bias(table):
    n_rel = REL_MAX - REL_MIN + 1
    span = BAND + CHUNK - 1
    head = jnp.broadcast_to(table[:, n_rel - 1:], (table.shape[0], span - n_rel))
    ext = jnp.concatenate([head, table[:, ::-1]], axis=1)
    rows = [ext[:, CHUNK - 1 - q:CHUNK - 1 - q + BAND] for q in range(CHUNK)]
    return jnp.stack(rows, axis=1)


def kernel(x, p, a_w_in, a_lb_logits, a_out_norm, a_w_o, kv_norm, w_kv, b_w_q, b_rel_bias, b_w_o,
           norm_mix, norm_ffn, norm_ple, moe_w_group, moe_b_group, moe_w_expert, moe_b_expert,
           moe_w_gate, moe_w_up, moe_w_down, ple_w_proj, ple_w_gate, final_norm):
    b, s, _ = x.shape
    lower_bounds = jnp.cumsum(jax.nn.softmax(a_lb_logits.astype(F32), axis=0), axis=0)

    def moe(xi, i, final):
        return _moe_ple(xi, p, i, norm_ffn[i], moe_w_group[i], moe_b_group[i], moe_w_expert[i],
                        moe_b_expert[i], moe_w_gate, moe_w_up, moe_w_down, norm_ple[i],
                        ple_w_proj[i], ple_w_gate[i], final_norm, final)

    x = _mixer_a(x, norm_mix[0], a_w_in[0], lower_bounds[0], a_out_norm[0], a_w_o[0])
    x = moe(x, 0, False)

    qt, k, vt = _qkv(x, norm_mix[1], kv_norm, b_w_q[0], w_kv)
    x = _attn(x, qt, k, vt, _group_bias(b_rel_bias[0].astype(F32)), b_w_o[0])
    x = moe(x, 1, True)
    return x
```

```python
import functools

import jax
import jax.numpy as jnp
from jax import lax
from jax.experimental import pallas as pl
from jax.experimental.pallas import tpu as pltpu

F32 = jnp.float32
BF16 = jnp.bfloat16
U32 = jnp.uint32

D = 1024
CHUNK = 64
A_HEADS = 8
A_HEAD_DIM = 128
B_HEADS = 16
B_HEAD_DIM = 64
LEFT_CHUNKS = 8
BAND = (LEFT_CHUNKS + 1) * CHUNK
REL_MIN = -(CHUNK - 1)
REL_MAX = 256
ATTN_SCALE = B_HEAD_DIM ** -0.5
N_GROUPS = 4
EXPERTS_PER_GROUP = 8
N_EXPERTS = 32
TOP_K = 2
D_EXPERT = 512
MOE_BLOCK = 512
D_PLE = 256
EPS = 1e-6
NEG_INF = -1e30
LOG2E = 1.4426950408889634

SEQ_TILE = 512
ROUTER_TILE = 512
COMBINE_TILE = 512
ROUTER_ROWS = 48
VMEM_LIMIT = 56 * 1024 * 1024


def _params(n_axes, vmem=VMEM_LIMIT):
    return pltpu.CompilerParams(dimension_semantics=("arbitrary",) * n_axes,
                                vmem_limit_bytes=vmem)


def _rms_scale(x):
    return lax.rsqrt(jnp.mean(x * x, axis=-1, keepdims=True) + EPS)


def _sigmoid(x):
    return 1.0 / (1.0 + jnp.exp(-x))


def _mixer_a_kernel(x_ref, g_ref, win_ref, lb_ref, onorm_ref, wo_ref, out_ref,
                    proj_scr, o_scr, state_scr):
    @pl.when(pl.program_id(1) == 0)
    def _():
        state_scr[...] = jnp.zeros_like(state_scr)

    x = x_ref[0]
    h = (x * _rms_scale(x) * g_ref[...]).astype(BF16)
    proj_scr[...] = jnp.dot(h, win_ref[...], preferred_element_type=F32)

    row = lax.broadcasted_iota(jnp.int32, (CHUNK, CHUNK), 0)
    col = lax.broadcasted_iota(jnp.int32, (CHUNK, CHUNK), 1)
    causal = row >= col
    tril = causal.astype(BF16)
    lb = lb_ref[...]
    onorm = onorm_ref[...]

    def chunk_body(c, carry):
        r0 = pl.multiple_of(c * CHUNK, CHUNK)
        rows = pl.ds(r0, CHUNK)
        f = lb + (1.0 - lb) * _sigmoid(proj_scr[rows, D:2 * D])
        logf = jnp.log(f)
        hi = logf.astype(BF16)
        lo = (logf - hi.astype(F32)).astype(BF16)
        g = (jnp.dot(tril, hi, preferred_element_type=F32)
             + jnp.dot(tril, lo, preferred_element_type=F32))
        for hd in range(A_HEADS):
            sl = slice(hd * A_HEAD_DIM, (hd + 1) * A_HEAD_DIM)
            gh = g[:, sl]
            g_last = gh[CHUNK - 1:CHUNK, :]
            k = 1.0 - f[:, sl]
            q_dec = (proj_scr[rows, sl] * jnp.exp(gh)).astype(BF16)
            k_inv = (k * jnp.exp(-gh)).astype(BF16)
            k_tail = (k * jnp.exp(g_last - gh)).astype(BF16)
            v = proj_scr[rows, 2 * D + hd * A_HEAD_DIM:2 * D + (hd + 1) * A_HEAD_DIM]
            v_b = v.astype(BF16)
            att = lax.dot_general(q_dec, k_inv, (((1,), (1,)), ((), ())),
                                  preferred_element_type=F32)
            att = jnp.where(causal, att, 0.0).astype(BF16)
            st = state_scr[hd]
            o = (jnp.dot(att, v_b, preferred_element_type=F32)
                 + lax.dot_general(q_dec, st.astype(BF16), (((1,), (1,)), ((), ())),
                                   preferred_element_type=F32))
            v_t = v.T.astype(BF16)
            state_scr[hd] = st * jnp.exp(g_last) + jnp.dot(v_t, k_tail,
                                                           preferred_element_type=F32)
            o = o * _rms_scale(o)
            og = proj_scr[rows, 3 * D + hd * A_HEAD_DIM:3 * D + (hd + 1) * A_HEAD_DIM]
            o = o * onorm[:, sl] * (og * _sigmoid(og))
            o_scr[rows, sl] = o.astype(BF16)
        return carry

    lax.fori_loop(0, SEQ_TILE // CHUNK, chunk_body, 0, unroll=True)
    out_ref[0] = x + jnp.dot(o_scr[...], wo_ref[...], preferred_element_type=F32)


def _mixer_a(x, g, w_in, lb, out_norm, w_o):
    b, s, _ = x.shape
    const = lambda bi, si: (0, 0)
    return pl.pallas_call(
        _mixer_a_kernel,
        name="mixer_a",
        grid=(b, s // SEQ_TILE),
        in_specs=[
            pl.BlockSpec((1, SEQ_TILE, D), lambda bi, si: (bi, si, 0)),
            pl.BlockSpec((1, D), const),
            pl.BlockSpec((D, 4 * D), const, pipeline_mode=pl.Buffered(1)),
            pl.BlockSpec((1, D), const),
            pl.BlockSpec((1, D), const),
            pl.BlockSpec((D, D), const, pipeline_mode=pl.Buffered(1)),
        ],
        out_specs=pl.BlockSpec((1, SEQ_TILE, D), lambda bi, si: (bi, si, 0)),
        out_shape=jax.ShapeDtypeStruct(x.shape, F32),
        scratch_shapes=[
            pltpu.VMEM((SEQ_TILE, 4 * D), F32),
            pltpu.VMEM((SEQ_TILE, D), BF16),
            pltpu.VMEM((A_HEADS, A_HEAD_DIM, A_HEAD_DIM), F32),
        ],
        compiler_params=_params(2),
    )(x, g.reshape(1, D), w_in.astype(BF16), lb.reshape(1, D), out_norm.reshape(1, D),
      w_o.astype(BF16))


def _router_kernel(x_ref, g_ref, wr_ref, br_ref, ids_ref, gates_ref, counts_ref, cnt_scr,
                   before_scr):
    tm = ROUTER_TILE

    @pl.when(pl.program_id(0) == 0)
    def _():
        cnt_scr[...] = jnp.zeros_like(cnt_scr)
        tr = lax.broadcasted_iota(jnp.int32, (tm, tm), 0)
        tc = lax.broadcasted_iota(jnp.int32, (tm, tm), 1)
        before_scr[...] = (tr < tc).astype(BF16)

    x = x_ref[...]
    h = x * _rms_scale(x) * g_ref[...]
    h_hi = h.astype(BF16)
    h_lo = (h - h_hi.astype(F32)).astype(BF16)
    nt = (((1,), (1,)), ((), ()))
    both = lax.dot_general(wr_ref[...], h_hi, nt, preferred_element_type=F32)
    cross = lax.dot_general(wr_ref[0:ROUTER_ROWS], h_lo, nt, preferred_element_type=F32)
    logits = both[0:ROUTER_ROWS] + both[ROUTER_ROWS:] + cross + br_ref[...]
    el = logits[0:N_EXPERTS]
    gl = logits[N_EXPERTS:ROUTER_ROWS]
    grow = lax.broadcasted_iota(jnp.int32, gl.shape, 0)
    gl = jnp.where(grow < N_GROUPS, gl, -jnp.inf)
    gmax = jnp.max(gl, axis=0, keepdims=True)
    gsum = jnp.sum(jnp.exp(gl - gmax), axis=0, keepdims=True)
    grp_w = 1.0 / gsum
    gidx = jnp.min(jnp.where(gl == gmax, grow, N_GROUPS), axis=0, keepdims=True)

    erow = lax.broadcasted_iota(jnp.int32, el.shape, 0)
    masked = jnp.where((erow // EXPERTS_PER_GROUP) == gidx, el, -jnp.inf)
    top1 = jnp.max(masked, axis=0, keepdims=True)
    i1 = jnp.min(jnp.where(masked == top1, erow, N_EXPERTS), axis=0, keepdims=True)
    masked2 = jnp.where(erow == i1, -jnp.inf, masked)
    top2 = jnp.max(masked2, axis=0, keepdims=True)
    i2 = jnp.min(jnp.where(masked2 == top2, erow, N_EXPERTS), axis=0, keepdims=True)
    e2 = jnp.exp(top2 - top1)
    denom = 1.0 + e2
    g1 = grp_w * (1.0 / denom)
    g2 = grp_w * (e2 / denom)

    sel1 = erow == i1
    sel2 = erow == i2
    onehot = (sel1 | sel2).astype(BF16)
    prefix = jnp.dot(onehot, before_scr[...], preferred_element_type=F32) + cnt_scr[...]
    r1 = jnp.sum(jnp.where(sel1, prefix, 0.0), axis=0, keepdims=True)
    r2 = jnp.sum(jnp.where(sel2, prefix, 0.0), axis=0, keepdims=True)
    cnt_scr[...] += jnp.sum(onehot.astype(F32), axis=1, keepdims=True)

    zi = jnp.zeros((4, tm), jnp.int32)
    ids_ref[0] = jnp.concatenate(
        [i1, i2, r1.astype(jnp.int32), r2.astype(jnp.int32), zi], axis=0)
    gates_ref[0] = jnp.concatenate([g1, g2, jnp.zeros((6, tm), F32)], axis=0)
    counts_ref[...] = jnp.broadcast_to(cnt_scr[...], counts_ref.shape).astype(jnp.int32)


def _router(x2d, g, w_group, b_group, w_expert, b_expert):
    t = x2d.shape[0]
    nt = t // ROUTER_TILE
    pad = ROUTER_ROWS - N_EXPERTS - N_GROUPS
    wr = jnp.concatenate([w_expert.T, w_group.T, jnp.zeros((pad, D), F32)], axis=0)
    wr_hi = wr.astype(BF16)
    wr_lo = (wr - wr_hi.astype(F32)).astype(BF16)
    br = jnp.concatenate([b_expert, b_group, jnp.zeros((pad,), F32)]).reshape(ROUTER_ROWS, 1)
    const = lambda i: (0, 0)
    return pl.pallas_call(
        _router_kernel,
        name="router",
        grid=(nt,),
        in_specs=[
            pl.BlockSpec((ROUTER_TILE, D), lambda i: (i, 0)),
            pl.BlockSpec((1, D), const),
            pl.BlockSpec((2 * ROUTER_ROWS, D), const),
            pl.BlockSpec((ROUTER_ROWS, 1), const),
        ],
        out_specs=[
            pl.BlockSpec((1, 8, ROUTER_TILE), lambda i: (i, 0, 0)),
            pl.BlockSpec((1, 8, ROUTER_TILE), lambda i: (i, 0, 0)),
            pl.BlockSpec((N_EXPERTS, 128), const),
        ],
        out_shape=[
            jax.ShapeDtypeStruct((nt, 8, ROUTER_TILE), jnp.int32),
            jax.ShapeDtypeStruct((nt, 8, ROUTER_TILE), F32),
            jax.ShapeDtypeStruct((N_EXPERTS, 128), jnp.int32),
        ],
        scratch_shapes=[pltpu.VMEM((N_EXPERTS, 1), F32),
                        pltpu.VMEM((ROUTER_TILE, ROUTER_TILE), BF16)],
        compiler_params=_params(1),
    )(x2d, g.reshape(1, D), jnp.concatenate([wr_hi, wr_lo], axis=0), br)


SUBLANES = 8
SLOTS_PER_TILE = SUBLANES * TOP_K


def _issue_rows(n_rows, dest_ref, first_tile, make_copy):
    def body(j, c):
        for u in range(SUBLANES):
            for k in range(TOP_K):
                slot = u * TOP_K + k
                make_copy(j, u, k, dest_ref[0, first_tile + j, slot]).start(priority=slot % 2)
        return c

    lax.fori_loop(0, n_rows // SUBLANES, body, 0)


def _issue_rows_inline(n_rows, dest_ref, first_tile, make_copy):
    for j in range(n_rows // SUBLANES):
        for u in range(SUBLANES):
            for k in range(TOP_K):
                slot = u * TOP_K + k
                make_copy(j, u, k, dest_ref[0, first_tile + j, slot]).start(priority=slot % 2)


def _pack_bf16_pairs(x):
    half = x.shape[1] // 2
    lo = lax.bitcast_convert_type(x[:, :half].astype(BF16).astype(F32), U32)
    hi = lax.bitcast_convert_type(x[:, half:].astype(BF16).astype(F32), U32)
    return (hi & jnp.uint32(0xFFFF0000)) | (lo >> 16)


def _unpack_bf16_pairs(u):
    lo = lax.bitcast_convert_type(u << 16, F32).astype(BF16)
    hi = lax.bitcast_convert_type(u & jnp.uint32(0xFFFF0000), F32).astype(BF16)
    return lo, hi


def _tile_rows(x):
    return x.reshape(x.shape[0] // SUBLANES, SUBLANES, x.shape[1])


def _wait_rows(buf_ref, sem):
    pltpu.make_async_copy(buf_ref, buf_ref, sem).wait()


def _dispatch_kernel(zrow_ref, dest_ref, x_ref, g_ref, xs_ref, hbuf, zbuf, zsem, sem):
    @pl.when(pl.program_id(0) == 0)
    def _():
        zbuf[...] = jnp.zeros_like(zbuf)

        def zcopy(e):
            return pltpu.make_async_copy(zbuf, xs_ref.at[pl.ds(zrow_ref[e], MOE_BLOCK)], zsem)

        def zstart(e, c):
            @pl.when(zrow_ref[e] >= 0)
            def _():
                zcopy(e).start()
            return c

        def zwait(e, c):
            @pl.when(zrow_ref[e] >= 0)
            def _():
                zcopy(e).wait()
            return c

        lax.fori_loop(0, 2 * N_EXPERTS, zstart, 0)
        lax.fori_loop(0, 2 * N_EXPERTS, zwait, 0)

    i = pl.program_id(0)
    slot = i % 2
    x = x_ref[...]
    hbuf[slot] = _tile_rows(_pack_bf16_pairs(x * _rms_scale(x) * g_ref[...]))

    def row_copy(tile, sub, k, dest):
        return pltpu.make_async_copy(hbuf.at[slot, tile, pl.ds(sub, 1), :], xs_ref.at[dest],
                                     sem.at[slot])

    _issue_rows(ROUTER_TILE, dest_ref, 0, row_copy)

    def drain(which):
        for _ in range(TOP_K):
            _wait_rows(hbuf.at[which], sem.at[which])

    @pl.when(i > 0)
    def _():
        drain(1 - slot)

    @pl.when(i == pl.num_programs(0) - 1)
    def _():
        drain(slot)


def _dispatch(x2d, g, dest, zrow, n_pad):
    t = x2d.shape[0]
    nt = t // ROUTER_TILE
    return pl.pallas_call(
        _dispatch_kernel,
        name="dispatch",
        grid_spec=pltpu.PrefetchScalarGridSpec(
            num_scalar_prefetch=1,
            grid=(nt,),
            in_specs=[
                pl.BlockSpec((1, ROUTER_TILE // SUBLANES, SLOTS_PER_TILE),
                             lambda i, z: (i, 0, 0), memory_space=pltpu.SMEM),
                pl.BlockSpec((ROUTER_TILE, D), lambda i, z: (i, 0)),
                pl.BlockSpec((1, D), lambda i, z: (0, 0)),
            ],
            out_specs=pl.BlockSpec(memory_space=pl.ANY),
            scratch_shapes=[
                pltpu.VMEM((2, ROUTER_TILE // SUBLANES, SUBLANES, D // 2), U32),
                pltpu.VMEM((MOE_BLOCK, 1, D // 2), U32),
                pltpu.SemaphoreType.DMA(()),
                pltpu.SemaphoreType.DMA((2,)),
            ],
        ),
        out_shape=jax.ShapeDtypeStruct((n_pad, 1, D // 2), U32),
        compiler_params=_params(1),
    )(zrow, dest, x2d, g.reshape(1, D))


def _experts_kernel(be_ref, nxt_ref, nu_ref, xs_ref, wg_ref, wu_ref, wd_ref, y_ref,
                    wg_f, wu_f, wd_f, wg_b, wu_b, wd_b, xbuf, ybuf, w_sem, in_sem, out_sem, *, layer):
    i = pl.program_id(0)
    n_used = nu_ref[0]
    slot = i % 2

    def block_rows(ref, blk):
        return ref.at[pl.ds(pl.multiple_of(blk * MOE_BLOCK, MOE_BLOCK), MOE_BLOCK), 0]

    def fetch(blk, into):
        return pltpu.make_async_copy(block_rows(xs_ref, blk), xbuf.at[into], in_sem.at[into])

    def write_back(blk, from_):
        return pltpu.make_async_copy(ybuf.at[from_], block_rows(y_ref, blk), out_sem.at[from_])

    def weight_copies(e):
        return [pltpu.make_async_copy(src.at[layer, e], dst, w_sem.at[n])
                for n, (src, dst) in enumerate(((wg_ref, wg_f), (wu_ref, wu_f), (wd_ref, wd_f)))]

    @pl.when(i == 0)
    def _():
        for c in weight_copies(be_ref[0]):
            c.start()
        fetch(0, 0).start()

    @pl.when(i + 1 < n_used)
    def _():
        fetch(i + 1, 1 - slot).start()

    prev = be_ref[jnp.maximum(i - 1, 0)]

    @pl.when((i < n_used) & ((i == 0) | (be_ref[i] != prev)))
    def _():
        for c in weight_copies(be_ref[i]):
            c.wait()
        wg_b[...] = wg_f[...].astype(BF16)
        wu_b[...] = wu_f[...].astype(BF16)
        wd_b[...] = wd_f[...].astype(BF16)

        @pl.when(nxt_ref[i] >= 0)
        def _():
            for c in weight_copies(nxt_ref[i]):
                c.start()

    @pl.when(i >= 2)
    def _():
        write_back(i - 2, slot).wait()

    @pl.when(i < n_used)
    def _():
        fetch(i, slot).wait()
        h_lo, h_hi = _unpack_bf16_pairs(xbuf[slot])
        half = D // 2
        a = (jnp.dot(h_lo, wg_b[:half], preferred_element_type=F32)
             + jnp.dot(h_hi, wg_b[half:], preferred_element_type=F32))
        u = (jnp.dot(h_lo, wu_b[:half], preferred_element_type=F32)
             + jnp.dot(h_hi, wu_b[half:], preferred_element_type=F32))
        hid = (a * _sigmoid(a) * u).astype(BF16)
        ybuf[slot] = jnp.dot(hid, wd_b[...], preferred_element_type=F32)

    @pl.when(i >= n_used)
    def _():
        ybuf[slot] = jnp.zeros((MOE_BLOCK, D), F32)

    write_back(i, slot).start()

    @pl.when(i == pl.num_programs(0) - 1)
    def _():
        write_back(i - 1, 1 - slot).wait()
        write_back(i, slot).wait()


def _experts(xs, block_e, next_e, n_used, layer, w_gate, w_up, w_down):
    n_pad = xs.shape[0]
    n_blocks = n_pad // MOE_BLOCK
    hbm = pl.BlockSpec(memory_space=pl.ANY)
    return pl.pallas_call(
        functools.partial(_experts_kernel, layer=layer),
        name="experts",
        grid_spec=pltpu.PrefetchScalarGridSpec(
            num_scalar_prefetch=3,
            grid=(n_blocks,),
            in_specs=[hbm, hbm, hbm, hbm],
            out_specs=hbm,
            scratch_shapes=[
                pltpu.VMEM((D, D_EXPERT), F32),
                pltpu.VMEM((D, D_EXPERT), F32),
                pltpu.VMEM((D_EXPERT, D), F32),
                pltpu.VMEM((D, D_EXPERT), BF16),
                pltpu.VMEM((D, D_EXPERT), BF16),
                pltpu.VMEM((D_EXPERT, D), BF16),
                pltpu.VMEM((2, MOE_BLOCK, D // 2), U32),
                pltpu.VMEM((2, MOE_BLOCK, D), F32),
                pltpu.SemaphoreType.DMA((3,)),
                pltpu.SemaphoreType.DMA((2,)),
                pltpu.SemaphoreType.DMA((2,)),
            ],
        ),
        out_shape=jax.ShapeDtypeStruct((n_pad, 1, D), F32),
        compiler_params=_params(1),
    )(block_e, next_e, n_used, xs, w_gate, w_up, w_down)


def _combine_kernel(dest_ref, dest_next_ref, x_ref, gate_ref, y_ref, p_ref, gple_ref, wp_ref,
                    wg_ref, gfin_ref, out_ref, ybuf0, ybuf1, sem, *, final):
    i = pl.program_id(0)
    tiles = COMBINE_TILE // SUBLANES
    ybuf = (ybuf0, ybuf1)

    def row_copy(into):
        def make(tile, sub, k, dest):
            return pltpu.make_async_copy(y_ref.at[dest], ybuf[into].at[k, tile, pl.ds(sub, 1), :],
                                         sem.at[into])
        return make

    def combine(half, slot):
        rows = slice(half * COMBINE_TILE, (half + 1) * COMBINE_TILE)
        gates = gate_ref[rows, :]
        y0 = ybuf[slot][0].reshape(COMBINE_TILE, D)
        y1 = ybuf[slot][1].reshape(COMBINE_TILE, D)
        x = x_ref[rows, :] + gates[:, 0:1] * y0 + gates[:, 1:2] * y1
        h = (x * _rms_scale(x) * gple_ref[...]).astype(BF16)
        gate = _sigmoid(jnp.dot(h, wg_ref[...], preferred_element_type=F32))
        proj = jnp.dot(p_ref[0, rows, :].astype(BF16), wp_ref[...], preferred_element_type=F32)
        x = x + proj * gate
        if final:
            x = x * _rms_scale(x) * gfin_ref[...]
        out_ref[rows, :] = x

    def wait(slot):
        for k in range(TOP_K):
            _wait_rows(ybuf[slot].at[k], sem.at[slot])

    @pl.when(i == 0)
    def _():
        _issue_rows(COMBINE_TILE, dest_ref, 0, row_copy(0))

    wait(0)
    _issue_rows_inline(COMBINE_TILE, dest_ref, tiles, row_copy(1))
    combine(0, 0)
    wait(1)
    _issue_rows_inline(COMBINE_TILE, dest_next_ref, 0, row_copy(0))
    combine(1, 1)

    @pl.when(i == pl.num_programs(0) - 1)
    def _():
        wait(0)


def _combine_ple(x2d, dest, gates, y, p3d, layer, g_ple, w_proj, w_gate, g_final, final):
    t = x2d.shape[0]
    step = 2 * COMBINE_TILE
    nt = t // step
    dest_blk = (1, step // SUBLANES, SLOTS_PER_TILE)
    const = lambda i: (0, 0)
    return pl.pallas_call(
        functools.partial(_combine_kernel, final=final),
        name="combine_final" if final else "combine",
        grid=(nt,),
        in_specs=[
            pl.BlockSpec(dest_blk, lambda i: (i, 0, 0), memory_space=pltpu.SMEM),
            pl.BlockSpec(dest_blk, lambda i: (jnp.minimum(i + 1, nt - 1), 0, 0),
                         memory_space=pltpu.SMEM),
            pl.BlockSpec((step, D), lambda i: (i, 0)),
            pl.BlockSpec((step, TOP_K), lambda i: (i, 0)),
            pl.BlockSpec(memory_space=pl.ANY),
            pl.BlockSpec((1, step, D_PLE), lambda i: (layer, i, 0)),
            pl.BlockSpec((1, D), const),
            pl.BlockSpec((D_PLE, D), const, pipeline_mode=pl.Buffered(1)),
            pl.BlockSpec((D, D), const, pipeline_mode=pl.Buffered(1)),
            pl.BlockSpec((1, D), const),
        ],
        out_specs=pl.BlockSpec((step, D), lambda i: (i, 0)),
        out_shape=jax.ShapeDtypeStruct((t, D), F32),
        scratch_shapes=[
            pltpu.VMEM((TOP_K, COMBINE_TILE // SUBLANES, SUBLANES, D), F32),
            pltpu.VMEM((TOP_K, COMBINE_TILE // SUBLANES, SUBLANES, D), F32),
            pltpu.SemaphoreType.DMA((2,)),
        ],
        compiler_params=_params(1),
    )(dest, dest, x2d, gates, y, p3d, g_ple.reshape(1, D), w_proj.astype(BF16),
      w_gate.astype(BF16), g_final.reshape(1, D))


def _moe_ple(x, p, layer, norm_ffn, w_group, b_group, w_expert, b_expert, w_gate, w_up, w_down,
             norm_ple, ple_w_proj, ple_w_gate, final_norm, final):
    b, s, _ = x.shape
    t = b * s
    x2d = x.reshape(t, D)
    ids, gates, counts = _router(x2d, norm_ffn, w_group, b_group, w_expert, b_expert)

    counts = counts[:, 0]
    padded = (counts + MOE_BLOCK - 1) // MOE_BLOCK * MOE_BLOCK
    pad_end = jnp.cumsum(padded)
    pad_start = pad_end - padded
    n_blocks = t * TOP_K // MOE_BLOCK + N_EXPERTS
    n_pad = n_blocks * MOE_BLOCK
    e = ids[:, 0:TOP_K, :].transpose(1, 0, 2).reshape(TOP_K, t)
    r = ids[:, TOP_K:2 * TOP_K, :].transpose(1, 0, 2).reshape(TOP_K, t)
    experts = jnp.arange(N_EXPERTS, dtype=jnp.int32)
    dest = r + jnp.sum(jnp.where(e[..., None] == experts, pad_start, 0), axis=-1)
    gate_cols = gates[:, 0:TOP_K, :].transpose(0, 2, 1).reshape(t, TOP_K)
    block_row = jnp.arange(n_blocks, dtype=jnp.int32) * MOE_BLOCK
    block_e = jnp.minimum(jnp.sum(pad_end[None, :] <= block_row[:, None], axis=-1),
                          N_EXPERTS - 1).astype(jnp.int32)
    n_used = (pad_end[-1:] // MOE_BLOCK).astype(jnp.int32)
    tail = (n_used[0] + jnp.arange(N_EXPERTS, dtype=jnp.int32)) * MOE_BLOCK
    zrow = jnp.concatenate([jnp.where(padded > 0, pad_end - MOE_BLOCK, -1),
                            jnp.where(tail < n_pad, tail, -1)]).astype(jnp.int32)

    def tiles(a, tile):
        return a.T.reshape(t // tile, tile // SUBLANES, SLOTS_PER_TILE).astype(jnp.int32)

    xs = _dispatch(x2d, norm_ffn, tiles(dest, ROUTER_TILE), zrow, n_pad)
    later = jnp.where((experts[None, :] > block_e[:, None]) & (padded[None, :] > 0), experts[None, :],
                      N_EXPERTS)
    next_e = jnp.min(later, axis=-1)
    next_e = jnp.where(next_e < N_EXPERTS, next_e, -1).astype(jnp.int32)
    y = _experts(xs, block_e, next_e, n_used, layer, w_gate, w_up, w_down)
    out = _combine_ple(x2d, tiles(dest, 2 * COMBINE_TILE), gate_cols, y, p.reshape(-1, t, D_PLE), layer,
                       norm_ple, ple_w_proj, ple_w_gate, final_norm, final)
    return out.reshape(b, s, D)


def _qkv_kernel(x_ref, gq_ref, gkv_ref, wq_ref, wkv_ref, qt_ref, k_ref, vt_ref):
    x = x_ref[0]
    xn = x * _rms_scale(x)
    hq = (xn * gq_ref[...]).astype(BF16)
    hkv = (xn * gkv_ref[...]).astype(BF16)
    q = jnp.dot(hq, wq_ref[...], preferred_element_type=F32) * (ATTN_SCALE * LOG2E)
    kv = jnp.dot(hkv, wkv_ref[...], preferred_element_type=F32)
    qt_ref[0] = q.T.astype(BF16)
    k_ref[0] = kv[:, :D].astype(BF16)
    vt_ref[0] = kv[:, D:].T.astype(BF16)


def _qkv(x, g_q, g_kv, w_q, w_kv):
    b, s, _ = x.shape
    const = lambda bi, si: (0, 0)
    row_major = pl.BlockSpec((1, SEQ_TILE, D), lambda bi, si: (bi, si, 0))
    feat_major = pl.BlockSpec((1, D, SEQ_TILE), lambda bi, si: (bi, 0, si))
    return pl.pallas_call(
        _qkv_kernel,
        name="qkv",
        grid=(b, s // SEQ_TILE),
        in_specs=[
            row_major,
            pl.BlockSpec((1, D), const),
            pl.BlockSpec((1, D), const),
            pl.BlockSpec((D, D), const),
            pl.BlockSpec((D, 2 * D), const),
        ],
        out_specs=[feat_major, row_major, feat_major],
        out_shape=[jax.ShapeDtypeStruct((b, D, s), BF16),
                   jax.ShapeDtypeStruct((b, s, D), BF16),
                   jax.ShapeDtypeStruct((b, D, s), BF16)],
        compiler_params=_params(2),
    )(x, g_q.reshape(1, D), g_kv.reshape(1, D), w_q.astype(BF16), w_kv.astype(BF16))


Q_GROUP = 2 * CHUNK
G_BAND = BAND + CHUNK
PAIR = 2 * B_HEAD_DIM
ONES_ROWS = 16


def _attn_kernel(qt_ref, kp_ref, kc_ref, vtp_ref, vtc_ref, bias_ref, x_ref, wo_ref, out_ref, o_scr,
                 s_scr0, s_scr1, p_scr0, p_scr1):
    drow = lax.broadcasted_iota(jnp.int32, (PAIR, Q_GROUP), 0)
    first_head = drow < B_HEAD_DIM
    s_scr = (s_scr0, s_scr1)
    p_scr = (p_scr0, p_scr1)

    def attend(first_tile):
        units = [(g, pr) for g in range(SEQ_TILE // Q_GROUP) for pr in range(B_HEADS // 2)]

        def geometry(g):
            w0 = g * Q_GROUP
            n_prev = SEQ_TILE - w0
            return w0, n_prev, G_BAND - n_prev

        def keys(g):
            return slice(geometry(g)[1], None) if first_tile else slice(None)

        def scores(unit, s_ref):
            g, pr = unit
            w0, n_prev, n_cur = geometry(g)
            feat = slice(pr * PAIR, (pr + 1) * PAIR)
            qt = qt_ref[0, feat, w0:w0 + Q_GROUP]
            zero = jnp.zeros_like(qt)
            qblk = jnp.concatenate([jnp.where(first_head, qt, zero),
                                    jnp.where(first_head, zero, qt)], axis=1)
            if first_tile:
                kb = kc_ref[0, :n_cur, feat]
            else:
                kb = jnp.concatenate([kp_ref[0, w0:, feat], kc_ref[0, :n_cur, feat]], axis=0)
            s_ref[keys(g), :] = (jnp.dot(kb, qblk, preferred_element_type=F32)
                                 + bias_ref[pr, keys(g), :])

        def weights(unit, s_ref, p_ref):
            g, _ = unit
            s = s_ref[keys(g), :]
            m = jnp.max(s, axis=0, keepdims=True)
            p_ref[keys(g), :] = jnp.exp2(s - m).astype(BF16)

        def values(unit, p_ref):
            g, pr = unit
            w0, n_prev, n_cur = geometry(g)
            feat = slice(pr * PAIR, (pr + 1) * PAIR)
            if first_tile:
                vt = vtc_ref[0, feat, :n_cur]
            else:
                vt = jnp.concatenate([vtp_ref[0, feat, w0:], vtc_ref[0, feat, :n_cur]], axis=1)
            ones = jnp.ones((ONES_ROWS, vt.shape[1]), BF16)
            ot = jnp.dot(jnp.concatenate([vt, ones], axis=0), p_ref[keys(g), :],
                         preferred_element_type=F32)
            inv = 1.0 / ot[PAIR:PAIR + 1, :]
            ot = jnp.where(first_head, ot[:PAIR, :Q_GROUP] * inv[:, :Q_GROUP],
                           ot[:PAIR, Q_GROUP:] * inv[:, Q_GROUP:])
            o_scr[w0:w0 + Q_GROUP, feat] = ot.T.astype(BF16)

        n_units = len(units)
        scores(units[0], s_scr[0])
        scores(units[1], s_scr[1])
        weights(units[0], s_scr[0], p_scr[0])
        for n, unit in enumerate(units):
            if n + 2 < n_units:
                scores(units[n + 2], s_scr[n % 2])
            if n + 1 < n_units:
                weights(units[n + 1], s_scr[(n + 1) % 2], p_scr[(n + 1) % 2])
            values(unit, p_scr[n % 2])

    @pl.when(pl.program_id(1) == 0)
    def _():
        attend(True)

    @pl.when(pl.program_id(1) > 0)
    def _():
        attend(False)

    out_ref[0] = x_ref[0] + jnp.dot(o_scr[...], wo_ref[...], preferred_element_type=F32)


def _attn(x, qt, k, vt, bias_t, w_o):
    b, s, _ = x.shape
    cur = lambda bi, si: (bi, si, 0)
    prev = lambda bi, si: (bi, jnp.maximum(si - 1, 0), 0)
    cur_t = lambda bi, si: (bi, 0, si)
    prev_t = lambda bi, si: (bi, 0, jnp.maximum(si - 1, 0))
    blk = (1, SEQ_TILE, D)
    blk_t = (1, D, SEQ_TILE)
    return pl.pallas_call(
        _attn_kernel,
        name="attn",
        grid=(b, s // SEQ_TILE),
        in_specs=[
            pl.BlockSpec(blk_t, cur_t),
            pl.BlockSpec(blk, prev),
            pl.BlockSpec(blk, cur),
            pl.BlockSpec(blk_t, prev_t),
            pl.BlockSpec(blk_t, cur_t),
            pl.BlockSpec((B_HEADS // 2, G_BAND, 2 * Q_GROUP), lambda bi, si: (0, 0, 0),
                         pipeline_mode=pl.Buffered(1)),
            pl.BlockSpec(blk, cur),
            pl.BlockSpec((D, D), lambda bi, si: (0, 0), pipeline_mode=pl.Buffered(1)),
        ],
        out_specs=pl.BlockSpec(blk, cur),
        out_shape=jax.ShapeDtypeStruct(x.shape, F32),
        scratch_shapes=[pltpu.VMEM((SEQ_TILE, D), BF16)]
        + [pltpu.VMEM((G_BAND, 2 * Q_GROUP), F32)] * 2
        + [pltpu.VMEM((G_BAND, 2 * Q_GROUP), BF16)] * 2,
        compiler_params=_params(2),
    )(qt, k, k, vt, vt, bias_t, x, w_o.astype(BF16))


def _group_bias(table):
    band = _band_bias(table) * LOG2E
    pad = lambda lo, hi: jnp.pad(band, ((0, 0), (0, 0), (lo, hi)), constant_values=NEG_INF)
    both = jnp.concatenate([pad(0, CHUNK), pad(CHUNK, 0)], axis=1)
    both = both.reshape(B_HEADS // 2, 2, Q_GROUP, G_BAND)
    return both.transpose(0, 3, 1, 2).reshape(B_HEADS // 2, G_BAND, 2 * Q_GROUP)


def _band_bias(table):
    n_rel = REL_MAX - REL_MIN + 1
    span = BAND + CHUNK - 1
    head = jnp.broadcast_to(table[:, n_rel - 1:], (table.shape[0], span - n_rel))
    ext = jnp.concatenate([head, table[:, ::-1]], axis=1)
    rows = [ext[:, CHUNK - 1 - q:CHUNK - 1 - q + BAND] for q in range(CHUNK)]
    return jnp.stack(rows, axis=1)


def kernel(x, p, a_w_in, a_lb_logits, a_out_norm, a_w_o, kv_norm, w_kv, b_w_q, b_rel_bias, b_w_o,
           norm_mix, norm_ffn, norm_ple, moe_w_group, moe_b_group, moe_w_expert, moe_b_expert,
           moe_w_gate, moe_w_up, moe_w_down, ple_w_proj, ple_w_gate, final_norm):
    b, s, _ = x.shape
    lower_bounds = jnp.cumsum(jax.nn.softmax(a_lb_logits.astype(F32), axis=0), axis=0)

    def moe(xi, i, final):
        return _moe_ple(xi, p, i, norm_ffn[i], moe_w_group[i], moe_b_group[i], moe_w_expert[i],
                        moe_b_expert[i], moe_w_gate, moe_w_up, moe_w_down, norm_ple[i],
                        ple_w_proj[i], ple_w_gate[i], final_norm, final)

    x = _mixer_a(x, norm_mix[0], a_w_in[0], lower_bounds[0], a_out_norm[0], a_w_o[0])
    x = moe(x, 0, False)

    qt, k, vt = _qkv(x, norm_mix[1], kv_norm, b_w_q[0], w_kv)
    x = _attn(x, qt, k, vt, _group_bias(b_rel_bias[0].astype(F32)), b_w_o[0])
    x = moe(x, 1, True)
    return x
```

```python
import functools

import jax
import jax.numpy as jnp
from jax import lax
from jax.experimental import pallas as pl
from jax.experimental.pallas import tpu as pltpu

F32 = jnp.float32
BF16 = jnp.bfloat16
U32 = jnp.uint32

D = 1024
CHUNK = 64
A_HEADS = 8
A_HEAD_DIM = 128
B_HEADS = 16
B_HEAD_DIM = 64
LEFT_CHUNKS = 8
BAND = (LEFT_CHUNKS + 1) * CHUNK
REL_MIN = -(CHUNK - 1)
REL_MAX = 256
ATTN_SCALE = B_HEAD_DIM ** -0.5
N_GROUPS = 4
EXPERTS_PER_GROUP = 8
N_EXPERTS = 32
TOP_K = 2
D_EXPERT = 512
MOE_BLOCK = 512
D_PLE = 256
EPS = 1e-6
NEG_INF = -1e30
LOG2E = 1.4426950408889634

SEQ_TILE = 512
ROUTER_TILE = 512
COMBINE_TILE = 512
ROUTER_ROWS = 48
VMEM_LIMIT = 56 * 1024 * 1024


def _params(n_axes, vmem=VMEM_LIMIT):
    return pltpu.CompilerParams(dimension_semantics=("arbitrary",) * n_axes,
                                vmem_limit_bytes=vmem)


def _rms_scale(x):
    return lax.rsqrt(jnp.mean(x * x, axis=-1, keepdims=True) + EPS)


def _sigmoid(x):
    return 1.0 / (1.0 + jnp.exp(-x))


def _mixer_a_kernel(x_ref, g_ref, win_ref, lb_ref, onorm_ref, wo_ref, out_ref,
                    proj_scr, o_scr, state_scr, g_scr0, g_scr1, k_scr0, k_scr1,
                    qd_scr0, qd_scr1, qd_scr2, kt_scr0, kt_scr1, kt_scr2, att_scr0, att_scr1, att_scr2):
    @pl.when(pl.program_id(1) == 0)
    def _():
        state_scr[...] = jnp.zeros_like(state_scr)

    x = x_ref[0]
    h = (x * _rms_scale(x) * g_ref[...]).astype(BF16)
    proj_scr[...] = jnp.dot(h, win_ref[...], preferred_element_type=F32)

    row = lax.broadcasted_iota(jnp.int32, (CHUNK, CHUNK), 0)
    col = lax.broadcasted_iota(jnp.int32, (CHUNK, CHUNK), 1)
    causal = row >= col
    tril = causal.astype(BF16)
    lb = lb_ref[...]
    onorm = onorm_ref[...]

    g_scr, k_scr = (g_scr0, g_scr1), (k_scr0, k_scr1)
    qd_scr, kt_scr = (qd_scr0, qd_scr1, qd_scr2), (kt_scr0, kt_scr1, kt_scr2)
    att_scr = (att_scr0, att_scr1, att_scr2)
    n_chunks = SEQ_TILE // CHUNK
    units = [(c, hd) for c in range(n_chunks) for hd in range(A_HEADS)]

    def rows_of(c):
        return slice(c * CHUNK, (c + 1) * CHUNK)

    def decay(c):
        f = lb + (1.0 - lb) * _sigmoid(proj_scr[rows_of(c), D:2 * D])
        logf = jnp.log(f)
        hi = logf.astype(BF16)
        lo = (logf - hi.astype(F32)).astype(BF16)
        g_scr[c % 2][...] = (jnp.dot(tril, hi, preferred_element_type=F32)
                             + jnp.dot(tril, lo, preferred_element_type=F32))
        k_scr[c % 2][...] = 1.0 - f

    def intra(n):
        c, hd = units[n]
        sl = slice(hd * A_HEAD_DIM, (hd + 1) * A_HEAD_DIM)
        gh = g_scr[c % 2][:, sl]
        g_last = gh[CHUNK - 1:CHUNK, :]
        k = k_scr[c % 2][:, sl]
        q_dec = (proj_scr[rows_of(c), sl] * jnp.exp(gh)).astype(BF16)
        k_inv = (k * jnp.exp(-gh)).astype(BF16)
        qd_scr[n % 3][...] = q_dec
        kt_scr[n % 3][...] = (k * jnp.exp(g_last - gh)).astype(BF16)
        att = lax.dot_general(q_dec, k_inv, (((1,), (1,)), ((), ())),
                              preferred_element_type=F32)
        att_scr[n % 3][...] = jnp.where(causal, att, 0.0).astype(BF16)

    def output(n):
        c, hd = units[n]
        sl = slice(hd * A_HEAD_DIM, (hd + 1) * A_HEAD_DIM)
        rows = rows_of(c)
        g_last = g_scr[c % 2][CHUNK - 1:CHUNK, sl]
        v = proj_scr[rows, 2 * D + hd * A_HEAD_DIM:2 * D + (hd + 1) * A_HEAD_DIM]
        st = state_scr[hd]
        o = (jnp.dot(att_scr[n % 3][...], v.astype(BF16), preferred_element_type=F32)
             + lax.dot_general(qd_scr[n % 3][...], st.astype(BF16), (((1,), (1,)), ((), ())),
                               preferred_element_type=F32))
        v_t = v.T.astype(BF16)
        state_scr[hd] = st * jnp.exp(g_last) + jnp.dot(v_t, kt_scr[n % 3][...],
                                                       preferred_element_type=F32)
        o = o * _rms_scale(o)
        og = proj_scr[rows, 3 * D + hd * A_HEAD_DIM:3 * D + (hd + 1) * A_HEAD_DIM]
        o = o * onorm[:, sl] * (og * _sigmoid(og))
        o_scr[rows, sl] = o.astype(BF16)

    decay(0)
    intra(0)
    intra(1)
    for n, (c, hd) in enumerate(units):
        if hd == 0 and c + 1 < n_chunks:
            decay(c + 1)
        if n + 2 < len(units):
            intra(n + 2)
        output(n)
    out_ref[0] = x + jnp.dot(o_scr[...], wo_ref[...], preferred_element_type=F32)


def _mixer_a(x, g, w_in, lb, out_norm, w_o):
    b, s, _ = x.shape
    const = lambda bi, si: (0, 0)
    return pl.pallas_call(
        _mixer_a_kernel,
        name="mixer_a",
        grid=(b, s // SEQ_TILE),
        in_specs=[
            pl.BlockSpec((1, SEQ_TILE, D), lambda bi, si: (bi, si, 0)),
            pl.BlockSpec((1, D), const),
            pl.BlockSpec((D, 4 * D), const, pipeline_mode=pl.Buffered(1)),
            pl.BlockSpec((1, D), const),
            pl.BlockSpec((1, D), const),
            pl.BlockSpec((D, D), const, pipeline_mode=pl.Buffered(1)),
        ],
        out_specs=pl.BlockSpec((1, SEQ_TILE, D), lambda bi, si: (bi, si, 0)),
        out_shape=jax.ShapeDtypeStruct(x.shape, F32),
        scratch_shapes=[
            pltpu.VMEM((SEQ_TILE, 4 * D), F32),
            pltpu.VMEM((SEQ_TILE, D), BF16),
            pltpu.VMEM((A_HEADS, A_HEAD_DIM, A_HEAD_DIM), F32),
        ] + [pltpu.VMEM((CHUNK, D), F32)] * 4
          + [pltpu.VMEM((CHUNK, A_HEAD_DIM), BF16)] * 6
          + [pltpu.VMEM((CHUNK, CHUNK), BF16)] * 3,
        compiler_params=_params(2),
    )(x, g.reshape(1, D), w_in.astype(BF16), lb.reshape(1, D), out_norm.reshape(1, D),
      w_o.astype(BF16))


def _router_kernel(x_ref, g_ref, wr_ref, br_ref, ids_ref, gates_ref, counts_ref, cnt_scr,
                   before_scr):
    tm = ROUTER_TILE

    @pl.when(pl.program_id(0) == 0)
    def _():
        cnt_scr[...] = jnp.zeros_like(cnt_scr)
        tr = lax.broadcasted_iota(jnp.int32, (tm, tm), 0)
        tc = lax.broadcasted_iota(jnp.int32, (tm, tm), 1)
        before_scr[...] = (tr < tc).astype(BF16)

    x = x_ref[...]
    h = x * _rms_scale(x) * g_ref[...]
    h_hi = h.astype(BF16)
    h_lo = (h - h_hi.astype(F32)).astype(BF16)
    nt = (((1,), (1,)), ((), ()))
    both = lax.dot_general(wr_ref[...], h_hi, nt, preferred_element_type=F32)
    cross = lax.dot_general(wr_ref[0:ROUTER_ROWS], h_lo, nt, preferred_element_type=F32)
    logits = both[0:ROUTER_ROWS] + both[ROUTER_ROWS:] + cross + br_ref[...]
    el = logits[0:N_EXPERTS]
    gl = logits[N_EXPERTS:ROUTER_ROWS]
    grow = lax.broadcasted_iota(jnp.int32, gl.shape, 0)
    gl = jnp.where(grow < N_GROUPS, gl, -jnp.inf)
    gmax = jnp.max(gl, axis=0, keepdims=True)
    gsum = jnp.sum(jnp.exp(gl - gmax), axis=0, keepdims=True)
    grp_w = 1.0 / gsum
    gidx = jnp.min(jnp.where(gl == gmax, grow, N_GROUPS), axis=0, keepdims=True)

    erow = lax.broadcasted_iota(jnp.int32, el.shape, 0)
    masked = jnp.where((erow // EXPERTS_PER_GROUP) == gidx, el, -jnp.inf)
    top1 = jnp.max(masked, axis=0, keepdims=True)
    i1 = jnp.min(jnp.where(masked == top1, erow, N_EXPERTS), axis=0, keepdims=True)
    masked2 = jnp.where(erow == i1, -jnp.inf, masked)
    top2 = jnp.max(masked2, axis=0, keepdims=True)
    i2 = jnp.min(jnp.where(masked2 == top2, erow, N_EXPERTS), axis=0, keepdims=True)
    e2 = jnp.exp(top2 - top1)
    denom = 1.0 + e2
    g1 = grp_w * (1.0 / denom)
    g2 = grp_w * (e2 / denom)

    sel1 = erow == i1
    sel2 = erow == i2
    onehot = (sel1 | sel2).astype(BF16)
    prefix = jnp.dot(onehot, before_scr[...], preferred_element_type=F32) + cnt_scr[...]
    r1 = jnp.sum(jnp.where(sel1, prefix, 0.0), axis=0, keepdims=True)
    r2 = jnp.sum(jnp.where(sel2, prefix, 0.0), axis=0, keepdims=True)
    cnt_scr[...] += jnp.sum(onehot.astype(F32), axis=1, keepdims=True)

    zi = jnp.zeros((4, tm), jnp.int32)
    ids_ref[0] = jnp.concatenate(
        [i1, i2, r1.astype(jnp.int32), r2.astype(jnp.int32), zi], axis=0)
    gates_ref[0] = jnp.concatenate([g1, g2, jnp.zeros((6, tm), F32)], axis=0)
    counts_ref[...] = jnp.broadcast_to(cnt_scr[...], counts_ref.shape).astype(jnp.int32)


def _router(x2d, g, w_group, b_group, w_expert, b_expert):
    t = x2d.shape[0]
    nt = t // ROUTER_TILE
    pad = ROUTER_ROWS - N_EXPERTS - N_GROUPS
    wr = jnp.concatenate([w_expert.T, w_group.T, jnp.zeros((pad, D), F32)], axis=0)
    wr_hi = wr.astype(BF16)
    wr_lo = (wr - wr_hi.astype(F32)).astype(BF16)
    br = jnp.concatenate([b_expert, b_group, jnp.zeros((pad,), F32)]).reshape(ROUTER_ROWS, 1)
    const = lambda i: (0, 0)
    return pl.pallas_call(
        _router_kernel,
        name="router",
        grid=(nt,),
        in_specs=[
            pl.BlockSpec((ROUTER_TILE, D), lambda i: (i, 0)),
            pl.BlockSpec((1, D), const),
            pl.BlockSpec((2 * ROUTER_ROWS, D), const),
            pl.BlockSpec((ROUTER_ROWS, 1), const),
        ],
        out_specs=[
            pl.BlockSpec((1, 8, ROUTER_TILE), lambda i: (i, 0, 0)),
            pl.BlockSpec((1, 8, ROUTER_TILE), lambda i: (i, 0, 0)),
            pl.BlockSpec((N_EXPERTS, 128), const),
        ],
        out_shape=[
            jax.ShapeDtypeStruct((nt, 8, ROUTER_TILE), jnp.int32),
            jax.ShapeDtypeStruct((nt, 8, ROUTER_TILE), F32),
            jax.ShapeDtypeStruct((N_EXPERTS, 128), jnp.int32),
        ],
        scratch_shapes=[pltpu.VMEM((N_EXPERTS, 1), F32),
                        pltpu.VMEM((ROUTER_TILE, ROUTER_TILE), BF16)],
        compiler_params=_params(1),
    )(x2d, g.reshape(1, D), jnp.concatenate([wr_hi, wr_lo], axis=0), br)


SUBLANES = 8
SLOTS_PER_TILE = SUBLANES * TOP_K


def _issue_rows(n_rows, dest_ref, first_tile, make_copy):
    def body(j, c):
        for u in range(SUBLANES):
            for k in range(TOP_K):
                slot = u * TOP_K + k
                make_copy(j, u, k, dest_ref[0, first_tile + j, slot]).start(priority=slot % 2)
        return c

    lax.fori_loop(0, n_rows // SUBLANES, body, 0)


def _issue_rows_inline(n_rows, dest_ref, first_tile, make_copy):
    for j in range(n_rows // SUBLANES):
        for u in range(SUBLANES):
            for k in range(TOP_K):
                slot = u * TOP_K + k
                make_copy(j, u, k, dest_ref[0, first_tile + j, slot]).start(priority=slot % 2)


def _pack_bf16_pairs(x):
    half = x.shape[1] // 2
    lo = lax.bitcast_convert_type(x[:, :half].astype(BF16).astype(F32), U32)
    hi = lax.bitcast_convert_type(x[:, half:].astype(BF16).astype(F32), U32)
    return (hi & jnp.uint32(0xFFFF0000)) | (lo >> 16)


def _unpack_bf16_pairs(u):
    lo = lax.bitcast_convert_type(u << 16, F32).astype(BF16)
    hi = lax.bitcast_convert_type(u & jnp.uint32(0xFFFF0000), F32).astype(BF16)
    return lo, hi


def _tile_rows(x):
    return x.reshape(x.shape[0] // SUBLANES, SUBLANES, x.shape[1])


def _wait_rows(buf_ref, sem):
    pltpu.make_async_copy(buf_ref, buf_ref, sem).wait()


def _dispatch_kernel(zrow_ref, dest_ref, x_ref, g_ref, xs_ref, hbuf, zbuf, zsem, sem):
    @pl.when(pl.program_id(0) == 0)
    def _():
        zbuf[...] = jnp.zeros_like(zbuf)

        def zcopy(e):
            return pltpu.make_async_copy(zbuf, xs_ref.at[pl.ds(zrow_ref[e], MOE_BLOCK)], zsem)

        def zstart(e, c):
            @pl.when(zrow_ref[e] >= 0)
            def _():
                zcopy(e).start()
            return c

        def zwait(e, c):
            @pl.when(zrow_ref[e] >= 0)
            def _():
                zcopy(e).wait()
            return c

        lax.fori_loop(0, 2 * N_EXPERTS, zstart, 0)
        lax.fori_loop(0, 2 * N_EXPERTS, zwait, 0)

    i = pl.program_id(0)
    slot = i % 2
    x = x_ref[...]
    hbuf[slot] = _tile_rows(_pack_bf16_pairs(x * _rms_scale(x) * g_ref[...]))

    def row_copy(tile, sub, k, dest):
        return pltpu.make_async_copy(hbuf.at[slot, tile, pl.ds(sub, 1), :], xs_ref.at[dest],
                                     sem.at[slot])

    _issue_rows(ROUTER_TILE, dest_ref, 0, row_copy)

    def drain(which):
        for _ in range(TOP_K):
            _wait_rows(hbuf.at[which], sem.at[which])

    @pl.when(i > 0)
    def _():
        drain(1 - slot)

    @pl.when(i == pl.num_programs(0) - 1)
    def _():
        drain(slot)


def _dispatch(x2d, g, dest, zrow, n_pad):
    t = x2d.shape[0]
    nt = t // ROUTER_TILE
    return pl.pallas_call(
        _dispatch_kernel,
        name="dispatch",
        grid_spec=pltpu.PrefetchScalarGridSpec(
            num_scalar_prefetch=1,
            grid=(nt,),
            in_specs=[
                pl.BlockSpec((1, ROUTER_TILE // SUBLANES, SLOTS_PER_TILE),
                             lambda i, z: (i, 0, 0), memory_space=pltpu.SMEM),
                pl.BlockSpec((ROUTER_TILE, D), lambda i, z: (i, 0)),
                pl.BlockSpec((1, D), lambda i, z: (0, 0)),
            ],
            out_specs=pl.BlockSpec(memory_space=pl.ANY),
            scratch_shapes=[
                pltpu.VMEM((2, ROUTER_TILE // SUBLANES, SUBLANES, D // 2), U32),
                pltpu.VMEM((MOE_BLOCK, 1, D // 2), U32),
                pltpu.SemaphoreType.DMA(()),
                pltpu.SemaphoreType.DMA((2,)),
            ],
        ),
        out_shape=jax.ShapeDtypeStruct((n_pad, 1, D // 2), U32),
        compiler_params=_params(1),
    )(zrow, dest, x2d, g.reshape(1, D))


def _experts_kernel(be_ref, nxt_ref, nu_ref, xs_ref, wg_ref, wu_ref, wd_ref, y_ref,
                    wg_f, wu_f, wd_f, wg_b, wu_b, wd_b, xbuf, ybuf, w_sem, in_sem, out_sem, *, layer):
    i = pl.program_id(0)
    n_used = nu_ref[0]
    slot = i % 2

    def block_rows(ref, blk):
        return ref.at[pl.ds(pl.multiple_of(blk * MOE_BLOCK, MOE_BLOCK), MOE_BLOCK), 0]

    def fetch(blk, into):
        return pltpu.make_async_copy(block_rows(xs_ref, blk), xbuf.at[into], in_sem.at[into])

    def write_back(blk, from_):
        return pltpu.make_async_copy(ybuf.at[from_], block_rows(y_ref, blk), out_sem.at[from_])

    def weight_copies(e):
        return [pltpu.make_async_copy(src.at[layer, e], dst, w_sem.at[n])
                for n, (src, dst) in enumerate(((wg_ref, wg_f), (wu_ref, wu_f), (wd_ref, wd_f)))]

    @pl.when(i == 0)
    def _():
        for c in weight_copies(be_ref[0]):
            c.start()
        fetch(0, 0).start()

    @pl.when(i + 1 < n_used)
    def _():
        fetch(i + 1, 1 - slot).start()

    prev = be_ref[jnp.maximum(i - 1, 0)]

    @pl.when((i < n_used) & ((i == 0) | (be_ref[i] != prev)))
    def _():
        for c in weight_copies(be_ref[i]):
            c.wait()
        wg_b[...] = wg_f[...].astype(BF16)
        wu_b[...] = wu_f[...].astype(BF16)
        wd_b[...] = wd_f[...].astype(BF16)

        @pl.when(nxt_ref[i] >= 0)
        def _():
            for c in weight_copies(nxt_ref[i]):
                c.start()

    @pl.when(i >= 2)
    def _():
        write_back(i - 2, slot).wait()

    @pl.when(i < n_used)
    def _():
        fetch(i, slot).wait()
        h_lo, h_hi = _unpack_bf16_pairs(xbuf[slot])
        half = D // 2
        a = (jnp.dot(h_lo, wg_b[:half], preferred_element_type=F32)
             + jnp.dot(h_hi, wg_b[half:], preferred_element_type=F32))
        u = (jnp.dot(h_lo, wu_b[:half], preferred_element_type=F32)
             + jnp.dot(h_hi, wu_b[half:], preferred_element_type=F32))
        hid = (a * _sigmoid(a) * u).astype(BF16)
        ybuf[slot] = jnp.dot(hid, wd_b[...], preferred_element_type=F32)

    @pl.when(i >= n_used)
    def _():
        ybuf[slot] = jnp.zeros((MOE_BLOCK, D), F32)

    write_back(i, slot).start()

    @pl.when(i == pl.num_programs(0) - 1)
    def _():
        write_back(i - 1, 1 - slot).wait()
        write_back(i, slot).wait()


def _experts(xs, block_e, next_e, n_used, layer, w_gate, w_up, w_down):
    n_pad = xs.shape[0]
    n_blocks = n_pad // MOE_BLOCK
    hbm = pl.BlockSpec(memory_space=pl.ANY)
    return pl.pallas_call(
        functools.partial(_experts_kernel, layer=layer),
        name="experts",
        grid_spec=pltpu.PrefetchScalarGridSpec(
            num_scalar_prefetch=3,
            grid=(n_blocks,),
            in_specs=[hbm, hbm, hbm, hbm],
            out_specs=hbm,
            scratch_shapes=[
                pltpu.VMEM((D, D_EXPERT), F32),
                pltpu.VMEM((D, D_EXPERT), F32),
                pltpu.VMEM((D_EXPERT, D), F32),
                pltpu.VMEM((D, D_EXPERT), BF16),
                pltpu.VMEM((D, D_EXPERT), BF16),
                pltpu.VMEM((D_EXPERT, D), BF16),
                pltpu.VMEM((2, MOE_BLOCK, D // 2), U32),
                pltpu.VMEM((2, MOE_BLOCK, D), F32),
                pltpu.SemaphoreType.DMA((3,)),
                pltpu.SemaphoreType.DMA((2,)),
                pltpu.SemaphoreType.DMA((2,)),
            ],
        ),
        out_shape=jax.ShapeDtypeStruct((n_pad, 1, D), F32),
        compiler_params=_params(1),
    )(block_e, next_e, n_used, xs, w_gate, w_up, w_down)


def _combine_kernel(dest_ref, dest_next_ref, x_ref, gate_ref, y_ref, p_ref, gple_ref, wp_ref,
                    wg_ref, gfin_ref, out_ref, ybuf0, ybuf1, sem, *, final):
    i = pl.program_id(0)
    tiles = COMBINE_TILE // SUBLANES
    ybuf = (ybuf0, ybuf1)

    def row_copy(into):
        def make(tile, sub, k, dest):
            return pltpu.make_async_copy(y_ref.at[dest], ybuf[into].at[k, tile, pl.ds(sub, 1), :],
                                         sem.at[into])
        return make

    def combine(half, slot):
        rows = slice(half * COMBINE_TILE, (half + 1) * COMBINE_TILE)
        gates = gate_ref[rows, :]
        y0 = ybuf[slot][0].reshape(COMBINE_TILE, D)
        y1 = ybuf[slot][1].reshape(COMBINE_TILE, D)
        x = x_ref[rows, :] + gates[:, 0:1] * y0 + gates[:, 1:2] * y1
        h = (x * _rms_scale(x) * gple_ref[...]).astype(BF16)
        gate = _sigmoid(jnp.dot(h, wg_ref[...], preferred_element_type=F32))
        proj = jnp.dot(p_ref[0, rows, :].astype(BF16), wp_ref[...], preferred_element_type=F32)
        x = x + proj * gate
        if final:
            x = x * _rms_scale(x) * gfin_ref[...]
        out_ref[rows, :] = x

    def wait(slot):
        for k in range(TOP_K):
            _wait_rows(ybuf[slot].at[k], sem.at[slot])

    @pl.when(i == 0)
    def _():
        _issue_rows(COMBINE_TILE, dest_ref, 0, row_copy(0))

    wait(0)
    _issue_rows_inline(COMBINE_TILE, dest_ref, tiles, row_copy(1))
    combine(0, 0)
    wait(1)
    _issue_rows_inline(COMBINE_TILE, dest_next_ref, 0, row_copy(0))
    combine(1, 1)

    @pl.when(i == pl.num_programs(0) - 1)
    def _():
        wait(0)


def _combine_ple(x2d, dest, gates, y, p3d, layer, g_ple, w_proj, w_gate, g_final, final):
    t = x2d.shape[0]
    step = 2 * COMBINE_TILE
    nt = t // step
    dest_blk = (1, step // SUBLANES, SLOTS_PER_TILE)
    const = lambda i: (0, 0)
    return pl.pallas_call(
        functools.partial(_combine_kernel, final=final),
        name="combine_final" if final else "combine",
        grid=(nt,),
        in_specs=[
            pl.BlockSpec(dest_blk, lambda i: (i, 0, 0), memory_space=pltpu.SMEM),
            pl.BlockSpec(dest_blk, lambda i: (jnp.minimum(i + 1, nt - 1), 0, 0),
                         memory_space=pltpu.SMEM),
            pl.BlockSpec((step, D), lambda i: (i, 0)),
            pl.BlockSpec((step, TOP_K), lambda i: (i, 0)),
            pl.BlockSpec(memory_space=pl.ANY),
            pl.BlockSpec((1, step, D_PLE), lambda i: (layer, i, 0)),
            pl.BlockSpec((1, D), const),
            pl.BlockSpec((D_PLE, D), const, pipeline_mode=pl.Buffered(1)),
            pl.BlockSpec((D, D), const, pipeline_mode=pl.Buffered(1)),
            pl.BlockSpec((1, D), const),
        ],
        out_specs=pl.BlockSpec((step, D), lambda i: (i, 0)),
        out_shape=jax.ShapeDtypeStruct((t, D), F32),
        scratch_shapes=[
            pltpu.VMEM((TOP_K, COMBINE_TILE // SUBLANES, SUBLANES, D), F32),
            pltpu.VMEM((TOP_K, COMBINE_TILE // SUBLANES, SUBLANES, D), F32),
            pltpu.SemaphoreType.DMA((2,)),
        ],
        compiler_params=_params(1),
    )(dest, dest, x2d, gates, y, p3d, g_ple.reshape(1, D), w_proj.astype(BF16),
      w_gate.astype(BF16), g_final.reshape(1, D))


def _moe_ple(x, p, layer, norm_ffn, w_group, b_group, w_expert, b_expert, w_gate, w_up, w_down,
             norm_ple, ple_w_proj, ple_w_gate, final_norm, final):
    b, s, _ = x.shape
    t = b * s
    x2d = x.reshape(t, D)
    ids, gates, counts = _router(x2d, norm_ffn, w_group, b_group, w_expert, b_expert)

    counts = counts[:, 0]
    padded = (counts + MOE_BLOCK - 1) // MOE_BLOCK * MOE_BLOCK
    pad_end = jnp.cumsum(padded)
    pad_start = pad_end - padded
    n_blocks = t * TOP_K // MOE_BLOCK + N_EXPERTS
    n_pad = n_blocks * MOE_BLOCK
    e = ids[:, 0:TOP_K, :].transpose(1, 0, 2).reshape(TOP_K, t)
    r = ids[:, TOP_K:2 * TOP_K, :].transpose(1, 0, 2).reshape(TOP_K, t)
    experts = jnp.arange(N_EXPERTS, dtype=jnp.int32)
    dest = r + jnp.sum(jnp.where(e[..., None] == experts, pad_start, 0), axis=-1)
    gate_cols = gates[:, 0:TOP_K, :].transpose(0, 2, 1).reshape(t, TOP_K)
    block_row = jnp.arange(n_blocks, dtype=jnp.int32) * MOE_BLOCK
    block_e = jnp.minimum(jnp.sum(pad_end[None, :] <= block_row[:, None], axis=-1),
                          N_EXPERTS - 1).astype(jnp.int32)
    n_used = (pad_end[-1:] // MOE_BLOCK).astype(jnp.int32)
    tail = (n_used[0] + jnp.arange(N_EXPERTS, dtype=jnp.int32)) * MOE_BLOCK
    zrow = jnp.concatenate([jnp.where(padded > 0, pad_end - MOE_BLOCK, -1),
                            jnp.where(tail < n_pad, tail, -1)]).astype(jnp.int32)

    def tiles(a, tile):
        return a.T.reshape(t // tile, tile // SUBLANES, SLOTS_PER_TILE).astype(jnp.int32)

    xs = _dispatch(x2d, norm_ffn, tiles(dest, ROUTER_TILE), zrow, n_pad)
    later = jnp.where((experts[None, :] > block_e[:, None]) & (padded[None, :] > 0), experts[None, :],
                      N_EXPERTS)
    next_e = jnp.min(later, axis=-1)
    next_e = jnp.where(next_e < N_EXPERTS, next_e, -1).astype(jnp.int32)
    y = _experts(xs, block_e, next_e, n_used, layer, w_gate, w_up, w_down)
    out = _combine_ple(x2d, tiles(dest, 2 * COMBINE_TILE), gate_cols, y, p.reshape(-1, t, D_PLE), layer,
                       norm_ple, ple_w_proj, ple_w_gate, final_norm, final)
    return out.reshape(b, s, D)


def _qkv_kernel(x_ref, gq_ref, gkv_ref, wq_ref, wkv_ref, qt_ref, k_ref, vt_ref):
    x = x_ref[0]
    xn = x * _rms_scale(x)
    hq = (xn * gq_ref[...]).astype(BF16)
    hkv = (xn * gkv_ref[...]).astype(BF16)
    q = jnp.dot(hq, wq_ref[...], preferred_element_type=F32) * (ATTN_SCALE * LOG2E)
    kv = jnp.dot(hkv, wkv_ref[...], preferred_element_type=F32)
    qt_ref[0] = q.T.astype(BF16)
    k_ref[0] = kv[:, :D].astype(BF16)
    vt_ref[0] = kv[:, D:].T.astype(BF16)


def _qkv(x, g_q, g_kv, w_q, w_kv):
    b, s, _ = x.shape
    const = lambda bi, si: (0, 0)
    row_major = pl.BlockSpec((1, SEQ_TILE, D), lambda bi, si: (bi, si, 0))
    feat_major = pl.BlockSpec((1, D, SEQ_TILE), lambda bi, si: (bi, 0, si))
    return pl.pallas_call(
        _qkv_kernel,
        name="qkv",
        grid=(b, s // SEQ_TILE),
        in_specs=[
            row_major,
            pl.BlockSpec((1, D), const),
            pl.BlockSpec((1, D), const),
            pl.BlockSpec((D, D), const),
            pl.BlockSpec((D, 2 * D), const),
        ],
        out_specs=[feat_major, row_major, feat_major],
        out_shape=[jax.ShapeDtypeStruct((b, D, s), BF16),
                   jax.ShapeDtypeStruct((b, s, D), BF16),
                   jax.ShapeDtypeStruct((b, D, s), BF16)],
        compiler_params=_params(2),
    )(x, g_q.reshape(1, D), g_kv.reshape(1, D), w_q.astype(BF16), w_kv.astype(BF16))


Q_GROUP = 2 * CHUNK
G_BAND = BAND + CHUNK
PAIR = 2 * B_HEAD_DIM
ONES_ROWS = 16


def _attn_kernel(qt_ref, kp_ref, kc_ref, vtp_ref, vtc_ref, bias_ref, x_ref, wo_ref, out_ref, o_scr,
                 s_scr0, s_scr1, p_scr0, p_scr1):
    drow = lax.broadcasted_iota(jnp.int32, (PAIR, Q_GROUP), 0)
    first_head = drow < B_HEAD_DIM
    s_scr = (s_scr0, s_scr1)
    p_scr = (p_scr0, p_scr1)

    def attend(first_tile):
        units = [(g, pr) for g in range(SEQ_TILE // Q_GROUP) for pr in range(B_HEADS // 2)]

        def geometry(g):
            w0 = g * Q_GROUP
            n_prev = SEQ_TILE - w0
            return w0, n_prev, G_BAND - n_prev

        def keys(g):
            return slice(geometry(g)[1], None) if first_tile else slice(None)

        def scores(unit, s_ref):
            g, pr = unit
            w0, n_prev, n_cur = geometry(g)
            feat = slice(pr * PAIR, (pr + 1) * PAIR)
            qt = qt_ref[0, feat, w0:w0 + Q_GROUP]
            zero = jnp.zeros_like(qt)
            qblk = jnp.concatenate([jnp.where(first_head, qt, zero),
                                    jnp.where(first_head, zero, qt)], axis=1)
            if first_tile:
                kb = kc_ref[0, :n_cur, feat]
            else:
                kb = jnp.concatenate([kp_ref[0, w0:, feat], kc_ref[0, :n_cur, feat]], axis=0)
            s_ref[keys(g), :] = (jnp.dot(kb, qblk, preferred_element_type=F32)
                                 + bias_ref[pr, keys(g), :])

        def weights(unit, s_ref, p_ref):
            g, _ = unit
            s = s_ref[keys(g), :]
            m = jnp.max(s, axis=0, keepdims=True)
            p_ref[keys(g), :] = jnp.exp2(s - m).astype(BF16)

        def values(unit, p_ref):
            g, pr = unit
            w0, n_prev, n_cur = geometry(g)
            feat = slice(pr * PAIR, (pr + 1) * PAIR)
            if first_tile:
                vt = vtc_ref[0, feat, :n_cur]
            else:
                vt = jnp.concatenate([vtp_ref[0, feat, w0:], vtc_ref[0, feat, :n_cur]], axis=1)
            ones = jnp.ones((ONES_ROWS, vt.shape[1]), BF16)
            ot = jnp.dot(jnp.concatenate([vt, ones], axis=0), p_ref[keys(g), :],
                         preferred_element_type=F32)
            inv = 1.0 / ot[PAIR:PAIR + 1, :]
            ot = jnp.where(first_head, ot[:PAIR, :Q_GROUP] * inv[:, :Q_GROUP],
                           ot[:PAIR, Q_GROUP:] * inv[:, Q_GROUP:])
            o_scr[w0:w0 + Q_GROUP, feat] = ot.T.astype(BF16)

        n_units = len(units)
        scores(units[0], s_scr[0])
        scores(units[1], s_scr[1])
        weights(units[0], s_scr[0], p_scr[0])
        for n, unit in enumerate(units):
            if n + 2 < n_units:
                scores(units[n + 2], s_scr[n % 2])
            if n + 1 < n_units:
                weights(units[n + 1], s_scr[(n + 1) % 2], p_scr[(n + 1) % 2])
            values(unit, p_scr[n % 2])

    @pl.when(pl.program_id(1) == 0)
    def _():
        attend(True)

    @pl.when(pl.program_id(1) > 0)
    def _():
        attend(False)

    out_ref[0] = x_ref[0] + jnp.dot(o_scr[...], wo_ref[...], preferred_element_type=F32)


def _attn(x, qt, k, vt, bias_t, w_o):
    b, s, _ = x.shape
    cur = lambda bi, si: (bi, si, 0)
    prev = lambda bi, si: (bi, jnp.maximum(si - 1, 0), 0)
    cur_t = lambda bi, si: (bi, 0, si)
    prev_t = lambda bi, si: (bi, 0, jnp.maximum(si - 1, 0))
    blk = (1, SEQ_TILE, D)
    blk_t = (1, D, SEQ_TILE)
    return pl.pallas_call(
        _attn_kernel,
        name="attn",
        grid=(b, s // SEQ_TILE),
        in_specs=[
            pl.BlockSpec(blk_t, cur_t),
            pl.BlockSpec(blk, prev),
            pl.BlockSpec(blk, cur),
            pl.BlockSpec(blk_t, prev_t),
            pl.BlockSpec(blk_t, cur_t),
            pl.BlockSpec((B_HEADS // 2, G_BAND, 2 * Q_GROUP), lambda bi, si: (0, 0, 0),
                         pipeline_mode=pl.Buffered(1)),
            pl.BlockSpec(blk, cur),
            pl.BlockSpec((D, D), lambda bi, si: (0, 0), pipeline_mode=pl.Buffered(1)),
        ],
        out_specs=pl.BlockSpec(blk, cur),
        out_shape=jax.ShapeDtypeStruct(x.shape, F32),
        scratch_shapes=[pltpu.VMEM((SEQ_TILE, D), BF16)]
        + [pltpu.VMEM((G_BAND, 2 * Q_GROUP), F32)] * 2
        + [pltpu.VMEM((G_BAND, 2 * Q_GROUP), BF16)] * 2,
        compiler_params=_params(2),
    )(qt, k, k, vt, vt, bias_t, x, w_o.astype(BF16))


def _group_bias(table):
    band = _band_bias(table) * LOG2E
    pad = lambda lo, hi: jnp.pad(band, ((0, 0), (0, 0), (lo, hi)), constant_values=NEG_INF)
    both = jnp.concatenate([pad(0, CHUNK), pad(CHUNK, 0)], axis=1)
    both = both.reshape(B_HEADS // 2, 2, Q_GROUP, G_BAND)
    return both.transpose(0, 3, 1, 2).reshape(B_HEADS // 2, G_BAND, 2 * Q_GROUP)


def _band_bias(table):
    n_rel = REL_MAX - REL_MIN + 1
    span = BAND + CHUNK - 1
    head = jnp.broadcast_to(table[:, n_rel - 1:], (table.shape[0], span - n_rel))
    ext = jnp.concatenate([head, table[:, ::-1]], axis=1)
    rows = [ext[:, CHUNK - 1 - q:CHUNK - 1 - q + BAND] for q in range(CHUNK)]
    return jnp.stack(rows, axis=1)


def kernel(x, p, a_w_in, a_lb_logits, a_out_norm, a_w_o, kv_norm, w_kv, b_w_q, b_rel_bias, b_w_o,
           norm_mix, norm_ffn, norm_ple, moe_w_group, moe_b_group, moe_w_expert, moe_b_expert,
           moe_w_gate, moe_w_up, moe_w_down, ple_w_proj, ple_w_gate, final_norm):
    b, s, _ = x.shape
    lower_bounds = jnp.cumsum(jax.nn.softmax(a_lb_logits.astype(F32), axis=0), axis=0)

    def moe(xi, i, final):
        return _moe_ple(xi, p, i, norm_ffn[i], moe_w_group[i], moe_b_group[i], moe_w_expert[i],
                        moe_b_expert[i], moe_w_gate, moe_w_up, moe_w_down, norm_ple[i],
                        ple_w_proj[i], ple_w_gate[i], final_norm, final)

    x = _mixer_a(x, norm_mix[0], a_w_in[0], lower_bounds[0], a_out_norm[0], a_w_o[0])
    x = moe(x, 0, False)

    qt, k, vt = _qkv(x, norm_mix[1], kv_norm, b_w_q[0], w_kv)
    x = _attn(x, qt, k, vt, _group_bias(b_rel_bias[0].astype(F32)), b_w_o[0])
    x = moe(x, 1, True)
    return x
```

```python
import functools

import jax
import jax.numpy as jnp
from jax import lax
from jax.experimental import pallas as pl
from jax.experimental.pallas import tpu as pltpu

F32 = jnp.float32
BF16 = jnp.bfloat16
U32 = jnp.uint32

D = 1024
CHUNK = 64
A_HEADS = 8
A_HEAD_DIM = 128
B_HEADS = 16
B_HEAD_DIM = 64
LEFT_CHUNKS = 8
BAND = (LEFT_CHUNKS + 1) * CHUNK
REL_MIN = -(CHUNK - 1)
REL_MAX = 256
ATTN_SCALE = B_HEAD_DIM ** -0.5
N_GROUPS = 4
EXPERTS_PER_GROUP = 8
N_EXPERTS = 32
TOP_K = 2
D_EXPERT = 512
MOE_BLOCK = 512
D_PLE = 256
EPS = 1e-6
NEG_INF = -1e30
LOG2E = 1.4426950408889634

SEQ_TILE = 512
ROUTER_TILE = 512
COMBINE_TILE = 512
ROUTER_ROWS = 48
VMEM_LIMIT = 56 * 1024 * 1024


def _params(n_axes, vmem=VMEM_LIMIT):
    return pltpu.CompilerParams(dimension_semantics=("arbitrary",) * n_axes,
                                vmem_limit_bytes=vmem)


def _rms_scale(x):
    return lax.rsqrt(jnp.mean(x * x, axis=-1, keepdims=True) + EPS)


def _sigmoid(x):
    return 1.0 / (1.0 + jnp.exp(-x))


def _mixer_a_kernel(x_ref, g_ref, win_ref, lb_ref, onorm_ref, wo_ref, out_ref,
                    proj_scr, o_scr, state_scr, g_scr0, g_scr1, k_scr0, k_scr1,
                    qd_scr0, qd_scr1, qd_scr2, kt_scr0, kt_scr1, kt_scr2, att_scr0, att_scr1, att_scr2):
    @pl.when(pl.program_id(1) == 0)
    def _():
        state_scr[...] = jnp.zeros_like(state_scr)

    x = x_ref[0]
    h = (x * _rms_scale(x) * g_ref[...]).astype(BF16)
    proj_scr[...] = jnp.dot(h, win_ref[...], preferred_element_type=F32)

    row = lax.broadcasted_iota(jnp.int32, (CHUNK, CHUNK), 0)
    col = lax.broadcasted_iota(jnp.int32, (CHUNK, CHUNK), 1)
    causal = row >= col
    tril = causal.astype(BF16)
    lb = lb_ref[...]
    onorm = onorm_ref[...]

    g_scr, k_scr = (g_scr0, g_scr1), (k_scr0, k_scr1)
    qd_scr, kt_scr = (qd_scr0, qd_scr1, qd_scr2), (kt_scr0, kt_scr1, kt_scr2)
    att_scr = (att_scr0, att_scr1, att_scr2)
    n_chunks = SEQ_TILE // CHUNK
    units = [(c, hd) for c in range(n_chunks) for hd in range(A_HEADS)]

    def rows_of(c):
        return slice(c * CHUNK, (c + 1) * CHUNK)

    def decay(c):
        f = lb + (1.0 - lb) * _sigmoid(proj_scr[rows_of(c), D:2 * D])
        logf = jnp.log(f)
        hi = logf.astype(BF16)
        lo = (logf - hi.astype(F32)).astype(BF16)
        g_scr[c % 2][...] = (jnp.dot(tril, hi, preferred_element_type=F32)
                             + jnp.dot(tril, lo, preferred_element_type=F32))
        k_scr[c % 2][...] = 1.0 - f

    def intra(n):
        c, hd = units[n]
        sl = slice(hd * A_HEAD_DIM, (hd + 1) * A_HEAD_DIM)
        gh = g_scr[c % 2][:, sl]
        g_last = gh[CHUNK - 1:CHUNK, :]
        k = k_scr[c % 2][:, sl]
        q_dec = (proj_scr[rows_of(c), sl] * jnp.exp(gh)).astype(BF16)
        k_inv = (k * jnp.exp(-gh)).astype(BF16)
        qd_scr[n % 3][...] = q_dec
        kt_scr[n % 3][...] = (k * jnp.exp(g_last - gh)).astype(BF16)
        att = lax.dot_general(q_dec, k_inv, (((1,), (1,)), ((), ())),
                              preferred_element_type=F32)
        att_scr[n % 3][...] = jnp.where(causal, att, 0.0).astype(BF16)

    def output(n):
        c, hd = units[n]
        sl = slice(hd * A_HEAD_DIM, (hd + 1) * A_HEAD_DIM)
        rows = rows_of(c)
        g_last = g_scr[c % 2][CHUNK - 1:CHUNK, sl]
        v = proj_scr[rows, 2 * D + hd * A_HEAD_DIM:2 * D + (hd + 1) * A_HEAD_DIM]
        st = state_scr[hd]
        o = (jnp.dot(att_scr[n % 3][...], v.astype(BF16), preferred_element_type=F32)
             + lax.dot_general(qd_scr[n % 3][...], st.astype(BF16), (((1,), (1,)), ((), ())),
                               preferred_element_type=F32))
        v_t = v.T.astype(BF16)
        state_scr[hd] = st * jnp.exp(g_last) + jnp.dot(v_t, kt_scr[n % 3][...],
                                                       preferred_element_type=F32)
        o = o * _rms_scale(o)
        og = proj_scr[rows, 3 * D + hd * A_HEAD_DIM:3 * D + (hd + 1) * A_HEAD_DIM]
        o = o * onorm[:, sl] * (og * _sigmoid(og))
        o_scr[rows, sl] = o.astype(BF16)

    decay(0)
    intra(0)
    intra(1)
    for n, (c, hd) in enumerate(units):
        if hd == 0 and c + 1 < n_chunks:
            decay(c + 1)
        if n + 2 < len(units):
            intra(n + 2)
        output(n)
    out_ref[0] = x + jnp.dot(o_scr[...], wo_ref[...], preferred_element_type=F32)


def _mixer_a(x, g, w_in, lb, out_norm, w_o):
    b, s, _ = x.shape
    const = lambda bi, si: (0, 0)
    return pl.pallas_call(
        _mixer_a_kernel,
        name="mixer_a",
        grid=(b, s // SEQ_TILE),
        in_specs=[
            pl.BlockSpec((1, SEQ_TILE, D), lambda bi, si: (bi, si, 0)),
            pl.BlockSpec((1, D), const),
            pl.BlockSpec((D, 4 * D), const, pipeline_mode=pl.Buffered(1)),
            pl.BlockSpec((1, D), const),
            pl.BlockSpec((1, D), const),
            pl.BlockSpec((D, D), const, pipeline_mode=pl.Buffered(1)),
        ],
        out_specs=pl.BlockSpec((1, SEQ_TILE, D), lambda bi, si: (bi, si, 0)),
        out_shape=jax.ShapeDtypeStruct(x.shape, F32),
        scratch_shapes=[
            pltpu.VMEM((SEQ_TILE, 4 * D), F32),
            pltpu.VMEM((SEQ_TILE, D), BF16),
            pltpu.VMEM((A_HEADS, A_HEAD_DIM, A_HEAD_DIM), F32),
        ] + [pltpu.VMEM((CHUNK, D), F32)] * 4
          + [pltpu.VMEM((CHUNK, A_HEAD_DIM), BF16)] * 6
          + [pltpu.VMEM((CHUNK, CHUNK), BF16)] * 3,
        compiler_params=_params(2),
    )(x, g.reshape(1, D), w_in.astype(BF16), lb.reshape(1, D), out_norm.reshape(1, D),
      w_o.astype(BF16))


def _router_kernel(x_ref, g_ref, wr_ref, br_ref, ids_ref, gates_ref, counts_ref, cnt_scr,
                   before_scr):
    tm = ROUTER_TILE

    @pl.when(pl.program_id(0) == 0)
    def _():
        cnt_scr[...] = jnp.zeros_like(cnt_scr)
        tr = lax.broadcasted_iota(jnp.int32, (tm, tm), 0)
        tc = lax.broadcasted_iota(jnp.int32, (tm, tm), 1)
        before_scr[...] = (tr < tc).astype(BF16)

    x = x_ref[...]
    h = x * _rms_scale(x) * g_ref[...]
    h_hi = h.astype(BF16)
    h_lo = (h - h_hi.astype(F32)).astype(BF16)
    nt = (((1,), (1,)), ((), ()))
    both = lax.dot_general(wr_ref[...], h_hi, nt, preferred_element_type=F32)
    cross = lax.dot_general(wr_ref[0:ROUTER_ROWS], h_lo, nt, preferred_element_type=F32)
    logits = both[0:ROUTER_ROWS] + both[ROUTER_ROWS:] + cross + br_ref[...]
    el = logits[0:N_EXPERTS]
    gl = logits[N_EXPERTS:ROUTER_ROWS]
    grow = lax.broadcasted_iota(jnp.int32, gl.shape, 0)
    gl = jnp.where(grow < N_GROUPS, gl, -jnp.inf)
    gmax = jnp.max(gl, axis=0, keepdims=True)
    gsum = jnp.sum(jnp.exp(gl - gmax), axis=0, keepdims=True)
    grp_w = 1.0 / gsum
    gidx = jnp.min(jnp.where(gl == gmax, grow, N_GROUPS), axis=0, keepdims=True)

    erow = lax.broadcasted_iota(jnp.int32, el.shape, 0)
    masked = jnp.where((erow // EXPERTS_PER_GROUP) == gidx, el, -jnp.inf)
    top1 = jnp.max(masked, axis=0, keepdims=True)
    i1 = jnp.min(jnp.where(masked == top1, erow, N_EXPERTS), axis=0, keepdims=True)
    masked2 = jnp.where(erow == i1, -jnp.inf, masked)
    top2 = jnp.max(masked2, axis=0, keepdims=True)
    i2 = jnp.min(jnp.where(masked2 == top2, erow, N_EXPERTS), axis=0, keepdims=True)
    e2 = jnp.exp(top2 - top1)
    denom = 1.0 + e2
    g1 = grp_w * (1.0 / denom)
    g2 = grp_w * (e2 / denom)

    sel1 = erow == i1
    sel2 = erow == i2
    onehot = (sel1 | sel2).astype(BF16)
    prefix = jnp.dot(onehot, before_scr[...], preferred_element_type=F32) + cnt_scr[...]
    r1 = jnp.sum(jnp.where(sel1, prefix, 0.0), axis=0, keepdims=True)
    r2 = jnp.sum(jnp.where(sel2, prefix, 0.0), axis=0, keepdims=True)
    cnt_scr[...] += jnp.sum(onehot.astype(F32), axis=1, keepdims=True)

    zi = jnp.zeros((4, tm), jnp.int32)
    ids_ref[0] = jnp.concatenate(
        [i1, i2, r1.astype(jnp.int32), r2.astype(jnp.int32), zi], axis=0)
    gates_ref[0] = jnp.concatenate([g1, g2, jnp.zeros((6, tm), F32)], axis=0)
    counts_ref[...] = jnp.broadcast_to(cnt_scr[...], counts_ref.shape).astype(jnp.int32)


def _router(x2d, g, w_group, b_group, w_expert, b_expert):
    t = x2d.shape[0]
    nt = t // ROUTER_TILE
    pad = ROUTER_ROWS - N_EXPERTS - N_GROUPS
    wr = jnp.concatenate([w_expert.T, w_group.T, jnp.zeros((pad, D), F32)], axis=0)
    wr_hi = wr.astype(BF16)
    wr_lo = (wr - wr_hi.astype(F32)).astype(BF16)
    br = jnp.concatenate([b_expert, b_group, jnp.zeros((pad,), F32)]).reshape(ROUTER_ROWS, 1)
    const = lambda i: (0, 0)
    return pl.pallas_call(
        _router_kernel,
        name="router",
        grid=(nt,),
        in_specs=[
            pl.BlockSpec((ROUTER_TILE, D), lambda i: (i, 0)),
            pl.BlockSpec((1, D), const),
            pl.BlockSpec((2 * ROUTER_ROWS, D), const),
            pl.BlockSpec((ROUTER_ROWS, 1), const),
        ],
        out_specs=[
            pl.BlockSpec((1, 8, ROUTER_TILE), lambda i: (i, 0, 0)),
            pl.BlockSpec((1, 8, ROUTER_TILE), lambda i: (i, 0, 0)),
            pl.BlockSpec((N_EXPERTS, 128), const),
        ],
        out_shape=[
            jax.ShapeDtypeStruct((nt, 8, ROUTER_TILE), jnp.int32),
            jax.ShapeDtypeStruct((nt, 8, ROUTER_TILE), F32),
            jax.ShapeDtypeStruct((N_EXPERTS, 128), jnp.int32),
        ],
        scratch_shapes=[pltpu.VMEM((N_EXPERTS, 1), F32),
                        pltpu.VMEM((ROUTER_TILE, ROUTER_TILE), BF16)],
        compiler_params=_params(1),
    )(x2d, g.reshape(1, D), jnp.concatenate([wr_hi, wr_lo], axis=0), br)


SUBLANES = 8
SLOTS_PER_TILE = SUBLANES * TOP_K


def _issue_rows(n_rows, dest_ref, first_tile, make_copy):
    def body(j, c):
        for u in range(SUBLANES):
            for k in range(TOP_K):
                slot = u * TOP_K + k
                make_copy(j, u, k, dest_ref[0, first_tile + j, slot]).start(priority=slot % 2)
        return c

    lax.fori_loop(0, n_rows // SUBLANES, body, 0)


def _issue_rows_inline(n_rows, dest_ref, first_tile, make_copy):
    for j in range(n_rows // SUBLANES):
        for u in range(SUBLANES):
            for k in range(TOP_K):
                slot = u * TOP_K + k
                make_copy(j, u, k, dest_ref[0, first_tile + j, slot]).start(priority=slot % 2)


def _pack_bf16_pairs(x):
    half = x.shape[1] // 2
    lo = lax.bitcast_convert_type(x[:, :half].astype(BF16).astype(F32), U32)
    hi = lax.bitcast_convert_type(x[:, half:].astype(BF16).astype(F32), U32)
    return (hi & jnp.uint32(0xFFFF0000)) | (lo >> 16)


def _unpack_bf16_pairs(u):
    lo = lax.bitcast_convert_type(u << 16, F32).astype(BF16)
    hi = lax.bitcast_convert_type(u & jnp.uint32(0xFFFF0000), F32).astype(BF16)
    return lo, hi


def _tile_rows(x):
    return x.reshape(x.shape[0] // SUBLANES, SUBLANES, x.shape[1])


def _wait_rows(buf_ref, sem):
    pltpu.make_async_copy(buf_ref, buf_ref, sem).wait()


def _dispatch_kernel(zrow_ref, dest_ref, x_ref, g_ref, xs_ref, hbuf, zbuf, zsem, sem):
    @pl.when(pl.program_id(0) == 0)
    def _():
        zbuf[...] = jnp.zeros_like(zbuf)

        def zcopy(e):
            return pltpu.make_async_copy(zbuf, xs_ref.at[pl.ds(zrow_ref[e], MOE_BLOCK)], zsem)

        def zstart(e, c):
            @pl.when(zrow_ref[e] >= 0)
            def _():
                zcopy(e).start()
            return c

        def zwait(e, c):
            @pl.when(zrow_ref[e] >= 0)
            def _():
                zcopy(e).wait()
            return c

        lax.fori_loop(0, 2 * N_EXPERTS, zstart, 0)
        lax.fori_loop(0, 2 * N_EXPERTS, zwait, 0)

    i = pl.program_id(0)
    slot = i % 2
    x = x_ref[...]
    hbuf[slot] = _tile_rows(_pack_bf16_pairs(x * _rms_scale(x) * g_ref[...]))

    def row_copy(tile, sub, k, dest):
        return pltpu.make_async_copy(hbuf.at[slot, tile, pl.ds(sub, 1), :], xs_ref.at[dest],
                                     sem.at[slot])

    _issue_rows(ROUTER_TILE, dest_ref, 0, row_copy)

    def drain(which):
        for _ in range(TOP_K):
            _wait_rows(hbuf.at[which], sem.at[which])

    @pl.when(i > 0)
    def _():
        drain(1 - slot)

    @pl.when(i == pl.num_programs(0) - 1)
    def _():
        drain(slot)


def _dispatch(x2d, g, dest, zrow, n_pad):
    t = x2d.shape[0]
    nt = t // ROUTER_TILE
    return pl.pallas_call(
        _dispatch_kernel,
        name="dispatch",
        grid_spec=pltpu.PrefetchScalarGridSpec(
            num_scalar_prefetch=1,
            grid=(nt,),
            in_specs=[
                pl.BlockSpec((1, ROUTER_TILE // SUBLANES, SLOTS_PER_TILE),
                             lambda i, z: (i, 0, 0), memory_space=pltpu.SMEM),
                pl.BlockSpec((ROUTER_TILE, D), lambda i, z: (i, 0)),
                pl.BlockSpec((1, D), lambda i, z: (0, 0)),
            ],
            out_specs=pl.BlockSpec(memory_space=pl.ANY),
            scratch_shapes=[
                pltpu.VMEM((2, ROUTER_TILE // SUBLANES, SUBLANES, D // 2), U32),
                pltpu.VMEM((MOE_BLOCK, 1, D // 2), U32),
                pltpu.SemaphoreType.DMA(()),
                pltpu.SemaphoreType.DMA((2,)),
            ],
        ),
        out_shape=jax.ShapeDtypeStruct((n_pad, 1, D // 2), U32),
        compiler_params=_params(1),
    )(zrow, dest, x2d, g.reshape(1, D))


def _experts_kernel(be_ref, nxt_ref, nu_ref, xs_ref, wg_ref, wu_ref, wd_ref, y_ref,
                    wg_f, wu_f, wd_f, wg_b, wu_b, wd_b, xbuf, ybuf, w_sem, in_sem, out_sem, *, layer):
    i = pl.program_id(0)
    n_used = nu_ref[0]
    slot = i % 2

    def block_rows(ref, blk):
        return ref.at[pl.ds(pl.multiple_of(blk * MOE_BLOCK, MOE_BLOCK), MOE_BLOCK), 0]

    def fetch(blk, into):
        return pltpu.make_async_copy(block_rows(xs_ref, blk), xbuf.at[into], in_sem.at[into])

    def write_back(blk, from_):
        return pltpu.make_async_copy(ybuf.at[from_], block_rows(y_ref, blk), out_sem.at[from_])

    def weight_copies(e):
        return [pltpu.make_async_copy(src.at[layer, e], dst, w_sem.at[n])
                for n, (src, dst) in enumerate(((wg_ref, wg_f), (wu_ref, wu_f), (wd_ref, wd_f)))]

    @pl.when(i == 0)
    def _():
        for c in weight_copies(be_ref[0]):
            c.start()
        fetch(0, 0).start()

    @pl.when(i + 1 < n_used)
    def _():
        fetch(i + 1, 1 - slot).start()

    prev = be_ref[jnp.maximum(i - 1, 0)]

    @pl.when((i < n_used) & ((i == 0) | (be_ref[i] != prev)))
    def _():
        for c in weight_copies(be_ref[i]):
            c.wait()
        wg_b[...] = wg_f[...].astype(BF16)
        wu_b[...] = wu_f[...].astype(BF16)
        wd_b[...] = wd_f[...].astype(BF16)

        @pl.when(nxt_ref[i] >= 0)
        def _():
            for c in weight_copies(nxt_ref[i]):
                c.start()

    @pl.when(i >= 2)
    def _():
        write_back(i - 2, slot).wait()

    @pl.when(i < n_used)
    def _():
        fetch(i, slot).wait()
        h_lo, h_hi = _unpack_bf16_pairs(xbuf[slot])
        half = D // 2
        a = (jnp.dot(h_lo, wg_b[:half], preferred_element_type=F32)
             + jnp.dot(h_hi, wg_b[half:], preferred_element_type=F32))
        u = (jnp.dot(h_lo, wu_b[:half], preferred_element_type=F32)
             + jnp.dot(h_hi, wu_b[half:], preferred_element_type=F32))
        hid = (a * _sigmoid(a) * u).astype(BF16)
        ybuf[slot] = jnp.dot(hid, wd_b[...], preferred_element_type=F32)

    @pl.when(i >= n_used)
    def _():
        ybuf[slot] = jnp.zeros((MOE_BLOCK, D), F32)

    write_back(i, slot).start()

    @pl.when(i == pl.num_programs(0) - 1)
    def _():
        write_back(i - 1, 1 - slot).wait()
        write_back(i, slot).wait()


def _experts(xs, block_e, next_e, n_used, layer, w_gate, w_up, w_down):
    n_pad = xs.shape[0]
    n_blocks = n_pad // MOE_BLOCK
    hbm = pl.BlockSpec(memory_space=pl.ANY)
    return pl.pallas_call(
        functools.partial(_experts_kernel, layer=layer),
        name="experts",
        grid_spec=pltpu.PrefetchScalarGridSpec(
            num_scalar_prefetch=3,
            grid=(n_blocks,),
            in_specs=[hbm, hbm, hbm, hbm],
            out_specs=hbm,
            scratch_shapes=[
                pltpu.VMEM((D, D_EXPERT), F32),
                pltpu.VMEM((D, D_EXPERT), F32),
                pltpu.VMEM((D_EXPERT, D), F32),
                pltpu.VMEM((D, D_EXPERT), BF16),
                pltpu.VMEM((D, D_EXPERT), BF16),
                pltpu.VMEM((D_EXPERT, D), BF16),
                pltpu.VMEM((2, MOE_BLOCK, D // 2), U32),
                pltpu.VMEM((2, MOE_BLOCK, D), F32),
                pltpu.SemaphoreType.DMA((3,)),
                pltpu.SemaphoreType.DMA((2,)),
                pltpu.SemaphoreType.DMA((2,)),
            ],
        ),
        out_shape=jax.ShapeDtypeStruct((n_pad, 1, D), F32),
        compiler_params=_params(1),
    )(block_e, next_e, n_used, xs, w_gate, w_up, w_down)


def _combine_kernel(*refs, final, project):
    (dest_ref, dest_next_ref, x_ref, gate_ref, y_ref, p_ref, gple_ref, wp_ref, wg_ref,
     gfin_ref) = refs[:10]
    if project:
        gq_ref, gkv_ref, wq_ref, wkv_ref, out_ref, qt_ref, k_ref, vt_ref = refs[10:18]
        ybuf0, ybuf1, sem = refs[18:]
    else:
        out_ref, ybuf0, ybuf1, sem = refs[10:]
    i = pl.program_id(0)
    tiles = COMBINE_TILE // SUBLANES
    ybuf = (ybuf0, ybuf1)

    def row_copy(into):
        def make(tile, sub, k, dest):
            return pltpu.make_async_copy(y_ref.at[dest], ybuf[into].at[k, tile, pl.ds(sub, 1), :],
                                         sem.at[into])
        return make

    def combine(half, slot):
        rows = slice(half * COMBINE_TILE, (half + 1) * COMBINE_TILE)
        gates = gate_ref[rows, :]
        y0 = ybuf[slot][0].reshape(COMBINE_TILE, D)
        y1 = ybuf[slot][1].reshape(COMBINE_TILE, D)
        x = x_ref[rows, :] + gates[:, 0:1] * y0 + gates[:, 1:2] * y1
        h = (x * _rms_scale(x) * gple_ref[...]).astype(BF16)
        gate = _sigmoid(jnp.dot(h, wg_ref[...], preferred_element_type=F32))
        proj = jnp.dot(p_ref[0, rows, :].astype(BF16), wp_ref[...], preferred_element_type=F32)
        x = x + proj * gate
        if final:
            x = x * _rms_scale(x) * gfin_ref[...]
        out_ref[rows, :] = x
        if project:
            q, k, v = _qkv_rows(x, gq_ref, gkv_ref, wq_ref, wkv_ref)
            qt_ref[0, :, rows] = q.T.astype(BF16)
            k_ref[0, rows, :] = k.astype(BF16)
            vt_ref[0, :, rows] = v.T.astype(BF16)

    def wait(slot):
        for k in range(TOP_K):
            _wait_rows(ybuf[slot].at[k], sem.at[slot])

    @pl.when(i == 0)
    def _():
        _issue_rows(COMBINE_TILE, dest_ref, 0, row_copy(0))

    wait(0)
    _issue_rows_inline(COMBINE_TILE, dest_ref, tiles, row_copy(1))
    combine(0, 0)
    wait(1)
    _issue_rows_inline(COMBINE_TILE, dest_next_ref, 0, row_copy(0))
    combine(1, 1)

    @pl.when(i == pl.num_programs(0) - 1)
    def _():
        wait(0)


def _qkv_rows(x, gq_ref, gkv_ref, wq_ref, wkv_ref):
    xn = x * _rms_scale(x)
    hq = (xn * gq_ref[...]).astype(BF16)
    hkv = (xn * gkv_ref[...]).astype(BF16)
    q = jnp.dot(hq, wq_ref[...], preferred_element_type=F32) * (ATTN_SCALE * LOG2E)
    kv = jnp.dot(hkv, wkv_ref[...], preferred_element_type=F32)
    return q, kv[:, :D], kv[:, D:]


def _combine_ple(x2d, dest, gates, y, p3d, layer, g_ple, w_proj, w_gate, g_final, final, qkv=None):
    t = x2d.shape[0]
    step = 2 * COMBINE_TILE
    nt = t // step
    dest_blk = (1, step // SUBLANES, SLOTS_PER_TILE)
    const = lambda i: (0, 0)
    resident = functools.partial(pl.BlockSpec, index_map=const, pipeline_mode=pl.Buffered(1))
    in_specs = [
        pl.BlockSpec(dest_blk, lambda i: (i, 0, 0), memory_space=pltpu.SMEM),
        pl.BlockSpec(dest_blk, lambda i: (jnp.minimum(i + 1, nt - 1), 0, 0),
                     memory_space=pltpu.SMEM),
        pl.BlockSpec((step, D), lambda i: (i, 0)),
        pl.BlockSpec((step, TOP_K), lambda i: (i, 0)),
        pl.BlockSpec(memory_space=pl.ANY),
        pl.BlockSpec((1, step, D_PLE), lambda i: (layer, i, 0)),
        pl.BlockSpec((1, D), const),
        resident((D_PLE, D)),
        resident((D, D)),
        pl.BlockSpec((1, D), const),
    ]
    args = [dest, dest, x2d, gates, y, p3d, g_ple.reshape(1, D), w_proj.astype(BF16),
            w_gate.astype(BF16), g_final.reshape(1, D)]
    out_specs = [pl.BlockSpec((step, D), lambda i: (i, 0))]
    out_shape = [jax.ShapeDtypeStruct((t, D), F32)]
    if qkv is not None:
        seq, g_q, g_kv, w_q, w_kv = qkv
        per_seq = seq // step
        in_specs += [pl.BlockSpec((1, D), const), pl.BlockSpec((1, D), const),
                     resident((D, D)), resident((D, 2 * D))]
        args += [g_q.reshape(1, D), g_kv.reshape(1, D), w_q.astype(BF16), w_kv.astype(BF16)]
        feat_major = pl.BlockSpec((1, D, step), lambda i: (i // per_seq, 0, i % per_seq))
        row_major = pl.BlockSpec((1, step, D), lambda i: (i // per_seq, i % per_seq, 0))
        out_specs += [feat_major, row_major, feat_major]
        out_shape += [jax.ShapeDtypeStruct((t // seq, D, seq), BF16),
                      jax.ShapeDtypeStruct((t // seq, seq, D), BF16),
                      jax.ShapeDtypeStruct((t // seq, D, seq), BF16)]
    outs = pl.pallas_call(
        functools.partial(_combine_kernel, final=final, project=qkv is not None),
        name="combine_final" if final else "combine",
        grid=(nt,),
        in_specs=in_specs,
        out_specs=out_specs,
        out_shape=out_shape,
        scratch_shapes=[
            pltpu.VMEM((TOP_K, COMBINE_TILE // SUBLANES, SUBLANES, D), F32),
            pltpu.VMEM((TOP_K, COMBINE_TILE // SUBLANES, SUBLANES, D), F32),
            pltpu.SemaphoreType.DMA((2,)),
        ],
        compiler_params=_params(1),
    )(*args)
    return outs[0] if qkv is None else tuple(outs)


def _moe_ple(x, p, layer, norm_ffn, w_group, b_group, w_expert, b_expert, w_gate, w_up, w_down,
             norm_ple, ple_w_proj, ple_w_gate, final_norm, final, qkv=None):
    b, s, _ = x.shape
    t = b * s
    x2d = x.reshape(t, D)
    ids, gates, counts = _router(x2d, norm_ffn, w_group, b_group, w_expert, b_expert)

    counts = counts[:, 0]
    padded = (counts + MOE_BLOCK - 1) // MOE_BLOCK * MOE_BLOCK
    pad_end = jnp.cumsum(padded)
    pad_start = pad_end - padded
    n_blocks = t * TOP_K // MOE_BLOCK + N_EXPERTS
    n_pad = n_blocks * MOE_BLOCK
    e = ids[:, 0:TOP_K, :].transpose(1, 0, 2).reshape(TOP_K, t)
    r = ids[:, TOP_K:2 * TOP_K, :].transpose(1, 0, 2).reshape(TOP_K, t)
    experts = jnp.arange(N_EXPERTS, dtype=jnp.int32)
    dest = r + jnp.sum(jnp.where(e[..., None] == experts, pad_start, 0), axis=-1)
    gate_cols = gates[:, 0:TOP_K, :].transpose(0, 2, 1).reshape(t, TOP_K)
    block_row = jnp.arange(n_blocks, dtype=jnp.int32) * MOE_BLOCK
    block_e = jnp.minimum(jnp.sum(pad_end[None, :] <= block_row[:, None], axis=-1),
                          N_EXPERTS - 1).astype(jnp.int32)
    n_used = (pad_end[-1:] // MOE_BLOCK).astype(jnp.int32)
    tail = (n_used[0] + jnp.arange(N_EXPERTS, dtype=jnp.int32)) * MOE_BLOCK
    zrow = jnp.concatenate([jnp.where(padded > 0, pad_end - MOE_BLOCK, -1),
                            jnp.where(tail < n_pad, tail, -1)]).astype(jnp.int32)

    def tiles(a, tile):
        return a.T.reshape(t // tile, tile // SUBLANES, SLOTS_PER_TILE).astype(jnp.int32)

    xs = _dispatch(x2d, norm_ffn, tiles(dest, ROUTER_TILE), zrow, n_pad)
    later = jnp.where((experts[None, :] > block_e[:, None]) & (padded[None, :] > 0), experts[None, :],
                      N_EXPERTS)
    next_e = jnp.min(later, axis=-1)
    next_e = jnp.where(next_e < N_EXPERTS, next_e, -1).astype(jnp.int32)
    y = _experts(xs, block_e, next_e, n_used, layer, w_gate, w_up, w_down)
    out = _combine_ple(x2d, tiles(dest, 2 * COMBINE_TILE), gate_cols, y, p.reshape(-1, t, D_PLE), layer,
                       norm_ple, ple_w_proj, ple_w_gate, final_norm, final,
                       None if qkv is None else (s,) + tuple(qkv))
    if qkv is None:
        return out.reshape(b, s, D)
    return (out[0].reshape(b, s, D),) + out[1:]


Q_GROUP = 2 * CHUNK
G_BAND = BAND + CHUNK
PAIR = 2 * B_HEAD_DIM
ONES_ROWS = 16


def _attn_kernel(qt_ref, kp_ref, kc_ref, vtp_ref, vtc_ref, bias_ref, x_ref, wo_ref, out_ref, o_scr,
                 s_scr0, s_scr1, p_scr0, p_scr1):
    drow = lax.broadcasted_iota(jnp.int32, (PAIR, Q_GROUP), 0)
    first_head = drow < B_HEAD_DIM
    s_scr = (s_scr0, s_scr1)
    p_scr = (p_scr0, p_scr1)

    def attend(first_tile):
        units = [(g, pr) for g in range(SEQ_TILE // Q_GROUP) for pr in range(B_HEADS // 2)]

        def geometry(g):
            w0 = g * Q_GROUP
            n_prev = SEQ_TILE - w0
            return w0, n_prev, G_BAND - n_prev

        def keys(g):
            return slice(geometry(g)[1], None) if first_tile else slice(None)

        def scores(unit, s_ref):
            g, pr = unit
            w0, n_prev, n_cur = geometry(g)
            feat = slice(pr * PAIR, (pr + 1) * PAIR)
            qt = qt_ref[0, feat, w0:w0 + Q_GROUP]
            zero = jnp.zeros_like(qt)
            qblk = jnp.concatenate([jnp.where(first_head, qt, zero),
                                    jnp.where(first_head, zero, qt)], axis=1)
            if first_tile:
                kb = kc_ref[0, :n_cur, feat]
            else:
                kb = jnp.concatenate([kp_ref[0, w0:, feat], kc_ref[0, :n_cur, feat]], axis=0)
            s_ref[keys(g), :] = (jnp.dot(kb, qblk, preferred_element_type=F32)
                                 + bias_ref[pr, keys(g), :])

        def weights(unit, s_ref, p_ref):
            g, _ = unit
            s = s_ref[keys(g), :]
            m = jnp.max(s, axis=0, keepdims=True)
            p_ref[keys(g), :] = jnp.exp2(s - m).astype(BF16)

        def values(unit, p_ref):
            g, pr = unit
            w0, n_prev, n_cur = geometry(g)
            feat = slice(pr * PAIR, (pr + 1) * PAIR)
            if first_tile:
                vt = vtc_ref[0, feat, :n_cur]
            else:
                vt = jnp.concatenate([vtp_ref[0, feat, w0:], vtc_ref[0, feat, :n_cur]], axis=1)
            ones = jnp.ones((ONES_ROWS, vt.shape[1]), BF16)
            ot = jnp.dot(jnp.concatenate([vt, ones], axis=0), p_ref[keys(g), :],
                         preferred_element_type=F32)
            inv = 1.0 / ot[PAIR:PAIR + 1, :]
            ot = jnp.where(first_head, ot[:PAIR, :Q_GROUP] * inv[:, :Q_GROUP],
                           ot[:PAIR, Q_GROUP:] * inv[:, Q_GROUP:])
            o_scr[w0:w0 + Q_GROUP, feat] = ot.T.astype(BF16)

        n_units = len(units)
        scores(units[0], s_scr[0])
        scores(units[1], s_scr[1])
        weights(units[0], s_scr[0], p_scr[0])
        for n, unit in enumerate(units):
            if n + 2 < n_units:
                scores(units[n + 2], s_scr[n % 2])
            if n + 1 < n_units:
                weights(units[n + 1], s_scr[(n + 1) % 2], p_scr[(n + 1) % 2])
            values(unit, p_scr[n % 2])

    @pl.when(pl.program_id(1) == 0)
    def _():
        attend(True)

    @pl.when(pl.program_id(1) > 0)
    def _():
        attend(False)

    out_ref[0] = x_ref[0] + jnp.dot(o_scr[...], wo_ref[...], preferred_element_type=F32)


def _attn(x, qt, k, vt, bias_t, w_o):
    b, s, _ = x.shape
    cur = lambda bi, si: (bi, si, 0)
    prev = lambda bi, si: (bi, jnp.maximum(si - 1, 0), 0)
    cur_t = lambda bi, si: (bi, 0, si)
    prev_t = lambda bi, si: (bi, 0, jnp.maximum(si - 1, 0))
    blk = (1, SEQ_TILE, D)
    blk_t = (1, D, SEQ_TILE)
    return pl.pallas_call(
        _attn_kernel,
        name="attn",
        grid=(b, s // SEQ_TILE),
        in_specs=[
            pl.BlockSpec(blk_t, cur_t),
            pl.BlockSpec(blk, prev),
            pl.BlockSpec(blk, cur),
            pl.BlockSpec(blk_t, prev_t),
            pl.BlockSpec(blk_t, cur_t),
            pl.BlockSpec((B_HEADS // 2, G_BAND, 2 * Q_GROUP), lambda bi, si: (0, 0, 0),
                         pipeline_mode=pl.Buffered(1)),
            pl.BlockSpec(blk, cur),
            pl.BlockSpec((D, D), lambda bi, si: (0, 0), pipeline_mode=pl.Buffered(1)),
        ],
        out_specs=pl.BlockSpec(blk, cur),
        out_shape=jax.ShapeDtypeStruct(x.shape, F32),
        scratch_shapes=[pltpu.VMEM((SEQ_TILE, D), BF16)]
        + [pltpu.VMEM((G_BAND, 2 * Q_GROUP), F32)] * 2
        + [pltpu.VMEM((G_BAND, 2 * Q_GROUP), BF16)] * 2,
        compiler_params=_params(2),
    )(qt, k, k, vt, vt, bias_t, x, w_o.astype(BF16))


def _group_bias(table):
    band = _band_bias(table) * LOG2E
    pad = lambda lo, hi: jnp.pad(band, ((0, 0), (0, 0), (lo, hi)), constant_values=NEG_INF)
    both = jnp.concatenate([pad(0, CHUNK), pad(CHUNK, 0)], axis=1)
    both = both.reshape(B_HEADS // 2, 2, Q_GROUP, G_BAND)
    return both.transpose(0, 3, 1, 2).reshape(B_HEADS // 2, G_BAND, 2 * Q_GROUP)


def _band_bias(table):
    n_rel = REL_MAX - REL_MIN + 1
    span = BAND + CHUNK - 1
    head = jnp.broadcast_to(table[:, n_rel - 1:], (table.shape[0], span - n_rel))
    ext = jnp.concatenate([head, table[:, ::-1]], axis=1)
    rows = [ext[:, CHUNK - 1 - q:CHUNK - 1 - q + BAND] for q in range(CHUNK)]
    return jnp.stack(rows, axis=1)


def kernel(x, p, a_w_in, a_lb_logits, a_out_norm, a_w_o, kv_norm, w_kv, b_w_q, b_rel_bias, b_w_o,
           norm_mix, norm_ffn, norm_ple, moe_w_group, moe_b_group, moe_w_expert, moe_b_expert,
           moe_w_gate, moe_w_up, moe_w_down, ple_w_proj, ple_w_gate, final_norm):
    b, s, _ = x.shape
    lower_bounds = jnp.cumsum(jax.nn.softmax(a_lb_logits.astype(F32), axis=0), axis=0)

    def moe(xi, i, final, qkv=None):
        return _moe_ple(xi, p, i, norm_ffn[i], moe_w_group[i], moe_b_group[i], moe_w_expert[i],
                        moe_b_expert[i], moe_w_gate, moe_w_up, moe_w_down, norm_ple[i],
                        ple_w_proj[i], ple_w_gate[i], final_norm, final, qkv)

    x = _mixer_a(x, norm_mix[0], a_w_in[0], lower_bounds[0], a_out_norm[0], a_w_o[0])
    x, qt, k, vt = moe(x, 0, False, (norm_mix[1], kv_norm, b_w_q[0], w_kv))

    x = _attn(x, qt, k, vt, _group_bias(b_rel_bias[0].astype(F32)), b_w_o[0])
    x = moe(x, 1, True)
    return x
```

```python
import functools

import jax
import jax.numpy as jnp
from jax import lax
from jax.experimental import pallas as pl
from jax.experimental.pallas import tpu as pltpu

F32 = jnp.float32
BF16 = jnp.bfloat16
U32 = jnp.uint32

D = 1024
CHUNK = 64
A_HEADS = 8
A_HEAD_DIM = 128
B_HEADS = 16
B_HEAD_DIM = 64
LEFT_CHUNKS = 8
BAND = (LEFT_CHUNKS + 1) * CHUNK
REL_MIN = -(CHUNK - 1)
REL_MAX = 256
ATTN_SCALE = B_HEAD_DIM ** -0.5
N_GROUPS = 4
EXPERTS_PER_GROUP = 8
N_EXPERTS = 32
TOP_K = 2
D_EXPERT = 512
MOE_BLOCK = 512
D_PLE = 256
EPS = 1e-6
NEG_INF = -1e30
LOG2E = 1.4426950408889634

SEQ_TILE = 512
ROUTER_TILE = 512
COMBINE_TILE = 512
ROUTER_ROWS = 48
VMEM_LIMIT = 56 * 1024 * 1024


def _params(n_axes, vmem=VMEM_LIMIT):
    return pltpu.CompilerParams(dimension_semantics=("arbitrary",) * n_axes,
                                vmem_limit_bytes=vmem)


def _rms_scale(x):
    return lax.rsqrt(jnp.mean(x * x, axis=-1, keepdims=True) + EPS)


def _sigmoid(x):
    return 1.0 / (1.0 + jnp.exp(-x))


def _mixer_a_kernel(x_ref, g_ref, win_ref, lb_ref, onorm_ref, wo_ref, out_ref,
                    proj_scr, o_scr, state_scr, g_scr0, g_scr1, k_scr0, k_scr1,
                    qd_scr0, qd_scr1, qd_scr2, kt_scr0, kt_scr1, kt_scr2, att_scr0, att_scr1, att_scr2):
    @pl.when(pl.program_id(1) == 0)
    def _():
        state_scr[...] = jnp.zeros_like(state_scr)

    x = x_ref[0]
    h = (x * _rms_scale(x) * g_ref[...]).astype(BF16)
    proj_scr[...] = jnp.dot(h, win_ref[...], preferred_element_type=F32)

    row = lax.broadcasted_iota(jnp.int32, (CHUNK, CHUNK), 0)
    col = lax.broadcasted_iota(jnp.int32, (CHUNK, CHUNK), 1)
    causal = row >= col
    tril = causal.astype(BF16)
    lb = lb_ref[...]
    onorm = onorm_ref[...]

    g_scr, k_scr = (g_scr0, g_scr1), (k_scr0, k_scr1)
    qd_scr, kt_scr = (qd_scr0, qd_scr1, qd_scr2), (kt_scr0, kt_scr1, kt_scr2)
    att_scr = (att_scr0, att_scr1, att_scr2)
    n_chunks = SEQ_TILE // CHUNK
    units = [(c, hd) for c in range(n_chunks) for hd in range(A_HEADS)]

    def rows_of(c):
        return slice(c * CHUNK, (c + 1) * CHUNK)

    def decay(c):
        f = lb + (1.0 - lb) * _sigmoid(proj_scr[rows_of(c), D:2 * D])
        logf = jnp.log(f)
        hi = logf.astype(BF16)
        lo = (logf - hi.astype(F32)).astype(BF16)
        g_scr[c % 2][...] = (jnp.dot(tril, hi, preferred_element_type=F32)
                             + jnp.dot(tril, lo, preferred_element_type=F32))
        k_scr[c % 2][...] = 1.0 - f

    def intra(n):
        c, hd = units[n]
        sl = slice(hd * A_HEAD_DIM, (hd + 1) * A_HEAD_DIM)
        gh = g_scr[c % 2][:, sl]
        g_last = gh[CHUNK - 1:CHUNK, :]
        k = k_scr[c % 2][:, sl]
        q_dec = (proj_scr[rows_of(c), sl] * jnp.exp(gh)).astype(BF16)
        k_inv = (k * jnp.exp(-gh)).astype(BF16)
        qd_scr[n % 3][...] = q_dec
        kt_scr[n % 3][...] = (k * jnp.exp(g_last - gh)).astype(BF16)
        att = lax.dot_general(q_dec, k_inv, (((1,), (1,)), ((), ())),
                              preferred_element_type=F32)
        att_scr[n % 3][...] = jnp.where(causal, att, 0.0).astype(BF16)

    def output(n):
        c, hd = units[n]
        sl = slice(hd * A_HEAD_DIM, (hd + 1) * A_HEAD_DIM)
        rows = rows_of(c)
        g_last = g_scr[c % 2][CHUNK - 1:CHUNK, sl]
        v = proj_scr[rows, 2 * D + hd * A_HEAD_DIM:2 * D + (hd + 1) * A_HEAD_DIM]
        st = state_scr[hd]
        o = (jnp.dot(att_scr[n % 3][...], v.astype(BF16), preferred_element_type=F32)
             + lax.dot_general(qd_scr[n % 3][...], st.astype(BF16), (((1,), (1,)), ((), ())),
                               preferred_element_type=F32))
        v_t = v.T.astype(BF16)
        state_scr[hd] = st * jnp.exp(g_last) + jnp.dot(v_t, kt_scr[n % 3][...],
                                                       preferred_element_type=F32)
        o = o * _rms_scale(o)
        og = proj_scr[rows, 3 * D + hd * A_HEAD_DIM:3 * D + (hd + 1) * A_HEAD_DIM]
        o = o * onorm[:, sl] * (og * _sigmoid(og))
        o_scr[rows, sl] = o.astype(BF16)

    decay(0)
    intra(0)
    intra(1)
    for n, (c, hd) in enumerate(units):
        if hd == 0 and c + 1 < n_chunks:
            decay(c + 1)
        if n + 2 < len(units):
            intra(n + 2)
        output(n)
    out_ref[0] = x + jnp.dot(o_scr[...], wo_ref[...], preferred_element_type=F32)


def _mixer_a(x, g, w_in, lb, out_norm, w_o):
    b, s, _ = x.shape
    const = lambda bi, si: (0, 0)
    return pl.pallas_call(
        _mixer_a_kernel,
        name="mixer_a",
        grid=(b, s // SEQ_TILE),
        in_specs=[
            pl.BlockSpec((1, SEQ_TILE, D), lambda bi, si: (bi, si, 0)),
            pl.BlockSpec((1, D), const),
            pl.BlockSpec((D, 4 * D), const, pipeline_mode=pl.Buffered(1)),
            pl.BlockSpec((1, D), const),
            pl.BlockSpec((1, D), const),
            pl.BlockSpec((D, D), const, pipeline_mode=pl.Buffered(1)),
        ],
        out_specs=pl.BlockSpec((1, SEQ_TILE, D), lambda bi, si: (bi, si, 0)),
        out_shape=jax.ShapeDtypeStruct(x.shape, F32),
        scratch_shapes=[
            pltpu.VMEM((SEQ_TILE, 4 * D), F32),
            pltpu.VMEM((SEQ_TILE, D), BF16),
            pltpu.VMEM((A_HEADS, A_HEAD_DIM, A_HEAD_DIM), F32),
        ] + [pltpu.VMEM((CHUNK, D), F32)] * 4
          + [pltpu.VMEM((CHUNK, A_HEAD_DIM), BF16)] * 6
          + [pltpu.VMEM((CHUNK, CHUNK), BF16)] * 3,
        compiler_params=_params(2),
    )(x, g.reshape(1, D), w_in.astype(BF16), lb.reshape(1, D), out_norm.reshape(1, D),
      w_o.astype(BF16))


def _router_kernel(x_ref, g_ref, wr_ref, br_ref, ids_ref, gates_ref, counts_ref, cnt_scr,
                   before_scr):
    tm = ROUTER_TILE

    @pl.when(pl.program_id(0) == 0)
    def _():
        cnt_scr[...] = jnp.zeros_like(cnt_scr)
        tr = lax.broadcasted_iota(jnp.int32, (tm, tm), 0)
        tc = lax.broadcasted_iota(jnp.int32, (tm, tm), 1)
        before_scr[...] = (tr < tc).astype(BF16)

    x = x_ref[...]
    h = x * _rms_scale(x) * g_ref[...]
    h_hi = h.astype(BF16)
    h_lo = (h - h_hi.astype(F32)).astype(BF16)
    nt = (((1,), (1,)), ((), ()))
    both = lax.dot_general(wr_ref[...], h_hi, nt, preferred_element_type=F32)
    cross = lax.dot_general(wr_ref[0:ROUTER_ROWS], h_lo, nt, preferred_element_type=F32)
    logits = both[0:ROUTER_ROWS] + both[ROUTER_ROWS:] + cross + br_ref[...]
    el = logits[0:N_EXPERTS]
    gl = logits[N_EXPERTS:ROUTER_ROWS]
    grow = lax.broadcasted_iota(jnp.int32, gl.shape, 0)
    gl = jnp.where(grow < N_GROUPS, gl, -jnp.inf)
    gmax = jnp.max(gl, axis=0, keepdims=True)
    gsum = jnp.sum(jnp.exp(gl - gmax), axis=0, keepdims=True)
    grp_w = 1.0 / gsum
    gidx = jnp.min(jnp.where(gl == gmax, grow, N_GROUPS), axis=0, keepdims=True)

    erow = lax.broadcasted_iota(jnp.int32, el.shape, 0)
    masked = jnp.where((erow // EXPERTS_PER_GROUP) == gidx, el, -jnp.inf)
    top1 = jnp.max(masked, axis=0, keepdims=True)
    i1 = jnp.min(jnp.where(masked == top1, erow, N_EXPERTS), axis=0, keepdims=True)
    masked2 = jnp.where(erow == i1, -jnp.inf, masked)
    top2 = jnp.max(masked2, axis=0, keepdims=True)
    i2 = jnp.min(jnp.where(masked2 == top2, erow, N_EXPERTS), axis=0, keepdims=True)
    e2 = jnp.exp(top2 - top1)
    denom = 1.0 + e2
    g1 = grp_w * (1.0 / denom)
    g2 = grp_w * (e2 / denom)

    sel1 = erow == i1
    sel2 = erow == i2
    onehot = (sel1 | sel2).astype(BF16)
    prefix = jnp.dot(onehot, before_scr[...], preferred_element_type=F32) + cnt_scr[...]
    r1 = jnp.sum(jnp.where(sel1, prefix, 0.0), axis=0, keepdims=True)
    r2 = jnp.sum(jnp.where(sel2, prefix, 0.0), axis=0, keepdims=True)
    cnt_scr[...] += jnp.sum(onehot.astype(F32), axis=1, keepdims=True)

    zi = jnp.zeros((4, tm), jnp.int32)
    ids_ref[0] = jnp.concatenate(
        [i1, i2, r1.astype(jnp.int32), r2.astype(jnp.int32), zi], axis=0)
    gates_ref[0] = jnp.concatenate([g1, g2, jnp.zeros((6, tm), F32)], axis=0)
    counts_ref[...] = jnp.broadcast_to(cnt_scr[...], counts_ref.shape).astype(jnp.int32)


def _router(x2d, g, w_group, b_group, w_expert, b_expert):
    t = x2d.shape[0]
    nt = t // ROUTER_TILE
    pad = ROUTER_ROWS - N_EXPERTS - N_GROUPS
    wr = jnp.concatenate([w_expert.T, w_group.T, jnp.zeros((pad, D), F32)], axis=0)
    wr_hi = wr.astype(BF16)
    wr_lo = (wr - wr_hi.astype(F32)).astype(BF16)
    br = jnp.concatenate([b_expert, b_group, jnp.zeros((pad,), F32)]).reshape(ROUTER_ROWS, 1)
    const = lambda i: (0, 0)
    return pl.pallas_call(
        _router_kernel,
        name="router",
        grid=(nt,),
        in_specs=[
            pl.BlockSpec((ROUTER_TILE, D), lambda i: (i, 0)),
            pl.BlockSpec((1, D), const),
            pl.BlockSpec((2 * ROUTER_ROWS, D), const),
            pl.BlockSpec((ROUTER_ROWS, 1), const),
        ],
        out_specs=[
            pl.BlockSpec((1, 8, ROUTER_TILE), lambda i: (i, 0, 0)),
            pl.BlockSpec((1, 8, ROUTER_TILE), lambda i: (i, 0, 0)),
            pl.BlockSpec((N_EXPERTS, 128), const),
        ],
        out_shape=[
            jax.ShapeDtypeStruct((nt, 8, ROUTER_TILE), jnp.int32),
            jax.ShapeDtypeStruct((nt, 8, ROUTER_TILE), F32),
            jax.ShapeDtypeStruct((N_EXPERTS, 128), jnp.int32),
        ],
        scratch_shapes=[pltpu.VMEM((N_EXPERTS, 1), F32),
                        pltpu.VMEM((ROUTER_TILE, ROUTER_TILE), BF16)],
        compiler_params=_params(1),
    )(x2d, g.reshape(1, D), jnp.concatenate([wr_hi, wr_lo], axis=0), br)


SUBLANES = 8
SLOTS_PER_TILE = SUBLANES * TOP_K


def _issue_rows(n_rows, dest_ref, first_tile, make_copy):
    def body(j, c):
        for u in range(SUBLANES):
            for k in range(TOP_K):
                slot = u * TOP_K + k
                make_copy(j, u, k, dest_ref[0, first_tile + j, slot]).start(priority=slot % 2)
        return c

    lax.fori_loop(0, n_rows // SUBLANES, body, 0)


def _issue_rows_inline(n_rows, dest_ref, first_tile, make_copy):
    for j in range(n_rows // SUBLANES):
        for u in range(SUBLANES):
            for k in range(TOP_K):
                slot = u * TOP_K + k
                make_copy(j, u, k, dest_ref[0, first_tile + j, slot]).start(priority=slot % 2)


def _pack_bf16_pairs(x):
    half = x.shape[1] // 2
    lo = lax.bitcast_convert_type(x[:, :half].astype(BF16).astype(F32), U32)
    hi = lax.bitcast_convert_type(x[:, half:].astype(BF16).astype(F32), U32)
    return (hi & jnp.uint32(0xFFFF0000)) | (lo >> 16)


def _unpack_bf16_pairs(u):
    lo = lax.bitcast_convert_type(u << 16, F32).astype(BF16)
    hi = lax.bitcast_convert_type(u & jnp.uint32(0xFFFF0000), F32).astype(BF16)
    return lo, hi


def _tile_rows(x):
    return x.reshape(x.shape[0] // SUBLANES, SUBLANES, x.shape[1])


def _wait_rows(buf_ref, sem):
    pltpu.make_async_copy(buf_ref, buf_ref, sem).wait()


def _dispatch_kernel(zrow_ref, dest_ref, x_ref, g_ref, xs_ref, hbuf, zbuf, zsem, sem):
    @pl.when(pl.program_id(0) == 0)
    def _():
        zbuf[...] = jnp.zeros_like(zbuf)

        def zcopy(e):
            return pltpu.make_async_copy(zbuf, xs_ref.at[pl.ds(zrow_ref[e], MOE_BLOCK)], zsem)

        def zstart(e, c):
            @pl.when(zrow_ref[e] >= 0)
            def _():
                zcopy(e).start()
            return c

        def zwait(e, c):
            @pl.when(zrow_ref[e] >= 0)
            def _():
                zcopy(e).wait()
            return c

        lax.fori_loop(0, 2 * N_EXPERTS, zstart, 0)
        lax.fori_loop(0, 2 * N_EXPERTS, zwait, 0)

    i = pl.program_id(0)
    slot = i % 2
    x = x_ref[...]
    hbuf[slot] = _tile_rows(_pack_bf16_pairs(x * _rms_scale(x) * g_ref[...]))

    def row_copy(tile, sub, k, dest):
        return pltpu.make_async_copy(hbuf.at[slot, tile, pl.ds(sub, 1), :], xs_ref.at[dest],
                                     sem.at[slot])

    _issue_rows(ROUTER_TILE, dest_ref, 0, row_copy)

    def drain(which):
        for _ in range(TOP_K):
            _wait_rows(hbuf.at[which], sem.at[which])

    @pl.when(i > 0)
    def _():
        drain(1 - slot)

    @pl.when(i == pl.num_programs(0) - 1)
    def _():
        drain(slot)


def _dispatch(x2d, g, dest, zrow, n_pad):
    t = x2d.shape[0]
    nt = t // ROUTER_TILE
    return pl.pallas_call(
        _dispatch_kernel,
        name="dispatch",
        grid_spec=pltpu.PrefetchScalarGridSpec(
            num_scalar_prefetch=1,
            grid=(nt,),
            in_specs=[
                pl.BlockSpec((1, ROUTER_TILE // SUBLANES, SLOTS_PER_TILE),
                             lambda i, z: (i, 0, 0), memory_space=pltpu.SMEM),
                pl.BlockSpec((ROUTER_TILE, D), lambda i, z: (i, 0)),
                pl.BlockSpec((1, D), lambda i, z: (0, 0)),
            ],
            out_specs=pl.BlockSpec(memory_space=pl.ANY),
            scratch_shapes=[
                pltpu.VMEM((2, ROUTER_TILE // SUBLANES, SUBLANES, D // 2), U32),
                pltpu.VMEM((MOE_BLOCK, 1, D // 2), U32),
                pltpu.SemaphoreType.DMA(()),
                pltpu.SemaphoreType.DMA((2,)),
            ],
        ),
        out_shape=jax.ShapeDtypeStruct((n_pad, 1, D // 2), U32),
        compiler_params=_params(1),
    )(zrow, dest, x2d, g.reshape(1, D))


def _experts_kernel(be_ref, nxt_ref, nu_ref, xs_ref, wg_ref, wu_ref, wd_ref, y_ref,
                    wg_f, wu_f, wd_f, wg_b, wu_b, wd_b, xbuf, ybuf, w_sem, in_sem, out_sem, *, layer):
    i = pl.program_id(0)
    n_used = nu_ref[0]
    slot = i % 2

    def block_rows(ref, blk):
        return ref.at[pl.ds(pl.multiple_of(blk * MOE_BLOCK, MOE_BLOCK), MOE_BLOCK), 0]

    def fetch(blk, into):
        return pltpu.make_async_copy(block_rows(xs_ref, blk), xbuf.at[into], in_sem.at[into])

    def write_back(blk, from_):
        return pltpu.make_async_copy(ybuf.at[from_], block_rows(y_ref, blk), out_sem.at[from_])

    def weight_copies(e):
        return [pltpu.make_async_copy(src.at[layer, e], dst, w_sem.at[n])
                for n, (src, dst) in enumerate(((wg_ref, wg_f), (wu_ref, wu_f), (wd_ref, wd_f)))]

    @pl.when(i == 0)
    def _():
        for c in weight_copies(be_ref[0]):
            c.start()
        fetch(0, 0).start()

    @pl.when(i + 1 < n_used)
    def _():
        fetch(i + 1, 1 - slot).start()

    prev = be_ref[jnp.maximum(i - 1, 0)]

    @pl.when((i < n_used) & ((i == 0) | (be_ref[i] != prev)))
    def _():
        for c in weight_copies(be_ref[i]):
            c.wait()
        wg_b[...] = wg_f[...].astype(BF16)
        wu_b[...] = wu_f[...].astype(BF16)
        wd_b[...] = wd_f[...].astype(BF16)

        @pl.when(nxt_ref[i] >= 0)
        def _():
            for c in weight_copies(nxt_ref[i]):
                c.start()

    @pl.when(i >= 2)
    def _():
        write_back(i - 2, slot).wait()

    @pl.when(i < n_used)
    def _():
        fetch(i, slot).wait()
        h_lo, h_hi = _unpack_bf16_pairs(xbuf[slot])
        half = D // 2
        a = (jnp.dot(h_lo, wg_b[:half], preferred_element_type=F32)
             + jnp.dot(h_hi, wg_b[half:], preferred_element_type=F32))
        u = (jnp.dot(h_lo, wu_b[:half], preferred_element_type=F32)
             + jnp.dot(h_hi, wu_b[half:], preferred_element_type=F32))
        hid = (a * _sigmoid(a) * u).astype(BF16)
        ybuf[slot] = jnp.dot(hid, wd_b[...], preferred_element_type=F32)

    @pl.when(i >= n_used)
    def _():
        ybuf[slot] = jnp.zeros((MOE_BLOCK, D), F32)

    write_back(i, slot).start()

    @pl.when(i == pl.num_programs(0) - 1)
    def _():
        write_back(i - 1, 1 - slot).wait()
        write_back(i, slot).wait()


def _experts(xs, block_e, next_e, n_used, layer, w_gate, w_up, w_down):
    n_pad = xs.shape[0]
    n_blocks = n_pad // MOE_BLOCK
    hbm = pl.BlockSpec(memory_space=pl.ANY)
    return pl.pallas_call(
        functools.partial(_experts_kernel, layer=layer),
        name="experts",
        grid_spec=pltpu.PrefetchScalarGridSpec(
            num_scalar_prefetch=3,
            grid=(n_blocks,),
            in_specs=[hbm, hbm, hbm, hbm],
            out_specs=hbm,
            scratch_shapes=[
                pltpu.VMEM((D, D_EXPERT), F32),
                pltpu.VMEM((D, D_EXPERT), F32),
                pltpu.VMEM((D_EXPERT, D), F32),
                pltpu.VMEM((D, D_EXPERT), BF16),
                pltpu.VMEM((D, D_EXPERT), BF16),
                pltpu.VMEM((D_EXPERT, D), BF16),
                pltpu.VMEM((2, MOE_BLOCK, D // 2), U32),
                pltpu.VMEM((2, MOE_BLOCK, D), F32),
                pltpu.SemaphoreType.DMA((3,)),
                pltpu.SemaphoreType.DMA((2,)),
                pltpu.SemaphoreType.DMA((2,)),
            ],
        ),
        out_shape=jax.ShapeDtypeStruct((n_pad, 1, D), F32),
        compiler_params=_params(1),
    )(block_e, next_e, n_used, xs, w_gate, w_up, w_down)


def _combine_kernel(*refs, final, project):
    (dest_ref, dest_next_ref, x_ref, gate_ref, y_ref, p_ref, gple_ref, wp_ref, wg_ref,
     gfin_ref) = refs[:10]
    if project:
        gq_ref, gkv_ref, wq_ref, wkv_ref, out_ref, qt_ref, k_ref, vt_ref = refs[10:18]
        ybuf0, ybuf1, sem = refs[18:]
    else:
        out_ref, ybuf0, ybuf1, sem = refs[10:]
    i = pl.program_id(0)
    tiles = COMBINE_TILE // SUBLANES
    ybuf = (ybuf0, ybuf1)

    def row_copy(into):
        def make(tile, sub, k, dest):
            return pltpu.make_async_copy(y_ref.at[dest], ybuf[into].at[k, tile, pl.ds(sub, 1), :],
                                         sem.at[into])
        return make

    def combine(half, slot):
        rows = slice(half * COMBINE_TILE, (half + 1) * COMBINE_TILE)
        gates = gate_ref[rows, :]
        y0 = ybuf[slot][0].reshape(COMBINE_TILE, D)
        y1 = ybuf[slot][1].reshape(COMBINE_TILE, D)
        x = x_ref[rows, :] + gates[:, 0:1] * y0 + gates[:, 1:2] * y1
        h = (x * _rms_scale(x) * gple_ref[...]).astype(BF16)
        gate = _sigmoid(jnp.dot(h, wg_ref[...], preferred_element_type=F32))
        proj = jnp.dot(p_ref[0, rows, :].astype(BF16), wp_ref[...], preferred_element_type=F32)
        x = x + proj * gate
        if final:
            x = x * _rms_scale(x) * gfin_ref[...]
        out_ref[rows, :] = x
        if project:
            @pl.when(i >= 0)
            def _():
                q, k, v = _qkv_rows(out_ref[rows, :], gq_ref, gkv_ref, wq_ref, wkv_ref)
                qt_ref[0, :, rows] = q.T.astype(BF16)
                k_ref[0, rows, :] = k.astype(BF16)
                vt_ref[0, :, rows] = v.T.astype(BF16)

    def wait(slot):
        for k in range(TOP_K):
            _wait_rows(ybuf[slot].at[k], sem.at[slot])

    @pl.when(i == 0)
    def _():
        _issue_rows(COMBINE_TILE, dest_ref, 0, row_copy(0))

    wait(0)
    _issue_rows_inline(COMBINE_TILE, dest_ref, tiles, row_copy(1))
    combine(0, 0)
    wait(1)
    _issue_rows_inline(COMBINE_TILE, dest_next_ref, 0, row_copy(0))
    combine(1, 1)

    @pl.when(i == pl.num_programs(0) - 1)
    def _():
        wait(0)


def _qkv_rows(x, gq_ref, gkv_ref, wq_ref, wkv_ref):
    xn = x * _rms_scale(x)
    hq = (xn * gq_ref[...]).astype(BF16)
    hkv = (xn * gkv_ref[...]).astype(BF16)
    q = jnp.dot(hq, wq_ref[...], preferred_element_type=F32) * (ATTN_SCALE * LOG2E)
    kv = jnp.dot(hkv, wkv_ref[...], preferred_element_type=F32)
    return q, kv[:, :D], kv[:, D:]


def _combine_ple(x2d, dest, gates, y, p3d, layer, g_ple, w_proj, w_gate, g_final, final, qkv=None):
    t = x2d.shape[0]
    step = 2 * COMBINE_TILE
    nt = t // step
    dest_blk = (1, step // SUBLANES, SLOTS_PER_TILE)
    const = lambda i: (0, 0)
    resident = functools.partial(pl.BlockSpec, index_map=const, pipeline_mode=pl.Buffered(1))
    in_specs = [
        pl.BlockSpec(dest_blk, lambda i: (i, 0, 0), memory_space=pltpu.SMEM),
        pl.BlockSpec(dest_blk, lambda i: (jnp.minimum(i + 1, nt - 1), 0, 0),
                     memory_space=pltpu.SMEM),
        pl.BlockSpec((step, D), lambda i: (i, 0)),
        pl.BlockSpec((step, TOP_K), lambda i: (i, 0)),
        pl.BlockSpec(memory_space=pl.ANY),
        pl.BlockSpec((1, step, D_PLE), lambda i: (layer, i, 0)),
        pl.BlockSpec((1, D), const),
        resident((D_PLE, D)),
        resident((D, D)),
        pl.BlockSpec((1, D), const),
    ]
    args = [dest, dest, x2d, gates, y, p3d, g_ple.reshape(1, D), w_proj.astype(BF16),
            w_gate.astype(BF16), g_final.reshape(1, D)]
    out_specs = [pl.BlockSpec((step, D), lambda i: (i, 0))]
    out_shape = [jax.ShapeDtypeStruct((t, D), F32)]
    if qkv is not None:
        seq, g_q, g_kv, w_q, w_kv = qkv
        per_seq = seq // step
        in_specs += [pl.BlockSpec((1, D), const), pl.BlockSpec((1, D), const),
                     resident((D, D)), resident((D, 2 * D))]
        args += [g_q.reshape(1, D), g_kv.reshape(1, D), w_q.astype(BF16), w_kv.astype(BF16)]
        feat_major = pl.BlockSpec((1, D, step), lambda i: (i // per_seq, 0, i % per_seq))
        row_major = pl.BlockSpec((1, step, D), lambda i: (i // per_seq, i % per_seq, 0))
        out_specs += [feat_major, row_major, feat_major]
        out_shape += [jax.ShapeDtypeStruct((t // seq, D, seq), BF16),
                      jax.ShapeDtypeStruct((t // seq, seq, D), BF16),
                      jax.ShapeDtypeStruct((t // seq, D, seq), BF16)]
    outs = pl.pallas_call(
        functools.partial(_combine_kernel, final=final, project=qkv is not None),
        name="combine_final" if final else "combine",
        grid=(nt,),
        in_specs=in_specs,
        out_specs=out_specs,
        out_shape=out_shape,
        scratch_shapes=[
            pltpu.VMEM((TOP_K, COMBINE_TILE // SUBLANES, SUBLANES, D), F32),
            pltpu.VMEM((TOP_K, COMBINE_TILE // SUBLANES, SUBLANES, D), F32),
            pltpu.SemaphoreType.DMA((2,)),
        ],
        compiler_params=_params(1),
    )(*args)
    return outs[0] if qkv is None else tuple(outs)


def _moe_ple(x, p, layer, norm_ffn, w_group, b_group, w_expert, b_expert, w_gate, w_up, w_down,
             norm_ple, ple_w_proj, ple_w_gate, final_norm, final, qkv=None):
    b, s, _ = x.shape
    t = b * s
    x2d = x.reshape(t, D)
    ids, gates, counts = _router(x2d, norm_ffn, w_group, b_group, w_expert, b_expert)

    counts = counts[:, 0]
    padded = (counts + MOE_BLOCK - 1) // MOE_BLOCK * MOE_BLOCK
    pad_end = jnp.cumsum(padded)
    pad_start = pad_end - padded
    n_blocks = t * TOP_K // MOE_BLOCK + N_EXPERTS
    n_pad = n_blocks * MOE_BLOCK
    e = ids[:, 0:TOP_K, :].transpose(1, 0, 2).reshape(TOP_K, t)
    r = ids[:, TOP_K:2 * TOP_K, :].transpose(1, 0, 2).reshape(TOP_K, t)
    experts = jnp.arange(N_EXPERTS, dtype=jnp.int32)
    dest = r + jnp.sum(jnp.where(e[..., None] == experts, pad_start, 0), axis=-1)
    gate_cols = gates[:, 0:TOP_K, :].transpose(0, 2, 1).reshape(t, TOP_K)
    block_row = jnp.arange(n_blocks, dtype=jnp.int32) * MOE_BLOCK
    block_e = jnp.minimum(jnp.sum(pad_end[None, :] <= block_row[:, None], axis=-1),
                          N_EXPERTS - 1).astype(jnp.int32)
    n_used = (pad_end[-1:] // MOE_BLOCK).astype(jnp.int32)
    tail = (n_used[0] + jnp.arange(N_EXPERTS, dtype=jnp.int32)) * MOE_BLOCK
    zrow = jnp.concatenate([jnp.where(padded > 0, pad_end - MOE_BLOCK, -1),
                            jnp.where(tail < n_pad, tail, -1)]).astype(jnp.int32)

    def tiles(a, tile):
        return a.T.reshape(t // tile, tile // SUBLANES, SLOTS_PER_TILE).astype(jnp.int32)

    xs = _dispatch(x2d, norm_ffn, tiles(dest, ROUTER_TILE), zrow, n_pad)
    later = jnp.where((experts[None, :] > block_e[:, None]) & (padded[None, :] > 0), experts[None, :],
                      N_EXPERTS)
    next_e = jnp.min(later, axis=-1)
    next_e = jnp.where(next_e < N_EXPERTS, next_e, -1).astype(jnp.int32)
    y = _experts(xs, block_e, next_e, n_used, layer, w_gate, w_up, w_down)
    out = _combine_ple(x2d, tiles(dest, 2 * COMBINE_TILE), gate_cols, y, p.reshape(-1, t, D_PLE), layer,
                       norm_ple, ple_w_proj, ple_w_gate, final_norm, final,
                       None if qkv is None else (s,) + tuple(qkv))
    if qkv is None:
        return out.reshape(b, s, D)
    return (out[0].reshape(b, s, D),) + out[1:]


Q_GROUP = 2 * CHUNK
G_BAND = BAND + CHUNK
PAIR = 2 * B_HEAD_DIM
ONES_ROWS = 16


def _attn_kernel(qt_ref, kp_ref, kc_ref, vtp_ref, vtc_ref, bias_ref, x_ref, wo_ref, out_ref, o_scr,
                 s_scr0, s_scr1, p_scr0, p_scr1):
    drow = lax.broadcasted_iota(jnp.int32, (PAIR, Q_GROUP), 0)
    first_head = drow < B_HEAD_DIM
    s_scr = (s_scr0, s_scr1)
    p_scr = (p_scr0, p_scr1)

    def attend(first_tile):
        units = [(g, pr) for g in range(SEQ_TILE // Q_GROUP) for pr in range(B_HEADS // 2)]

        def geometry(g):
            w0 = g * Q_GROUP
            n_prev = SEQ_TILE - w0
            return w0, n_prev, G_BAND - n_prev

        def keys(g):
            return slice(geometry(g)[1], None) if first_tile else slice(None)

        def scores(unit, s_ref):
            g, pr = unit
            w0, n_prev, n_cur = geometry(g)
            feat = slice(pr * PAIR, (pr + 1) * PAIR)
            qt = qt_ref[0, feat, w0:w0 + Q_GROUP]
            zero = jnp.zeros_like(qt)
            qblk = jnp.concatenate([jnp.where(first_head, qt, zero),
                                    jnp.where(first_head, zero, qt)], axis=1)
            if first_tile:
                kb = kc_ref[0, :n_cur, feat]
            else:
                kb = jnp.concatenate([kp_ref[0, w0:, feat], kc_ref[0, :n_cur, feat]], axis=0)
            s_ref[keys(g), :] = (jnp.dot(kb, qblk, preferred_element_type=F32)
                                 + bias_ref[pr, keys(g), :])

        def weights(unit, s_ref, p_ref):
            g, _ = unit
            s = s_ref[keys(g), :]
            m = jnp.max(s, axis=0, keepdims=True)
            p_ref[keys(g), :] = jnp.exp2(s - m).astype(BF16)

        def values(unit, p_ref):
            g, pr = unit
            w0, n_prev, n_cur = geometry(g)
            feat = slice(pr * PAIR, (pr + 1) * PAIR)
            if first_tile:
                vt = vtc_ref[0, feat, :n_cur]
            else:
                vt = jnp.concatenate([vtp_ref[0, feat, w0:], vtc_ref[0, feat, :n_cur]], axis=1)
            ones = jnp.ones((ONES_ROWS, vt.shape[1]), BF16)
            ot = jnp.dot(jnp.concatenate([vt, ones], axis=0), p_ref[keys(g), :],
                         preferred_element_type=F32)
            inv = 1.0 / ot[PAIR:PAIR + 1, :]
            ot = jnp.where(first_head, ot[:PAIR, :Q_GROUP] * inv[:, :Q_GROUP],
                           ot[:PAIR, Q_GROUP:] * inv[:, Q_GROUP:])
            o_scr[w0:w0 + Q_GROUP, feat] = ot.T.astype(BF16)

        n_units = len(units)
        scores(units[0], s_scr[0])
        scores(units[1], s_scr[1])
        weights(units[0], s_scr[0], p_scr[0])
        for n, unit in enumerate(units):
            if n + 2 < n_units:
                scores(units[n + 2], s_scr[n % 2])
            if n + 1 < n_units:
                weights(units[n + 1], s_scr[(n + 1) % 2], p_scr[(n + 1) % 2])
            values(unit, p_scr[n % 2])

    @pl.when(pl.program_id(1) == 0)
    def _():
        attend(True)

    @pl.when(pl.program_id(1) > 0)
    def _():
        attend(False)

    out_ref[0] = x_ref[0] + jnp.dot(o_scr[...], wo_ref[...], preferred_element_type=F32)


def _attn(x, qt, k, vt, bias_t, w_o):
    b, s, _ = x.shape
    cur = lambda bi, si: (bi, si, 0)
    prev = lambda bi, si: (bi, jnp.maximum(si - 1, 0), 0)
    cur_t = lambda bi, si: (bi, 0, si)
    prev_t = lambda bi, si: (bi, 0, jnp.maximum(si - 1, 0))
    blk = (1, SEQ_TILE, D)
    blk_t = (1, D, SEQ_TILE)
    return pl.pallas_call(
        _attn_kernel,
        name="attn",
        grid=(b, s // SEQ_TILE),
        in_specs=[
            pl.BlockSpec(blk_t, cur_t),
            pl.BlockSpec(blk, prev),
            pl.BlockSpec(blk, cur),
            pl.BlockSpec(blk_t, prev_t),
            pl.BlockSpec(blk_t, cur_t),
            pl.BlockSpec((B_HEADS // 2, G_BAND, 2 * Q_GROUP), lambda bi, si: (0, 0, 0),
                         pipeline_mode=pl.Buffered(1)),
            pl.BlockSpec(blk, cur),
            pl.BlockSpec((D, D), lambda bi, si: (0, 0), pipeline_mode=pl.Buffered(1)),
        ],
        out_specs=pl.BlockSpec(blk, cur),
        out_shape=jax.ShapeDtypeStruct(x.shape, F32),
        scratch_shapes=[pltpu.VMEM((SEQ_TILE, D), BF16)]
        + [pltpu.VMEM((G_BAND, 2 * Q_GROUP), F32)] * 2
        + [pltpu.VMEM((G_BAND, 2 * Q_GROUP), BF16)] * 2,
        compiler_params=_params(2),
    )(qt, k, k, vt, vt, bias_t, x, w_o.astype(BF16))


def _group_bias(table):
    band = _band_bias(table) * LOG2E
    pad = lambda lo, hi: jnp.pad(band, ((0, 0), (0, 0), (lo, hi)), constant_values=NEG_INF)
    both = jnp.concatenate([pad(0, CHUNK), pad(CHUNK, 0)], axis=1)
    both = both.reshape(B_HEADS // 2, 2, Q_GROUP, G_BAND)
    return both.transpose(0, 3, 1, 2).reshape(B_HEADS // 2, G_BAND, 2 * Q_GROUP)


def _band_bias(table):
    n_rel = REL_MAX - REL_MIN + 1
    span = BAND + CHUNK - 1
    head = jnp.broadcast_to(table[:, n_rel - 1:], (table.shape[0], span - n_rel))
    ext = jnp.concatenate([head, table[:, ::-1]], axis=1)
    rows = [ext[:, CHUNK - 1 - q:CHUNK - 1 - q + BAND] for q in range(CHUNK)]
    return jnp.stack(rows, axis=1)


def kernel(x, p, a_w_in, a_lb_logits, a_out_norm, a_w_o, kv_norm, w_kv, b_w_q, b_rel_bias, b_w_o,
           norm_mix, norm_ffn, norm_ple, moe_w_group, moe_b_group, moe_w_expert, moe_b_expert,
           moe_w_gate, moe_w_up, moe_w_down, ple_w_proj, ple_w_gate, final_norm):
    b, s, _ = x.shape
    lower_bounds = jnp.cumsum(jax.nn.softmax(a_lb_logits.astype(F32), axis=0), axis=0)

    def moe(xi, i, final, qkv=None):
        return _moe_ple(xi, p, i, norm_ffn[i], moe_w_group[i], moe_b_group[i], moe_w_expert[i],
                        moe_b_expert[i], moe_w_gate, moe_w_up, moe_w_down, norm_ple[i],
                        ple_w_proj[i], ple_w_gate[i], final_norm, final, qkv)

    x = _mixer_a(x, norm_mix[0], a_w_in[0], lower_bounds[0], a_out_norm[0], a_w_o[0])
    x, qt, k, vt = moe(x, 0, False, (norm_mix[1], kv_norm, b_w_q[0], w_kv))

    x = _attn(x, qt, k, vt, _group_bias(b_rel_bias[0].astype(F32)), b_w_o[0])
    x = moe(x, 1, True)
    return x
```

```python
import functools

import jax
import jax.numpy as jnp
from jax import lax
from jax.experimental import pallas as pl
from jax.experimental.pallas import tpu as pltpu

F32 = jnp.float32
BF16 = jnp.bfloat16
U32 = jnp.uint32

D = 1024
CHUNK = 64
A_HEADS = 8
A_HEAD_DIM = 128
B_HEADS = 16
B_HEAD_DIM = 64
LEFT_CHUNKS = 8
BAND = (LEFT_CHUNKS + 1) * CHUNK
REL_MIN = -(CHUNK - 1)
REL_MAX = 256
ATTN_SCALE = B_HEAD_DIM ** -0.5
N_GROUPS = 4
EXPERTS_PER_GROUP = 8
N_EXPERTS = 32
TOP_K = 2
D_EXPERT = 512
MOE_BLOCK = 512
D_PLE = 256
EPS = 1e-6
NEG_INF = -1e30
LOG2E = 1.4426950408889634

SEQ_TILE = 512
ROUTER_TILE = 512
COMBINE_TILE = 512
ROUTER_ROWS = 48
VMEM_LIMIT = 56 * 1024 * 1024


def _params(n_axes, vmem=VMEM_LIMIT):
    return pltpu.CompilerParams(dimension_semantics=("arbitrary",) * n_axes,
                                vmem_limit_bytes=vmem)


def _rms_scale(x):
    return lax.rsqrt(jnp.mean(x * x, axis=-1, keepdims=True) + EPS)


def _sigmoid(x):
    return 1.0 / (1.0 + jnp.exp(-x))


def _mixer_a_kernel(x_ref, g_ref, win_ref, lb_ref, onorm_ref, wo_ref, out_ref,
                    proj_scr, o_scr, state_scr, g_scr0, g_scr1, k_scr0, k_scr1,
                    qd_scr0, qd_scr1, qd_scr2, kt_scr0, kt_scr1, kt_scr2, att_scr0, att_scr1, att_scr2):
    @pl.when(pl.program_id(1) == 0)
    def _():
        state_scr[...] = jnp.zeros_like(state_scr)

    x = x_ref[0]
    h = (x * _rms_scale(x) * g_ref[...]).astype(BF16)
    proj_scr[...] = jnp.dot(h, win_ref[...], preferred_element_type=F32)

    row = lax.broadcasted_iota(jnp.int32, (CHUNK, CHUNK), 0)
    col = lax.broadcasted_iota(jnp.int32, (CHUNK, CHUNK), 1)
    causal = row >= col
    tril = causal.astype(BF16)
    lb = lb_ref[...]
    onorm = onorm_ref[...]

    g_scr, k_scr = (g_scr0, g_scr1), (k_scr0, k_scr1)
    qd_scr, kt_scr = (qd_scr0, qd_scr1, qd_scr2), (kt_scr0, kt_scr1, kt_scr2)
    att_scr = (att_scr0, att_scr1, att_scr2)
    n_chunks = SEQ_TILE // CHUNK
    units = [(c, hd) for c in range(n_chunks) for hd in range(A_HEADS)]

    def rows_of(c):
        return slice(c * CHUNK, (c + 1) * CHUNK)

    def decay(c):
        f = lb + (1.0 - lb) * _sigmoid(proj_scr[rows_of(c), D:2 * D])
        logf = jnp.log(f)
        hi = logf.astype(BF16)
        lo = (logf - hi.astype(F32)).astype(BF16)
        g_scr[c % 2][...] = (jnp.dot(tril, hi, preferred_element_type=F32)
                             + jnp.dot(tril, lo, preferred_element_type=F32))
        k_scr[c % 2][...] = 1.0 - f

    def intra(n):
        c, hd = units[n]
        sl = slice(hd * A_HEAD_DIM, (hd + 1) * A_HEAD_DIM)
        gh = g_scr[c % 2][:, sl]
        g_last = gh[CHUNK - 1:CHUNK, :]
        k = k_scr[c % 2][:, sl]
        q_dec = (proj_scr[rows_of(c), sl] * jnp.exp(gh)).astype(BF16)
        k_inv = (k * jnp.exp(-gh)).astype(BF16)
        qd_scr[n % 3][...] = q_dec
        kt_scr[n % 3][...] = (k * jnp.exp(g_last - gh)).astype(BF16)
        att = lax.dot_general(q_dec, k_inv, (((1,), (1,)), ((), ())),
                              preferred_element_type=F32)
        att_scr[n % 3][...] = jnp.where(causal, att, 0.0).astype(BF16)

    def output(n):
        c, hd = units[n]
        sl = slice(hd * A_HEAD_DIM, (hd + 1) * A_HEAD_DIM)
        rows = rows_of(c)
        g_last = g_scr[c % 2][CHUNK - 1:CHUNK, sl]
        v = proj_scr[rows, 2 * D + hd * A_HEAD_DIM:2 * D + (hd + 1) * A_HEAD_DIM]
        st = state_scr[hd]
        o = (jnp.dot(att_scr[n % 3][...], v.astype(BF16), preferred_element_type=F32)
             + lax.dot_general(qd_scr[n % 3][...], st.astype(BF16), (((1,), (1,)), ((), ())),
                               preferred_element_type=F32))
        v_t = v.T.astype(BF16)
        state_scr[hd] = st * jnp.exp(g_last) + jnp.dot(v_t, kt_scr[n % 3][...],
                                                       preferred_element_type=F32)
        o = o * _rms_scale(o)
        og = proj_scr[rows, 3 * D + hd * A_HEAD_DIM:3 * D + (hd + 1) * A_HEAD_DIM]
        o = o * onorm[:, sl] * (og * _sigmoid(og))
        o_scr[rows, sl] = o.astype(BF16)

    decay(0)
    intra(0)
    intra(1)
    for n, (c, hd) in enumerate(units):
        if hd == 0 and c + 1 < n_chunks:
            decay(c + 1)
        if n + 2 < len(units):
            intra(n + 2)
        output(n)
    out_ref[0] = x + jnp.dot(o_scr[...], wo_ref[...], preferred_element_type=F32)


def _mixer_a(x, g, w_in, lb, out_norm, w_o):
    b, s, _ = x.shape
    const = lambda bi, si: (0, 0)
    return pl.pallas_call(
        _mixer_a_kernel,
        name="mixer_a",
        grid=(b, s // SEQ_TILE),
        in_specs=[
            pl.BlockSpec((1, SEQ_TILE, D), lambda bi, si: (bi, si, 0)),
            pl.BlockSpec((1, D), const),
            pl.BlockSpec((D, 4 * D), const, pipeline_mode=pl.Buffered(1)),
            pl.BlockSpec((1, D), const),
            pl.BlockSpec((1, D), const),
            pl.BlockSpec((D, D), const, pipeline_mode=pl.Buffered(1)),
        ],
        out_specs=pl.BlockSpec((1, SEQ_TILE, D), lambda bi, si: (bi, si, 0)),
        out_shape=jax.ShapeDtypeStruct(x.shape, F32),
        scratch_shapes=[
            pltpu.VMEM((SEQ_TILE, 4 * D), F32),
            pltpu.VMEM((SEQ_TILE, D), BF16),
            pltpu.VMEM((A_HEADS, A_HEAD_DIM, A_HEAD_DIM), F32),
        ] + [pltpu.VMEM((CHUNK, D), F32)] * 4
          + [pltpu.VMEM((CHUNK, A_HEAD_DIM), BF16)] * 6
          + [pltpu.VMEM((CHUNK, CHUNK), BF16)] * 3,
        compiler_params=_params(2),
    )(x, g.reshape(1, D), w_in.astype(BF16), lb.reshape(1, D), out_norm.reshape(1, D),
      w_o.astype(BF16))


def _router_kernel(x_ref, g_ref, wr_ref, br_ref, ids_ref, gates_ref, counts_ref, cnt_scr,
                   before_scr):
    tm = ROUTER_TILE

    @pl.when(pl.program_id(0) == 0)
    def _():
        cnt_scr[...] = jnp.zeros_like(cnt_scr)
        tr = lax.broadcasted_iota(jnp.int32, (tm, tm), 0)
        tc = lax.broadcasted_iota(jnp.int32, (tm, tm), 1)
        before_scr[...] = (tr < tc).astype(BF16)

    x = x_ref[...]
    h = x * _rms_scale(x) * g_ref[...]
    h_hi = h.astype(BF16)
    h_lo = (h - h_hi.astype(F32)).astype(BF16)
    nt = (((1,), (1,)), ((), ()))
    both = lax.dot_general(wr_ref[...], h_hi, nt, preferred_element_type=F32)
    cross = lax.dot_general(wr_ref[0:ROUTER_ROWS], h_lo, nt, preferred_element_type=F32)
    logits = both[0:ROUTER_ROWS] + both[ROUTER_ROWS:] + cross + br_ref[...]
    el = logits[0:N_EXPERTS]
    gl = logits[N_EXPERTS:ROUTER_ROWS]
    grow = lax.broadcasted_iota(jnp.int32, gl.shape, 0)
    gl = jnp.where(grow < N_GROUPS, gl, -jnp.inf)
    gmax = jnp.max(gl, axis=0, keepdims=True)
    gsum = jnp.sum(jnp.exp(gl - gmax), axis=0, keepdims=True)
    grp_w = 1.0 / gsum
    gidx = jnp.min(jnp.where(gl == gmax, grow, N_GROUPS), axis=0, keepdims=True)

    erow = lax.broadcasted_iota(jnp.int32, el.shape, 0)
    masked = jnp.where((erow // EXPERTS_PER_GROUP) == gidx, el, -jnp.inf)
    top1 = jnp.max(masked, axis=0, keepdims=True)
    i1 = jnp.min(jnp.where(masked == top1, erow, N_EXPERTS), axis=0, keepdims=True)
    masked2 = jnp.where(erow == i1, -jnp.inf, masked)
    top2 = jnp.max(masked2, axis=0, keepdims=True)
    i2 = jnp.min(jnp.where(masked2 == top2, erow, N_EXPERTS), axis=0, keepdims=True)
    e2 = jnp.exp(top2 - top1)
    denom = 1.0 + e2
    g1 = grp_w * (1.0 / denom)
    g2 = grp_w * (e2 / denom)

    sel1 = erow == i1
    sel2 = erow == i2
    onehot = (sel1 | sel2).astype(BF16)
    prefix = jnp.dot(onehot, before_scr[...], preferred_element_type=F32) + cnt_scr[...]
    r1 = jnp.sum(jnp.where(sel1, prefix, 0.0), axis=0, keepdims=True)
    r2 = jnp.sum(jnp.where(sel2, prefix, 0.0), axis=0, keepdims=True)
    cnt_scr[...] += jnp.sum(onehot.astype(F32), axis=1, keepdims=True)

    zi = jnp.zeros((4, tm), jnp.int32)
    ids_ref[0] = jnp.concatenate(
        [i1, i2, r1.astype(jnp.int32), r2.astype(jnp.int32), zi], axis=0)
    gates_ref[0] = jnp.concatenate([g1, g2, jnp.zeros((6, tm), F32)], axis=0)
    counts_ref[...] = jnp.broadcast_to(cnt_scr[...], counts_ref.shape).astype(jnp.int32)


def _router(x2d, g, w_group, b_group, w_expert, b_expert):
    t = x2d.shape[0]
    nt = t // ROUTER_TILE
    pad = ROUTER_ROWS - N_EXPERTS - N_GROUPS
    wr = jnp.concatenate([w_expert.T, w_group.T, jnp.zeros((pad, D), F32)], axis=0)
    wr_hi = wr.astype(BF16)
    wr_lo = (wr - wr_hi.astype(F32)).astype(BF16)
    br = jnp.concatenate([b_expert, b_group, jnp.zeros((pad,), F32)]).reshape(ROUTER_ROWS, 1)
    const = lambda i: (0, 0)
    return pl.pallas_call(
        _router_kernel,
        name="router",
        grid=(nt,),
        in_specs=[
            pl.BlockSpec((ROUTER_TILE, D), lambda i: (i, 0)),
            pl.BlockSpec((1, D), const),
            pl.BlockSpec((2 * ROUTER_ROWS, D), const),
            pl.BlockSpec((ROUTER_ROWS, 1), const),
        ],
        out_specs=[
            pl.BlockSpec((1, 8, ROUTER_TILE), lambda i: (i, 0, 0)),
            pl.BlockSpec((1, 8, ROUTER_TILE), lambda i: (i, 0, 0)),
            pl.BlockSpec((N_EXPERTS, 128), const),
        ],
        out_shape=[
            jax.ShapeDtypeStruct((nt, 8, ROUTER_TILE), jnp.int32),
            jax.ShapeDtypeStruct((nt, 8, ROUTER_TILE), F32),
            jax.ShapeDtypeStruct((N_EXPERTS, 128), jnp.int32),
        ],
        scratch_shapes=[pltpu.VMEM((N_EXPERTS, 1), F32),
                        pltpu.VMEM((ROUTER_TILE, ROUTER_TILE), BF16)],
        compiler_params=_params(1),
    )(x2d, g.reshape(1, D), jnp.concatenate([wr_hi, wr_lo], axis=0), br)


SUBLANES = 8
SLOTS_PER_TILE = SUBLANES * TOP_K


def _issue_rows(n_rows, dest_ref, first_tile, make_copy):
    def body(j, c):
        for u in range(SUBLANES):
            for k in range(TOP_K):
                slot = u * TOP_K + k
                make_copy(j, u, k, dest_ref[0, first_tile + j, slot]).start(priority=slot % 2)
        return c

    lax.fori_loop(0, n_rows // SUBLANES, body, 0)


def _issue_rows_inline(tile0, n_tiles, dest_ref, first_tile, make_copy):
    for j in range(tile0, tile0 + n_tiles):
        for u in range(SUBLANES):
            for k in range(TOP_K):
                slot = u * TOP_K + k
                make_copy(j, u, k, dest_ref[0, first_tile + j, slot]).start(priority=slot % 2)


def _pack_bf16_pairs(x):
    half = x.shape[1] // 2
    lo = lax.bitcast_convert_type(x[:, :half].astype(BF16).astype(F32), U32)
    hi = lax.bitcast_convert_type(x[:, half:].astype(BF16).astype(F32), U32)
    return (hi & jnp.uint32(0xFFFF0000)) | (lo >> 16)


def _unpack_bf16_pairs(u):
    lo = lax.bitcast_convert_type(u << 16, F32).astype(BF16)
    hi = lax.bitcast_convert_type(u & jnp.uint32(0xFFFF0000), F32).astype(BF16)
    return lo, hi


def _tile_rows(x):
    return x.reshape(x.shape[0] // SUBLANES, SUBLANES, x.shape[1])


def _wait_rows(buf_ref, sem):
    pltpu.make_async_copy(buf_ref, buf_ref, sem).wait()


def _dispatch_kernel(zrow_ref, dest_ref, x_ref, g_ref, xs_ref, hbuf, zbuf, zsem, sem):
    @pl.when(pl.program_id(0) == 0)
    def _():
        zbuf[...] = jnp.zeros_like(zbuf)

        def zcopy(e):
            return pltpu.make_async_copy(zbuf, xs_ref.at[pl.ds(zrow_ref[e], MOE_BLOCK)], zsem)

        def zstart(e, c):
            @pl.when(zrow_ref[e] >= 0)
            def _():
                zcopy(e).start()
            return c

        def zwait(e, c):
            @pl.when(zrow_ref[e] >= 0)
            def _():
                zcopy(e).wait()
            return c

        lax.fori_loop(0, 2 * N_EXPERTS, zstart, 0)
        lax.fori_loop(0, 2 * N_EXPERTS, zwait, 0)

    i = pl.program_id(0)
    slot = i % 2
    x = x_ref[...]
    hbuf[slot] = _tile_rows(_pack_bf16_pairs(x * _rms_scale(x) * g_ref[...]))

    def row_copy(tile, sub, k, dest):
        return pltpu.make_async_copy(hbuf.at[slot, tile, pl.ds(sub, 1), :], xs_ref.at[dest],
                                     sem.at[slot])

    _issue_rows(ROUTER_TILE, dest_ref, 0, row_copy)

    def drain(which):
        for _ in range(TOP_K):
            _wait_rows(hbuf.at[which], sem.at[which])

    @pl.when(i > 0)
    def _():
        drain(1 - slot)

    @pl.when(i == pl.num_programs(0) - 1)
    def _():
        drain(slot)


def _dispatch(x2d, g, dest, zrow, n_pad):
    t = x2d.shape[0]
    nt = t // ROUTER_TILE
    return pl.pallas_call(
        _dispatch_kernel,
        name="dispatch",
        grid_spec=pltpu.PrefetchScalarGridSpec(
            num_scalar_prefetch=1,
            grid=(nt,),
            in_specs=[
                pl.BlockSpec((1, ROUTER_TILE // SUBLANES, SLOTS_PER_TILE),
                             lambda i, z: (i, 0, 0), memory_space=pltpu.SMEM),
                pl.BlockSpec((ROUTER_TILE, D), lambda i, z: (i, 0)),
                pl.BlockSpec((1, D), lambda i, z: (0, 0)),
            ],
            out_specs=pl.BlockSpec(memory_space=pl.ANY),
            scratch_shapes=[
                pltpu.VMEM((2, ROUTER_TILE // SUBLANES, SUBLANES, D // 2), U32),
                pltpu.VMEM((MOE_BLOCK, 1, D // 2), U32),
                pltpu.SemaphoreType.DMA(()),
                pltpu.SemaphoreType.DMA((2,)),
            ],
        ),
        out_shape=jax.ShapeDtypeStruct((n_pad, 1, D // 2), U32),
        compiler_params=_params(1),
    )(zrow, dest, x2d, g.reshape(1, D))


def _experts_kernel(be_ref, nxt_ref, nu_ref, xs_ref, wg_ref, wu_ref, wd_ref, y_ref,
                    wg_f, wu_f, wd_f, wg_b, wu_b, wd_b, xbuf, ybuf, w_sem, in_sem, out_sem, *, layer):
    i = pl.program_id(0)
    n_used = nu_ref[0]
    slot = i % 2

    def block_rows(ref, blk):
        return ref.at[pl.ds(pl.multiple_of(blk * MOE_BLOCK, MOE_BLOCK), MOE_BLOCK), 0]

    def fetch(blk, into):
        return pltpu.make_async_copy(block_rows(xs_ref, blk), xbuf.at[into], in_sem.at[into])

    def write_back(blk, from_):
        return pltpu.make_async_copy(ybuf.at[from_], block_rows(y_ref, blk), out_sem.at[from_])

    def weight_copies(e):
        return [pltpu.make_async_copy(src.at[layer, e], dst, w_sem.at[n])
                for n, (src, dst) in enumerate(((wg_ref, wg_f), (wu_ref, wu_f), (wd_ref, wd_f)))]

    @pl.when(i == 0)
    def _():
        for c in weight_copies(be_ref[0]):
            c.start()
        fetch(0, 0).start()

    @pl.when(i + 1 < n_used)
    def _():
        fetch(i + 1, 1 - slot).start()

    prev = be_ref[jnp.maximum(i - 1, 0)]

    @pl.when((i < n_used) & ((i == 0) | (be_ref[i] != prev)))
    def _():
        for c in weight_copies(be_ref[i]):
            c.wait()
        wg_b[...] = wg_f[...].astype(BF16)
        wu_b[...] = wu_f[...].astype(BF16)
        wd_b[...] = wd_f[...].astype(BF16)

        @pl.when(nxt_ref[i] >= 0)
        def _():
            for c in weight_copies(nxt_ref[i]):
                c.start()

    @pl.when(i >= 2)
    def _():
        write_back(i - 2, slot).wait()

    @pl.when(i < n_used)
    def _():
        fetch(i, slot).wait()
        h_lo, h_hi = _unpack_bf16_pairs(xbuf[slot])
        half = D // 2
        a = (jnp.dot(h_lo, wg_b[:half], preferred_element_type=F32)
             + jnp.dot(h_hi, wg_b[half:], preferred_element_type=F32))
        u = (jnp.dot(h_lo, wu_b[:half], preferred_element_type=F32)
             + jnp.dot(h_hi, wu_b[half:], preferred_element_type=F32))
        hid = (a * _sigmoid(a) * u).astype(BF16)
        ybuf[slot] = jnp.dot(hid, wd_b[...], preferred_element_type=F32)

    @pl.when(i >= n_used)
    def _():
        ybuf[slot] = jnp.zeros((MOE_BLOCK, D), F32)

    write_back(i, slot).start()

    @pl.when(i == pl.num_programs(0) - 1)
    def _():
        write_back(i - 1, 1 - slot).wait()
        write_back(i, slot).wait()


def _experts(xs, block_e, next_e, n_used, layer, w_gate, w_up, w_down):
    n_pad = xs.shape[0]
    n_blocks = n_pad // MOE_BLOCK
    hbm = pl.BlockSpec(memory_space=pl.ANY)
    return pl.pallas_call(
        functools.partial(_experts_kernel, layer=layer),
        name="experts",
        grid_spec=pltpu.PrefetchScalarGridSpec(
            num_scalar_prefetch=3,
            grid=(n_blocks,),
            in_specs=[hbm, hbm, hbm, hbm],
            out_specs=hbm,
            scratch_shapes=[
                pltpu.VMEM((D, D_EXPERT), F32),
                pltpu.VMEM((D, D_EXPERT), F32),
                pltpu.VMEM((D_EXPERT, D), F32),
                pltpu.VMEM((D, D_EXPERT), BF16),
                pltpu.VMEM((D, D_EXPERT), BF16),
                pltpu.VMEM((D_EXPERT, D), BF16),
                pltpu.VMEM((2, MOE_BLOCK, D // 2), U32),
                pltpu.VMEM((2, MOE_BLOCK, D), F32),
                pltpu.SemaphoreType.DMA((3,)),
                pltpu.SemaphoreType.DMA((2,)),
                pltpu.SemaphoreType.DMA((2,)),
            ],
        ),
        out_shape=jax.ShapeDtypeStruct((n_pad, 1, D), F32),
        compiler_params=_params(1),
    )(block_e, next_e, n_used, xs, w_gate, w_up, w_down)


def _combine_kernel(*refs, final, project):
    (dest_ref, dest_next_ref, x_ref, gate_ref, y_ref, p_ref, gple_ref, wp_ref, wg_ref,
     gfin_ref) = refs[:10]
    if project:
        gq_ref, gkv_ref, wq_ref, wkv_ref, out_ref, qt_ref, k_ref, vt_ref = refs[10:18]
        ybuf0, ybuf1, sem, h_scr, hq_scr, hkv_scr = refs[18:]
    else:
        out_ref, ybuf0, ybuf1, sem, h_scr = refs[10:]
    i = pl.program_id(0)
    tiles = COMBINE_TILE // SUBLANES
    ybuf = (ybuf0, ybuf1)

    def row_copy(into):
        def make(tile, sub, k, dest):
            return pltpu.make_async_copy(y_ref.at[dest], ybuf[into].at[k, tile, pl.ds(sub, 1), :],
                                         sem.at[into])
        return make

    def mix(rows, slot):
        gates = gate_ref[rows, :]
        y0 = ybuf[slot][0].reshape(COMBINE_TILE, D)
        y1 = ybuf[slot][1].reshape(COMBINE_TILE, D)
        x = x_ref[rows, :] + gates[:, 0:1] * y0 + gates[:, 1:2] * y1
        out_ref[rows, :] = x
        h_scr[...] = (x * _rms_scale(x) * gple_ref[...]).astype(BF16)

    def ple(rows, slot):
        gate = _sigmoid(jnp.dot(h_scr[...], wg_ref[...], preferred_element_type=F32))
        proj = jnp.dot(p_ref[0, rows, :].astype(BF16), wp_ref[...], preferred_element_type=F32)
        x = out_ref[rows, :] + proj * gate
        if final:
            x = x * _rms_scale(x) * gfin_ref[...]
        out_ref[rows, :] = x
        if project:
            xn = x * _rms_scale(x)
            hq_scr[...] = (xn * gq_ref[...]).astype(BF16)
            hkv_scr[...] = (xn * gkv_ref[...]).astype(BF16)

    def queries(rows, slot):
        q = jnp.dot(hq_scr[...], wq_ref[...], preferred_element_type=F32) * (ATTN_SCALE * LOG2E)
        qt_ref[0, :, rows] = q.T.astype(BF16)

    def keys_values(rows, slot):
        kv = jnp.dot(hkv_scr[...], wkv_ref[...], preferred_element_type=F32)
        k_ref[0, rows, :] = kv[:, :D].astype(BF16)
        vt_ref[0, :, rows] = kv[:, D:].T.astype(BF16)

    stages = [mix, ple] + ([queries, keys_values] if project else [])

    def tile(half, slot, next_dest_ref, next_first_tile):
        for k in range(TOP_K):
            _wait_rows(ybuf[slot].at[k], sem.at[slot])
        rows = slice(half * COMBINE_TILE, (half + 1) * COMBINE_TILE)
        share = tiles // len(stages)
        for n, stage in enumerate(stages):
            def block(n=n, stage=stage):
                _issue_rows_inline(n * share, share, next_dest_ref, next_first_tile,
                                   row_copy(1 - slot))
                stage(rows, slot)
            if n == 0:
                block()
            else:
                pl.when(i >= 0)(block)

    @pl.when(i == 0)
    def _():
        _issue_rows(COMBINE_TILE, dest_ref, 0, row_copy(0))

    tile(0, 0, dest_ref, tiles)
    tile(1, 1, dest_next_ref, 0)

    @pl.when(i == pl.num_programs(0) - 1)
    def _():
        for k in range(TOP_K):
            _wait_rows(ybuf[0].at[k], sem.at[0])


def _combine_ple(x2d, dest, gates, y, p3d, layer, g_ple, w_proj, w_gate, g_final, final, qkv=None):
    t = x2d.shape[0]
    step = 2 * COMBINE_TILE
    nt = t // step
    dest_blk = (1, step // SUBLANES, SLOTS_PER_TILE)
    const = lambda i: (0, 0)
    resident = functools.partial(pl.BlockSpec, index_map=const, pipeline_mode=pl.Buffered(1))
    in_specs = [
        pl.BlockSpec(dest_blk, lambda i: (i, 0, 0), memory_space=pltpu.SMEM),
        pl.BlockSpec(dest_blk, lambda i: (jnp.minimum(i + 1, nt - 1), 0, 0),
                     memory_space=pltpu.SMEM),
        pl.BlockSpec((step, D), lambda i: (i, 0)),
        pl.BlockSpec((step, TOP_K), lambda i: (i, 0)),
        pl.BlockSpec(memory_space=pl.ANY),
        pl.BlockSpec((1, step, D_PLE), lambda i: (layer, i, 0)),
        pl.BlockSpec((1, D), const),
        resident((D_PLE, D)),
        resident((D, D)),
        pl.BlockSpec((1, D), const),
    ]
    args = [dest, dest, x2d, gates, y, p3d, g_ple.reshape(1, D), w_proj.astype(BF16),
            w_gate.astype(BF16), g_final.reshape(1, D)]
    out_specs = [pl.BlockSpec((step, D), lambda i: (i, 0))]
    out_shape = [jax.ShapeDtypeStruct((t, D), F32)]
    if qkv is not None:
        seq, g_q, g_kv, w_q, w_kv = qkv
        per_seq = seq // step
        in_specs += [pl.BlockSpec((1, D), const), pl.BlockSpec((1, D), const),
                     resident((D, D)), resident((D, 2 * D))]
        args += [g_q.reshape(1, D), g_kv.reshape(1, D), w_q.astype(BF16), w_kv.astype(BF16)]
        feat_major = pl.BlockSpec((1, D, step), lambda i: (i // per_seq, 0, i % per_seq))
        row_major = pl.BlockSpec((1, step, D), lambda i: (i // per_seq, i % per_seq, 0))
        out_specs += [feat_major, row_major, feat_major]
        out_shape += [jax.ShapeDtypeStruct((t // seq, D, seq), BF16),
                      jax.ShapeDtypeStruct((t // seq, seq, D), BF16),
                      jax.ShapeDtypeStruct((t // seq, D, seq), BF16)]
    outs = pl.pallas_call(
        functools.partial(_combine_kernel, final=final, project=qkv is not None),
        name="combine_final" if final else "combine",
        grid=(nt,),
        in_specs=in_specs,
        out_specs=out_specs,
        out_shape=out_shape,
        scratch_shapes=[
            pltpu.VMEM((TOP_K, COMBINE_TILE // SUBLANES, SUBLANES, D), F32),
            pltpu.VMEM((TOP_K, COMBINE_TILE // SUBLANES, SUBLANES, D), F32),
            pltpu.SemaphoreType.DMA((2,)),
        ] + [pltpu.VMEM((COMBINE_TILE, D), BF16)] * (1 if qkv is None else 3),
        compiler_params=_params(1),
    )(*args)
    return outs[0] if qkv is None else tuple(outs)


def _moe_ple(x, p, layer, norm_ffn, w_group, b_group, w_expert, b_expert, w_gate, w_up, w_down,
             norm_ple, ple_w_proj, ple_w_gate, final_norm, final, qkv=None):
    b, s, _ = x.shape
    t = b * s
    x2d = x.reshape(t, D)
    ids, gates, counts = _router(x2d, norm_ffn, w_group, b_group, w_expert, b_expert)

    counts = counts[:, 0]
    padded = (counts + MOE_BLOCK - 1) // MOE_BLOCK * MOE_BLOCK
    pad_end = jnp.cumsum(padded)
    pad_start = pad_end - padded
    n_blocks = t * TOP_K // MOE_BLOCK + N_EXPERTS
    n_pad = n_blocks * MOE_BLOCK
    e = ids[:, 0:TOP_K, :].transpose(1, 0, 2).reshape(TOP_K, t)
    r = ids[:, TOP_K:2 * TOP_K, :].transpose(1, 0, 2).reshape(TOP_K, t)
    experts = jnp.arange(N_EXPERTS, dtype=jnp.int32)
    dest = r + jnp.sum(jnp.where(e[..., None] == experts, pad_start, 0), axis=-1)
    gate_cols = gates[:, 0:TOP_K, :].transpose(0, 2, 1).reshape(t, TOP_K)
    block_row = jnp.arange(n_blocks, dtype=jnp.int32) * MOE_BLOCK
    block_e = jnp.minimum(jnp.sum(pad_end[None, :] <= block_row[:, None], axis=-1),
                          N_EXPERTS - 1).astype(jnp.int32)
    n_used = (pad_end[-1:] // MOE_BLOCK).astype(jnp.int32)
    tail = (n_used[0] + jnp.arange(N_EXPERTS, dtype=jnp.int32)) * MOE_BLOCK
    zrow = jnp.concatenate([jnp.where(padded > 0, pad_end - MOE_BLOCK, -1),
                            jnp.where(tail < n_pad, tail, -1)]).astype(jnp.int32)

    def tiles(a, tile):
        return a.T.reshape(t // tile, tile // SUBLANES, SLOTS_PER_TILE).astype(jnp.int32)

    xs = _dispatch(x2d, norm_ffn, tiles(dest, ROUTER_TILE), zrow, n_pad)
    later = jnp.where((experts[None, :] > block_e[:, None]) & (padded[None, :] > 0), experts[None, :],
                      N_EXPERTS)
    next_e = jnp.min(later, axis=-1)
    next_e = jnp.where(next_e < N_EXPERTS, next_e, -1).astype(jnp.int32)
    y = _experts(xs, block_e, next_e, n_used, layer, w_gate, w_up, w_down)
    out = _combine_ple(x2d, tiles(dest, 2 * COMBINE_TILE), gate_cols, y, p.reshape(-1, t, D_PLE), layer,
                       norm_ple, ple_w_proj, ple_w_gate, final_norm, final,
                       None if qkv is None else (s,) + tuple(qkv))
    if qkv is None:
        return out.reshape(b, s, D)
    return (out[0].reshape(b, s, D),) + out[1:]


Q_GROUP = 2 * CHUNK
G_BAND = BAND + CHUNK
PAIR = 2 * B_HEAD_DIM
ONES_ROWS = 16


def _attn_kernel(qt_ref, kp_ref, kc_ref, vtp_ref, vtc_ref, bias_ref, x_ref, wo_ref, out_ref, o_scr,
                 s_scr0, s_scr1, p_scr0, p_scr1):
    drow = lax.broadcasted_iota(jnp.int32, (PAIR, Q_GROUP), 0)
    first_head = drow < B_HEAD_DIM
    s_scr = (s_scr0, s_scr1)
    p_scr = (p_scr0, p_scr1)

    def attend(first_tile):
        units = [(g, pr) for g in range(SEQ_TILE // Q_GROUP) for pr in range(B_HEADS // 2)]

        def geometry(g):
            w0 = g * Q_GROUP
            n_prev = SEQ_TILE - w0
            return w0, n_prev, G_BAND - n_prev

        def keys(g):
            return slice(geometry(g)[1], None) if first_tile else slice(None)

        def scores(unit, s_ref):
            g, pr = unit
            w0, n_prev, n_cur = geometry(g)
            feat = slice(pr * PAIR, (pr + 1) * PAIR)
            qt = qt_ref[0, feat, w0:w0 + Q_GROUP]
            zero = jnp.zeros_like(qt)
            qblk = jnp.concatenate([jnp.where(first_head, qt, zero),
                                    jnp.where(first_head, zero, qt)], axis=1)
            if first_tile:
                kb = kc_ref[0, :n_cur, feat]
            else:
                kb = jnp.concatenate([kp_ref[0, w0:, feat], kc_ref[0, :n_cur, feat]], axis=0)
            s_ref[keys(g), :] = (jnp.dot(kb, qblk, preferred_element_type=F32)
                                 + bias_ref[pr, keys(g), :])

        def weights(unit, s_ref, p_ref):
            g, _ = unit
            s = s_ref[keys(g), :]
            m = jnp.max(s, axis=0, keepdims=True)
            p_ref[keys(g), :] = jnp.exp2(s - m).astype(BF16)

        def values(unit, p_ref):
            g, pr = unit
            w0, n_prev, n_cur = geometry(g)
            feat = slice(pr * PAIR, (pr + 1) * PAIR)
            if first_tile:
                vt = vtc_ref[0, feat, :n_cur]
            else:
                vt = jnp.concatenate([vtp_ref[0, feat, w0:], vtc_ref[0, feat, :n_cur]], axis=1)
            ones = jnp.ones((ONES_ROWS, vt.shape[1]), BF16)
            ot = jnp.dot(jnp.concatenate([vt, ones], axis=0), p_ref[keys(g), :],
                         preferred_element_type=F32)
            inv = 1.0 / ot[PAIR:PAIR + 1, :]
            ot = jnp.where(first_head, ot[:PAIR, :Q_GROUP] * inv[:, :Q_GROUP],
                           ot[:PAIR, Q_GROUP:] * inv[:, Q_GROUP:])
            o_scr[w0:w0 + Q_GROUP, feat] = ot.T.astype(BF16)

        n_units = len(units)
        scores(units[0], s_scr[0])
        scores(units[1], s_scr[1])
        weights(units[0], s_scr[0], p_scr[0])
        for n, unit in enumerate(units):
            if n + 2 < n_units:
                scores(units[n + 2], s_scr[n % 2])
            if n + 1 < n_units:
                weights(units[n + 1], s_scr[(n + 1) % 2], p_scr[(n + 1) % 2])
            values(unit, p_scr[n % 2])

    @pl.when(pl.program_id(1) == 0)
    def _():
        attend(True)

    @pl.when(pl.program_id(1) > 0)
    def _():
        attend(False)

    out_ref[0] = x_ref[0] + jnp.dot(o_scr[...], wo_ref[...], preferred_element_type=F32)


def _attn(x, qt, k, vt, bias_t, w_o):
    b, s, _ = x.shape
    cur = lambda bi, si: (bi, si, 0)
    prev = lambda bi, si: (bi, jnp.maximum(si - 1, 0), 0)
    cur_t = lambda bi, si: (bi, 0, si)
    prev_t = lambda bi, si: (bi, 0, jnp.maximum(si - 1, 0))
    blk = (1, SEQ_TILE, D)
    blk_t = (1, D, SEQ_TILE)
    return pl.pallas_call(
        _attn_kernel,
        name="attn",
        grid=(b, s // SEQ_TILE),
        in_specs=[
            pl.BlockSpec(blk_t, cur_t),
            pl.BlockSpec(blk, prev),
            pl.BlockSpec(blk, cur),
            pl.BlockSpec(blk_t, prev_t),
            pl.BlockSpec(blk_t, cur_t),
            pl.BlockSpec((B_HEADS // 2, G_BAND, 2 * Q_GROUP), lambda bi, si: (0, 0, 0),
                         pipeline_mode=pl.Buffered(1)),
            pl.BlockSpec(blk, cur),
            pl.BlockSpec((D, D), lambda bi, si: (0, 0), pipeline_mode=pl.Buffered(1)),
        ],
        out_specs=pl.BlockSpec(blk, cur),
        out_shape=jax.ShapeDtypeStruct(x.shape, F32),
        scratch_shapes=[pltpu.VMEM((SEQ_TILE, D), BF16)]
        + [pltpu.VMEM((G_BAND, 2 * Q_GROUP), F32)] * 2
        + [pltpu.VMEM((G_BAND, 2 * Q_GROUP), BF16)] * 2,
        compiler_params=_params(2),
    )(qt, k, k, vt, vt, bias_t, x, w_o.astype(BF16))


def _group_bias(table):
    band = _band_bias(table) * LOG2E
    pad = lambda lo, hi: jnp.pad(band, ((0, 0), (0, 0), (lo, hi)), constant_values=NEG_INF)
    both = jnp.concatenate([pad(0, CHUNK), pad(CHUNK, 0)], axis=1)
    both = both.reshape(B_HEADS // 2, 2, Q_GROUP, G_BAND)
    return both.transpose(0, 3, 1, 2).reshape(B_HEADS // 2, G_BAND, 2 * Q_GROUP)


def _band_bias(table):
    n_rel = REL_MAX - REL_MIN + 1
    span = BAND + CHUNK - 1
    head = jnp.broadcast_to(table[:, n_rel - 1:], (table.shape[0], span - n_rel))
    ext = jnp.concatenate([head, table[:, ::-1]], axis=1)
    rows = [ext[:, CHUNK - 1 - q:CHUNK - 1 - q + BAND] for q in range(CHUNK)]
    return jnp.stack(rows, axis=1)


def kernel(x, p, a_w_in, a_lb_logits, a_out_norm, a_w_o, kv_norm, w_kv, b_w_q, b_rel_bias, b_w_o,
           norm_mix, norm_ffn, norm_ple, moe_w_group, moe_b_group, moe_w_expert, moe_b_expert,
           moe_w_gate, moe_w_up, moe_w_down, ple_w_proj, ple_w_gate, final_norm):
    b, s, _ = x.shape
    lower_bounds = jnp.cumsum(jax.nn.softmax(a_lb_logits.astype(F32), axis=0), axis=0)

    def moe(xi, i, final, qkv=None):
        return _moe_ple(xi, p, i, norm_ffn[i], moe_w_group[i], moe_b_group[i], moe_w_expert[i],
                        moe_b_expert[i], moe_w_gate, moe_w_up, moe_w_down, norm_ple[i],
                        ple_w_proj[i], ple_w_gate[i], final_norm, final, qkv)

    x = _mixer_a(x, norm_mix[0], a_w_in[0], lower_bounds[0], a_out_norm[0], a_w_o[0])
    x, qt, k, vt = moe(x, 0, False, (norm_mix[1], kv_norm, b_w_q[0], w_kv))

    x = _attn(x, qt, k, vt, _group_bias(b_rel_bias[0].astype(F32)), b_w_o[0])
    x = moe(x, 1, True)
    return x
```

```python
import functools

import jax
import jax.numpy as jnp
from jax import lax
from jax.experimental import pallas as pl
from jax.experimental.pallas import tpu as pltpu

F32 = jnp.float32
BF16 = jnp.bfloat16
U32 = jnp.uint32

D = 1024
CHUNK = 64
A_HEADS = 8
A_HEAD_DIM = 128
B_HEADS = 16
B_HEAD_DIM = 64
LEFT_CHUNKS = 8
BAND = (LEFT_CHUNKS + 1) * CHUNK
REL_MIN = -(CHUNK - 1)
REL_MAX = 256
ATTN_SCALE = B_HEAD_DIM ** -0.5
N_GROUPS = 4
EXPERTS_PER_GROUP = 8
N_EXPERTS = 32
TOP_K = 2
D_EXPERT = 512
MOE_BLOCK = 512
D_PLE = 256
EPS = 1e-6
NEG_INF = -1e30
LOG2E = 1.4426950408889634

SUBLANES = 8
BF16_SUBLANES = 16
V7X_VMEM_BYTES = 64 * 1024 * 1024

SEQ_TILE = 512
ROUTER_TILE = 512
COMBINE_TILE = 512
ROUTER_ROWS = -(-(N_EXPERTS + N_GROUPS) // BF16_SUBLANES) * BF16_SUBLANES
VMEM_LIMIT = V7X_VMEM_BYTES - 8 * 1024 * 1024


def _params(n_axes, vmem=VMEM_LIMIT):
    return pltpu.CompilerParams(dimension_semantics=("arbitrary",) * n_axes,
                                vmem_limit_bytes=vmem)


def _rms_scale(x):
    return lax.rsqrt(jnp.mean(x * x, axis=-1, keepdims=True) + EPS)


def _sigmoid(x):
    return 1.0 / (1.0 + jnp.exp(-x))


def _mixer_a_kernel(x_ref, g_ref, win_ref, lb_ref, onorm_ref, wo_ref, out_ref,
                    proj_scr, o_scr, state_scr, g_scr0, g_scr1, k_scr0, k_scr1,
                    qd_scr0, qd_scr1, qd_scr2, kt_scr0, kt_scr1, kt_scr2, att_scr0, att_scr1, att_scr2,
                    xbuf, obuf, in_sem, out_sem):
    n_seq = pl.num_programs(1)
    step = pl.program_id(0) * n_seq + pl.program_id(1)
    n_steps = pl.num_programs(0) * n_seq
    slot = step % 2

    def tile_of(ref, n):
        return ref.at[n // n_seq, pl.ds(pl.multiple_of((n % n_seq) * SEQ_TILE, SEQ_TILE), SEQ_TILE)]

    def fetch(n, into):
        return pltpu.make_async_copy(tile_of(x_ref, n), xbuf.at[into], in_sem.at[into])

    def write_back(n, from_):
        return pltpu.make_async_copy(obuf.at[from_], tile_of(out_ref, n), out_sem.at[from_])

    @pl.when(step == 0)
    def _():
        fetch(0, 0).start()

    @pl.when(step + 1 < n_steps)
    def _():
        fetch(step + 1, 1 - slot).start()

    @pl.when(pl.program_id(1) == 0)
    def _():
        state_scr[...] = jnp.zeros_like(state_scr)

    @pl.when(step >= 2)
    def _():
        write_back(step - 2, slot).wait()

    fetch(step, slot).wait()
    x = xbuf[slot]
    h = (x * _rms_scale(x) * g_ref[...]).astype(BF16)
    proj_scr[...] = jnp.dot(h, win_ref[...], preferred_element_type=F32)

    row = lax.broadcasted_iota(jnp.int32, (CHUNK, CHUNK), 0)
    col = lax.broadcasted_iota(jnp.int32, (CHUNK, CHUNK), 1)
    causal = row >= col
    tril = causal.astype(BF16)
    lb = lb_ref[...]
    onorm = onorm_ref[...]

    g_scr, k_scr = (g_scr0, g_scr1), (k_scr0, k_scr1)
    qd_scr, kt_scr = (qd_scr0, qd_scr1, qd_scr2), (kt_scr0, kt_scr1, kt_scr2)
    att_scr = (att_scr0, att_scr1, att_scr2)
    n_chunks = SEQ_TILE // CHUNK
    units = [(c, hd) for c in range(n_chunks) for hd in range(A_HEADS)]

    def rows_of(c):
        return slice(c * CHUNK, (c + 1) * CHUNK)

    def decay(c):
        f = lb + (1.0 - lb) * _sigmoid(proj_scr[rows_of(c), D:2 * D])
        logf = jnp.log(f)
        hi = logf.astype(BF16)
        lo = (logf - hi.astype(F32)).astype(BF16)
        g_scr[c % 2][...] = (jnp.dot(tril, hi, preferred_element_type=F32)
                             + jnp.dot(tril, lo, preferred_element_type=F32))
        k_scr[c % 2][...] = 1.0 - f

    def intra(n):
        c, hd = units[n]
        sl = slice(hd * A_HEAD_DIM, (hd + 1) * A_HEAD_DIM)
        gh = g_scr[c % 2][:, sl]
        g_last = gh[CHUNK - 1:CHUNK, :]
        k = k_scr[c % 2][:, sl]
        q_dec = (proj_scr[rows_of(c), sl] * jnp.exp(gh)).astype(BF16)
        k_inv = (k * jnp.exp(-gh)).astype(BF16)
        qd_scr[n % 3][...] = q_dec
        kt_scr[n % 3][...] = (k * jnp.exp(g_last - gh)).astype(BF16)
        att = lax.dot_general(q_dec, k_inv, (((1,), (1,)), ((), ())),
                              preferred_element_type=F32)
        att_scr[n % 3][...] = jnp.where(causal, att, 0.0).astype(BF16)

    def output(n):
        c, hd = units[n]
        sl = slice(hd * A_HEAD_DIM, (hd + 1) * A_HEAD_DIM)
        rows = rows_of(c)
        g_last = g_scr[c % 2][CHUNK - 1:CHUNK, sl]
        v = proj_scr[rows, 2 * D + hd * A_HEAD_DIM:2 * D + (hd + 1) * A_HEAD_DIM]
        st = state_scr[hd]
        o = (jnp.dot(att_scr[n % 3][...], v.astype(BF16), preferred_element_type=F32)
             + lax.dot_general(qd_scr[n % 3][...], st.astype(BF16), (((1,), (1,)), ((), ())),
                               preferred_element_type=F32))
        v_t = v.T.astype(BF16)
        state_scr[hd] = st * jnp.exp(g_last) + jnp.dot(v_t, kt_scr[n % 3][...],
                                                       preferred_element_type=F32)
        o = o * _rms_scale(o)
        og = proj_scr[rows, 3 * D + hd * A_HEAD_DIM:3 * D + (hd + 1) * A_HEAD_DIM]
        o = o * onorm[:, sl] * (og * _sigmoid(og))
        o_scr[rows, sl] = o.astype(BF16)

    decay(0)
    intra(0)
    intra(1)
    for n, (c, hd) in enumerate(units):
        if hd == 0 and c + 1 < n_chunks:
            decay(c + 1)
        if n + 2 < len(units):
            intra(n + 2)
        output(n)
    obuf[slot] = x + jnp.dot(o_scr[...], wo_ref[...], preferred_element_type=F32)
    write_back(step, slot).start()

    @pl.when(step == n_steps - 1)
    def _():
        write_back(step - 1, 1 - slot).wait()
        write_back(step, slot).wait()


def _mixer_a(x, g, w_in, lb, out_norm, w_o):
    b, s, _ = x.shape
    const = lambda bi, si: (0, 0)
    return pl.pallas_call(
        _mixer_a_kernel,
        name="mixer_a",
        grid=(b, s // SEQ_TILE),
        in_specs=[
            pl.BlockSpec(memory_space=pl.ANY),
            pl.BlockSpec((1, D), const),
            pl.BlockSpec((D, 4 * D), const, pipeline_mode=pl.Buffered(1)),
            pl.BlockSpec((1, D), const),
            pl.BlockSpec((1, D), const),
            pl.BlockSpec((D, D), const, pipeline_mode=pl.Buffered(1)),
        ],
        out_specs=pl.BlockSpec(memory_space=pl.ANY),
        out_shape=jax.ShapeDtypeStruct(x.shape, F32),
        scratch_shapes=[
            pltpu.VMEM((SEQ_TILE, 4 * D), F32),
            pltpu.VMEM((SEQ_TILE, D), BF16),
            pltpu.VMEM((A_HEADS, A_HEAD_DIM, A_HEAD_DIM), F32),
        ] + [pltpu.VMEM((CHUNK, D), F32)] * 4
          + [pltpu.VMEM((CHUNK, A_HEAD_DIM), BF16)] * 6
          + [pltpu.VMEM((CHUNK, CHUNK), BF16)] * 3
          + [pltpu.VMEM((2, SEQ_TILE, D), F32)] * 2 + [pltpu.SemaphoreType.DMA((2,))] * 2,
        compiler_params=_params(2),
    )(x, g.reshape(1, D), w_in.astype(BF16), lb.reshape(1, D), out_norm.reshape(1, D),
      w_o.astype(BF16))


def _router_kernel(x_ref, g_ref, wr_ref, br_ref, ids_ref, gates_ref, counts_ref, cnt_scr,
                   before_scr):
    tm = ROUTER_TILE

    @pl.when(pl.program_id(0) == 0)
    def _():
        cnt_scr[...] = jnp.zeros_like(cnt_scr)
        tr = lax.broadcasted_iota(jnp.int32, (tm, tm), 0)
        tc = lax.broadcasted_iota(jnp.int32, (tm, tm), 1)
        before_scr[...] = (tr < tc).astype(BF16)

    x = x_ref[...]
    h = x * _rms_scale(x) * g_ref[...]
    h_hi = h.astype(BF16)
    h_lo = (h - h_hi.astype(F32)).astype(BF16)
    nt = (((1,), (1,)), ((), ()))
    both = lax.dot_general(wr_ref[...], h_hi, nt, preferred_element_type=F32)
    cross = lax.dot_general(wr_ref[0:ROUTER_ROWS], h_lo, nt, preferred_element_type=F32)
    logits = both[0:ROUTER_ROWS] + both[ROUTER_ROWS:] + cross + br_ref[...]
    el = logits[0:N_EXPERTS]
    gl = logits[N_EXPERTS:ROUTER_ROWS]
    grow = lax.broadcasted_iota(jnp.int32, gl.shape, 0)
    gl = jnp.where(grow < N_GROUPS, gl, -jnp.inf)
    gmax = jnp.max(gl, axis=0, keepdims=True)
    gsum = jnp.sum(jnp.exp(gl - gmax), axis=0, keepdims=True)
    grp_w = 1.0 / gsum
    gidx = jnp.min(jnp.where(gl == gmax, grow, N_GROUPS), axis=0, keepdims=True)

    erow = lax.broadcasted_iota(jnp.int32, el.shape, 0)
    masked = jnp.where((erow // EXPERTS_PER_GROUP) == gidx, el, -jnp.inf)
    top1 = jnp.max(masked, axis=0, keepdims=True)
    i1 = jnp.min(jnp.where(masked == top1, erow, N_EXPERTS), axis=0, keepdims=True)
    masked2 = jnp.where(erow == i1, -jnp.inf, masked)
    top2 = jnp.max(masked2, axis=0, keepdims=True)
    i2 = jnp.min(jnp.where(masked2 == top2, erow, N_EXPERTS), axis=0, keepdims=True)
    e2 = jnp.exp(top2 - top1)
    denom = 1.0 + e2
    g1 = grp_w * (1.0 / denom)
    g2 = grp_w * (e2 / denom)

    sel1 = erow == i1
    sel2 = erow == i2
    onehot = (sel1 | sel2).astype(BF16)
    prefix = jnp.dot(onehot, before_scr[...], preferred_element_type=F32) + cnt_scr[...]
    r1 = jnp.sum(jnp.where(sel1, prefix, 0.0), axis=0, keepdims=True)
    r2 = jnp.sum(jnp.where(sel2, prefix, 0.0), axis=0, keepdims=True)
    cnt_scr[...] += jnp.sum(onehot.astype(F32), axis=1, keepdims=True)

    zi = jnp.zeros((4, tm), jnp.int32)
    ids_ref[0] = jnp.concatenate(
        [i1, i2, r1.astype(jnp.int32), r2.astype(jnp.int32), zi], axis=0)
    gates_ref[0] = jnp.concatenate([g1, g2, jnp.zeros((6, tm), F32)], axis=0)
    counts_ref[...] = jnp.broadcast_to(cnt_scr[...], counts_ref.shape).astype(jnp.int32)


def _router(x2d, g, w_group, b_group, w_expert, b_expert):
    t = x2d.shape[0]
    nt = t // ROUTER_TILE
    pad = ROUTER_ROWS - N_EXPERTS - N_GROUPS
    wr = jnp.concatenate([w_expert.T, w_group.T, jnp.zeros((pad, D), F32)], axis=0)
    wr_hi = wr.astype(BF16)
    wr_lo = (wr - wr_hi.astype(F32)).astype(BF16)
    br = jnp.concatenate([b_expert, b_group, jnp.zeros((pad,), F32)]).reshape(ROUTER_ROWS, 1)
    const = lambda i: (0, 0)
    return pl.pallas_call(
        _router_kernel,
        name="router",
        grid=(nt,),
        in_specs=[
            pl.BlockSpec((ROUTER_TILE, D), lambda i: (i, 0)),
            pl.BlockSpec((1, D), const),
            pl.BlockSpec((2 * ROUTER_ROWS, D), const),
            pl.BlockSpec((ROUTER_ROWS, 1), const),
        ],
        out_specs=[
            pl.BlockSpec((1, 8, ROUTER_TILE), lambda i: (i, 0, 0)),
            pl.BlockSpec((1, 8, ROUTER_TILE), lambda i: (i, 0, 0)),
            pl.BlockSpec((N_EXPERTS, 128), const),
        ],
        out_shape=[
            jax.ShapeDtypeStruct((nt, 8, ROUTER_TILE), jnp.int32),
            jax.ShapeDtypeStruct((nt, 8, ROUTER_TILE), F32),
            jax.ShapeDtypeStruct((N_EXPERTS, 128), jnp.int32),
        ],
        scratch_shapes=[pltpu.VMEM((N_EXPERTS, 1), F32),
                        pltpu.VMEM((ROUTER_TILE, ROUTER_TILE), BF16)],
        compiler_params=_params(1),
    )(x2d, g.reshape(1, D), jnp.concatenate([wr_hi, wr_lo], axis=0), br)


SLOTS_PER_TILE = SUBLANES * TOP_K


def _issue_rows(n_rows, dest_ref, first_tile, make_copy):
    def body(j, c):
        for u in range(SUBLANES):
            for k in range(TOP_K):
                slot = u * TOP_K + k
                make_copy(j, u, k, dest_ref[0, first_tile + j, slot]).start(priority=slot % 2)
        return c

    lax.fori_loop(0, n_rows // SUBLANES, body, 0)


def _issue_rows_inline(n_rows, dest_ref, first_tile, make_copy):
    for j in range(n_rows // SUBLANES):
        for u in range(SUBLANES):
            for k in range(TOP_K):
                slot = u * TOP_K + k
                make_copy(j, u, k, dest_ref[0, first_tile + j, slot]).start(priority=slot % 2)


def _pack_bf16_pairs(x):
    half = x.shape[1] // 2
    lo = lax.bitcast_convert_type(x[:, :half].astype(BF16).astype(F32), U32)
    hi = lax.bitcast_convert_type(x[:, half:].astype(BF16).astype(F32), U32)
    return (hi & jnp.uint32(0xFFFF0000)) | (lo >> 16)


def _unpack_bf16_pairs(u):
    lo = lax.bitcast_convert_type(u << 16, F32).astype(BF16)
    hi = lax.bitcast_convert_type(u & jnp.uint32(0xFFFF0000), F32).astype(BF16)
    return lo, hi


def _tile_rows(x):
    return x.reshape(x.shape[0] // SUBLANES, SUBLANES, x.shape[1])


def _wait_rows(buf_ref, sem):
    pltpu.make_async_copy(buf_ref, buf_ref, sem).wait()


def _dispatch_kernel(zrow_ref, dest_ref, x_ref, g_ref, xs_ref, hbuf, zbuf, zsem, sem):
    @pl.when(pl.program_id(0) == 0)
    def _():
        zbuf[...] = jnp.zeros_like(zbuf)

        def zcopy(e):
            return pltpu.make_async_copy(zbuf, xs_ref.at[pl.ds(zrow_ref[e], MOE_BLOCK)], zsem)

        def zstart(e, c):
            @pl.when(zrow_ref[e] >= 0)
            def _():
                zcopy(e).start()
            return c

        def zwait(e, c):
            @pl.when(zrow_ref[e] >= 0)
            def _():
                zcopy(e).wait()
            return c

        lax.fori_loop(0, 2 * N_EXPERTS, zstart, 0)
        lax.fori_loop(0, 2 * N_EXPERTS, zwait, 0)

    i = pl.program_id(0)
    slot = i % 2
    x = x_ref[...]
    hbuf[slot] = _tile_rows(_pack_bf16_pairs(x * _rms_scale(x) * g_ref[...]))

    def row_copy(tile, sub, k, dest):
        return pltpu.make_async_copy(hbuf.at[slot, tile, pl.ds(sub, 1), :], xs_ref.at[dest],
                                     sem.at[slot])

    _issue_rows(ROUTER_TILE, dest_ref, 0, row_copy)

    def drain(which):
        for _ in range(TOP_K):
            _wait_rows(hbuf.at[which], sem.at[which])

    @pl.when(i > 0)
    def _():
        drain(1 - slot)

    @pl.when(i == pl.num_programs(0) - 1)
    def _():
        drain(slot)


def _dispatch(x2d, g, dest, zrow, n_pad):
    t = x2d.shape[0]
    nt = t // ROUTER_TILE
    return pl.pallas_call(
        _dispatch_kernel,
        name="dispatch",
        grid_spec=pltpu.PrefetchScalarGridSpec(
            num_scalar_prefetch=1,
            grid=(nt,),
            in_specs=[
                pl.BlockSpec((1, ROUTER_TILE // SUBLANES, SLOTS_PER_TILE),
                             lambda i, z: (i, 0, 0), memory_space=pltpu.SMEM),
                pl.BlockSpec((ROUTER_TILE, D), lambda i, z: (i, 0)),
                pl.BlockSpec((1, D), lambda i, z: (0, 0)),
            ],
            out_specs=pl.BlockSpec(memory_space=pl.ANY),
            scratch_shapes=[
                pltpu.VMEM((2, ROUTER_TILE // SUBLANES, SUBLANES, D // 2), U32),
                pltpu.VMEM((MOE_BLOCK, 1, D // 2), U32),
                pltpu.SemaphoreType.DMA(()),
                pltpu.SemaphoreType.DMA((2,)),
            ],
        ),
        out_shape=jax.ShapeDtypeStruct((n_pad, 1, D // 2), U32),
        compiler_params=_params(1),
    )(zrow, dest, x2d, g.reshape(1, D))


def _experts_kernel(be_ref, nxt_ref, nu_ref, xs_ref, wg_ref, wu_ref, wd_ref, y_ref,
                    wg_f, wu_f, wd_f, wg_b, wu_b, wd_b, xbuf, ybuf, w_sem, in_sem, out_sem, *, layer):
    i = pl.program_id(0)
    n_used = nu_ref[0]
    slot = i % 2

    def block_rows(ref, blk):
        return ref.at[pl.ds(pl.multiple_of(blk * MOE_BLOCK, MOE_BLOCK), MOE_BLOCK), 0]

    def fetch(blk, into):
        return pltpu.make_async_copy(block_rows(xs_ref, blk), xbuf.at[into], in_sem.at[into])

    def write_back(blk, from_):
        return pltpu.make_async_copy(ybuf.at[from_], block_rows(y_ref, blk), out_sem.at[from_])

    def weight_copies(e):
        return [pltpu.make_async_copy(src.at[layer, e], dst, w_sem.at[n])
                for n, (src, dst) in enumerate(((wg_ref, wg_f), (wu_ref, wu_f), (wd_ref, wd_f)))]

    @pl.when(i == 0)
    def _():
        for c in weight_copies(be_ref[0]):
            c.start()
        fetch(0, 0).start()

    @pl.when(i + 1 < n_used)
    def _():
        fetch(i + 1, 1 - slot).start()

    prev = be_ref[jnp.maximum(i - 1, 0)]

    @pl.when((i < n_used) & ((i == 0) | (be_ref[i] != prev)))
    def _():
        for c in weight_copies(be_ref[i]):
            c.wait()
        wg_b[...] = wg_f[...].astype(BF16)
        wu_b[...] = wu_f[...].astype(BF16)
        wd_b[...] = wd_f[...].astype(BF16)

        @pl.when(nxt_ref[i] >= 0)
        def _():
            for c in weight_copies(nxt_ref[i]):
                c.start()

    @pl.when(i >= 2)
    def _():
        write_back(i - 2, slot).wait()

    @pl.when(i < n_used)
    def _():
        fetch(i, slot).wait()
        h_lo, h_hi = _unpack_bf16_pairs(xbuf[slot])
        half = D // 2
        a = (jnp.dot(h_lo, wg_b[:half], preferred_element_type=F32)
             + jnp.dot(h_hi, wg_b[half:], preferred_element_type=F32))
        u = (jnp.dot(h_lo, wu_b[:half], preferred_element_type=F32)
             + jnp.dot(h_hi, wu_b[half:], preferred_element_type=F32))
        hid = (a * _sigmoid(a) * u).astype(BF16)
        ybuf[slot] = jnp.dot(hid, wd_b[...], preferred_element_type=F32)

    @pl.when(i >= n_used)
    def _():
        ybuf[slot] = jnp.zeros((MOE_BLOCK, D), F32)

    write_back(i, slot).start()

    @pl.when(i == pl.num_programs(0) - 1)
    def _():
        write_back(i - 1, 1 - slot).wait()
        write_back(i, slot).wait()


def _experts(xs, block_e, next_e, n_used, layer, w_gate, w_up, w_down):
    n_pad = xs.shape[0]
    n_blocks = n_pad // MOE_BLOCK
    hbm = pl.BlockSpec(memory_space=pl.ANY)
    return pl.pallas_call(
        functools.partial(_experts_kernel, layer=layer),
        name="experts",
        grid_spec=pltpu.PrefetchScalarGridSpec(
            num_scalar_prefetch=3,
            grid=(n_blocks,),
            in_specs=[hbm, hbm, hbm, hbm],
            out_specs=hbm,
            scratch_shapes=[
                pltpu.VMEM((D, D_EXPERT), F32),
                pltpu.VMEM((D, D_EXPERT), F32),
                pltpu.VMEM((D_EXPERT, D), F32),
                pltpu.VMEM((D, D_EXPERT), BF16),
                pltpu.VMEM((D, D_EXPERT), BF16),
                pltpu.VMEM((D_EXPERT, D), BF16),
                pltpu.VMEM((2, MOE_BLOCK, D // 2), U32),
                pltpu.VMEM((2, MOE_BLOCK, D), F32),
                pltpu.SemaphoreType.DMA((3,)),
                pltpu.SemaphoreType.DMA((2,)),
                pltpu.SemaphoreType.DMA((2,)),
            ],
        ),
        out_shape=jax.ShapeDtypeStruct((n_pad, 1, D), F32),
        compiler_params=_params(1),
    )(block_e, next_e, n_used, xs, w_gate, w_up, w_down)


def _combine_kernel(*refs, final, project):
    (dest_ref, dest_next_ref, x_ref, gate_ref, y_ref, p_ref, gple_ref, wp_ref, wg_ref,
     gfin_ref) = refs[:10]
    if project:
        gq_ref, gkv_ref, wq_ref, wkv_ref, out_ref, qt_ref, k_ref, vt_ref = refs[10:18]
        ybuf0, ybuf1, sem = refs[18:]
    else:
        out_ref, ybuf0, ybuf1, sem = refs[10:]
    i = pl.program_id(0)
    tiles = COMBINE_TILE // SUBLANES
    ybuf = (ybuf0, ybuf1)

    def row_copy(into):
        def make(tile, sub, k, dest):
            return pltpu.make_async_copy(y_ref.at[dest], ybuf[into].at[k, tile, pl.ds(sub, 1), :],
                                         sem.at[into])
        return make

    def combine(half, slot):
        rows = slice(half * COMBINE_TILE, (half + 1) * COMBINE_TILE)
        gates = gate_ref[rows, :]
        y0 = ybuf[slot][0].reshape(COMBINE_TILE, D)
        y1 = ybuf[slot][1].reshape(COMBINE_TILE, D)
        x = x_ref[rows, :] + gates[:, 0:1] * y0 + gates[:, 1:2] * y1
        h = (x * _rms_scale(x) * gple_ref[...]).astype(BF16)
        gate = _sigmoid(jnp.dot(h, wg_ref[...], preferred_element_type=F32))
        proj = jnp.dot(p_ref[0, rows, :].astype(BF16), wp_ref[...], preferred_element_type=F32)
        x = x + proj * gate
        if final:
            x = x * _rms_scale(x) * gfin_ref[...]
        out_ref[rows, :] = x
        if project:
            q, k, v = _qkv_rows(x, gq_ref, gkv_ref, wq_ref, wkv_ref)
            qt_ref[0, :, rows] = q.T.astype(BF16)
            k_ref[0, rows, :] = k.astype(BF16)
            vt_ref[0, :, rows] = v.T.astype(BF16)

    def wait(slot):
        for k in range(TOP_K):
            _wait_rows(ybuf[slot].at[k], sem.at[slot])

    @pl.when(i == 0)
    def _():
        _issue_rows(COMBINE_TILE, dest_ref, 0, row_copy(0))

    wait(0)
    _issue_rows_inline(COMBINE_TILE, dest_ref, tiles, row_copy(1))
    combine(0, 0)
    wait(1)
    _issue_rows_inline(COMBINE_TILE, dest_next_ref, 0, row_copy(0))
    combine(1, 1)

    @pl.when(i == pl.num_programs(0) - 1)
    def _():
        wait(0)


def _qkv_rows(x, gq_ref, gkv_ref, wq_ref, wkv_ref):
    xn = x * _rms_scale(x)
    hq = (xn * gq_ref[...]).astype(BF16)
    hkv = (xn * gkv_ref[...]).astype(BF16)
    q = jnp.dot(hq, wq_ref[...], preferred_element_type=F32) * (ATTN_SCALE * LOG2E)
    kv = jnp.dot(hkv, wkv_ref[...], preferred_element_type=F32)
    return q, kv[:, :D], kv[:, D:]


def _combine_ple(x2d, dest, gates, y, p3d, layer, g_ple, w_proj, w_gate, g_final, final, qkv=None):
    t = x2d.shape[0]
    step = 2 * COMBINE_TILE
    nt = t // step
    dest_blk = (1, step // SUBLANES, SLOTS_PER_TILE)
    const = lambda i: (0, 0)
    resident = functools.partial(pl.BlockSpec, index_map=const, pipeline_mode=pl.Buffered(1))
    in_specs = [
        pl.BlockSpec(dest_blk, lambda i: (i, 0, 0), memory_space=pltpu.SMEM),
        pl.BlockSpec(dest_blk, lambda i: (jnp.minimum(i + 1, nt - 1), 0, 0),
                     memory_space=pltpu.SMEM),
        pl.BlockSpec((step, D), lambda i: (i, 0)),
        pl.BlockSpec((step, TOP_K), lambda i: (i, 0)),
        pl.BlockSpec(memory_space=pl.ANY),
        pl.BlockSpec((1, step, D_PLE), lambda i: (layer, i, 0)),
        pl.BlockSpec((1, D), const),
        resident((D_PLE, D)),
        resident((D, D)),
        pl.BlockSpec((1, D), const),
    ]
    args = [dest, dest, x2d, gates, y, p3d, g_ple.reshape(1, D), w_proj.astype(BF16),
            w_gate.astype(BF16), g_final.reshape(1, D)]
    out_specs = [pl.BlockSpec((step, D), lambda i: (i, 0))]
    out_shape = [jax.ShapeDtypeStruct((t, D), F32)]
    if qkv is not None:
        seq, g_q, g_kv, w_q, w_kv = qkv
        per_seq = seq // step
        in_specs += [pl.BlockSpec((1, D), const), pl.BlockSpec((1, D), const),
                     resident((D, D)), resident((D, 2 * D))]
        args += [g_q.reshape(1, D), g_kv.reshape(1, D), w_q.astype(BF16), w_kv.astype(BF16)]
        feat_major = pl.BlockSpec((1, D, step), lambda i: (i // per_seq, 0, i % per_seq))
        row_major = pl.BlockSpec((1, step, D), lambda i: (i // per_seq, i % per_seq, 0))
        out_specs += [feat_major, row_major, feat_major]
        out_shape += [jax.ShapeDtypeStruct((t // seq, D, seq), BF16),
                      jax.ShapeDtypeStruct((t // seq, seq, D), BF16),
                      jax.ShapeDtypeStruct((t // seq, D, seq), BF16)]
    outs = pl.pallas_call(
        functools.partial(_combine_kernel, final=final, project=qkv is not None),
        name="combine_final" if final else "combine",
        grid=(nt,),
        in_specs=in_specs,
        out_specs=out_specs,
        out_shape=out_shape,
        scratch_shapes=[
            pltpu.VMEM((TOP_K, COMBINE_TILE // SUBLANES, SUBLANES, D), F32),
            pltpu.VMEM((TOP_K, COMBINE_TILE // SUBLANES, SUBLANES, D), F32),
            pltpu.SemaphoreType.DMA((2,)),
        ],
        compiler_params=_params(1),
    )(*args)
    return outs[0] if qkv is None else tuple(outs)


def _moe_ple(x, p, layer, norm_ffn, w_group, b_group, w_expert, b_expert, w_gate, w_up, w_down,
             norm_ple, ple_w_proj, ple_w_gate, final_norm, final, qkv=None):
    b, s, _ = x.shape
    t = b * s
    x2d = x.reshape(t, D)
    ids, gates, counts = _router(x2d, norm_ffn, w_group, b_group, w_expert, b_expert)

    counts = counts[:, 0]
    padded = (counts + MOE_BLOCK - 1) // MOE_BLOCK * MOE_BLOCK
    pad_end = jnp.cumsum(padded)
    pad_start = pad_end - padded
    n_blocks = t * TOP_K // MOE_BLOCK + N_EXPERTS
    n_pad = n_blocks * MOE_BLOCK
    e = ids[:, 0:TOP_K, :].transpose(1, 0, 2).reshape(TOP_K, t)
    r = ids[:, TOP_K:2 * TOP_K, :].transpose(1, 0, 2).reshape(TOP_K, t)
    experts = jnp.arange(N_EXPERTS, dtype=jnp.int32)
    dest = r + jnp.sum(jnp.where(e[..., None] == experts, pad_start, 0), axis=-1)
    gate_cols = gates[:, 0:TOP_K, :].transpose(0, 2, 1).reshape(t, TOP_K)
    block_row = jnp.arange(n_blocks, dtype=jnp.int32) * MOE_BLOCK
    block_e = jnp.minimum(jnp.sum(pad_end[None, :] <= block_row[:, None], axis=-1),
                          N_EXPERTS - 1).astype(jnp.int32)
    n_used = (pad_end[-1:] // MOE_BLOCK).astype(jnp.int32)
    tail = (n_used[0] + jnp.arange(N_EXPERTS, dtype=jnp.int32)) * MOE_BLOCK
    zrow = jnp.concatenate([jnp.where(padded > 0, pad_end - MOE_BLOCK, -1),
                            jnp.where(tail < n_pad, tail, -1)]).astype(jnp.int32)

    def tiles(a, tile):
        return a.T.reshape(t // tile, tile // SUBLANES, SLOTS_PER_TILE).astype(jnp.int32)

    xs = _dispatch(x2d, norm_ffn, tiles(dest, ROUTER_TILE), zrow, n_pad)
    later = jnp.where((experts[None, :] > block_e[:, None]) & (padded[None, :] > 0), experts[None, :],
                      N_EXPERTS)
    next_e = jnp.min(later, axis=-1)
    next_e = jnp.where(next_e < N_EXPERTS, next_e, -1).astype(jnp.int32)
    y = _experts(xs, block_e, next_e, n_used, layer, w_gate, w_up, w_down)
    out = _combine_ple(x2d, tiles(dest, 2 * COMBINE_TILE), gate_cols, y, p.reshape(-1, t, D_PLE), layer,
                       norm_ple, ple_w_proj, ple_w_gate, final_norm, final,
                       None if qkv is None else (s,) + tuple(qkv))
    if qkv is None:
        return out.reshape(b, s, D)
    return (out[0].reshape(b, s, D),) + out[1:]


Q_GROUP = 2 * CHUNK
G_BAND = BAND + CHUNK
PAIR = 2 * B_HEAD_DIM
ONES_ROWS = BF16_SUBLANES


def _attn_kernel(qt_ref, kp_ref, kc_ref, vtp_ref, vtc_ref, bias_ref, x_ref, wo_ref, out_ref, o_scr,
                 s_scr0, s_scr1, p_scr0, p_scr1):
    drow = lax.broadcasted_iota(jnp.int32, (PAIR, Q_GROUP), 0)
    first_head = drow < B_HEAD_DIM
    s_scr = (s_scr0, s_scr1)
    p_scr = (p_scr0, p_scr1)

    def attend(first_tile):
        units = [(g, pr) for g in range(SEQ_TILE // Q_GROUP) for pr in range(B_HEADS // 2)]

        def geometry(g):
            w0 = g * Q_GROUP
            n_prev = SEQ_TILE - w0
            return w0, n_prev, G_BAND - n_prev

        def keys(g):
            return slice(geometry(g)[1], None) if first_tile else slice(None)

        def scores(unit, s_ref):
            g, pr = unit
            w0, n_prev, n_cur = geometry(g)
            feat = slice(pr * PAIR, (pr + 1) * PAIR)
            qt = qt_ref[0, feat, w0:w0 + Q_GROUP]
            zero = jnp.zeros_like(qt)
            qblk = jnp.concatenate([jnp.where(first_head, qt, zero),
                                    jnp.where(first_head, zero, qt)], axis=1)
            if first_tile:
                kb = kc_ref[0, :n_cur, feat]
            else:
                kb = jnp.concatenate([kp_ref[0, w0:, feat], kc_ref[0, :n_cur, feat]], axis=0)
            s_ref[keys(g), :] = (jnp.dot(kb, qblk, preferred_element_type=F32)
                                 + bias_ref[pr, keys(g), :])

        def weights(unit, s_ref, p_ref):
            g, _ = unit
            s = s_ref[keys(g), :]
            m = jnp.max(s, axis=0, keepdims=True)
            p_ref[keys(g), :] = jnp.exp2(s - m).astype(BF16)

        def values(unit, p_ref):
            g, pr = unit
            w0, n_prev, n_cur = geometry(g)
            feat = slice(pr * PAIR, (pr + 1) * PAIR)
            if first_tile:
                vt = vtc_ref[0, feat, :n_cur]
            else:
                vt = jnp.concatenate([vtp_ref[0, feat, w0:], vtc_ref[0, feat, :n_cur]], axis=1)
            ones = jnp.ones((ONES_ROWS, vt.shape[1]), BF16)
            ot = jnp.dot(jnp.concatenate([vt, ones], axis=0), p_ref[keys(g), :],
                         preferred_element_type=F32)
            inv = 1.0 / ot[PAIR:PAIR + 1, :]
            ot = jnp.where(first_head, ot[:PAIR, :Q_GROUP] * inv[:, :Q_GROUP],
                           ot[:PAIR, Q_GROUP:] * inv[:, Q_GROUP:])
            o_scr[w0:w0 + Q_GROUP, feat] = ot.T.astype(BF16)

        n_units = len(units)
        scores(units[0], s_scr[0])
        scores(units[1], s_scr[1])
        weights(units[0], s_scr[0], p_scr[0])
        for n, unit in enumerate(units):
            if n + 2 < n_units:
                scores(units[n + 2], s_scr[n % 2])
            if n + 1 < n_units:
                weights(units[n + 1], s_scr[(n + 1) % 2], p_scr[(n + 1) % 2])
            values(unit, p_scr[n % 2])

    @pl.when(pl.program_id(1) == 0)
    def _():
        attend(True)

    @pl.when(pl.program_id(1) > 0)
    def _():
        attend(False)

    out_ref[0] = x_ref[0] + jnp.dot(o_scr[...], wo_ref[...], preferred_element_type=F32)


def _attn(x, qt, k, vt, bias_t, w_o):
    b, s, _ = x.shape
    cur = lambda bi, si: (bi, si, 0)
    prev = lambda bi, si: (bi, jnp.maximum(si - 1, 0), 0)
    cur_t = lambda bi, si: (bi, 0, si)
    prev_t = lambda bi, si: (bi, 0, jnp.maximum(si - 1, 0))
    blk = (1, SEQ_TILE, D)
    blk_t = (1, D, SEQ_TILE)
    return pl.pallas_call(
        _attn_kernel,
        name="attn",
        grid=(b, s // SEQ_TILE),
        in_specs=[
            pl.BlockSpec(blk_t, cur_t),
            pl.BlockSpec(blk, prev),
            pl.BlockSpec(blk, cur),
            pl.BlockSpec(blk_t, prev_t),
            pl.BlockSpec(blk_t, cur_t),
            pl.BlockSpec((B_HEADS // 2, G_BAND, 2 * Q_GROUP), lambda bi, si: (0, 0, 0),
                         pipeline_mode=pl.Buffered(1)),
            pl.BlockSpec(blk, cur),
            pl.BlockSpec((D, D), lambda bi, si: (0, 0), pipeline_mode=pl.Buffered(1)),
        ],
        out_specs=pl.BlockSpec(blk, cur),
        out_shape=jax.ShapeDtypeStruct(x.shape, F32),
        scratch_shapes=[pltpu.VMEM((SEQ_TILE, D), BF16)]
        + [pltpu.VMEM((G_BAND, 2 * Q_GROUP), F32)] * 2
        + [pltpu.VMEM((G_BAND, 2 * Q_GROUP), BF16)] * 2,
        compiler_params=_params(2),
    )(qt, k, k, vt, vt, bias_t, x, w_o.astype(BF16))


def _group_bias(table):
    band = _band_bias(table) * LOG2E
    pad = lambda lo, hi: jnp.pad(band, ((0, 0), (0, 0), (lo, hi)), constant_values=NEG_INF)
    both = jnp.concatenate([pad(0, CHUNK), pad(CHUNK, 0)], axis=1)
    both = both.reshape(B_HEADS // 2, 2, Q_GROUP, G_BAND)
    return both.transpose(0, 3, 1, 2).reshape(B_HEADS // 2, G_BAND, 2 * Q_GROUP)


def _band_bias(table):
    n_rel = REL_MAX - REL_MIN + 1
    span = BAND + CHUNK - 1
    head = jnp.broadcast_to(table[:, n_rel - 1:], (table.shape[0], span - n_rel))
    ext = jnp.concatenate([head, table[:, ::-1]], axis=1)
    rows = [ext[:, CHUNK - 1 - q:CHUNK - 1 - q + BAND] for q in range(CHUNK)]
    return jnp.stack(rows, axis=1)


def kernel(x, p, a_w_in, a_lb_logits, a_out_norm, a_w_o, kv_norm, w_kv, b_w_q, b_rel_bias, b_w_o,
           norm_mix, norm_ffn, norm_ple, moe_w_group, moe_b_group, moe_w_expert, moe_b_expert,
           moe_w_gate, moe_w_up, moe_w_down, ple_w_proj, ple_w_gate, final_norm):
    b, s, _ = x.shape
    lower_bounds = jnp.cumsum(jax.nn.softmax(a_lb_logits.astype(F32), axis=0), axis=0)

    def moe(xi, i, final, qkv=None):
        return _moe_ple(xi, p, i, norm_ffn[i], moe_w_group[i], moe_b_group[i], moe_w_expert[i],
                        moe_b_expert[i], moe_w_gate, moe_w_up, moe_w_down, norm_ple[i],
                        ple_w_proj[i], ple_w_gate[i], final_norm, final, qkv)

    x = _mixer_a(x, norm_mix[0], a_w_in[0], lower_bounds[0], a_out_norm[0], a_w_o[0])
    x, qt, k, vt = moe(x, 0, False, (norm_mix[1], kv_norm, b_w_q[0], w_kv))

    x = _attn(x, qt, k, vt, _group_bias(b_rel_bias[0].astype(F32)), b_w_o[0])
    x = moe(x, 1, True)
    return x
```

```python
import functools

import jax
import jax.numpy as jnp
from jax import lax
from jax.experimental import pallas as pl
from jax.experimental.pallas import tpu as pltpu

F32 = jnp.float32
BF16 = jnp.bfloat16
U32 = jnp.uint32

D = 1024
CHUNK = 64
A_HEADS = 8
A_HEAD_DIM = 128
B_HEADS = 16
B_HEAD_DIM = 64
LEFT_CHUNKS = 8
BAND = (LEFT_CHUNKS + 1) * CHUNK
REL_MIN = -(CHUNK - 1)
REL_MAX = 256
ATTN_SCALE = B_HEAD_DIM ** -0.5
N_GROUPS = 4
EXPERTS_PER_GROUP = 8
N_EXPERTS = 32
TOP_K = 2
D_EXPERT = 512
MOE_BLOCK = 512
D_PLE = 256
EPS = 1e-6
NEG_INF = -1e30
LOG2E = 1.4426950408889634

SUBLANES = 8
BF16_SUBLANES = 16
V7X_VMEM_BYTES = 64 * 1024 * 1024

SEQ_TILE = 512
ROUTER_TILE = 512
COMBINE_TILE = 512
ROUTER_ROWS = -(-(N_EXPERTS + N_GROUPS) // BF16_SUBLANES) * BF16_SUBLANES
VMEM_LIMIT = V7X_VMEM_BYTES - 8 * 1024 * 1024


def _params(n_axes, vmem=VMEM_LIMIT):
    return pltpu.CompilerParams(dimension_semantics=("arbitrary",) * n_axes,
                                vmem_limit_bytes=vmem)


def _rms_scale(x):
    return lax.rsqrt(jnp.mean(x * x, axis=-1, keepdims=True) + EPS)


def _sigmoid(x):
    return 1.0 / (1.0 + jnp.exp(-x))


def _mixer_a_kernel(x_ref, g_ref, win_ref, lb_ref, onorm_ref, wo_ref, out_ref,
                    proj_scr, o_scr, state_scr, g_scr0, g_scr1, k_scr0, k_scr1,
                    qd_scr0, qd_scr1, qd_scr2, kt_scr0, kt_scr1, kt_scr2, att_scr0, att_scr1, att_scr2):
    @pl.when(pl.program_id(1) == 0)
    def _():
        state_scr[...] = jnp.zeros_like(state_scr)

    x = x_ref[0]
    h = (x * _rms_scale(x) * g_ref[...]).astype(BF16)
    proj_scr[...] = jnp.dot(h, win_ref[...], preferred_element_type=F32)

    row = lax.broadcasted_iota(jnp.int32, (CHUNK, CHUNK), 0)
    col = lax.broadcasted_iota(jnp.int32, (CHUNK, CHUNK), 1)
    causal = row >= col
    tril = causal.astype(BF16)
    lb = lb_ref[...]
    onorm = onorm_ref[...]

    g_scr, k_scr = (g_scr0, g_scr1), (k_scr0, k_scr1)
    qd_scr, kt_scr = (qd_scr0, qd_scr1, qd_scr2), (kt_scr0, kt_scr1, kt_scr2)
    att_scr = (att_scr0, att_scr1, att_scr2)
    n_chunks = SEQ_TILE // CHUNK
    units = [(c, hd) for c in range(n_chunks) for hd in range(A_HEADS)]

    def rows_of(c):
        return slice(c * CHUNK, (c + 1) * CHUNK)

    def decay(c):
        f = lb + (1.0 - lb) * _sigmoid(proj_scr[rows_of(c), D:2 * D])
        logf = jnp.log(f)
        hi = logf.astype(BF16)
        lo = (logf - hi.astype(F32)).astype(BF16)
        g_scr[c % 2][...] = (jnp.dot(tril, hi, preferred_element_type=F32)
                             + jnp.dot(tril, lo, preferred_element_type=F32))
        k_scr[c % 2][...] = 1.0 - f

    def intra(n):
        c, hd = units[n]
        sl = slice(hd * A_HEAD_DIM, (hd + 1) * A_HEAD_DIM)
        gh = g_scr[c % 2][:, sl]
        g_last = gh[CHUNK - 1:CHUNK, :]
        k = k_scr[c % 2][:, sl]
        q_dec = (proj_scr[rows_of(c), sl] * jnp.exp(gh)).astype(BF16)
        k_inv = (k * jnp.exp(-gh)).astype(BF16)
        qd_scr[n % 3][...] = q_dec
        kt_scr[n % 3][...] = (k * jnp.exp(g_last - gh)).astype(BF16)
        att = lax.dot_general(q_dec, k_inv, (((1,), (1,)), ((), ())),
                              preferred_element_type=F32)
        att_scr[n % 3][...] = jnp.where(causal, att, 0.0).astype(BF16)

    def output(n):
        c, hd = units[n]
        sl = slice(hd * A_HEAD_DIM, (hd + 1) * A_HEAD_DIM)
        rows = rows_of(c)
        g_last = g_scr[c % 2][CHUNK - 1:CHUNK, sl]
        v = proj_scr[rows, 2 * D + hd * A_HEAD_DIM:2 * D + (hd + 1) * A_HEAD_DIM]
        st = state_scr[hd]
        o = (jnp.dot(att_scr[n % 3][...], v.astype(BF16), preferred_element_type=F32)
             + lax.dot_general(qd_scr[n % 3][...], st.astype(BF16), (((1,), (1,)), ((), ())),
                               preferred_element_type=F32))
        v_t = v.T.astype(BF16)
        state_scr[hd] = st * jnp.exp(g_last) + jnp.dot(v_t, kt_scr[n % 3][...],
                                                       preferred_element_type=F32)
        o = o * _rms_scale(o)
        og = proj_scr[rows, 3 * D + hd * A_HEAD_DIM:3 * D + (hd + 1) * A_HEAD_DIM]
        o = o * onorm[:, sl] * (og * _sigmoid(og))
        o_scr[rows, sl] = o.astype(BF16)

    decay(0)
    intra(0)
    intra(1)
    for n, (c, hd) in enumerate(units):
        if hd == 0 and c + 1 < n_chunks:
            decay(c + 1)
        if n + 2 < len(units):
            intra(n + 2)
        output(n)
    out_ref[0] = x + jnp.dot(o_scr[...], wo_ref[...], preferred_element_type=F32)


def _mixer_a(x, g, w_in, lb, out_norm, w_o):
    b, s, _ = x.shape
    const = lambda bi, si: (0, 0)
    return pl.pallas_call(
        _mixer_a_kernel,
        name="mixer_a",
        grid=(b, s // SEQ_TILE),
        in_specs=[
            pl.BlockSpec((1, SEQ_TILE, D), lambda bi, si: (bi, si, 0)),
            pl.BlockSpec((1, D), const),
            pl.BlockSpec((D, 4 * D), const, pipeline_mode=pl.Buffered(1)),
            pl.BlockSpec((1, D), const),
            pl.BlockSpec((1, D), const),
            pl.BlockSpec((D, D), const, pipeline_mode=pl.Buffered(1)),
        ],
        out_specs=pl.BlockSpec((1, SEQ_TILE, D), lambda bi, si: (bi, si, 0)),
        out_shape=jax.ShapeDtypeStruct(x.shape, F32),
        scratch_shapes=[
            pltpu.VMEM((SEQ_TILE, 4 * D), F32),
            pltpu.VMEM((SEQ_TILE, D), BF16),
            pltpu.VMEM((A_HEADS, A_HEAD_DIM, A_HEAD_DIM), F32),
        ] + [pltpu.VMEM((CHUNK, D), F32)] * 4
          + [pltpu.VMEM((CHUNK, A_HEAD_DIM), BF16)] * 6
          + [pltpu.VMEM((CHUNK, CHUNK), BF16)] * 3,
        compiler_params=_params(2),
    )(x, g.reshape(1, D), w_in.astype(BF16), lb.reshape(1, D), out_norm.reshape(1, D),
      w_o.astype(BF16))


def _router_kernel(x_ref, g_ref, wr_ref, br_ref, ids_ref, gates_ref, counts_ref, cnt_scr,
                   before_scr):
    tm = ROUTER_TILE

    @pl.when(pl.program_id(0) == 0)
    def _():
        cnt_scr[...] = jnp.zeros_like(cnt_scr)
        tr = lax.broadcasted_iota(jnp.int32, (tm, tm), 0)
        tc = lax.broadcasted_iota(jnp.int32, (tm, tm), 1)
        before_scr[...] = (tr < tc).astype(BF16)

    x = x_ref[...]
    h = x * _rms_scale(x) * g_ref[...]
    h_hi = h.astype(BF16)
    h_lo = (h - h_hi.astype(F32)).astype(BF16)
    nt = (((1,), (1,)), ((), ()))
    both = lax.dot_general(wr_ref[...], h_hi, nt, preferred_element_type=F32)
    cross = lax.dot_general(wr_ref[0:ROUTER_ROWS], h_lo, nt, preferred_element_type=F32)
    logits = both[0:ROUTER_ROWS] + both[ROUTER_ROWS:] + cross + br_ref[...]
    el = logits[0:N_EXPERTS]
    gl = logits[N_EXPERTS:ROUTER_ROWS]
    grow = lax.broadcasted_iota(jnp.int32, gl.shape, 0)
    gl = jnp.where(grow < N_GROUPS, gl, -jnp.inf)
    gmax = jnp.max(gl, axis=0, keepdims=True)
    gsum = jnp.sum(jnp.exp(gl - gmax), axis=0, keepdims=True)
    grp_w = 1.0 / gsum
    gidx = jnp.min(jnp.where(gl == gmax, grow, N_GROUPS), axis=0, keepdims=True)

    erow = lax.broadcasted_iota(jnp.int32, el.shape, 0)
    masked = jnp.where((erow // EXPERTS_PER_GROUP) == gidx, el, -jnp.inf)
    top1 = jnp.max(masked, axis=0, keepdims=True)
    i1 = jnp.min(jnp.where(masked == top1, erow, N_EXPERTS), axis=0, keepdims=True)
    masked2 = jnp.where(erow == i1, -jnp.inf, masked)
    top2 = jnp.max(masked2, axis=0, keepdims=True)
    i2 = jnp.min(jnp.where(masked2 == top2, erow, N_EXPERTS), axis=0, keepdims=True)
    e2 = jnp.exp(top2 - top1)
    denom = 1.0 + e2
    g1 = grp_w * (1.0 / denom)
    g2 = grp_w * (e2 / denom)

    sel1 = erow == i1
    sel2 = erow == i2
    onehot = (sel1 | sel2).astype(BF16)
    prefix = jnp.dot(onehot, before_scr[...], preferred_element_type=F32) + cnt_scr[...]
    r1 = jnp.sum(jnp.where(sel1, prefix, 0.0), axis=0, keepdims=True)
    r2 = jnp.sum(jnp.where(sel2, prefix, 0.0), axis=0, keepdims=True)
    cnt_scr[...] += jnp.sum(onehot.astype(F32), axis=1, keepdims=True)

    zi = jnp.zeros((4, tm), jnp.int32)
    ids_ref[0] = jnp.concatenate(
        [i1, i2, r1.astype(jnp.int32), r2.astype(jnp.int32), zi], axis=0)
    gates_ref[0] = jnp.concatenate([g1, g2, jnp.zeros((6, tm), F32)], axis=0)
    counts_ref[...] = jnp.broadcast_to(cnt_scr[...], counts_ref.shape).astype(jnp.int32)


def _router(x2d, g, w_group, b_group, w_expert, b_expert):
    t = x2d.shape[0]
    nt = t // ROUTER_TILE
    pad = ROUTER_ROWS - N_EXPERTS - N_GROUPS
    wr = jnp.concatenate([w_expert.T, w_group.T, jnp.zeros((pad, D), F32)], axis=0)
    wr_hi = wr.astype(BF16)
    wr_lo = (wr - wr_hi.astype(F32)).astype(BF16)
    br = jnp.concatenate([b_expert, b_group, jnp.zeros((pad,), F32)]).reshape(ROUTER_ROWS, 1)
    const = lambda i: (0, 0)
    return pl.pallas_call(
        _router_kernel,
        name="router",
        grid=(nt,),
        in_specs=[
            pl.BlockSpec((ROUTER_TILE, D), lambda i: (i, 0)),
            pl.BlockSpec((1, D), const),
            pl.BlockSpec((2 * ROUTER_ROWS, D), const),
            pl.BlockSpec((ROUTER_ROWS, 1), const),
        ],
        out_specs=[
            pl.BlockSpec((1, 8, ROUTER_TILE), lambda i: (i, 0, 0)),
            pl.BlockSpec((1, 8, ROUTER_TILE), lambda i: (i, 0, 0)),
            pl.BlockSpec((N_EXPERTS, 128), const),
        ],
        out_shape=[
            jax.ShapeDtypeStruct((nt, 8, ROUTER_TILE), jnp.int32),
            jax.ShapeDtypeStruct((nt, 8, ROUTER_TILE), F32),
            jax.ShapeDtypeStruct((N_EXPERTS, 128), jnp.int32),
        ],
        scratch_shapes=[pltpu.VMEM((N_EXPERTS, 1), F32),
                        pltpu.VMEM((ROUTER_TILE, ROUTER_TILE), BF16)],
        compiler_params=_params(1),
    )(x2d, g.reshape(1, D), jnp.concatenate([wr_hi, wr_lo], axis=0), br)


SLOTS_PER_TILE = SUBLANES * TOP_K


def _issue_tile_rows(dest_ref, which, j, make_copy):
    for u in range(SUBLANES):
        for k in range(TOP_K):
            dest = dest_ref[which, j, k * SUBLANES + u]
            make_copy(j, u, k, dest).start(priority=(u * TOP_K + k) % 2)


def _issue_rows(n_rows, dest_ref, which, make_copy):
    def body(j, c):
        _issue_tile_rows(dest_ref, which, j, make_copy)
        return c

    lax.fori_loop(0, n_rows // SUBLANES, body, 0)


def _issue_rows_inline(n_rows, dest_ref, which, make_copy):
    for j in range(n_rows // SUBLANES):
        _issue_tile_rows(dest_ref, which, j, make_copy)


def _pack_bf16_pairs(x):
    half = x.shape[1] // 2
    lo = lax.bitcast_convert_type(x[:, :half].astype(BF16).astype(F32), U32)
    hi = lax.bitcast_convert_type(x[:, half:].astype(BF16).astype(F32), U32)
    return (hi & jnp.uint32(0xFFFF0000)) | (lo >> 16)


def _unpack_bf16_pairs(u):
    lo = lax.bitcast_convert_type(u << 16, F32).astype(BF16)
    hi = lax.bitcast_convert_type(u & jnp.uint32(0xFFFF0000), F32).astype(BF16)
    return lo, hi


def _tile_rows(x):
    return x.reshape(x.shape[0] // SUBLANES, SUBLANES, x.shape[1])


def _wait_rows(buf_ref, sem):
    pltpu.make_async_copy(buf_ref, buf_ref, sem).wait()


def _dispatch_kernel(zrow_ref, dest_ref, x_ref, g_ref, xs_ref, hbuf, zbuf, zsem, sem):
    @pl.when(pl.program_id(0) == 0)
    def _():
        zbuf[...] = jnp.zeros_like(zbuf)

        def zcopy(e):
            return pltpu.make_async_copy(zbuf, xs_ref.at[pl.ds(zrow_ref[e], MOE_BLOCK)], zsem)

        def zstart(e, c):
            @pl.when(zrow_ref[e] >= 0)
            def _():
                zcopy(e).start()
            return c

        def zwait(e, c):
            @pl.when(zrow_ref[e] >= 0)
            def _():
                zcopy(e).wait()
            return c

        lax.fori_loop(0, 2 * N_EXPERTS, zstart, 0)
        lax.fori_loop(0, 2 * N_EXPERTS, zwait, 0)

    i = pl.program_id(0)
    slot = i % 2
    x = x_ref[...]
    hbuf[slot] = _tile_rows(_pack_bf16_pairs(x * _rms_scale(x) * g_ref[...]))

    def row_copy(tile, sub, k, dest):
        return pltpu.make_async_copy(hbuf.at[slot, tile, pl.ds(sub, 1), :], xs_ref.at[dest],
                                     sem.at[slot])

    _issue_rows(ROUTER_TILE, dest_ref, 0, row_copy)

    def drain(which):
        for _ in range(TOP_K):
            _wait_rows(hbuf.at[which], sem.at[which])

    @pl.when(i > 0)
    def _():
        drain(1 - slot)

    @pl.when(i == pl.num_programs(0) - 1)
    def _():
        drain(slot)


def _dispatch(x2d, g, dest, zrow, n_pad):
    t = x2d.shape[0]
    nt = t // ROUTER_TILE
    return pl.pallas_call(
        _dispatch_kernel,
        name="dispatch",
        grid_spec=pltpu.PrefetchScalarGridSpec(
            num_scalar_prefetch=1,
            grid=(nt,),
            in_specs=[
                pl.BlockSpec((1, ROUTER_TILE // SUBLANES, SLOTS_PER_TILE),
                             lambda i, z: (i, 0, 0), memory_space=pltpu.SMEM),
                pl.BlockSpec((ROUTER_TILE, D), lambda i, z: (i, 0)),
                pl.BlockSpec((1, D), lambda i, z: (0, 0)),
            ],
            out_specs=pl.BlockSpec(memory_space=pl.ANY),
            scratch_shapes=[
                pltpu.VMEM((2, ROUTER_TILE // SUBLANES, SUBLANES, D // 2), U32),
                pltpu.VMEM((MOE_BLOCK, 1, D // 2), U32),
                pltpu.SemaphoreType.DMA(()),
                pltpu.SemaphoreType.DMA((2,)),
            ],
        ),
        out_shape=jax.ShapeDtypeStruct((n_pad, 1, D // 2), U32),
        compiler_params=_params(1),
    )(zrow, dest, x2d, g.reshape(1, D))


def _experts_kernel(be_ref, nxt_ref, nu_ref, xs_ref, wg_ref, wu_ref, wd_ref, y_ref,
                    wg_f, wu_f, wd_f, wg_b, wu_b, wd_b, xbuf, ybuf, w_sem, in_sem, out_sem, *, layer):
    i = pl.program_id(0)
    n_used = nu_ref[0]
    slot = i % 2

    def block_rows(ref, blk):
        return ref.at[pl.ds(pl.multiple_of(blk * MOE_BLOCK, MOE_BLOCK), MOE_BLOCK), 0]

    def fetch(blk, into):
        return pltpu.make_async_copy(block_rows(xs_ref, blk), xbuf.at[into], in_sem.at[into])

    def write_back(blk, from_):
        return pltpu.make_async_copy(ybuf.at[from_], block_rows(y_ref, blk), out_sem.at[from_])

    def weight_copies(e):
        return [pltpu.make_async_copy(src.at[layer, e], dst, w_sem.at[n])
                for n, (src, dst) in enumerate(((wg_ref, wg_f), (wu_ref, wu_f), (wd_ref, wd_f)))]

    @pl.when(i == 0)
    def _():
        for c in weight_copies(be_ref[0]):
            c.start()
        fetch(0, 0).start()

    @pl.when(i + 1 < n_used)
    def _():
        fetch(i + 1, 1 - slot).start()

    prev = be_ref[jnp.maximum(i - 1, 0)]

    @pl.when((i < n_used) & ((i == 0) | (be_ref[i] != prev)))
    def _():
        for c in weight_copies(be_ref[i]):
            c.wait()
        wg_b[...] = wg_f[...].astype(BF16)
        wu_b[...] = wu_f[...].astype(BF16)
        wd_b[...] = wd_f[...].astype(BF16)

        @pl.when(nxt_ref[i] >= 0)
        def _():
            for c in weight_copies(nxt_ref[i]):
                c.start()

    @pl.when(i >= 2)
    def _():
        write_back(i - 2, slot).wait()

    @pl.when(i < n_used)
    def _():
        fetch(i, slot).wait()
        h_lo, h_hi = _unpack_bf16_pairs(xbuf[slot])
        half = D // 2
        a = (jnp.dot(h_lo, wg_b[:half], preferred_element_type=F32)
             + jnp.dot(h_hi, wg_b[half:], preferred_element_type=F32))
        u = (jnp.dot(h_lo, wu_b[:half], preferred_element_type=F32)
             + jnp.dot(h_hi, wu_b[half:], preferred_element_type=F32))
        hid = (a * _sigmoid(a) * u).astype(BF16)
        ybuf[slot] = jnp.dot(hid, wd_b[...], preferred_element_type=F32)

    @pl.when(i >= n_used)
    def _():
        ybuf[slot] = jnp.zeros((MOE_BLOCK, D), F32)

    write_back(i, slot).start()

    @pl.when(i == pl.num_programs(0) - 1)
    def _():
        write_back(i - 1, 1 - slot).wait()
        write_back(i, slot).wait()


def _experts(xs, block_e, next_e, n_used, layer, w_gate, w_up, w_down):
    n_pad = xs.shape[0]
    n_blocks = n_pad // MOE_BLOCK
    hbm = pl.BlockSpec(memory_space=pl.ANY)
    return pl.pallas_call(
        functools.partial(_experts_kernel, layer=layer),
        name="experts",
        grid_spec=pltpu.PrefetchScalarGridSpec(
            num_scalar_prefetch=3,
            grid=(n_blocks,),
            in_specs=[hbm, hbm, hbm, hbm],
            out_specs=hbm,
            scratch_shapes=[
                pltpu.VMEM((D, D_EXPERT), F32),
                pltpu.VMEM((D, D_EXPERT), F32),
                pltpu.VMEM((D_EXPERT, D), F32),
                pltpu.VMEM((D, D_EXPERT), BF16),
                pltpu.VMEM((D, D_EXPERT), BF16),
                pltpu.VMEM((D_EXPERT, D), BF16),
                pltpu.VMEM((2, MOE_BLOCK, D // 2), U32),
                pltpu.VMEM((2, MOE_BLOCK, D), F32),
                pltpu.SemaphoreType.DMA((3,)),
                pltpu.SemaphoreType.DMA((2,)),
                pltpu.SemaphoreType.DMA((2,)),
            ],
        ),
        out_shape=jax.ShapeDtypeStruct((n_pad, 1, D), F32),
        compiler_params=_params(1),
    )(block_e, next_e, n_used, xs, w_gate, w_up, w_down)


def _combine_kernel(*refs, final, project):
    (dest_ref, dest_next_ref, x_ref, gate_ref, y_ref, p_ref, gple_ref, wp_ref, wg_ref,
     gfin_ref) = refs[:10]
    if project:
        gq_ref, gkv_ref, wq_ref, wkv_ref, out_ref, qt_ref, k_ref, vt_ref = refs[10:18]
        ybuf0, ybuf1, sem = refs[18:]
    else:
        out_ref, ybuf0, ybuf1, sem = refs[10:]
    i = pl.program_id(0)
    ybuf = (ybuf0, ybuf1)

    def row_copy(into):
        def make(tile, sub, k, dest):
            return pltpu.make_async_copy(y_ref.at[dest], ybuf[into].at[k, tile, pl.ds(sub, 1), :],
                                         sem.at[into])
        return make

    def combine(half, slot):
        rows = slice(half * COMBINE_TILE, (half + 1) * COMBINE_TILE)
        gates = gate_ref[half].T
        y0 = ybuf[slot][0].reshape(COMBINE_TILE, D)
        y1 = ybuf[slot][1].reshape(COMBINE_TILE, D)
        x = x_ref[rows, :] + gates[:, 0:1] * y0 + gates[:, 1:2] * y1
        h = (x * _rms_scale(x) * gple_ref[...]).astype(BF16)
        gate = _sigmoid(jnp.dot(h, wg_ref[...], preferred_element_type=F32))
        proj = jnp.dot(p_ref[0, rows, :].astype(BF16), wp_ref[...], preferred_element_type=F32)
        x = x + proj * gate
        if final:
            x = x * _rms_scale(x) * gfin_ref[...]
        out_ref[rows, :] = x
        if project:
            q, k, v = _qkv_rows(x, gq_ref, gkv_ref, wq_ref, wkv_ref)
            qt_ref[0, :, rows] = q.T.astype(BF16)
            k_ref[0, rows, :] = k.astype(BF16)
            vt_ref[0, :, rows] = v.T.astype(BF16)

    def wait(slot):
        for k in range(TOP_K):
            _wait_rows(ybuf[slot].at[k], sem.at[slot])

    @pl.when(i == 0)
    def _():
        _issue_rows(COMBINE_TILE, dest_ref, 0, row_copy(0))

    wait(0)
    _issue_rows_inline(COMBINE_TILE, dest_ref, 1, row_copy(1))
    combine(0, 0)
    wait(1)
    _issue_rows_inline(COMBINE_TILE, dest_next_ref, 0, row_copy(0))
    combine(1, 1)

    @pl.when(i == pl.num_programs(0) - 1)
    def _():
        wait(0)


def _qkv_rows(x, gq_ref, gkv_ref, wq_ref, wkv_ref):
    xn = x * _rms_scale(x)
    hq = (xn * gq_ref[...]).astype(BF16)
    hkv = (xn * gkv_ref[...]).astype(BF16)
    q = jnp.dot(hq, wq_ref[...], preferred_element_type=F32) * (ATTN_SCALE * LOG2E)
    kv = jnp.dot(hkv, wkv_ref[...], preferred_element_type=F32)
    return q, kv[:, :D], kv[:, D:]


def _combine_ple(x2d, dest, gates, y, p3d, layer, g_ple, w_proj, w_gate, g_final, final, qkv=None):
    assert COMBINE_TILE == ROUTER_TILE
    t = x2d.shape[0]
    step = 2 * COMBINE_TILE
    nt = t // step
    dest_blk = (2, COMBINE_TILE // SUBLANES, SLOTS_PER_TILE)
    const = lambda i: (0, 0)
    resident = functools.partial(pl.BlockSpec, index_map=const, pipeline_mode=pl.Buffered(1))
    in_specs = [
        pl.BlockSpec(dest_blk, lambda i: (i, 0, 0), memory_space=pltpu.SMEM),
        pl.BlockSpec(dest_blk, lambda i: (jnp.minimum(i + 1, nt - 1), 0, 0),
                     memory_space=pltpu.SMEM),
        pl.BlockSpec((step, D), lambda i: (i, 0)),
        pl.BlockSpec((2, 8, COMBINE_TILE), lambda i: (i, 0, 0)),
        pl.BlockSpec(memory_space=pl.ANY),
        pl.BlockSpec((1, step, D_PLE), lambda i: (layer, i, 0)),
        pl.BlockSpec((1, D), const),
        resident((D_PLE, D)),
        resident((D, D)),
        pl.BlockSpec((1, D), const),
    ]
    args = [dest, dest, x2d, gates, y, p3d, g_ple.reshape(1, D), w_proj.astype(BF16),
            w_gate.astype(BF16), g_final.reshape(1, D)]
    out_specs = [pl.BlockSpec((step, D), lambda i: (i, 0))]
    out_shape = [jax.ShapeDtypeStruct((t, D), F32)]
    if qkv is not None:
        seq, g_q, g_kv, w_q, w_kv = qkv
        per_seq = seq // step
        in_specs += [pl.BlockSpec((1, D), const), pl.BlockSpec((1, D), const),
                     resident((D, D)), resident((D, 2 * D))]
        args += [g_q.reshape(1, D), g_kv.reshape(1, D), w_q.astype(BF16), w_kv.astype(BF16)]
        feat_major = pl.BlockSpec((1, D, step), lambda i: (i // per_seq, 0, i % per_seq))
        row_major = pl.BlockSpec((1, step, D), lambda i: (i // per_seq, i % per_seq, 0))
        out_specs += [feat_major, row_major, feat_major]
        out_shape += [jax.ShapeDtypeStruct((t // seq, D, seq), BF16),
                      jax.ShapeDtypeStruct((t // seq, seq, D), BF16),
                      jax.ShapeDtypeStruct((t // seq, D, seq), BF16)]
    outs = pl.pallas_call(
        functools.partial(_combine_kernel, final=final, project=qkv is not None),
        name="combine_final" if final else "combine",
        grid=(nt,),
        in_specs=in_specs,
        out_specs=out_specs,
        out_shape=out_shape,
        scratch_shapes=[
            pltpu.VMEM((TOP_K, COMBINE_TILE // SUBLANES, SUBLANES, D), F32),
            pltpu.VMEM((TOP_K, COMBINE_TILE // SUBLANES, SUBLANES, D), F32),
            pltpu.SemaphoreType.DMA((2,)),
        ],
        compiler_params=_params(1),
    )(*args)
    return outs[0] if qkv is None else tuple(outs)


def _moe_ple(x, p, layer, norm_ffn, w_group, b_group, w_expert, b_expert, w_gate, w_up, w_down,
             norm_ple, ple_w_proj, ple_w_gate, final_norm, final, qkv=None):
    b, s, _ = x.shape
    t = b * s
    x2d = x.reshape(t, D)
    ids, gates, counts = _router(x2d, norm_ffn, w_group, b_group, w_expert, b_expert)

    counts = counts[:, 0]
    padded = (counts + MOE_BLOCK - 1) // MOE_BLOCK * MOE_BLOCK
    pad_end = jnp.cumsum(padded)
    pad_start = pad_end - padded
    n_blocks = t * TOP_K // MOE_BLOCK + N_EXPERTS
    n_pad = n_blocks * MOE_BLOCK
    e = ids[:, 0:TOP_K, :]
    r = ids[:, TOP_K:2 * TOP_K, :]
    experts = jnp.arange(N_EXPERTS, dtype=jnp.int32)
    dest = r + jnp.sum(jnp.where(e[..., None] == experts, pad_start, 0), axis=-1)
    nt = t // ROUTER_TILE
    dest = dest.reshape(nt, TOP_K, ROUTER_TILE // SUBLANES, SUBLANES).transpose(0, 2, 1, 3)
    dest = dest.reshape(nt, ROUTER_TILE // SUBLANES, SLOTS_PER_TILE).astype(jnp.int32)
    block_row = jnp.arange(n_blocks, dtype=jnp.int32) * MOE_BLOCK
    block_e = jnp.minimum(jnp.sum(pad_end[None, :] <= block_row[:, None], axis=-1),
                          N_EXPERTS - 1).astype(jnp.int32)
    n_used = (pad_end[-1:] // MOE_BLOCK).astype(jnp.int32)
    tail = (n_used[0] + jnp.arange(N_EXPERTS, dtype=jnp.int32)) * MOE_BLOCK
    zrow = jnp.concatenate([jnp.where(padded > 0, pad_end - MOE_BLOCK, -1),
                            jnp.where(tail < n_pad, tail, -1)]).astype(jnp.int32)

    xs = _dispatch(x2d, norm_ffn, dest, zrow, n_pad)
    later = jnp.where((experts[None, :] > block_e[:, None]) & (padded[None, :] > 0), experts[None, :],
                      N_EXPERTS)
    next_e = jnp.min(later, axis=-1)
    next_e = jnp.where(next_e < N_EXPERTS, next_e, -1).astype(jnp.int32)
    y = _experts(xs, block_e, next_e, n_used, layer, w_gate, w_up, w_down)
    out = _combine_ple(x2d, dest, gates, y, p.reshape(-1, t, D_PLE), layer,
                       norm_ple, ple_w_proj, ple_w_gate, final_norm, final,
                       None if qkv is None else (s,) + tuple(qkv))
    if qkv is None:
        return out.reshape(b, s, D)
    return (out[0].reshape(b, s, D),) + out[1:]


Q_GROUP = 2 * CHUNK
G_BAND = BAND + CHUNK
PAIR = 2 * B_HEAD_DIM
ONES_ROWS = BF16_SUBLANES


def _attn_kernel(qt_ref, kp_ref, kc_ref, vtp_ref, vtc_ref, bias_ref, x_ref, wo_ref, out_ref, o_scr,
                 s_scr0, s_scr1, p_scr0, p_scr1):
    drow = lax.broadcasted_iota(jnp.int32, (PAIR, Q_GROUP), 0)
    first_head = drow < B_HEAD_DIM
    s_scr = (s_scr0, s_scr1)
    p_scr = (p_scr0, p_scr1)

    def attend(first_tile):
        units = [(g, pr) for g in range(SEQ_TILE // Q_GROUP) for pr in range(B_HEADS // 2)]

        def geometry(g):
            w0 = g * Q_GROUP
            n_prev = SEQ_TILE - w0
            return w0, n_prev, G_BAND - n_prev

        def keys(g):
            return slice(geometry(g)[1], None) if first_tile else slice(None)

        def scores(unit, s_ref):
            g, pr = unit
            w0, n_prev, n_cur = geometry(g)
            feat = slice(pr * PAIR, (pr + 1) * PAIR)
            qt = qt_ref[0, feat, w0:w0 + Q_GROUP]
            zero = jnp.zeros_like(qt)
            qblk = jnp.concatenate([jnp.where(first_head, qt, zero),
                                    jnp.where(first_head, zero, qt)], axis=1)
            if first_tile:
                kb = kc_ref[0, :n_cur, feat]
            else:
                kb = jnp.concatenate([kp_ref[0, w0:, feat], kc_ref[0, :n_cur, feat]], axis=0)
            s_ref[keys(g), :] = (jnp.dot(kb, qblk, preferred_element_type=F32)
                                 + bias_ref[pr, keys(g), :])

        def weights(unit, s_ref, p_ref):
            g, _ = unit
            s = s_ref[keys(g), :]
            m = jnp.max(s, axis=0, keepdims=True)
            p_ref[keys(g), :] = jnp.exp2(s - m).astype(BF16)

        def values(unit, p_ref):
            g, pr = unit
            w0, n_prev, n_cur = geometry(g)
            feat = slice(pr * PAIR, (pr + 1) * PAIR)
            if first_tile:
                vt = vtc_ref[0, feat, :n_cur]
            else:
                vt = jnp.concatenate([vtp_ref[0, feat, w0:], vtc_ref[0, feat, :n_cur]], axis=1)
            ones = jnp.ones((ONES_ROWS, vt.shape[1]), BF16)
            ot = jnp.dot(jnp.concatenate([vt, ones], axis=0), p_ref[keys(g), :],
                         preferred_element_type=F32)
            inv = 1.0 / ot[PAIR:PAIR + 1, :]
            ot = jnp.where(first_head, ot[:PAIR, :Q_GROUP] * inv[:, :Q_GROUP],
                           ot[:PAIR, Q_GROUP:] * inv[:, Q_GROUP:])
            o_scr[w0:w0 + Q_GROUP, feat] = ot.T.astype(BF16)

        n_units = len(units)
        scores(units[0], s_scr[0])
        scores(units[1], s_scr[1])
        weights(units[0], s_scr[0], p_scr[0])
        for n, unit in enumerate(units):
            if n + 2 < n_units:
                scores(units[n + 2], s_scr[n % 2])
            if n + 1 < n_units:
                weights(units[n + 1], s_scr[(n + 1) % 2], p_scr[(n + 1) % 2])
            values(unit, p_scr[n % 2])

    @pl.when(pl.program_id(1) == 0)
    def _():
        attend(True)

    @pl.when(pl.program_id(1) > 0)
    def _():
        attend(False)

    out_ref[0] = x_ref[0] + jnp.dot(o_scr[...], wo_ref[...], preferred_element_type=F32)


def _attn(x, qt, k, vt, bias_t, w_o):
    b, s, _ = x.shape
    cur = lambda bi, si: (bi, si, 0)
    prev = lambda bi, si: (bi, jnp.maximum(si - 1, 0), 0)
    cur_t = lambda bi, si: (bi, 0, si)
    prev_t = lambda bi, si: (bi, 0, jnp.maximum(si - 1, 0))
    blk = (1, SEQ_TILE, D)
    blk_t = (1, D, SEQ_TILE)
    return pl.pallas_call(
        _attn_kernel,
        name="attn",
        grid=(b, s // SEQ_TILE),
        in_specs=[
            pl.BlockSpec(blk_t, cur_t),
            pl.BlockSpec(blk, prev),
            pl.BlockSpec(blk, cur),
            pl.BlockSpec(blk_t, prev_t),
            pl.BlockSpec(blk_t, cur_t),
            pl.BlockSpec((B_HEADS // 2, G_BAND, 2 * Q_GROUP), lambda bi, si: (0, 0, 0),
                         pipeline_mode=pl.Buffered(1)),
            pl.BlockSpec(blk, cur),
            pl.BlockSpec((D, D), lambda bi, si: (0, 0), pipeline_mode=pl.Buffered(1)),
        ],
        out_specs=pl.BlockSpec(blk, cur),
        out_shape=jax.ShapeDtypeStruct(x.shape, F32),
        scratch_shapes=[pltpu.VMEM((SEQ_TILE, D), BF16)]
        + [pltpu.VMEM((G_BAND, 2 * Q_GROUP), F32)] * 2
        + [pltpu.VMEM((G_BAND, 2 * Q_GROUP), BF16)] * 2,
        compiler_params=_params(2),
    )(qt, k, k, vt, vt, bias_t, x, w_o.astype(BF16))


def _group_bias(table):
    band = _band_bias(table) * LOG2E
    pad = lambda lo, hi: jnp.pad(band, ((0, 0), (0, 0), (lo, hi)), constant_values=NEG_INF)
    both = jnp.concatenate([pad(0, CHUNK), pad(CHUNK, 0)], axis=1)
    both = both.reshape(B_HEADS // 2, 2, Q_GROUP, G_BAND)
    return both.transpose(0, 3, 1, 2).reshape(B_HEADS // 2, G_BAND, 2 * Q_GROUP)


def _band_bias(table):
    n_rel = REL_MAX - REL_MIN + 1
    span = BAND + CHUNK - 1
    head = jnp.broadcast_to(table[:, n_rel - 1:], (table.shape[0], span - n_rel))
    ext = jnp.concatenate([head, table[:, ::-1]], axis=1)
    rows = [ext[:, CHUNK - 1 - q:CHUNK - 1 - q + BAND] for q in range(CHUNK)]
    return jnp.stack(rows, axis=1)


def kernel(x, p, a_w_in, a_lb_logits, a_out_norm, a_w_o, kv_norm, w_kv, b_w_q, b_rel_bias, b_w_o,
           norm_mix, norm_ffn, norm_ple, moe_w_group, moe_b_group, moe_w_expert, moe_b_expert,
           moe_w_gate, moe_w_up, moe_w_down, ple_w_proj, ple_w_gate, final_norm):
    b, s, _ = x.shape
    lower_bounds = jnp.cumsum(jax.nn.softmax(a_lb_logits.astype(F32), axis=0), axis=0)

    def moe(xi, i, final, qkv=None):
        return _moe_ple(xi, p, i, norm_ffn[i], moe_w_group[i], moe_b_group[i], moe_w_expert[i],
                        moe_b_expert[i], moe_w_gate, moe_w_up, moe_w_down, norm_ple[i],
                        ple_w_proj[i], ple_w_gate[i], final_norm, final, qkv)

    x = _mixer_a(x, norm_mix[0], a_w_in[0], lower_bounds[0], a_out_norm[0], a_w_o[0])
    x, qt, k, vt = moe(x, 0, False, (norm_mix[1], kv_norm, b_w_q[0], w_kv))

    x = _attn(x, qt, k, vt, _group_bias(b_rel_bias[0].astype(F32)), b_w_o[0])
    x = moe(x, 1, True)
    return x
```

```python
import functools

import jax
import jax.numpy as jnp
from jax import lax
from jax.experimental import pallas as pl
from jax.experimental.pallas import tpu as pltpu

F32 = jnp.float32
BF16 = jnp.bfloat16
U32 = jnp.uint32

D = 1024
CHUNK = 64
A_HEADS = 8
A_HEAD_DIM = 128
B_HEADS = 16
B_HEAD_DIM = 64
LEFT_CHUNKS = 8
BAND = (LEFT_CHUNKS + 1) * CHUNK
REL_MIN = -(CHUNK - 1)
REL_MAX = 256
ATTN_SCALE = B_HEAD_DIM ** -0.5
N_GROUPS = 4
EXPERTS_PER_GROUP = 8
N_EXPERTS = 32
TOP_K = 2
D_EXPERT = 512
MOE_BLOCK = 512
D_PLE = 256
EPS = 1e-6
NEG_INF = -1e30
LOG2E = 1.4426950408889634

SUBLANES = 8
LANES = 128
BF16_SUBLANES = 16
V7X_VMEM_BYTES = 64 * 1024 * 1024

SEQ_TILE = 512
ROUTER_TILE = 512
COMBINE_TILE = 512
ROUTER_ROWS = -(-(N_EXPERTS + N_GROUPS) // BF16_SUBLANES) * BF16_SUBLANES
VMEM_LIMIT = V7X_VMEM_BYTES - 8 * 1024 * 1024


def _params(n_axes, vmem=VMEM_LIMIT):
    return pltpu.CompilerParams(dimension_semantics=("arbitrary",) * n_axes,
                                vmem_limit_bytes=vmem)


def _rms_scale(x):
    return lax.rsqrt(jnp.mean(x * x, axis=-1, keepdims=True) + EPS)


def _sigmoid(x):
    return 1.0 / (1.0 + jnp.exp(-x))


def _mixer_a_kernel(x_ref, g_ref, win_ref, lb_ref, onorm_ref, wo_ref, out_ref,
                    proj_scr, o_scr, state_scr, g_scr0, g_scr1, k_scr0, k_scr1,
                    qd_scr0, qd_scr1, qd_scr2, qd_scr3, kt_scr0, kt_scr1, kt_scr2, kt_scr3,
                    att_scr0, att_scr1, att_scr2, att_scr3):
    @pl.when(pl.program_id(1) == 0)
    def _():
        state_scr[...] = jnp.zeros_like(state_scr)

    x = x_ref[0]
    h = (x * _rms_scale(x) * g_ref[...]).astype(BF16)
    proj_scr[...] = jnp.dot(h, win_ref[...], preferred_element_type=F32)

    row = lax.broadcasted_iota(jnp.int32, (CHUNK, CHUNK), 0)
    col = lax.broadcasted_iota(jnp.int32, (CHUNK, CHUNK), 1)
    causal = row >= col
    tril = causal.astype(BF16)
    lb = lb_ref[...]
    onorm = onorm_ref[...]

    g_scr, k_scr = (g_scr0, g_scr1), (k_scr0, k_scr1)
    qd_scr, kt_scr = (qd_scr0, qd_scr1, qd_scr2, qd_scr3), (kt_scr0, kt_scr1, kt_scr2, kt_scr3)
    att_scr = (att_scr0, att_scr1, att_scr2, att_scr3)
    n_chunks = SEQ_TILE // CHUNK
    units = [(c, hd) for c in range(n_chunks) for hd in range(A_HEADS)]

    def rows_of(c):
        return slice(c * CHUNK, (c + 1) * CHUNK)

    def decay(c):
        f = lb + (1.0 - lb) * _sigmoid(proj_scr[rows_of(c), D:2 * D])
        logf = jnp.log(f)
        hi = logf.astype(BF16)
        lo = (logf - hi.astype(F32)).astype(BF16)
        g_scr[c % 2][...] = (jnp.dot(tril, hi, preferred_element_type=F32)
                             + jnp.dot(tril, lo, preferred_element_type=F32))
        k_scr[c % 2][...] = 1.0 - f

    def intra(n):
        c, hd = units[n]
        sl = slice(hd * A_HEAD_DIM, (hd + 1) * A_HEAD_DIM)
        gh = g_scr[c % 2][:, sl]
        g_last = gh[CHUNK - 1:CHUNK, :]
        k = k_scr[c % 2][:, sl]
        q_dec = (proj_scr[rows_of(c), sl] * jnp.exp(gh)).astype(BF16)
        k_inv = (k * jnp.exp(-gh)).astype(BF16)
        qd_scr[n % 4][...] = q_dec
        kt_scr[n % 4][...] = (k * jnp.exp(g_last - gh)).astype(BF16)
        att = lax.dot_general(q_dec, k_inv, (((1,), (1,)), ((), ())),
                              preferred_element_type=F32)
        att_scr[n % 4][...] = jnp.where(causal, att, 0.0).astype(BF16)

    def output(n):
        c, hd = units[n]
        sl = slice(hd * A_HEAD_DIM, (hd + 1) * A_HEAD_DIM)
        rows = rows_of(c)
        g_last = g_scr[c % 2][CHUNK - 1:CHUNK, sl]
        v = proj_scr[rows, 2 * D + hd * A_HEAD_DIM:2 * D + (hd + 1) * A_HEAD_DIM]
        st = state_scr[hd]
        o = (jnp.dot(att_scr[n % 4][...], v.astype(BF16), preferred_element_type=F32)
             + lax.dot_general(qd_scr[n % 4][...], st.astype(BF16), (((1,), (1,)), ((), ())),
                               preferred_element_type=F32))
        v_t = v.T.astype(BF16)
        state_scr[hd] = st * jnp.exp(g_last) + jnp.dot(v_t, kt_scr[n % 4][...],
                                                       preferred_element_type=F32)
        o = o * _rms_scale(o)
        og = proj_scr[rows, 3 * D + hd * A_HEAD_DIM:3 * D + (hd + 1) * A_HEAD_DIM]
        o = o * onorm[:, sl] * (og * _sigmoid(og))
        o_scr[rows, sl] = o.astype(BF16)

    decay(0)
    intra(0)
    intra(1)
    intra(2)
    for n, (c, hd) in enumerate(units):
        if hd == 0 and c + 1 < n_chunks:
            decay(c + 1)
        if n + 3 < len(units):
            intra(n + 3)
        output(n)
    out_ref[0] = x + jnp.dot(o_scr[...], wo_ref[...], preferred_element_type=F32)


def _mixer_a(x, g, w_in, lb, out_norm, w_o):
    b, s, _ = x.shape
    const = lambda bi, si: (0, 0)
    return pl.pallas_call(
        _mixer_a_kernel,
        name="mixer_a",
        grid=(b, s // SEQ_TILE),
        in_specs=[
            pl.BlockSpec((1, SEQ_TILE, D), lambda bi, si: (bi, si, 0)),
            pl.BlockSpec((1, D), const),
            pl.BlockSpec((D, 4 * D), const, pipeline_mode=pl.Buffered(1)),
            pl.BlockSpec((1, D), const),
            pl.BlockSpec((1, D), const),
            pl.BlockSpec((D, D), const, pipeline_mode=pl.Buffered(1)),
        ],
        out_specs=pl.BlockSpec((1, SEQ_TILE, D), lambda bi, si: (bi, si, 0)),
        out_shape=jax.ShapeDtypeStruct(x.shape, F32),
        scratch_shapes=[
            pltpu.VMEM((SEQ_TILE, 4 * D), F32),
            pltpu.VMEM((SEQ_TILE, D), BF16),
            pltpu.VMEM((A_HEADS, A_HEAD_DIM, A_HEAD_DIM), F32),
        ] + [pltpu.VMEM((CHUNK, D), F32)] * 4
          + [pltpu.VMEM((CHUNK, A_HEAD_DIM), BF16)] * 8
          + [pltpu.VMEM((CHUNK, CHUNK), BF16)] * 4,
        compiler_params=_params(2),
    )(x, g.reshape(1, D), w_in.astype(BF16), lb.reshape(1, D), out_norm.reshape(1, D),
      w_o.astype(BF16))


def _router_kernel(x_ref, g_ref, wr_ref, br_ref, ids_ref, gates_ref, counts_ref, cnt_scr,
                   before_scr):
    tm = ROUTER_TILE

    @pl.when(pl.program_id(0) == 0)
    def _():
        cnt_scr[...] = jnp.zeros_like(cnt_scr)
        tr = lax.broadcasted_iota(jnp.int32, (tm, tm), 0)
        tc = lax.broadcasted_iota(jnp.int32, (tm, tm), 1)
        before_scr[...] = (tr < tc).astype(BF16)

    x = x_ref[...]
    h = x * _rms_scale(x) * g_ref[...]
    h_hi = h.astype(BF16)
    h_lo = (h - h_hi.astype(F32)).astype(BF16)
    nt = (((1,), (1,)), ((), ()))
    both = lax.dot_general(wr_ref[...], h_hi, nt, preferred_element_type=F32)
    cross = lax.dot_general(wr_ref[0:ROUTER_ROWS], h_lo, nt, preferred_element_type=F32)
    logits = both[0:ROUTER_ROWS] + both[ROUTER_ROWS:] + cross + br_ref[...]
    el = logits[0:N_EXPERTS]
    gl = logits[N_EXPERTS:ROUTER_ROWS]
    grow = lax.broadcasted_iota(jnp.int32, gl.shape, 0)
    gl = jnp.where(grow < N_GROUPS, gl, -jnp.inf)
    gmax = jnp.max(gl, axis=0, keepdims=True)
    gsum = jnp.sum(jnp.exp(gl - gmax), axis=0, keepdims=True)
    grp_w = 1.0 / gsum
    gidx = jnp.min(jnp.where(gl == gmax, grow, N_GROUPS), axis=0, keepdims=True)

    erow = lax.broadcasted_iota(jnp.int32, el.shape, 0)
    masked = jnp.where((erow // EXPERTS_PER_GROUP) == gidx, el, -jnp.inf)
    top1 = jnp.max(masked, axis=0, keepdims=True)
    i1 = jnp.min(jnp.where(masked == top1, erow, N_EXPERTS), axis=0, keepdims=True)
    masked2 = jnp.where(erow == i1, -jnp.inf, masked)
    top2 = jnp.max(masked2, axis=0, keepdims=True)
    i2 = jnp.min(jnp.where(masked2 == top2, erow, N_EXPERTS), axis=0, keepdims=True)
    e2 = jnp.exp(top2 - top1)
    denom = 1.0 + e2
    g1 = grp_w * (1.0 / denom)
    g2 = grp_w * (e2 / denom)

    sel1 = erow == i1
    sel2 = erow == i2
    onehot = (sel1 | sel2).astype(BF16)
    prefix = jnp.dot(onehot, before_scr[...], preferred_element_type=F32) + cnt_scr[...]
    r1 = jnp.sum(jnp.where(sel1, prefix, 0.0), axis=0, keepdims=True)
    r2 = jnp.sum(jnp.where(sel2, prefix, 0.0), axis=0, keepdims=True)
    cnt_scr[...] += jnp.sum(onehot.astype(F32), axis=1, keepdims=True)

    zi = jnp.zeros((SUBLANES - 2 * TOP_K, tm), jnp.int32)
    ids_ref[0] = jnp.concatenate(
        [i1, i2, r1.astype(jnp.int32), r2.astype(jnp.int32), zi], axis=0)
    gates_ref[0] = jnp.concatenate([g1, g2, jnp.zeros((SUBLANES - TOP_K, tm), F32)], axis=0)
    counts_ref[...] = jnp.broadcast_to(cnt_scr[...], counts_ref.shape).astype(jnp.int32)


def _router(x2d, g, w_group, b_group, w_expert, b_expert):
    t = x2d.shape[0]
    nt = t // ROUTER_TILE
    pad = ROUTER_ROWS - N_EXPERTS - N_GROUPS
    wr = jnp.concatenate([w_expert.T, w_group.T, jnp.zeros((pad, D), F32)], axis=0)
    wr_hi = wr.astype(BF16)
    wr_lo = (wr - wr_hi.astype(F32)).astype(BF16)
    br = jnp.concatenate([b_expert, b_group, jnp.zeros((pad,), F32)]).reshape(ROUTER_ROWS, 1)
    const = lambda i: (0, 0)
    return pl.pallas_call(
        _router_kernel,
        name="router",
        grid=(nt,),
        in_specs=[
            pl.BlockSpec((ROUTER_TILE, D), lambda i: (i, 0)),
            pl.BlockSpec((1, D), const),
            pl.BlockSpec((2 * ROUTER_ROWS, D), const),
            pl.BlockSpec((ROUTER_ROWS, 1), const),
        ],
        out_specs=[
            pl.BlockSpec((1, SUBLANES, ROUTER_TILE), lambda i: (i, 0, 0)),
            pl.BlockSpec((1, SUBLANES, ROUTER_TILE), lambda i: (i, 0, 0)),
            pl.BlockSpec((N_EXPERTS, LANES), const),
        ],
        out_shape=[
            jax.ShapeDtypeStruct((nt, SUBLANES, ROUTER_TILE), jnp.int32),
            jax.ShapeDtypeStruct((nt, SUBLANES, ROUTER_TILE), F32),
            jax.ShapeDtypeStruct((N_EXPERTS, LANES), jnp.int32),
        ],
        scratch_shapes=[pltpu.VMEM((N_EXPERTS, 1), F32),
                        pltpu.VMEM((ROUTER_TILE, ROUTER_TILE), BF16)],
        compiler_params=_params(1),
    )(x2d, g.reshape(1, D), jnp.concatenate([wr_hi, wr_lo], axis=0), br)


SLOTS_PER_TILE = SUBLANES * TOP_K


def _issue_tile_rows(dest_ref, which, j, make_copy):
    for u in range(SUBLANES):
        for k in range(TOP_K):
            dest = dest_ref[which, j, k * SUBLANES + u]
            make_copy(j, u, k, dest).start(priority=(u * TOP_K + k) % 2)


def _issue_rows(n_rows, dest_ref, which, make_copy):
    def body(j, c):
        _issue_tile_rows(dest_ref, which, j, make_copy)
        return c

    lax.fori_loop(0, n_rows // SUBLANES, body, 0)


def _issue_rows_inline(n_rows, dest_ref, which, make_copy):
    for j in range(n_rows // SUBLANES):
        _issue_tile_rows(dest_ref, which, j, make_copy)


def _pack_bf16_pairs(x):
    half = x.shape[1] // 2
    lo = lax.bitcast_convert_type(x[:, :half].astype(BF16).astype(F32), U32)
    hi = lax.bitcast_convert_type(x[:, half:].astype(BF16).astype(F32), U32)
    return (hi & jnp.uint32(0xFFFF0000)) | (lo >> 16)


def _unpack_bf16_pairs(u):
    lo = lax.bitcast_convert_type(u << 16, F32).astype(BF16)
    hi = lax.bitcast_convert_type(u & jnp.uint32(0xFFFF0000), F32).astype(BF16)
    return lo, hi


def _tile_rows(x):
    return x.reshape(x.shape[0] // SUBLANES, SUBLANES, x.shape[1])


def _wait_rows(buf_ref, sem):
    pltpu.make_async_copy(buf_ref, buf_ref, sem).wait()


def _dispatch_kernel(zrow_ref, dest_ref, x_ref, g_ref, xs_ref, hbuf, zbuf, zsem, sem):
    @pl.when(pl.program_id(0) == 0)
    def _():
        zbuf[...] = jnp.zeros_like(zbuf)

        def zcopy(e):
            return pltpu.make_async_copy(zbuf, xs_ref.at[pl.ds(zrow_ref[e], MOE_BLOCK)], zsem)

        def zstart(e, c):
            @pl.when(zrow_ref[e] >= 0)
            def _():
                zcopy(e).start()
            return c

        def zwait(e, c):
            @pl.when(zrow_ref[e] >= 0)
            def _():
                zcopy(e).wait()
            return c

        lax.fori_loop(0, 2 * N_EXPERTS, zstart, 0)
        lax.fori_loop(0, 2 * N_EXPERTS, zwait, 0)

    i = pl.program_id(0)
    slot = i % 2
    x = x_ref[...]
    hbuf[slot] = _tile_rows(_pack_bf16_pairs(x * _rms_scale(x) * g_ref[...]))

    def row_copy(tile, sub, k, dest):
        return pltpu.make_async_copy(hbuf.at[slot, tile, pl.ds(sub, 1), :], xs_ref.at[dest],
                                     sem.at[slot])

    _issue_rows(ROUTER_TILE, dest_ref, 0, row_copy)

    def drain(which):
        for _ in range(TOP_K):
            _wait_rows(hbuf.at[which], sem.at[which])

    @pl.when(i > 0)
    def _():
        drain(1 - slot)

    @pl.when(i == pl.num_programs(0) - 1)
    def _():
        drain(slot)


def _dispatch(x2d, g, dest, zrow, n_pad):
    t = x2d.shape[0]
    nt = t // ROUTER_TILE
    return pl.pallas_call(
        _dispatch_kernel,
        name="dispatch",
        grid_spec=pltpu.PrefetchScalarGridSpec(
            num_scalar_prefetch=1,
            grid=(nt,),
            in_specs=[
                pl.BlockSpec((1, ROUTER_TILE // SUBLANES, SLOTS_PER_TILE),
                             lambda i, z: (i, 0, 0), memory_space=pltpu.SMEM),
                pl.BlockSpec((ROUTER_TILE, D), lambda i, z: (i, 0)),
                pl.BlockSpec((1, D), lambda i, z: (0, 0)),
            ],
            out_specs=pl.BlockSpec(memory_space=pl.ANY),
            scratch_shapes=[
                pltpu.VMEM((2, ROUTER_TILE // SUBLANES, SUBLANES, D // 2), U32),
                pltpu.VMEM((MOE_BLOCK, 1, D // 2), U32),
                pltpu.SemaphoreType.DMA(()),
                pltpu.SemaphoreType.DMA((2,)),
            ],
        ),
        out_shape=jax.ShapeDtypeStruct((n_pad, 1, D // 2), U32),
        compiler_params=_params(1),
    )(zrow, dest, x2d, g.reshape(1, D))


def _experts_kernel(be_ref, nxt_ref, nu_ref, xs_ref, wg_ref, wu_ref, wd_ref, y_ref,
                    wg_f, wu_f, wd_f, wg_b, wu_b, wd_b, xbuf, ybuf, w_sem, in_sem, out_sem, *, layer):
    i = pl.program_id(0)
    n_used = nu_ref[0]
    slot = i % 2

    def block_rows(ref, blk):
        return ref.at[pl.ds(pl.multiple_of(blk * MOE_BLOCK, MOE_BLOCK), MOE_BLOCK), 0]

    def fetch(blk, into):
        return pltpu.make_async_copy(block_rows(xs_ref, blk), xbuf.at[into], in_sem.at[into])

    def write_back(blk, from_):
        return pltpu.make_async_copy(ybuf.at[from_], block_rows(y_ref, blk), out_sem.at[from_])

    def weight_copies(e):
        return [pltpu.make_async_copy(src.at[layer, e], dst, w_sem.at[n])
                for n, (src, dst) in enumerate(((wg_ref, wg_f), (wu_ref, wu_f), (wd_ref, wd_f)))]

    @pl.when(i == 0)
    def _():
        for c in weight_copies(be_ref[0]):
            c.start()
        fetch(0, 0).start()

    @pl.when(i + 1 < n_used)
    def _():
        fetch(i + 1, 1 - slot).start()

    prev = be_ref[jnp.maximum(i - 1, 0)]

    @pl.when((i < n_used) & ((i == 0) | (be_ref[i] != prev)))
    def _():
        for c in weight_copies(be_ref[i]):
            c.wait()
        wg_b[...] = wg_f[...].astype(BF16)
        wu_b[...] = wu_f[...].astype(BF16)
        wd_b[...] = wd_f[...].astype(BF16)

        @pl.when(nxt_ref[i] >= 0)
        def _():
            for c in weight_copies(nxt_ref[i]):
                c.start()

    @pl.when(i >= 2)
    def _():
        write_back(i - 2, slot).wait()

    @pl.when(i < n_used)
    def _():
        fetch(i, slot).wait()
        h_lo, h_hi = _unpack_bf16_pairs(xbuf[slot])
        half = D // 2
        a = (jnp.dot(h_lo, wg_b[:half], preferred_element_type=F32)
             + jnp.dot(h_hi, wg_b[half:], preferred_element_type=F32))
        u = (jnp.dot(h_lo, wu_b[:half], preferred_element_type=F32)
             + jnp.dot(h_hi, wu_b[half:], preferred_element_type=F32))
        hid = (a * _sigmoid(a) * u).astype(BF16)
        ybuf[slot] = jnp.dot(hid, wd_b[...], preferred_element_type=F32)

    @pl.when(i >= n_used)
    def _():
        ybuf[slot] = jnp.zeros((MOE_BLOCK, D), F32)

    write_back(i, slot).start()

    @pl.when(i == pl.num_programs(0) - 1)
    def _():
        write_back(i - 1, 1 - slot).wait()
        write_back(i, slot).wait()


def _experts(xs, block_e, next_e, n_used, layer, w_gate, w_up, w_down):
    n_pad = xs.shape[0]
    n_blocks = n_pad // MOE_BLOCK
    hbm = pl.BlockSpec(memory_space=pl.ANY)
    return pl.pallas_call(
        functools.partial(_experts_kernel, layer=layer),
        name="experts",
        grid_spec=pltpu.PrefetchScalarGridSpec(
            num_scalar_prefetch=3,
            grid=(n_blocks,),
            in_specs=[hbm, hbm, hbm, hbm],
            out_specs=hbm,
            scratch_shapes=[
                pltpu.VMEM((D, D_EXPERT), F32),
                pltpu.VMEM((D, D_EXPERT), F32),
                pltpu.VMEM((D_EXPERT, D), F32),
                pltpu.VMEM((D, D_EXPERT), BF16),
                pltpu.VMEM((D, D_EXPERT), BF16),
                pltpu.VMEM((D_EXPERT, D), BF16),
                pltpu.VMEM((2, MOE_BLOCK, D // 2), U32),
                pltpu.VMEM((2, MOE_BLOCK, D), F32),
                pltpu.SemaphoreType.DMA((3,)),
                pltpu.SemaphoreType.DMA((2,)),
                pltpu.SemaphoreType.DMA((2,)),
            ],
        ),
        out_shape=jax.ShapeDtypeStruct((n_pad, 1, D), F32),
        compiler_params=_params(1),
    )(block_e, next_e, n_used, xs, w_gate, w_up, w_down)


def _combine_kernel(*refs, final, project):
    (dest_ref, dest_next_ref, x_ref, gate_ref, y_ref, p_ref, gple_ref, wp_ref, wg_ref,
     gfin_ref) = refs[:10]
    if project:
        gq_ref, gkv_ref, wq_ref, wkv_ref, out_ref, qt_ref, k_ref, vt_ref = refs[10:18]
        ybuf0, ybuf1, sem = refs[18:]
    else:
        out_ref, ybuf0, ybuf1, sem = refs[10:]
    i = pl.program_id(0)
    ybuf = (ybuf0, ybuf1)

    def row_copy(into):
        def make(tile, sub, k, dest):
            return pltpu.make_async_copy(y_ref.at[dest], ybuf[into].at[k, tile, pl.ds(sub, 1), :],
                                         sem.at[into])
        return make

    def combine(half, slot):
        rows = slice(half * COMBINE_TILE, (half + 1) * COMBINE_TILE)
        gates = gate_ref[half].T
        y0 = ybuf[slot][0].reshape(COMBINE_TILE, D)
        y1 = ybuf[slot][1].reshape(COMBINE_TILE, D)
        x = x_ref[rows, :] + gates[:, 0:1] * y0 + gates[:, 1:2] * y1
        h = (x * _rms_scale(x) * gple_ref[...]).astype(BF16)
        gate = _sigmoid(jnp.dot(h, wg_ref[...], preferred_element_type=F32))
        proj = jnp.dot(p_ref[0, rows, :].astype(BF16), wp_ref[...], preferred_element_type=F32)
        x = x + proj * gate
        if final:
            x = x * _rms_scale(x) * gfin_ref[...]
        out_ref[rows, :] = x
        if project:
            q, k, v = _qkv_rows(x, gq_ref, gkv_ref, wq_ref, wkv_ref)
            qt_ref[0, :, rows] = q.T.astype(BF16)
            k_ref[0, rows, :] = k.astype(BF16)
            vt_ref[0, :, rows] = v.T.astype(BF16)

    def wait(slot):
        for k in range(TOP_K):
            _wait_rows(ybuf[slot].at[k], sem.at[slot])

    @pl.when(i == 0)
    def _():
        _issue_rows(COMBINE_TILE, dest_ref, 0, row_copy(0))

    wait(0)
    _issue_rows_inline(COMBINE_TILE, dest_ref, 1, row_copy(1))
    combine(0, 0)
    wait(1)
    _issue_rows_inline(COMBINE_TILE, dest_next_ref, 0, row_copy(0))
    combine(1, 1)

    @pl.when(i == pl.num_programs(0) - 1)
    def _():
        wait(0)


def _qkv_rows(x, gq_ref, gkv_ref, wq_ref, wkv_ref):
    xn = x * _rms_scale(x)
    hq = (xn * gq_ref[...]).astype(BF16)
    hkv = (xn * gkv_ref[...]).astype(BF16)
    q = jnp.dot(hq, wq_ref[...], preferred_element_type=F32) * (ATTN_SCALE * LOG2E)
    kv = jnp.dot(hkv, wkv_ref[...], preferred_element_type=F32)
    return q, kv[:, :D], kv[:, D:]


def _combine_ple(x2d, dest, gates, y, p3d, layer, g_ple, w_proj, w_gate, g_final, final, qkv=None):
    assert COMBINE_TILE == ROUTER_TILE
    t = x2d.shape[0]
    step = 2 * COMBINE_TILE
    nt = t // step
    dest_blk = (2, COMBINE_TILE // SUBLANES, SLOTS_PER_TILE)
    const = lambda i: (0, 0)
    resident = functools.partial(pl.BlockSpec, index_map=const, pipeline_mode=pl.Buffered(1))
    in_specs = [
        pl.BlockSpec(dest_blk, lambda i: (i, 0, 0), memory_space=pltpu.SMEM),
        pl.BlockSpec(dest_blk, lambda i: (jnp.minimum(i + 1, nt - 1), 0, 0),
                     memory_space=pltpu.SMEM),
        pl.BlockSpec((step, D), lambda i: (i, 0)),
        pl.BlockSpec((2, SUBLANES, COMBINE_TILE), lambda i: (i, 0, 0)),
        pl.BlockSpec(memory_space=pl.ANY),
        pl.BlockSpec((1, step, D_PLE), lambda i: (layer, i, 0)),
        pl.BlockSpec((1, D), const),
        resident((D_PLE, D)),
        resident((D, D)),
        pl.BlockSpec((1, D), const),
    ]
    args = [dest, dest, x2d, gates, y, p3d, g_ple.reshape(1, D), w_proj.astype(BF16),
            w_gate.astype(BF16), g_final.reshape(1, D)]
    out_specs = [pl.BlockSpec((step, D), lambda i: (i, 0))]
    out_shape = [jax.ShapeDtypeStruct((t, D), F32)]
    if qkv is not None:
        seq, g_q, g_kv, w_q, w_kv = qkv
        per_seq = seq // step
        in_specs += [pl.BlockSpec((1, D), const), pl.BlockSpec((1, D), const),
                     resident((D, D)), resident((D, 2 * D))]
        args += [g_q.reshape(1, D), g_kv.reshape(1, D), w_q.astype(BF16), w_kv.astype(BF16)]
        feat_major = pl.BlockSpec((1, D, step), lambda i: (i // per_seq, 0, i % per_seq))
        row_major = pl.BlockSpec((1, step, D), lambda i: (i // per_seq, i % per_seq, 0))
        out_specs += [feat_major, row_major, feat_major]
        out_shape += [jax.ShapeDtypeStruct((t // seq, D, seq), BF16),
                      jax.ShapeDtypeStruct((t // seq, seq, D), BF16),
                      jax.ShapeDtypeStruct((t // seq, D, seq), BF16)]
    outs = pl.pallas_call(
        functools.partial(_combine_kernel, final=final, project=qkv is not None),
        name="combine_final" if final else "combine",
        grid=(nt,),
        in_specs=in_specs,
        out_specs=out_specs,
        out_shape=out_shape,
        scratch_shapes=[
            pltpu.VMEM((TOP_K, COMBINE_TILE // SUBLANES, SUBLANES, D), F32),
            pltpu.VMEM((TOP_K, COMBINE_TILE // SUBLANES, SUBLANES, D), F32),
            pltpu.SemaphoreType.DMA((2,)),
        ],
        compiler_params=_params(1),
    )(*args)
    return outs[0] if qkv is None else tuple(outs)


def _moe_ple(x, p, layer, norm_ffn, w_group, b_group, w_expert, b_expert, w_gate, w_up, w_down,
             norm_ple, ple_w_proj, ple_w_gate, final_norm, final, qkv=None):
    b, s, _ = x.shape
    t = b * s
    x2d = x.reshape(t, D)
    ids, gates, counts = _router(x2d, norm_ffn, w_group, b_group, w_expert, b_expert)

    counts = counts[:, 0]
    padded = (counts + MOE_BLOCK - 1) // MOE_BLOCK * MOE_BLOCK
    pad_end = jnp.cumsum(padded)
    pad_start = pad_end - padded
    n_blocks = t * TOP_K // MOE_BLOCK + N_EXPERTS
    n_pad = n_blocks * MOE_BLOCK
    e = ids[:, 0:TOP_K, :]
    r = ids[:, TOP_K:2 * TOP_K, :]
    experts = jnp.arange(N_EXPERTS, dtype=jnp.int32)
    dest = r + jnp.sum(jnp.where(e[..., None] == experts, pad_start, 0), axis=-1)
    nt = t // ROUTER_TILE
    dest = dest.reshape(nt, TOP_K, ROUTER_TILE // SUBLANES, SUBLANES).transpose(0, 2, 1, 3)
    dest = dest.reshape(nt, ROUTER_TILE // SUBLANES, SLOTS_PER_TILE).astype(jnp.int32)
    block_row = jnp.arange(n_blocks, dtype=jnp.int32) * MOE_BLOCK
    block_e = jnp.minimum(jnp.sum(pad_end[None, :] <= block_row[:, None], axis=-1),
                          N_EXPERTS - 1).astype(jnp.int32)
    n_used = (pad_end[-1:] // MOE_BLOCK).astype(jnp.int32)
    tail = (n_used[0] + jnp.arange(N_EXPERTS, dtype=jnp.int32)) * MOE_BLOCK
    zrow = jnp.concatenate([jnp.where(padded > 0, pad_end - MOE_BLOCK, -1),
                            jnp.where(tail < n_pad, tail, -1)]).astype(jnp.int32)

    xs = _dispatch(x2d, norm_ffn, dest, zrow, n_pad)
    later = jnp.where((experts[None, :] > block_e[:, None]) & (padded[None, :] > 0), experts[None, :],
                      N_EXPERTS)
    next_e = jnp.min(later, axis=-1)
    next_e = jnp.where(next_e < N_EXPERTS, next_e, -1).astype(jnp.int32)
    y = _experts(xs, block_e, next_e, n_used, layer, w_gate, w_up, w_down)
    out = _combine_ple(x2d, dest, gates, y, p.reshape(-1, t, D_PLE), layer,
                       norm_ple, ple_w_proj, ple_w_gate, final_norm, final,
                       None if qkv is None else (s,) + tuple(qkv))
    if qkv is None:
        return out.reshape(b, s, D)
    return (out[0].reshape(b, s, D),) + out[1:]


Q_GROUP = 2 * CHUNK
G_BAND = BAND + CHUNK
PAIR = 2 * B_HEAD_DIM
ONES_ROWS = BF16_SUBLANES


def _attn_kernel(qt_ref, kp_ref, kc_ref, vtp_ref, vtc_ref, bias_ref, x_ref, wo_ref, out_ref, o_scr,
                 s_scr0, s_scr1, s_scr2, p_scr0, p_scr1, p_scr2):
    drow = lax.broadcasted_iota(jnp.int32, (PAIR, Q_GROUP), 0)
    first_head = drow < B_HEAD_DIM
    s_scr = (s_scr0, s_scr1, s_scr2)
    p_scr = (p_scr0, p_scr1, p_scr2)

    def attend(first_tile):
        units = [(g, pr) for g in range(SEQ_TILE // Q_GROUP) for pr in range(B_HEADS // 2)]

        def geometry(g):
            w0 = g * Q_GROUP
            n_prev = SEQ_TILE - w0
            return w0, n_prev, G_BAND - n_prev

        def keys(g):
            return slice(geometry(g)[1], None) if first_tile else slice(None)

        def scores(unit, s_ref):
            g, pr = unit
            w0, n_prev, n_cur = geometry(g)
            feat = slice(pr * PAIR, (pr + 1) * PAIR)
            qt = qt_ref[0, feat, w0:w0 + Q_GROUP]
            zero = jnp.zeros_like(qt)
            qblk = jnp.concatenate([jnp.where(first_head, qt, zero),
                                    jnp.where(first_head, zero, qt)], axis=1)
            if first_tile:
                kb = kc_ref[0, :n_cur, feat]
            else:
                kb = jnp.concatenate([kp_ref[0, w0:, feat], kc_ref[0, :n_cur, feat]], axis=0)
            s_ref[keys(g), :] = (jnp.dot(kb, qblk, preferred_element_type=F32)
                                 + bias_ref[pr, keys(g), :])

        def weights(unit, s_ref, p_ref):
            g, _ = unit
            s = s_ref[keys(g), :]
            m = jnp.max(s, axis=0, keepdims=True)
            p_ref[keys(g), :] = jnp.exp2(s - m).astype(BF16)

        def values(unit, p_ref):
            g, pr = unit
            w0, n_prev, n_cur = geometry(g)
            feat = slice(pr * PAIR, (pr + 1) * PAIR)
            if first_tile:
                vt = vtc_ref[0, feat, :n_cur]
            else:
                vt = jnp.concatenate([vtp_ref[0, feat, w0:], vtc_ref[0, feat, :n_cur]], axis=1)
            ones = jnp.ones((ONES_ROWS, vt.shape[1]), BF16)
            ot = jnp.dot(jnp.concatenate([vt, ones], axis=0), p_ref[keys(g), :],
                         preferred_element_type=F32)
            inv = 1.0 / ot[PAIR:PAIR + 1, :]
            ot = jnp.where(first_head, ot[:PAIR, :Q_GROUP] * inv[:, :Q_GROUP],
                           ot[:PAIR, Q_GROUP:] * inv[:, Q_GROUP:])
            o_scr[w0:w0 + Q_GROUP, feat] = ot.T.astype(BF16)

        n_units = len(units)
        scores(units[0], s_scr[0])
        scores(units[1], s_scr[1])
        scores(units[2], s_scr[2])
        weights(units[0], s_scr[0], p_scr[0])
        weights(units[1], s_scr[1], p_scr[1])
        for n, unit in enumerate(units):
            if n + 3 < n_units:
                scores(units[n + 3], s_scr[n % 3])
            if n + 2 < n_units:
                weights(units[n + 2], s_scr[(n + 2) % 3], p_scr[(n + 2) % 3])
            values(unit, p_scr[n % 3])

    @pl.when(pl.program_id(1) == 0)
    def _():
        attend(True)

    @pl.when(pl.program_id(1) > 0)
    def _():
        attend(False)

    out_ref[0] = x_ref[0] + jnp.dot(o_scr[...], wo_ref[...], preferred_element_type=F32)


def _attn(x, qt, k, vt, bias_t, w_o):
    b, s, _ = x.shape
    cur = lambda bi, si: (bi, si, 0)
    prev = lambda bi, si: (bi, jnp.maximum(si - 1, 0), 0)
    cur_t = lambda bi, si: (bi, 0, si)
    prev_t = lambda bi, si: (bi, 0, jnp.maximum(si - 1, 0))
    blk = (1, SEQ_TILE, D)
    blk_t = (1, D, SEQ_TILE)
    return pl.pallas_call(
        _attn_kernel,
        name="attn",
        grid=(b, s // SEQ_TILE),
        in_specs=[
            pl.BlockSpec(blk_t, cur_t),
            pl.BlockSpec(blk, prev),
            pl.BlockSpec(blk, cur),
            pl.BlockSpec(blk_t, prev_t),
            pl.BlockSpec(blk_t, cur_t),
            pl.BlockSpec((B_HEADS // 2, G_BAND, 2 * Q_GROUP), lambda bi, si: (0, 0, 0),
                         pipeline_mode=pl.Buffered(1)),
            pl.BlockSpec(blk, cur),
            pl.BlockSpec((D, D), lambda bi, si: (0, 0), pipeline_mode=pl.Buffered(1)),
        ],
        out_specs=pl.BlockSpec(blk, cur),
        out_shape=jax.ShapeDtypeStruct(x.shape, F32),
        scratch_shapes=[pltpu.VMEM((SEQ_TILE, D), BF16)]
        + [pltpu.VMEM((G_BAND, 2 * Q_GROUP), F32)] * 3
        + [pltpu.VMEM((G_BAND, 2 * Q_GROUP), BF16)] * 3,
        compiler_params=_params(2),
    )(qt, k, k, vt, vt, bias_t, x, w_o.astype(BF16))


def _group_bias(table):
    band = _band_bias(table) * LOG2E
    pad = lambda lo, hi: jnp.pad(band, ((0, 0), (0, 0), (lo, hi)), constant_values=NEG_INF)
    both = jnp.concatenate([pad(0, CHUNK), pad(CHUNK, 0)], axis=1)
    both = both.reshape(B_HEADS // 2, 2, Q_GROUP, G_BAND)
    return both.transpose(0, 3, 1, 2).reshape(B_HEADS // 2, G_BAND, 2 * Q_GROUP)


def _band_bias(table):
    n_rel = REL_MAX - REL_MIN + 1
    span = BAND + CHUNK - 1
    head = jnp.broadcast_to(table[:, n_rel - 1:], (table.shape[0], span - n_rel))
    ext = jnp.concatenate([head, table[:, ::-1]], axis=1)
    rows = [ext[:, CHUNK - 1 - q:CHUNK - 1 - q + BAND] for q in range(CHUNK)]
    return jnp.stack(rows, axis=1)


def kernel(x, p, a_w_in, a_lb_logits, a_out_norm, a_w_o, kv_norm, w_kv, b_w_q, b_rel_bias, b_w_o,
           norm_mix, norm_ffn, norm_ple, moe_w_group, moe_b_group, moe_w_expert, moe_b_expert,
           moe_w_gate, moe_w_up, moe_w_down, ple_w_proj, ple_w_gate, final_norm):
    b, s, _ = x.shape
    lower_bounds = jnp.cumsum(jax.nn.softmax(a_lb_logits.astype(F32), axis=0), axis=0)

    def moe(xi, i, final, qkv=None):
        return _moe_ple(xi, p, i, norm_ffn[i], moe_w_group[i], moe_b_group[i], moe_w_expert[i],
                        moe_b_expert[i], moe_w_gate, moe_w_up, moe_w_down, norm_ple[i],
                        ple_w_proj[i], ple_w_gate[i], final_norm, final, qkv)

    x = _mixer_a(x, norm_mix[0], a_w_in[0], lower_bounds[0], a_out_norm[0], a_w_o[0])
    x, qt, k, vt = moe(x, 0, False, (norm_mix[1], kv_norm, b_w_q[0], w_kv))

    x = _attn(x, qt, k, vt, _group_bias(b_rel_bias[0].astype(F32)), b_w_o[0])
    x = moe(x, 1, True)
    return x
```

```python
import functools

import jax
import jax.numpy as jnp
from jax import lax
from jax.experimental import pallas as pl
from jax.experimental.pallas import tpu as pltpu

F32 = jnp.float32
BF16 = jnp.bfloat16
U32 = jnp.uint32

D = 1024
CHUNK = 64
A_HEADS = 8
A_HEAD_DIM = 128
B_HEADS = 16
B_HEAD_DIM = 64
LEFT_CHUNKS = 8
BAND = (LEFT_CHUNKS + 1) * CHUNK
REL_MIN = -(CHUNK - 1)
REL_MAX = 256
ATTN_SCALE = B_HEAD_DIM ** -0.5
N_GROUPS = 4
EXPERTS_PER_GROUP = 8
N_EXPERTS = 32
TOP_K = 2
D_EXPERT = 512
MOE_BLOCK = 512
D_PLE = 256
EPS = 1e-6
NEG_INF = -1e30
LOG2E = 1.4426950408889634

SUBLANES = 8
LANES = 128
BF16_SUBLANES = 16
V7X_VMEM_BYTES = 64 * 1024 * 1024

SEQ_TILE = 512
ROUTER_TILE = 512
COMBINE_TILE = 512
ROUTER_ROWS = -(-(N_EXPERTS + N_GROUPS) // BF16_SUBLANES) * BF16_SUBLANES
VMEM_LIMIT = V7X_VMEM_BYTES - 8 * 1024 * 1024


def _params(n_axes, vmem=VMEM_LIMIT):
    return pltpu.CompilerParams(dimension_semantics=("arbitrary",) * n_axes,
                                vmem_limit_bytes=vmem)


def _rms_scale(x):
    return lax.rsqrt(jnp.mean(x * x, axis=-1, keepdims=True) + EPS)


def _sigmoid(x):
    return 1.0 / (1.0 + jnp.exp(-x))


def _mixer_a_kernel(x_ref, g_ref, win_ref, lb_ref, onorm_ref, wo_ref, out_ref,
                    proj_scr, o_scr, state_scr, g_scr0, g_scr1, k_scr0, k_scr1,
                    qd_scr0, qd_scr1, qd_scr2, qd_scr3, kt_scr0, kt_scr1, kt_scr2, kt_scr3,
                    att_scr0, att_scr1, att_scr2, att_scr3):
    @pl.when(pl.program_id(1) == 0)
    def _():
        state_scr[...] = jnp.zeros_like(state_scr)

    x = x_ref[0]
    h = (x * _rms_scale(x) * g_ref[...]).astype(BF16)
    proj_scr[...] = jnp.dot(h, win_ref[...], preferred_element_type=F32)

    row = lax.broadcasted_iota(jnp.int32, (CHUNK, CHUNK), 0)
    col = lax.broadcasted_iota(jnp.int32, (CHUNK, CHUNK), 1)
    causal = row >= col
    tril = causal.astype(BF16)
    lb = lb_ref[...]
    onorm = onorm_ref[...]

    g_scr, k_scr = (g_scr0, g_scr1), (k_scr0, k_scr1)
    qd_scr, kt_scr = (qd_scr0, qd_scr1, qd_scr2, qd_scr3), (kt_scr0, kt_scr1, kt_scr2, kt_scr3)
    att_scr = (att_scr0, att_scr1, att_scr2, att_scr3)
    n_chunks = SEQ_TILE // CHUNK
    units = [(c, hd) for c in range(n_chunks) for hd in range(A_HEADS)]

    def rows_of(c):
        return slice(c * CHUNK, (c + 1) * CHUNK)

    def decay(c):
        f = lb + (1.0 - lb) * _sigmoid(proj_scr[rows_of(c), D:2 * D])
        logf = jnp.log(f)
        hi = logf.astype(BF16)
        lo = (logf - hi.astype(F32)).astype(BF16)
        g_scr[c % 2][...] = (jnp.dot(tril, hi, preferred_element_type=F32)
                             + jnp.dot(tril, lo, preferred_element_type=F32))
        k_scr[c % 2][...] = 1.0 - f

    def intra(n):
        c, hd = units[n]
        sl = slice(hd * A_HEAD_DIM, (hd + 1) * A_HEAD_DIM)
        gh = g_scr[c % 2][:, sl]
        g_last = gh[CHUNK - 1:CHUNK, :]
        k = k_scr[c % 2][:, sl]
        q_dec = (proj_scr[rows_of(c), sl] * jnp.exp(gh)).astype(BF16)
        k_inv = (k * jnp.exp(-gh)).astype(BF16)
        qd_scr[n % 4][...] = q_dec
        kt_scr[n % 4][...] = (k * jnp.exp(g_last - gh)).astype(BF16)
        att = lax.dot_general(q_dec, k_inv, (((1,), (1,)), ((), ())),
                              preferred_element_type=F32)
        att_scr[n % 4][...] = jnp.where(causal, att, 0.0).astype(BF16)

    def output(n):
        c, hd = units[n]
        sl = slice(hd * A_HEAD_DIM, (hd + 1) * A_HEAD_DIM)
        rows = rows_of(c)
        g_last = g_scr[c % 2][CHUNK - 1:CHUNK, sl]
        v = proj_scr[rows, 2 * D + hd * A_HEAD_DIM:2 * D + (hd + 1) * A_HEAD_DIM]
        st = state_scr[hd]
        o = (jnp.dot(att_scr[n % 4][...], v.astype(BF16), preferred_element_type=F32)
             + lax.dot_general(qd_scr[n % 4][...], st.astype(BF16), (((1,), (1,)), ((), ())),
                               preferred_element_type=F32))
        v_t = v.T.astype(BF16)
        state_scr[hd] = st * jnp.exp(g_last) + jnp.dot(v_t, kt_scr[n % 4][...],
                                                       preferred_element_type=F32)
        o = o * _rms_scale(o)
        og = proj_scr[rows, 3 * D + hd * A_HEAD_DIM:3 * D + (hd + 1) * A_HEAD_DIM]
        o = o * onorm[:, sl] * (og * _sigmoid(og))
        o_scr[rows, sl] = o.astype(BF16)

    decay(0)
    intra(0)
    intra(1)
    intra(2)
    for n, (c, hd) in enumerate(units):
        if hd == 0 and c + 1 < n_chunks:
            decay(c + 1)
        if n + 3 < len(units):
            intra(n + 3)
        output(n)
    out_ref[0] = x + jnp.dot(o_scr[...], wo_ref[...], preferred_element_type=F32)


def _mixer_a(x, g, w_in, lb, out_norm, w_o):
    b, s, _ = x.shape
    const = lambda bi, si: (0, 0)
    return pl.pallas_call(
        _mixer_a_kernel,
        name="mixer_a",
        grid=(b, s // SEQ_TILE),
        in_specs=[
            pl.BlockSpec((1, SEQ_TILE, D), lambda bi, si: (bi, si, 0)),
            pl.BlockSpec((1, D), const),
            pl.BlockSpec((D, 4 * D), const, pipeline_mode=pl.Buffered(1)),
            pl.BlockSpec((1, D), const),
            pl.BlockSpec((1, D), const),
            pl.BlockSpec((D, D), const, pipeline_mode=pl.Buffered(1)),
        ],
        out_specs=pl.BlockSpec((1, SEQ_TILE, D), lambda bi, si: (bi, si, 0)),
        out_shape=jax.ShapeDtypeStruct(x.shape, F32),
        scratch_shapes=[
            pltpu.VMEM((SEQ_TILE, 4 * D), F32),
            pltpu.VMEM((SEQ_TILE, D), BF16),
            pltpu.VMEM((A_HEADS, A_HEAD_DIM, A_HEAD_DIM), F32),
        ] + [pltpu.VMEM((CHUNK, D), F32)] * 4
          + [pltpu.VMEM((CHUNK, A_HEAD_DIM), BF16)] * 8
          + [pltpu.VMEM((CHUNK, CHUNK), BF16)] * 4,
        compiler_params=_params(2),
    )(x, g.reshape(1, D), w_in.astype(BF16), lb.reshape(1, D), out_norm.reshape(1, D),
      w_o.astype(BF16))


def _router_kernel(x_ref, g_ref, wr_ref, br_ref, ids_ref, gates_ref, counts_ref, cnt_scr,
                   before_scr):
    tm = ROUTER_TILE

    @pl.when(pl.program_id(0) == 0)
    def _():
        cnt_scr[...] = jnp.zeros_like(cnt_scr)
        tr = lax.broadcasted_iota(jnp.int32, (tm, tm), 0)
        tc = lax.broadcasted_iota(jnp.int32, (tm, tm), 1)
        before_scr[...] = (tr < tc).astype(BF16)

    x = x_ref[...]
    h = x * _rms_scale(x) * g_ref[...]
    h_hi = h.astype(BF16)
    h_lo = (h - h_hi.astype(F32)).astype(BF16)
    nt = (((1,), (1,)), ((), ()))
    both = lax.dot_general(wr_ref[...], h_hi, nt, preferred_element_type=F32)
    cross = lax.dot_general(wr_ref[0:ROUTER_ROWS], h_lo, nt, preferred_element_type=F32)
    logits = both[0:ROUTER_ROWS] + both[ROUTER_ROWS:] + cross + br_ref[...]
    el = logits[0:N_EXPERTS]
    gl = logits[N_EXPERTS:ROUTER_ROWS]
    grow = lax.broadcasted_iota(jnp.int32, gl.shape, 0)
    gl = jnp.where(grow < N_GROUPS, gl, -jnp.inf)
    gmax = jnp.max(gl, axis=0, keepdims=True)
    gsum = jnp.sum(jnp.exp(gl - gmax), axis=0, keepdims=True)
    grp_w = 1.0 / gsum
    gidx = jnp.min(jnp.where(gl == gmax, grow, N_GROUPS), axis=0, keepdims=True)

    erow = lax.broadcasted_iota(jnp.int32, el.shape, 0)
    masked = jnp.where((erow // EXPERTS_PER_GROUP) == gidx, el, -jnp.inf)
    top1 = jnp.max(masked, axis=0, keepdims=True)
    i1 = jnp.min(jnp.where(masked == top1, erow, N_EXPERTS), axis=0, keepdims=True)
    masked2 = jnp.where(erow == i1, -jnp.inf, masked)
    top2 = jnp.max(masked2, axis=0, keepdims=True)
    i2 = jnp.min(jnp.where(masked2 == top2, erow, N_EXPERTS), axis=0, keepdims=True)
    e2 = jnp.exp(top2 - top1)
    denom = 1.0 + e2
    g1 = grp_w * (1.0 / denom)
    g2 = grp_w * (e2 / denom)

    sel1 = erow == i1
    sel2 = erow == i2
    onehot = (sel1 | sel2).astype(BF16)
    prefix = jnp.dot(onehot, before_scr[...], preferred_element_type=F32) + cnt_scr[...]
    r1 = jnp.sum(jnp.where(sel1, prefix, 0.0), axis=0, keepdims=True)
    r2 = jnp.sum(jnp.where(sel2, prefix, 0.0), axis=0, keepdims=True)
    cnt_scr[...] += jnp.sum(onehot.astype(F32), axis=1, keepdims=True)

    zi = jnp.zeros((SUBLANES - 2 * TOP_K, tm), jnp.int32)
    ids_ref[0] = jnp.concatenate(
        [i1, i2, r1.astype(jnp.int32), r2.astype(jnp.int32), zi], axis=0)
    gates_ref[0] = jnp.concatenate([g1, g2, jnp.zeros((SUBLANES - TOP_K, tm), F32)], axis=0)
    counts_ref[...] = jnp.broadcast_to(cnt_scr[...], counts_ref.shape).astype(jnp.int32)


def _router(x2d, g, w_group, b_group, w_expert, b_expert):
    t = x2d.shape[0]
    nt = t // ROUTER_TILE
    pad = ROUTER_ROWS - N_EXPERTS - N_GROUPS
    wr = jnp.concatenate([w_expert.T, w_group.T, jnp.zeros((pad, D), F32)], axis=0)
    wr_hi = wr.astype(BF16)
    wr_lo = (wr - wr_hi.astype(F32)).astype(BF16)
    br = jnp.concatenate([b_expert, b_group, jnp.zeros((pad,), F32)]).reshape(ROUTER_ROWS, 1)
    const = lambda i: (0, 0)
    return pl.pallas_call(
        _router_kernel,
        name="router",
        grid=(nt,),
        in_specs=[
            pl.BlockSpec((ROUTER_TILE, D), lambda i: (i, 0)),
            pl.BlockSpec((1, D), const),
            pl.BlockSpec((2 * ROUTER_ROWS, D), const),
            pl.BlockSpec((ROUTER_ROWS, 1), const),
        ],
        out_specs=[
            pl.BlockSpec((1, SUBLANES, ROUTER_TILE), lambda i: (i, 0, 0)),
            pl.BlockSpec((1, SUBLANES, ROUTER_TILE), lambda i: (i, 0, 0)),
            pl.BlockSpec((N_EXPERTS, LANES), const),
        ],
        out_shape=[
            jax.ShapeDtypeStruct((nt, SUBLANES, ROUTER_TILE), jnp.int32),
            jax.ShapeDtypeStruct((nt, SUBLANES, ROUTER_TILE), F32),
            jax.ShapeDtypeStruct((N_EXPERTS, LANES), jnp.int32),
        ],
        scratch_shapes=[pltpu.VMEM((N_EXPERTS, 1), F32),
                        pltpu.VMEM((ROUTER_TILE, ROUTER_TILE), BF16)],
        compiler_params=_params(1),
    )(x2d, g.reshape(1, D), jnp.concatenate([wr_hi, wr_lo], axis=0), br)


SLOTS_PER_TILE = SUBLANES * TOP_K


def _issue_tile_rows(dest_ref, which, j, make_copy):
    for u in range(SUBLANES):
        for k in range(TOP_K):
            dest = dest_ref[which, j, k * SUBLANES + u]
            make_copy(j, u, k, dest).start(priority=(u * TOP_K + k) % 2)


def _issue_rows(n_rows, dest_ref, which, make_copy):
    def body(j, c):
        _issue_tile_rows(dest_ref, which, j, make_copy)
        return c

    lax.fori_loop(0, n_rows // SUBLANES, body, 0)


def _issue_rows_inline(n_rows, dest_ref, which, make_copy):
    for j in range(n_rows // SUBLANES):
        _issue_tile_rows(dest_ref, which, j, make_copy)


def _pack_bf16_pairs(x):
    half = x.shape[1] // 2
    lo = lax.bitcast_convert_type(x[:, :half].astype(BF16).astype(F32), U32)
    hi = lax.bitcast_convert_type(x[:, half:].astype(BF16).astype(F32), U32)
    return (hi & jnp.uint32(0xFFFF0000)) | (lo >> 16)


def _unpack_bf16_pairs(u):
    lo = lax.bitcast_convert_type(u << 16, F32).astype(BF16)
    hi = lax.bitcast_convert_type(u & jnp.uint32(0xFFFF0000), F32).astype(BF16)
    return lo, hi


def _tile_rows(x):
    return x.reshape(x.shape[0] // SUBLANES, SUBLANES, x.shape[1])


def _wait_rows(buf_ref, sem):
    pltpu.make_async_copy(buf_ref, buf_ref, sem).wait()


def _dispatch_kernel(zrow_ref, dest_ref, x_ref, g_ref, xs_ref, hbuf, zbuf, zsem, sem):
    @pl.when(pl.program_id(0) == 0)
    def _():
        zbuf[...] = jnp.zeros_like(zbuf)

        def zcopy(e):
            return pltpu.make_async_copy(zbuf, xs_ref.at[pl.ds(zrow_ref[e], MOE_BLOCK)], zsem)

        def zstart(e, c):
            @pl.when(zrow_ref[e] >= 0)
            def _():
                zcopy(e).start()
            return c

        def zwait(e, c):
            @pl.when(zrow_ref[e] >= 0)
            def _():
                zcopy(e).wait()
            return c

        lax.fori_loop(0, 2 * N_EXPERTS, zstart, 0)
        lax.fori_loop(0, 2 * N_EXPERTS, zwait, 0)

    i = pl.program_id(0)
    slot = i % 2
    x = x_ref[...]
    hbuf[slot] = _tile_rows(_pack_bf16_pairs(x * _rms_scale(x) * g_ref[...]))

    def row_copy(tile, sub, k, dest):
        return pltpu.make_async_copy(hbuf.at[slot, tile, pl.ds(sub, 1), :], xs_ref.at[dest],
                                     sem.at[slot])

    _issue_rows(ROUTER_TILE, dest_ref, 0, row_copy)

    def drain(which):
        for _ in range(TOP_K):
            _wait_rows(hbuf.at[which], sem.at[which])

    @pl.when(i > 0)
    def _():
        drain(1 - slot)

    @pl.when(i == pl.num_programs(0) - 1)
    def _():
        drain(slot)


def _dispatch(x2d, g, dest, zrow, n_pad):
    t = x2d.shape[0]
    nt = t // ROUTER_TILE
    return pl.pallas_call(
        _dispatch_kernel,
        name="dispatch",
        grid_spec=pltpu.PrefetchScalarGridSpec(
            num_scalar_prefetch=1,
            grid=(nt,),
            in_specs=[
                pl.BlockSpec((1, ROUTER_TILE // SUBLANES, SLOTS_PER_TILE),
                             lambda i, z: (i, 0, 0), memory_space=pltpu.SMEM),
                pl.BlockSpec((ROUTER_TILE, D), lambda i, z: (i, 0)),
                pl.BlockSpec((1, D), lambda i, z: (0, 0)),
            ],
            out_specs=pl.BlockSpec(memory_space=pl.ANY),
            scratch_shapes=[
                pltpu.VMEM((2, ROUTER_TILE // SUBLANES, SUBLANES, D // 2), U32),
                pltpu.VMEM((MOE_BLOCK, 1, D // 2), U32),
                pltpu.SemaphoreType.DMA(()),
                pltpu.SemaphoreType.DMA((2,)),
            ],
        ),
        out_shape=jax.ShapeDtypeStruct((n_pad, 1, D // 2), U32),
        compiler_params=_params(1),
    )(zrow, dest, x2d, g.reshape(1, D))


def _experts_kernel(be_ref, nxt_ref, nu_ref, xs_ref, wg_ref, wu_ref, wd_ref, y_ref,
                    wg_f, wu_f, wd_f, wg_b, wu_b, wd_b, xbuf, ybuf, w_sem, in_sem, out_sem, *, layer):
    i = pl.program_id(0)
    n_used = nu_ref[0]
    slot = i % 2

    def block_rows(ref, blk):
        return ref.at[pl.ds(pl.multiple_of(blk * MOE_BLOCK, MOE_BLOCK), MOE_BLOCK), 0]

    def fetch(blk, into):
        return pltpu.make_async_copy(block_rows(xs_ref, blk), xbuf.at[into], in_sem.at[into])

    def write_back(blk, from_):
        return pltpu.make_async_copy(ybuf.at[from_], block_rows(y_ref, blk), out_sem.at[from_])

    def weight_copies(e):
        return [pltpu.make_async_copy(src.at[layer, e], dst, w_sem.at[n])
                for n, (src, dst) in enumerate(((wg_ref, wg_f), (wu_ref, wu_f), (wd_ref, wd_f)))]

    @pl.when(i == 0)
    def _():
        for c in weight_copies(be_ref[0]):
            c.start()
        fetch(0, 0).start()

    @pl.when(i + 1 < n_used)
    def _():
        fetch(i + 1, 1 - slot).start()

    prev = be_ref[jnp.maximum(i - 1, 0)]

    @pl.when((i < n_used) & ((i == 0) | (be_ref[i] != prev)))
    def _():
        for c in weight_copies(be_ref[i]):
            c.wait()
        wg_b[...] = wg_f[...].astype(BF16)
        wu_b[...] = wu_f[...].astype(BF16)
        wd_b[...] = wd_f[...].astype(BF16)

        @pl.when(nxt_ref[i] >= 0)
        def _():
            for c in weight_copies(nxt_ref[i]):
                c.start()

    @pl.when(i >= 2)
    def _():
        write_back(i - 2, slot).wait()

    @pl.when(i < n_used)
    def _():
        fetch(i, slot).wait()
        h_lo, h_hi = _unpack_bf16_pairs(xbuf[slot])
        half = D // 2
        a = (jnp.dot(h_lo, wg_b[:half], preferred_element_type=F32)
             + jnp.dot(h_hi, wg_b[half:], preferred_element_type=F32))
        u = (jnp.dot(h_lo, wu_b[:half], preferred_element_type=F32)
             + jnp.dot(h_hi, wu_b[half:], preferred_element_type=F32))
        hid = (a * _sigmoid(a) * u).astype(BF16)
        ybuf[slot] = jnp.dot(hid, wd_b[...], preferred_element_type=F32)

    @pl.when(i >= n_used)
    def _():
        ybuf[slot] = jnp.zeros((MOE_BLOCK, D), F32)

    write_back(i, slot).start()

    @pl.when(i == pl.num_programs(0) - 1)
    def _():
        write_back(i - 1, 1 - slot).wait()
        write_back(i, slot).wait()


def _experts(xs, block_e, next_e, n_used, layer, w_gate, w_up, w_down):
    n_pad = xs.shape[0]
    n_blocks = n_pad // MOE_BLOCK
    hbm = pl.BlockSpec(memory_space=pl.ANY)
    return pl.pallas_call(
        functools.partial(_experts_kernel, layer=layer),
        name="experts",
        grid_spec=pltpu.PrefetchScalarGridSpec(
            num_scalar_prefetch=3,
            grid=(n_blocks,),
            in_specs=[hbm, hbm, hbm, hbm],
            out_specs=hbm,
            scratch_shapes=[
                pltpu.VMEM((D, D_EXPERT), F32),
                pltpu.VMEM((D, D_EXPERT), F32),
                pltpu.VMEM((D_EXPERT, D), F32),
                pltpu.VMEM((D, D_EXPERT), BF16),
                pltpu.VMEM((D, D_EXPERT), BF16),
                pltpu.VMEM((D_EXPERT, D), BF16),
                pltpu.VMEM((2, MOE_BLOCK, D // 2), U32),
                pltpu.VMEM((2, MOE_BLOCK, D), F32),
                pltpu.SemaphoreType.DMA((3,)),
                pltpu.SemaphoreType.DMA((2,)),
                pltpu.SemaphoreType.DMA((2,)),
            ],
        ),
        out_shape=jax.ShapeDtypeStruct((n_pad, 1, D), F32),
        compiler_params=_params(1),
    )(block_e, next_e, n_used, xs, w_gate, w_up, w_down)


def _combine_kernel(*refs, final, project):
    (dest_ref, dest_next_ref, x_ref, gate_ref, y_ref, p_ref, gple_ref, wp_ref, wg_ref,
     gfin_ref) = refs[:10]
    if project:
        gq_ref, gkv_ref, wq_ref, wkv_ref, out_ref, qt_ref, k_ref, vt_ref = refs[10:18]
        ybuf0, ybuf1, sem, hq0, hq1, hkv0, hkv1 = refs[18:]
        hq_scr, hkv_scr = (hq0, hq1), (hkv0, hkv1)
    else:
        out_ref, ybuf0, ybuf1, sem = refs[10:]
    i = pl.program_id(0)
    ybuf = (ybuf0, ybuf1)

    def row_copy(into):
        def make(tile, sub, k, dest):
            return pltpu.make_async_copy(y_ref.at[dest], ybuf[into].at[k, tile, pl.ds(sub, 1), :],
                                         sem.at[into])
        return make

    def combine(half, slot):
        rows = slice(half * COMBINE_TILE, (half + 1) * COMBINE_TILE)
        gates = gate_ref[half].T
        y0 = ybuf[slot][0].reshape(COMBINE_TILE, D)
        y1 = ybuf[slot][1].reshape(COMBINE_TILE, D)
        x = x_ref[rows, :] + gates[:, 0:1] * y0 + gates[:, 1:2] * y1
        h = (x * _rms_scale(x) * gple_ref[...]).astype(BF16)
        gate = _sigmoid(jnp.dot(h, wg_ref[...], preferred_element_type=F32))
        proj = jnp.dot(p_ref[0, rows, :].astype(BF16), wp_ref[...], preferred_element_type=F32)
        x = x + proj * gate
        if final:
            x = x * _rms_scale(x) * gfin_ref[...]
        out_ref[rows, :] = x
        if project:
            xn = x * _rms_scale(x)
            hq_scr[slot][...] = (xn * gq_ref[...]).astype(BF16)
            hkv_scr[slot][...] = (xn * gkv_ref[...]).astype(BF16)

    def project_rows(half, slot):
        rows = slice(half * COMBINE_TILE, (half + 1) * COMBINE_TILE)
        q = jnp.dot(hq_scr[slot][...], wq_ref[...], preferred_element_type=F32) * (ATTN_SCALE * LOG2E)
        kv = jnp.dot(hkv_scr[slot][...], wkv_ref[...], preferred_element_type=F32)
        qt_ref[0, :, rows] = q.T.astype(BF16)
        k_ref[0, rows, :] = kv[:, :D].astype(BF16)
        vt_ref[0, :, rows] = kv[:, D:].T.astype(BF16)

    def wait(slot):
        for k in range(TOP_K):
            _wait_rows(ybuf[slot].at[k], sem.at[slot])

    @pl.when(i == 0)
    def _():
        _issue_rows(COMBINE_TILE, dest_ref, 0, row_copy(0))

    wait(0)
    _issue_rows_inline(COMBINE_TILE, dest_ref, 1, row_copy(1))
    combine(0, 0)
    wait(1)
    _issue_rows_inline(COMBINE_TILE, dest_next_ref, 0, row_copy(0))
    combine(1, 1)
    if project:
        project_rows(0, 0)
        project_rows(1, 1)

    @pl.when(i == pl.num_programs(0) - 1)
    def _():
        wait(0)


def _combine_ple(x2d, dest, gates, y, p3d, layer, g_ple, w_proj, w_gate, g_final, final, qkv=None):
    assert COMBINE_TILE == ROUTER_TILE
    t = x2d.shape[0]
    step = 2 * COMBINE_TILE
    nt = t // step
    dest_blk = (2, COMBINE_TILE // SUBLANES, SLOTS_PER_TILE)
    const = lambda i: (0, 0)
    resident = functools.partial(pl.BlockSpec, index_map=const, pipeline_mode=pl.Buffered(1))
    in_specs = [
        pl.BlockSpec(dest_blk, lambda i: (i, 0, 0), memory_space=pltpu.SMEM),
        pl.BlockSpec(dest_blk, lambda i: (jnp.minimum(i + 1, nt - 1), 0, 0),
                     memory_space=pltpu.SMEM),
        pl.BlockSpec((step, D), lambda i: (i, 0)),
        pl.BlockSpec((2, SUBLANES, COMBINE_TILE), lambda i: (i, 0, 0)),
        pl.BlockSpec(memory_space=pl.ANY),
        pl.BlockSpec((1, step, D_PLE), lambda i: (layer, i, 0)),
        pl.BlockSpec((1, D), const),
        resident((D_PLE, D)),
        resident((D, D)),
        pl.BlockSpec((1, D), const),
    ]
    args = [dest, dest, x2d, gates, y, p3d, g_ple.reshape(1, D), w_proj.astype(BF16),
            w_gate.astype(BF16), g_final.reshape(1, D)]
    out_specs = [pl.BlockSpec((step, D), lambda i: (i, 0))]
    out_shape = [jax.ShapeDtypeStruct((t, D), F32)]
    if qkv is not None:
        seq, g_q, g_kv, w_q, w_kv = qkv
        per_seq = seq // step
        in_specs += [pl.BlockSpec((1, D), const), pl.BlockSpec((1, D), const),
                     resident((D, D)), resident((D, 2 * D))]
        args += [g_q.reshape(1, D), g_kv.reshape(1, D), w_q.astype(BF16), w_kv.astype(BF16)]
        feat_major = pl.BlockSpec((1, D, step), lambda i: (i // per_seq, 0, i % per_seq))
        row_major = pl.BlockSpec((1, step, D), lambda i: (i // per_seq, i % per_seq, 0))
        out_specs += [feat_major, row_major, feat_major]
        out_shape += [jax.ShapeDtypeStruct((t // seq, D, seq), BF16),
                      jax.ShapeDtypeStruct((t // seq, seq, D), BF16),
                      jax.ShapeDtypeStruct((t // seq, D, seq), BF16)]
    outs = pl.pallas_call(
        functools.partial(_combine_kernel, final=final, project=qkv is not None),
        name="combine_final" if final else "combine",
        grid=(nt,),
        in_specs=in_specs,
        out_specs=out_specs,
        out_shape=out_shape,
        scratch_shapes=[
            pltpu.VMEM((TOP_K, COMBINE_TILE // SUBLANES, SUBLANES, D), F32),
            pltpu.VMEM((TOP_K, COMBINE_TILE // SUBLANES, SUBLANES, D), F32),
            pltpu.SemaphoreType.DMA((2,)),
        ] + [pltpu.VMEM((COMBINE_TILE, D), BF16)] * (4 if qkv is not None else 0),
        compiler_params=_params(1),
    )(*args)
    return outs[0] if qkv is None else tuple(outs)


def _moe_ple(x, p, layer, norm_ffn, w_group, b_group, w_expert, b_expert, w_gate, w_up, w_down,
             norm_ple, ple_w_proj, ple_w_gate, final_norm, final, qkv=None):
    b, s, _ = x.shape
    t = b * s
    x2d = x.reshape(t, D)
    ids, gates, counts = _router(x2d, norm_ffn, w_group, b_group, w_expert, b_expert)

    counts = counts[:, 0]
    padded = (counts + MOE_BLOCK - 1) // MOE_BLOCK * MOE_BLOCK
    pad_end = jnp.cumsum(padded)
    pad_start = pad_end - padded
    n_blocks = t * TOP_K // MOE_BLOCK + N_EXPERTS
    n_pad = n_blocks * MOE_BLOCK
    e = ids[:, 0:TOP_K, :]
    r = ids[:, TOP_K:2 * TOP_K, :]
    experts = jnp.arange(N_EXPERTS, dtype=jnp.int32)
    dest = r + jnp.sum(jnp.where(e[..., None] == experts, pad_start, 0), axis=-1)
    nt = t // ROUTER_TILE
    dest = dest.reshape(nt, TOP_K, ROUTER_TILE // SUBLANES, SUBLANES).transpose(0, 2, 1, 3)
    dest = dest.reshape(nt, ROUTER_TILE // SUBLANES, SLOTS_PER_TILE).astype(jnp.int32)
    block_row = jnp.arange(n_blocks, dtype=jnp.int32) * MOE_BLOCK
    block_e = jnp.minimum(jnp.sum(pad_end[None, :] <= block_row[:, None], axis=-1),
                          N_EXPERTS - 1).astype(jnp.int32)
    n_used = (pad_end[-1:] // MOE_BLOCK).astype(jnp.int32)
    tail = (n_used[0] + jnp.arange(N_EXPERTS, dtype=jnp.int32)) * MOE_BLOCK
    zrow = jnp.concatenate([jnp.where(padded > 0, pad_end - MOE_BLOCK, -1),
                            jnp.where(tail < n_pad, tail, -1)]).astype(jnp.int32)

    xs = _dispatch(x2d, norm_ffn, dest, zrow, n_pad)
    later = jnp.where((experts[None, :] > block_e[:, None]) & (padded[None, :] > 0), experts[None, :],
                      N_EXPERTS)
    next_e = jnp.min(later, axis=-1)
    next_e = jnp.where(next_e < N_EXPERTS, next_e, -1).astype(jnp.int32)
    y = _experts(xs, block_e, next_e, n_used, layer, w_gate, w_up, w_down)
    out = _combine_ple(x2d, dest, gates, y, p.reshape(-1, t, D_PLE), layer,
                       norm_ple, ple_w_proj, ple_w_gate, final_norm, final,
                       None if qkv is None else (s,) + tuple(qkv))
    if qkv is None:
        return out.reshape(b, s, D)
    return (out[0].reshape(b, s, D),) + out[1:]


Q_GROUP = 2 * CHUNK
G_BAND = BAND + CHUNK
PAIR = 2 * B_HEAD_DIM
ONES_ROWS = BF16_SUBLANES


def _attn_kernel(qt_ref, kp_ref, kc_ref, vtp_ref, vtc_ref, bias_ref, x_ref, wo_ref, out_ref, o_scr,
                 s_scr0, s_scr1, s_scr2, p_scr0, p_scr1, p_scr2):
    drow = lax.broadcasted_iota(jnp.int32, (PAIR, Q_GROUP), 0)
    first_head = drow < B_HEAD_DIM
    s_scr = (s_scr0, s_scr1, s_scr2)
    p_scr = (p_scr0, p_scr1, p_scr2)

    def attend(first_tile):
        units = [(g, pr) for g in range(SEQ_TILE // Q_GROUP) for pr in range(B_HEADS // 2)]

        def geometry(g):
            w0 = g * Q_GROUP
            n_prev = SEQ_TILE - w0
            return w0, n_prev, G_BAND - n_prev

        def keys(g):
            return slice(geometry(g)[1], None) if first_tile else slice(None)

        def scores(unit, s_ref):
            g, pr = unit
            w0, n_prev, n_cur = geometry(g)
            feat = slice(pr * PAIR, (pr + 1) * PAIR)
            qt = qt_ref[0, feat, w0:w0 + Q_GROUP]
            zero = jnp.zeros_like(qt)
            qblk = jnp.concatenate([jnp.where(first_head, qt, zero),
                                    jnp.where(first_head, zero, qt)], axis=1)
            if first_tile:
                kb = kc_ref[0, :n_cur, feat]
            else:
                kb = jnp.concatenate([kp_ref[0, w0:, feat], kc_ref[0, :n_cur, feat]], axis=0)
            s_ref[keys(g), :] = (jnp.dot(kb, qblk, preferred_element_type=F32)
                                 + bias_ref[pr, keys(g), :])

        def weights(unit, s_ref, p_ref):
            g, _ = unit
            s = s_ref[keys(g), :]
            m = jnp.max(s, axis=0, keepdims=True)
            p_ref[keys(g), :] = jnp.exp2(s - m).astype(BF16)

        def values(unit, p_ref):
            g, pr = unit
            w0, n_prev, n_cur = geometry(g)
            feat = slice(pr * PAIR, (pr + 1) * PAIR)
            if first_tile:
                vt = vtc_ref[0, feat, :n_cur]
            else:
                vt = jnp.concatenate([vtp_ref[0, feat, w0:], vtc_ref[0, feat, :n_cur]], axis=1)
            ones = jnp.ones((ONES_ROWS, vt.shape[1]), BF16)
            ot = jnp.dot(jnp.concatenate([vt, ones], axis=0), p_ref[keys(g), :],
                         preferred_element_type=F32)
            inv = 1.0 / ot[PAIR:PAIR + 1, :]
            ot = jnp.where(first_head, ot[:PAIR, :Q_GROUP] * inv[:, :Q_GROUP],
                           ot[:PAIR, Q_GROUP:] * inv[:, Q_GROUP:])
            o_scr[w0:w0 + Q_GROUP, feat] = ot.T.astype(BF16)

        n_units = len(units)
        scores(units[0], s_scr[0])
        scores(units[1], s_scr[1])
        scores(units[2], s_scr[2])
        weights(units[0], s_scr[0], p_scr[0])
        weights(units[1], s_scr[1], p_scr[1])
        for n, unit in enumerate(units):
            if n + 3 < n_units:
                scores(units[n + 3], s_scr[n % 3])
            if n + 2 < n_units:
                weights(units[n + 2], s_scr[(n + 2) % 3], p_scr[(n + 2) % 3])
            values(unit, p_scr[n % 3])

    @pl.when(pl.program_id(1) == 0)
    def _():
        attend(True)

    @pl.when(pl.program_id(1) > 0)
    def _():
        attend(False)

    out_ref[0] = x_ref[0] + jnp.dot(o_scr[...], wo_ref[...], preferred_element_type=F32)


def _attn(x, qt, k, vt, bias_t, w_o):
    b, s, _ = x.shape
    cur = lambda bi, si: (bi, si, 0)
    prev = lambda bi, si: (bi, jnp.maximum(si - 1, 0), 0)
    cur_t = lambda bi, si: (bi, 0, si)
    prev_t = lambda bi, si: (bi, 0, jnp.maximum(si - 1, 0))
    blk = (1, SEQ_TILE, D)
    blk_t = (1, D, SEQ_TILE)
    return pl.pallas_call(
        _attn_kernel,
        name="attn",
        grid=(b, s // SEQ_TILE),
        in_specs=[
            pl.BlockSpec(blk_t, cur_t),
            pl.BlockSpec(blk, prev),
            pl.BlockSpec(blk, cur),
            pl.BlockSpec(blk_t, prev_t),
            pl.BlockSpec(blk_t, cur_t),
            pl.BlockSpec((B_HEADS // 2, G_BAND, 2 * Q_GROUP), lambda bi, si: (0, 0, 0),
                         pipeline_mode=pl.Buffered(1)),
            pl.BlockSpec(blk, cur),
            pl.BlockSpec((D, D), lambda bi, si: (0, 0), pipeline_mode=pl.Buffered(1)),
        ],
        out_specs=pl.BlockSpec(blk, cur),
        out_shape=jax.ShapeDtypeStruct(x.shape, F32),
        scratch_shapes=[pltpu.VMEM((SEQ_TILE, D), BF16)]
        + [pltpu.VMEM((G_BAND, 2 * Q_GROUP), F32)] * 3
        + [pltpu.VMEM((G_BAND, 2 * Q_GROUP), BF16)] * 3,
        compiler_params=_params(2),
    )(qt, k, k, vt, vt, bias_t, x, w_o.astype(BF16))


def _group_bias(table):
    band = _band_bias(table) * LOG2E
    pad = lambda lo, hi: jnp.pad(band, ((0, 0), (0, 0), (lo, hi)), constant_values=NEG_INF)
    both = jnp.concatenate([pad(0, CHUNK), pad(CHUNK, 0)], axis=1)
    both = both.reshape(B_HEADS // 2, 2, Q_GROUP, G_BAND)
    return both.transpose(0, 3, 1, 2).reshape(B_HEADS // 2, G_BAND, 2 * Q_GROUP)


def _band_bias(table):
    n_rel = REL_MAX - REL_MIN + 1
    span = BAND + CHUNK - 1
    head = jnp.broadcast_to(table[:, n_rel - 1:], (table.shape[0], span - n_rel))
    ext = jnp.concatenate([head, table[:, ::-1]], axis=1)
    rows = [ext[:, CHUNK - 1 - q:CHUNK - 1 - q + BAND] for q in range(CHUNK)]
    return jnp.stack(rows, axis=1)


def kernel(x, p, a_w_in, a_lb_logits, a_out_norm, a_w_o, kv_norm, w_kv, b_w_q, b_rel_bias, b_w_o,
           norm_mix, norm_ffn, norm_ple, moe_w_group, moe_b_group, moe_w_expert, moe_b_expert,
           moe_w_gate, moe_w_up, moe_w_down, ple_w_proj, ple_w_gate, final_norm):
    b, s, _ = x.shape
    lower_bounds = jnp.cumsum(jax.nn.softmax(a_lb_logits.astype(F32), axis=0), axis=0)

    def moe(xi, i, final, qkv=None):
        return _moe_ple(xi, p, i, norm_ffn[i], moe_w_group[i], moe_b_group[i], moe_w_expert[i],
                        moe_b_expert[i], moe_w_gate, moe_w_up, moe_w_down, norm_ple[i],
                        ple_w_proj[i], ple_w_gate[i], final_norm, final, qkv)

    x = _mixer_a(x, norm_mix[0], a_w_in[0], lower_bounds[0], a_out_norm[0], a_w_o[0])
    x, qt, k, vt = moe(x, 0, False, (norm_mix[1], kv_norm, b_w_q[0], w_kv))

    x = _attn(x, qt, k, vt, _group_bias(b_rel_bias[0].astype(F32)), b_w_o[0])
    x = moe(x, 1, True)
    return x
```

```python
import functools

import jax
import jax.numpy as jnp
from jax import lax
from jax.experimental import pallas as pl
from jax.experimental.pallas import tpu as pltpu

F32 = jnp.float32
BF16 = jnp.bfloat16
U32 = jnp.uint32

D = 1024
CHUNK = 64
A_HEADS = 8
A_HEAD_DIM = 128
B_HEADS = 16
B_HEAD_DIM = 64
LEFT_CHUNKS = 8
BAND = (LEFT_CHUNKS + 1) * CHUNK
REL_MIN = -(CHUNK - 1)
REL_MAX = 256
ATTN_SCALE = B_HEAD_DIM ** -0.5
N_GROUPS = 4
EXPERTS_PER_GROUP = 8
N_EXPERTS = 32
TOP_K = 2
D_EXPERT = 512
MOE_BLOCK = 256
D_PLE = 256
EPS = 1e-6
NEG_INF = -1e30
LOG2E = 1.4426950408889634

SUBLANES = 8
LANES = 128
BF16_SUBLANES = 16
V7X_VMEM_BYTES = 64 * 1024 * 1024

SEQ_TILE = 512
ROUTER_TILE = 512
COMBINE_TILE = 512
ROUTER_ROWS = -(-(N_EXPERTS + N_GROUPS) // BF16_SUBLANES) * BF16_SUBLANES
VMEM_LIMIT = V7X_VMEM_BYTES - 8 * 1024 * 1024


def _params(n_axes, vmem=VMEM_LIMIT):
    return pltpu.CompilerParams(dimension_semantics=("arbitrary",) * n_axes,
                                vmem_limit_bytes=vmem)


def _rms_scale(x):
    return lax.rsqrt(jnp.mean(x * x, axis=-1, keepdims=True) + EPS)


def _sigmoid(x):
    return 1.0 / (1.0 + jnp.exp(-x))


def _mixer_a_kernel(x_ref, g_ref, win_ref, lb_ref, onorm_ref, wo_ref, out_ref,
                    proj_scr, o_scr, state_scr, g_scr0, g_scr1, k_scr0, k_scr1,
                    qd_scr0, qd_scr1, qd_scr2, qd_scr3, kt_scr0, kt_scr1, kt_scr2, kt_scr3,
                    att_scr0, att_scr1, att_scr2, att_scr3):
    @pl.when(pl.program_id(1) == 0)
    def _():
        state_scr[...] = jnp.zeros_like(state_scr)

    x = x_ref[0]
    h = (x * _rms_scale(x) * g_ref[...]).astype(BF16)
    proj_scr[...] = jnp.dot(h, win_ref[...], preferred_element_type=F32)

    row = lax.broadcasted_iota(jnp.int32, (CHUNK, CHUNK), 0)
    col = lax.broadcasted_iota(jnp.int32, (CHUNK, CHUNK), 1)
    causal = row >= col
    tril = causal.astype(BF16)
    lb = lb_ref[...]
    onorm = onorm_ref[...]

    g_scr, k_scr = (g_scr0, g_scr1), (k_scr0, k_scr1)
    qd_scr, kt_scr = (qd_scr0, qd_scr1, qd_scr2, qd_scr3), (kt_scr0, kt_scr1, kt_scr2, kt_scr3)
    att_scr = (att_scr0, att_scr1, att_scr2, att_scr3)
    n_chunks = SEQ_TILE // CHUNK
    units = [(c, hd) for c in range(n_chunks) for hd in range(A_HEADS)]

    def rows_of(c):
        return slice(c * CHUNK, (c + 1) * CHUNK)

    def decay(c):
        f = lb + (1.0 - lb) * _sigmoid(proj_scr[rows_of(c), D:2 * D])
        logf = jnp.log(f)
        hi = logf.astype(BF16)
        lo = (logf - hi.astype(F32)).astype(BF16)
        g_scr[c % 2][...] = (jnp.dot(tril, hi, preferred_element_type=F32)
                             + jnp.dot(tril, lo, preferred_element_type=F32))
        k_scr[c % 2][...] = 1.0 - f

    def intra(n):
        c, hd = units[n]
        sl = slice(hd * A_HEAD_DIM, (hd + 1) * A_HEAD_DIM)
        gh = g_scr[c % 2][:, sl]
        g_last = gh[CHUNK - 1:CHUNK, :]
        k = k_scr[c % 2][:, sl]
        q_dec = (proj_scr[rows_of(c), sl] * jnp.exp(gh)).astype(BF16)
        k_inv = (k * jnp.exp(-gh)).astype(BF16)
        qd_scr[n % 4][...] = q_dec
        kt_scr[n % 4][...] = (k * jnp.exp(g_last - gh)).astype(BF16)
        att = lax.dot_general(q_dec, k_inv, (((1,), (1,)), ((), ())),
                              preferred_element_type=F32)
        att_scr[n % 4][...] = jnp.where(causal, att, 0.0).astype(BF16)

    def output(n):
        c, hd = units[n]
        sl = slice(hd * A_HEAD_DIM, (hd + 1) * A_HEAD_DIM)
        rows = rows_of(c)
        g_last = g_scr[c % 2][CHUNK - 1:CHUNK, sl]
        v = proj_scr[rows, 2 * D + hd * A_HEAD_DIM:2 * D + (hd + 1) * A_HEAD_DIM]
        st = state_scr[hd]
        o = (jnp.dot(att_scr[n % 4][...], v.astype(BF16), preferred_element_type=F32)
             + lax.dot_general(qd_scr[n % 4][...], st.astype(BF16), (((1,), (1,)), ((), ())),
                               preferred_element_type=F32))
        v_t = v.T.astype(BF16)
        state_scr[hd] = st * jnp.exp(g_last) + jnp.dot(v_t, kt_scr[n % 4][...],
                                                       preferred_element_type=F32)
        o = o * _rms_scale(o)
        og = proj_scr[rows, 3 * D + hd * A_HEAD_DIM:3 * D + (hd + 1) * A_HEAD_DIM]
        o = o * onorm[:, sl] * (og * _sigmoid(og))
        o_scr[rows, sl] = o.astype(BF16)

    decay(0)
    intra(0)
    intra(1)
    intra(2)
    for n, (c, hd) in enumerate(units):
        if hd == 0 and c + 1 < n_chunks:
            decay(c + 1)
        if n + 3 < len(units):
            intra(n + 3)
        output(n)
    out_ref[0] = x + jnp.dot(o_scr[...], wo_ref[...], preferred_element_type=F32)


def _mixer_a(x, g, w_in, lb, out_norm, w_o):
    b, s, _ = x.shape
    const = lambda bi, si: (0, 0)
    return pl.pallas_call(
        _mixer_a_kernel,
        name="mixer_a",
        grid=(b, s // SEQ_TILE),
        in_specs=[
            pl.BlockSpec((1, SEQ_TILE, D), lambda bi, si: (bi, si, 0)),
            pl.BlockSpec((1, D), const),
            pl.BlockSpec((D, 4 * D), const, pipeline_mode=pl.Buffered(1)),
            pl.BlockSpec((1, D), const),
            pl.BlockSpec((1, D), const),
            pl.BlockSpec((D, D), const, pipeline_mode=pl.Buffered(1)),
        ],
        out_specs=pl.BlockSpec((1, SEQ_TILE, D), lambda bi, si: (bi, si, 0)),
        out_shape=jax.ShapeDtypeStruct(x.shape, F32),
        scratch_shapes=[
            pltpu.VMEM((SEQ_TILE, 4 * D), F32),
            pltpu.VMEM((SEQ_TILE, D), BF16),
            pltpu.VMEM((A_HEADS, A_HEAD_DIM, A_HEAD_DIM), F32),
        ] + [pltpu.VMEM((CHUNK, D), F32)] * 4
          + [pltpu.VMEM((CHUNK, A_HEAD_DIM), BF16)] * 8
          + [pltpu.VMEM((CHUNK, CHUNK), BF16)] * 4,
        compiler_params=_params(2),
    )(x, g.reshape(1, D), w_in.astype(BF16), lb.reshape(1, D), out_norm.reshape(1, D),
      w_o.astype(BF16))


def _router_kernel(x_ref, g_ref, wr_ref, br_ref, ids_ref, gates_ref, counts_ref, cnt_scr,
                   before_scr):
    tm = ROUTER_TILE

    @pl.when(pl.program_id(0) == 0)
    def _():
        cnt_scr[...] = jnp.zeros_like(cnt_scr)
        tr = lax.broadcasted_iota(jnp.int32, (tm, tm), 0)
        tc = lax.broadcasted_iota(jnp.int32, (tm, tm), 1)
        before_scr[...] = (tr < tc).astype(BF16)

    x = x_ref[...]
    h = x * _rms_scale(x) * g_ref[...]
    h_hi = h.astype(BF16)
    h_lo = (h - h_hi.astype(F32)).astype(BF16)
    nt = (((1,), (1,)), ((), ()))
    both = lax.dot_general(wr_ref[...], h_hi, nt, preferred_element_type=F32)
    cross = lax.dot_general(wr_ref[0:ROUTER_ROWS], h_lo, nt, preferred_element_type=F32)
    logits = both[0:ROUTER_ROWS] + both[ROUTER_ROWS:] + cross + br_ref[...]
    el = logits[0:N_EXPERTS]
    gl = logits[N_EXPERTS:ROUTER_ROWS]
    grow = lax.broadcasted_iota(jnp.int32, gl.shape, 0)
    gl = jnp.where(grow < N_GROUPS, gl, -jnp.inf)
    gmax = jnp.max(gl, axis=0, keepdims=True)
    gsum = jnp.sum(jnp.exp(gl - gmax), axis=0, keepdims=True)
    grp_w = 1.0 / gsum
    gidx = jnp.min(jnp.where(gl == gmax, grow, N_GROUPS), axis=0, keepdims=True)

    erow = lax.broadcasted_iota(jnp.int32, el.shape, 0)
    masked = jnp.where((erow // EXPERTS_PER_GROUP) == gidx, el, -jnp.inf)
    top1 = jnp.max(masked, axis=0, keepdims=True)
    i1 = jnp.min(jnp.where(masked == top1, erow, N_EXPERTS), axis=0, keepdims=True)
    masked2 = jnp.where(erow == i1, -jnp.inf, masked)
    top2 = jnp.max(masked2, axis=0, keepdims=True)
    i2 = jnp.min(jnp.where(masked2 == top2, erow, N_EXPERTS), axis=0, keepdims=True)
    e2 = jnp.exp(top2 - top1)
    denom = 1.0 + e2
    g1 = grp_w * (1.0 / denom)
    g2 = grp_w * (e2 / denom)

    sel1 = erow == i1
    sel2 = erow == i2
    onehot = (sel1 | sel2).astype(BF16)
    prefix = jnp.dot(onehot, before_scr[...], preferred_element_type=F32) + cnt_scr[...]
    r1 = jnp.sum(jnp.where(sel1, prefix, 0.0), axis=0, keepdims=True)
    r2 = jnp.sum(jnp.where(sel2, prefix, 0.0), axis=0, keepdims=True)
    cnt_scr[...] += jnp.sum(onehot.astype(F32), axis=1, keepdims=True)

    zi = jnp.zeros((SUBLANES - 2 * TOP_K, tm), jnp.int32)
    ids_ref[0] = jnp.concatenate(
        [i1, i2, r1.astype(jnp.int32), r2.astype(jnp.int32), zi], axis=0)
    gates_ref[0] = jnp.concatenate([g1, g2, jnp.zeros((SUBLANES - TOP_K, tm), F32)], axis=0)
    counts_ref[...] = jnp.broadcast_to(cnt_scr[...], counts_ref.shape).astype(jnp.int32)


def _router(x2d, g, w_group, b_group, w_expert, b_expert):
    t = x2d.shape[0]
    nt = t // ROUTER_TILE
    pad = ROUTER_ROWS - N_EXPERTS - N_GROUPS
    wr = jnp.concatenate([w_expert.T, w_group.T, jnp.zeros((pad, D), F32)], axis=0)
    wr_hi = wr.astype(BF16)
    wr_lo = (wr - wr_hi.astype(F32)).astype(BF16)
    br = jnp.concatenate([b_expert, b_group, jnp.zeros((pad,), F32)]).reshape(ROUTER_ROWS, 1)
    const = lambda i: (0, 0)
    return pl.pallas_call(
        _router_kernel,
        name="router",
        grid=(nt,),
        in_specs=[
            pl.BlockSpec((ROUTER_TILE, D), lambda i: (i, 0)),
            pl.BlockSpec((1, D), const),
            pl.BlockSpec((2 * ROUTER_ROWS, D), const),
            pl.BlockSpec((ROUTER_ROWS, 1), const),
        ],
        out_specs=[
            pl.BlockSpec((1, SUBLANES, ROUTER_TILE), lambda i: (i, 0, 0)),
            pl.BlockSpec((1, SUBLANES, ROUTER_TILE), lambda i: (i, 0, 0)),
            pl.BlockSpec((N_EXPERTS, LANES), const),
        ],
        out_shape=[
            jax.ShapeDtypeStruct((nt, SUBLANES, ROUTER_TILE), jnp.int32),
            jax.ShapeDtypeStruct((nt, SUBLANES, ROUTER_TILE), F32),
            jax.ShapeDtypeStruct((N_EXPERTS, LANES), jnp.int32),
        ],
        scratch_shapes=[pltpu.VMEM((N_EXPERTS, 1), F32),
                        pltpu.VMEM((ROUTER_TILE, ROUTER_TILE), BF16)],
        compiler_params=_params(1),
    )(x2d, g.reshape(1, D), jnp.concatenate([wr_hi, wr_lo], axis=0), br)


SLOTS_PER_TILE = SUBLANES * TOP_K


def _issue_tile_rows(dest_ref, which, j, make_copy):
    for u in range(SUBLANES):
        for k in range(TOP_K):
            dest = dest_ref[which, j, k * SUBLANES + u]
            make_copy(j, u, k, dest).start(priority=(u * TOP_K + k) % 2)


def _issue_rows(n_rows, dest_ref, which, make_copy):
    def body(j, c):
        _issue_tile_rows(dest_ref, which, j, make_copy)
        return c

    lax.fori_loop(0, n_rows // SUBLANES, body, 0)


def _issue_rows_inline(n_rows, dest_ref, which, make_copy):
    for j in range(n_rows // SUBLANES):
        _issue_tile_rows(dest_ref, which, j, make_copy)


def _pack_bf16_pairs(x):
    half = x.shape[1] // 2
    lo = lax.bitcast_convert_type(x[:, :half].astype(BF16).astype(F32), U32)
    hi = lax.bitcast_convert_type(x[:, half:].astype(BF16).astype(F32), U32)
    return (hi & jnp.uint32(0xFFFF0000)) | (lo >> 16)


def _unpack_bf16_pairs(u):
    lo = lax.bitcast_convert_type(u << 16, F32).astype(BF16)
    hi = lax.bitcast_convert_type(u & jnp.uint32(0xFFFF0000), F32).astype(BF16)
    return lo, hi


def _tile_rows(x):
    return x.reshape(x.shape[0] // SUBLANES, SUBLANES, x.shape[1])


def _wait_rows(buf_ref, sem):
    pltpu.make_async_copy(buf_ref, buf_ref, sem).wait()


def _dispatch_kernel(zrow_ref, dest_ref, x_ref, g_ref, xs_ref, hbuf, zbuf, zsem, sem):
    @pl.when(pl.program_id(0) == 0)
    def _():
        zbuf[...] = jnp.zeros_like(zbuf)

        def zcopy(e):
            return pltpu.make_async_copy(zbuf, xs_ref.at[pl.ds(zrow_ref[e], MOE_BLOCK)], zsem)

        def zstart(e, c):
            @pl.when(zrow_ref[e] >= 0)
            def _():
                zcopy(e).start()
            return c

        def zwait(e, c):
            @pl.when(zrow_ref[e] >= 0)
            def _():
                zcopy(e).wait()
            return c

        lax.fori_loop(0, 2 * N_EXPERTS, zstart, 0)
        lax.fori_loop(0, 2 * N_EXPERTS, zwait, 0)

    i = pl.program_id(0)
    slot = i % 2
    x = x_ref[...]
    hbuf[slot] = _tile_rows(_pack_bf16_pairs(x * _rms_scale(x) * g_ref[...]))

    def row_copy(tile, sub, k, dest):
        return pltpu.make_async_copy(hbuf.at[slot, tile, pl.ds(sub, 1), :], xs_ref.at[dest],
                                     sem.at[slot])

    _issue_rows(ROUTER_TILE, dest_ref, 0, row_copy)

    def drain(which):
        for _ in range(TOP_K):
            _wait_rows(hbuf.at[which], sem.at[which])

    @pl.when(i > 0)
    def _():
        drain(1 - slot)

    @pl.when(i == pl.num_programs(0) - 1)
    def _():
        drain(slot)


def _dispatch(x2d, g, dest, zrow, n_pad):
    t = x2d.shape[0]
    nt = t // ROUTER_TILE
    return pl.pallas_call(
        _dispatch_kernel,
        name="dispatch",
        grid_spec=pltpu.PrefetchScalarGridSpec(
            num_scalar_prefetch=1,
            grid=(nt,),
            in_specs=[
                pl.BlockSpec((1, ROUTER_TILE // SUBLANES, SLOTS_PER_TILE),
                             lambda i, z: (i, 0, 0), memory_space=pltpu.SMEM),
                pl.BlockSpec((ROUTER_TILE, D), lambda i, z: (i, 0)),
                pl.BlockSpec((1, D), lambda i, z: (0, 0)),
            ],
            out_specs=pl.BlockSpec(memory_space=pl.ANY),
            scratch_shapes=[
                pltpu.VMEM((2, ROUTER_TILE // SUBLANES, SUBLANES, D // 2), U32),
                pltpu.VMEM((MOE_BLOCK, 1, D // 2), U32),
                pltpu.SemaphoreType.DMA(()),
                pltpu.SemaphoreType.DMA((2,)),
            ],
        ),
        out_shape=jax.ShapeDtypeStruct((n_pad, 1, D // 2), U32),
        compiler_params=_params(1),
    )(zrow, dest, x2d, g.reshape(1, D))


def _experts_kernel(be_ref, nxt_ref, nu_ref, xs_ref, wg_ref, wu_ref, wd_ref, y_ref,
                    wg_f, wu_f, wd_f, wg_b, wu_b, wd_b, xbuf, ybuf, w_sem, in_sem, out_sem, *, layer):
    i = pl.program_id(0)
    n_used = nu_ref[0]
    slot = i % 2

    def block_rows(ref, blk):
        return ref.at[pl.ds(pl.multiple_of(blk * MOE_BLOCK, MOE_BLOCK), MOE_BLOCK), 0]

    def fetch(blk, into):
        return pltpu.make_async_copy(block_rows(xs_ref, blk), xbuf.at[into], in_sem.at[into])

    def write_back(blk, from_):
        return pltpu.make_async_copy(ybuf.at[from_], block_rows(y_ref, blk), out_sem.at[from_])

    def weight_copies(e):
        return [pltpu.make_async_copy(src.at[layer, e], dst, w_sem.at[n])
                for n, (src, dst) in enumerate(((wg_ref, wg_f), (wu_ref, wu_f), (wd_ref, wd_f)))]

    @pl.when(i == 0)
    def _():
        for c in weight_copies(be_ref[0]):
            c.start()
        fetch(0, 0).start()

    @pl.when(i + 1 < n_used)
    def _():
        fetch(i + 1, 1 - slot).start()

    prev = be_ref[jnp.maximum(i - 1, 0)]

    @pl.when((i < n_used) & ((i == 0) | (be_ref[i] != prev)))
    def _():
        for c in weight_copies(be_ref[i]):
            c.wait()
        wg_b[...] = wg_f[...].astype(BF16)
        wu_b[...] = wu_f[...].astype(BF16)
        wd_b[...] = wd_f[...].astype(BF16)

        @pl.when(nxt_ref[i] >= 0)
        def _():
            for c in weight_copies(nxt_ref[i]):
                c.start()

    @pl.when(i >= 2)
    def _():
        write_back(i - 2, slot).wait()

    @pl.when(i < n_used)
    def _():
        fetch(i, slot).wait()
        h_lo, h_hi = _unpack_bf16_pairs(xbuf[slot])
        half = D // 2
        a = (jnp.dot(h_lo, wg_b[:half], preferred_element_type=F32)
             + jnp.dot(h_hi, wg_b[half:], preferred_element_type=F32))
        u = (jnp.dot(h_lo, wu_b[:half], preferred_element_type=F32)
             + jnp.dot(h_hi, wu_b[half:], preferred_element_type=F32))
        hid = (a * _sigmoid(a) * u).astype(BF16)
        ybuf[slot] = jnp.dot(hid, wd_b[...], preferred_element_type=F32)

    @pl.when(i >= n_used)
    def _():
        ybuf[slot] = jnp.zeros((MOE_BLOCK, D), F32)

    write_back(i, slot).start()

    @pl.when(i == pl.num_programs(0) - 1)
    def _():
        write_back(i - 1, 1 - slot).wait()
        write_back(i, slot).wait()


def _experts(xs, block_e, next_e, n_used, layer, w_gate, w_up, w_down):
    n_pad = xs.shape[0]
    n_blocks = n_pad // MOE_BLOCK
    hbm = pl.BlockSpec(memory_space=pl.ANY)
    return pl.pallas_call(
        functools.partial(_experts_kernel, layer=layer),
        name="experts",
        grid_spec=pltpu.PrefetchScalarGridSpec(
            num_scalar_prefetch=3,
            grid=(n_blocks,),
            in_specs=[hbm, hbm, hbm, hbm],
            out_specs=hbm,
            scratch_shapes=[
                pltpu.VMEM((D, D_EXPERT), F32),
                pltpu.VMEM((D, D_EXPERT), F32),
                pltpu.VMEM((D_EXPERT, D), F32),
                pltpu.VMEM((D, D_EXPERT), BF16),
                pltpu.VMEM((D, D_EXPERT), BF16),
                pltpu.VMEM((D_EXPERT, D), BF16),
                pltpu.VMEM((2, MOE_BLOCK, D // 2), U32),
                pltpu.VMEM((2, MOE_BLOCK, D), F32),
                pltpu.SemaphoreType.DMA((3,)),
                pltpu.SemaphoreType.DMA((2,)),
                pltpu.SemaphoreType.DMA((2,)),
            ],
        ),
        out_shape=jax.ShapeDtypeStruct((n_pad, 1, D), F32),
        compiler_params=_params(1),
    )(block_e, next_e, n_used, xs, w_gate, w_up, w_down)


def _combine_kernel(*refs, final, project):
    (dest_ref, dest_next_ref, x_ref, gate_ref, y_ref, p_ref, gple_ref, wp_ref, wg_ref,
     gfin_ref) = refs[:10]
    if project:
        gq_ref, gkv_ref, wq_ref, wkv_ref, out_ref, qt_ref, k_ref, vt_ref = refs[10:18]
        ybuf0, ybuf1, sem = refs[18:]
    else:
        out_ref, ybuf0, ybuf1, sem = refs[10:]
    i = pl.program_id(0)
    ybuf = (ybuf0, ybuf1)

    def row_copy(into):
        def make(tile, sub, k, dest):
            return pltpu.make_async_copy(y_ref.at[dest], ybuf[into].at[k, tile, pl.ds(sub, 1), :],
                                         sem.at[into])
        return make

    def combine(half, slot):
        rows = slice(half * COMBINE_TILE, (half + 1) * COMBINE_TILE)
        gates = gate_ref[half].T
        y0 = ybuf[slot][0].reshape(COMBINE_TILE, D)
        y1 = ybuf[slot][1].reshape(COMBINE_TILE, D)
        x = x_ref[rows, :] + gates[:, 0:1] * y0 + gates[:, 1:2] * y1
        h = (x * _rms_scale(x) * gple_ref[...]).astype(BF16)
        gate = _sigmoid(jnp.dot(h, wg_ref[...], preferred_element_type=F32))
        proj = jnp.dot(p_ref[0, rows, :].astype(BF16), wp_ref[...], preferred_element_type=F32)
        x = x + proj * gate
        if final:
            x = x * _rms_scale(x) * gfin_ref[...]
        out_ref[rows, :] = x
        if project:
            q, k, v = _qkv_rows(x, gq_ref, gkv_ref, wq_ref, wkv_ref)
            qt_ref[0, :, rows] = q.T.astype(BF16)
            k_ref[0, rows, :] = k.astype(BF16)
            vt_ref[0, :, rows] = v.T.astype(BF16)

    def wait(slot):
        for k in range(TOP_K):
            _wait_rows(ybuf[slot].at[k], sem.at[slot])

    @pl.when(i == 0)
    def _():
        _issue_rows(COMBINE_TILE, dest_ref, 0, row_copy(0))

    wait(0)
    _issue_rows_inline(COMBINE_TILE, dest_ref, 1, row_copy(1))
    combine(0, 0)
    wait(1)
    _issue_rows_inline(COMBINE_TILE, dest_next_ref, 0, row_copy(0))
    combine(1, 1)

    @pl.when(i == pl.num_programs(0) - 1)
    def _():
        wait(0)


def _qkv_rows(x, gq_ref, gkv_ref, wq_ref, wkv_ref):
    xn = x * _rms_scale(x)
    hq = (xn * gq_ref[...]).astype(BF16)
    hkv = (xn * gkv_ref[...]).astype(BF16)
    q = jnp.dot(hq, wq_ref[...], preferred_element_type=F32) * (ATTN_SCALE * LOG2E)
    kv = jnp.dot(hkv, wkv_ref[...], preferred_element_type=F32)
    return q, kv[:, :D], kv[:, D:]


def _combine_ple(x2d, dest, gates, y, p3d, layer, g_ple, w_proj, w_gate, g_final, final, qkv=None):
    assert COMBINE_TILE == ROUTER_TILE
    t = x2d.shape[0]
    step = 2 * COMBINE_TILE
    nt = t // step
    dest_blk = (2, COMBINE_TILE // SUBLANES, SLOTS_PER_TILE)
    const = lambda i: (0, 0)
    resident = functools.partial(pl.BlockSpec, index_map=const, pipeline_mode=pl.Buffered(1))
    in_specs = [
        pl.BlockSpec(dest_blk, lambda i: (i, 0, 0), memory_space=pltpu.SMEM),
        pl.BlockSpec(dest_blk, lambda i: (jnp.minimum(i + 1, nt - 1), 0, 0),
                     memory_space=pltpu.SMEM),
        pl.BlockSpec((step, D), lambda i: (i, 0)),
        pl.BlockSpec((2, SUBLANES, COMBINE_TILE), lambda i: (i, 0, 0)),
        pl.BlockSpec(memory_space=pl.ANY),
        pl.BlockSpec((1, step, D_PLE), lambda i: (layer, i, 0)),
        pl.BlockSpec((1, D), const),
        resident((D_PLE, D)),
        resident((D, D)),
        pl.BlockSpec((1, D), const),
    ]
    args = [dest, dest, x2d, gates, y, p3d, g_ple.reshape(1, D), w_proj.astype(BF16),
            w_gate.astype(BF16), g_final.reshape(1, D)]
    out_specs = [pl.BlockSpec((step, D), lambda i: (i, 0))]
    out_shape = [jax.ShapeDtypeStruct((t, D), F32)]
    if qkv is not None:
        seq, g_q, g_kv, w_q, w_kv = qkv
        per_seq = seq // step
        in_specs += [pl.BlockSpec((1, D), const), pl.BlockSpec((1, D), const),
                     resident((D, D)), resident((D, 2 * D))]
        args += [g_q.reshape(1, D), g_kv.reshape(1, D), w_q.astype(BF16), w_kv.astype(BF16)]
        feat_major = pl.BlockSpec((1, D, step), lambda i: (i // per_seq, 0, i % per_seq))
        row_major = pl.BlockSpec((1, step, D), lambda i: (i // per_seq, i % per_seq, 0))
        out_specs += [feat_major, row_major, feat_major]
        out_shape += [jax.ShapeDtypeStruct((t // seq, D, seq), BF16),
                      jax.ShapeDtypeStruct((t // seq, seq, D), BF16),
                      jax.ShapeDtypeStruct((t // seq, D, seq), BF16)]
    outs = pl.pallas_call(
        functools.partial(_combine_kernel, final=final, project=qkv is not None),
        name="combine_final" if final else "combine",
        grid=(nt,),
        in_specs=in_specs,
        out_specs=out_specs,
        out_shape=out_shape,
        scratch_shapes=[
            pltpu.VMEM((TOP_K, COMBINE_TILE // SUBLANES, SUBLANES, D), F32),
            pltpu.VMEM((TOP_K, COMBINE_TILE // SUBLANES, SUBLANES, D), F32),
            pltpu.SemaphoreType.DMA((2,)),
        ],
        compiler_params=_params(1),
    )(*args)
    return outs[0] if qkv is None else tuple(outs)


def _moe_ple(x, p, layer, norm_ffn, w_group, b_group, w_expert, b_expert, w_gate, w_up, w_down,
             norm_ple, ple_w_proj, ple_w_gate, final_norm, final, qkv=None):
    b, s, _ = x.shape
    t = b * s
    x2d = x.reshape(t, D)
    ids, gates, counts = _router(x2d, norm_ffn, w_group, b_group, w_expert, b_expert)

    counts = counts[:, 0]
    padded = (counts + MOE_BLOCK - 1) // MOE_BLOCK * MOE_BLOCK
    pad_end = jnp.cumsum(padded)
    pad_start = pad_end - padded
    n_blocks = t * TOP_K // MOE_BLOCK + N_EXPERTS
    n_pad = n_blocks * MOE_BLOCK
    e = ids[:, 0:TOP_K, :]
    r = ids[:, TOP_K:2 * TOP_K, :]
    experts = jnp.arange(N_EXPERTS, dtype=jnp.int32)
    dest = r + jnp.sum(jnp.where(e[..., None] == experts, pad_start, 0), axis=-1)
    nt = t // ROUTER_TILE
    dest = dest.reshape(nt, TOP_K, ROUTER_TILE // SUBLANES, SUBLANES).transpose(0, 2, 1, 3)
    dest = dest.reshape(nt, ROUTER_TILE // SUBLANES, SLOTS_PER_TILE).astype(jnp.int32)
    block_row = jnp.arange(n_blocks, dtype=jnp.int32) * MOE_BLOCK
    block_e = jnp.minimum(jnp.sum(pad_end[None, :] <= block_row[:, None], axis=-1),
                          N_EXPERTS - 1).astype(jnp.int32)
    n_used = (pad_end[-1:] // MOE_BLOCK).astype(jnp.int32)
    tail = (n_used[0] + jnp.arange(N_EXPERTS, dtype=jnp.int32)) * MOE_BLOCK
    zrow = jnp.concatenate([jnp.where(padded > 0, pad_end - MOE_BLOCK, -1),
                            jnp.where(tail < n_pad, tail, -1)]).astype(jnp.int32)

    xs = _dispatch(x2d, norm_ffn, dest, zrow, n_pad)
    later = jnp.where((experts[None, :] > block_e[:, None]) & (padded[None, :] > 0), experts[None, :],
                      N_EXPERTS)
    next_e = jnp.min(later, axis=-1)
    next_e = jnp.where(next_e < N_EXPERTS, next_e, -1).astype(jnp.int32)
    y = _experts(xs, block_e, next_e, n_used, layer, w_gate, w_up, w_down)
    out = _combine_ple(x2d, dest, gates, y, p.reshape(-1, t, D_PLE), layer,
                       norm_ple, ple_w_proj, ple_w_gate, final_norm, final,
                       None if qkv is None else (s,) + tuple(qkv))
    if qkv is None:
        return out.reshape(b, s, D)
    return (out[0].reshape(b, s, D),) + out[1:]


Q_GROUP = 2 * CHUNK
G_BAND = BAND + CHUNK
PAIR = 2 * B_HEAD_DIM
ONES_ROWS = BF16_SUBLANES


def _attn_kernel(qt_ref, kp_ref, kc_ref, vtp_ref, vtc_ref, bias_ref, x_ref, wo_ref, out_ref, o_scr,
                 s_scr0, s_scr1, s_scr2, p_scr0, p_scr1, p_scr2):
    drow = lax.broadcasted_iota(jnp.int32, (PAIR, Q_GROUP), 0)
    first_head = drow < B_HEAD_DIM
    s_scr = (s_scr0, s_scr1, s_scr2)
    p_scr = (p_scr0, p_scr1, p_scr2)

    def attend(first_tile):
        units = [(g, pr) for g in range(SEQ_TILE // Q_GROUP) for pr in range(B_HEADS // 2)]

        def geometry(g):
            w0 = g * Q_GROUP
            n_prev = SEQ_TILE - w0
            return w0, n_prev, G_BAND - n_prev

        def keys(g):
            return slice(geometry(g)[1], None) if first_tile else slice(None)

        def scores(unit, s_ref):
            g, pr = unit
            w0, n_prev, n_cur = geometry(g)
            feat = slice(pr * PAIR, (pr + 1) * PAIR)
            qt = qt_ref[0, feat, w0:w0 + Q_GROUP]
            zero = jnp.zeros_like(qt)
            qblk = jnp.concatenate([jnp.where(first_head, qt, zero),
                                    jnp.where(first_head, zero, qt)], axis=1)
            if first_tile:
                kb = kc_ref[0, :n_cur, feat]
            else:
                kb = jnp.concatenate([kp_ref[0, w0:, feat], kc_ref[0, :n_cur, feat]], axis=0)
            s_ref[keys(g), :] = (jnp.dot(kb, qblk, preferred_element_type=F32)
                                 + bias_ref[pr, keys(g), :])

        def weights(unit, s_ref, p_ref):
            g, _ = unit
            s = s_ref[keys(g), :]
            m = jnp.max(s, axis=0, keepdims=True)
            p_ref[keys(g), :] = jnp.exp2(s - m).astype(BF16)

        def values(unit, p_ref):
            g, pr = unit
            w0, n_prev, n_cur = geometry(g)
            feat = slice(pr * PAIR, (pr + 1) * PAIR)
            if first_tile:
                vt = vtc_ref[0, feat, :n_cur]
            else:
                vt = jnp.concatenate([vtp_ref[0, feat, w0:], vtc_ref[0, feat, :n_cur]], axis=1)
            ones = jnp.ones((ONES_ROWS, vt.shape[1]), BF16)
            ot = jnp.dot(jnp.concatenate([vt, ones], axis=0), p_ref[keys(g), :],
                         preferred_element_type=F32)
            inv = 1.0 / ot[PAIR:PAIR + 1, :]
            ot = jnp.where(first_head, ot[:PAIR, :Q_GROUP] * inv[:, :Q_GROUP],
                           ot[:PAIR, Q_GROUP:] * inv[:, Q_GROUP:])
            o_scr[w0:w0 + Q_GROUP, feat] = ot.T.astype(BF16)

        n_units = len(units)
        scores(units[0], s_scr[0])
        scores(units[1], s_scr[1])
        scores(units[2], s_scr[2])
        weights(units[0], s_scr[0], p_scr[0])
        weights(units[1], s_scr[1], p_scr[1])
        for n, unit in enumerate(units):
            if n + 3 < n_units:
                scores(units[n + 3], s_scr[n % 3])
            if n + 2 < n_units:
                weights(units[n + 2], s_scr[(n + 2) % 3], p_scr[(n + 2) % 3])
            values(unit, p_scr[n % 3])

    @pl.when(pl.program_id(1) == 0)
    def _():
        attend(True)

    @pl.when(pl.program_id(1) > 0)
    def _():
        attend(False)

    out_ref[0] = x_ref[0] + jnp.dot(o_scr[...], wo_ref[...], preferred_element_type=F32)


def _attn(x, qt, k, vt, bias_t, w_o):
    b, s, _ = x.shape
    cur = lambda bi, si: (bi, si, 0)
    prev = lambda bi, si: (bi, jnp.maximum(si - 1, 0), 0)
    cur_t = lambda bi, si: (bi, 0, si)
    prev_t = lambda bi, si: (bi, 0, jnp.maximum(si - 1, 0))
    blk = (1, SEQ_TILE, D)
    blk_t = (1, D, SEQ_TILE)
    return pl.pallas_call(
        _attn_kernel,
        name="attn",
        grid=(b, s // SEQ_TILE),
        in_specs=[
            pl.BlockSpec(blk_t, cur_t),
            pl.BlockSpec(blk, prev),
            pl.BlockSpec(blk, cur),
            pl.BlockSpec(blk_t, prev_t),
            pl.BlockSpec(blk_t, cur_t),
            pl.BlockSpec((B_HEADS // 2, G_BAND, 2 * Q_GROUP), lambda bi, si: (0, 0, 0),
                         pipeline_mode=pl.Buffered(1)),
            pl.BlockSpec(blk, cur),
            pl.BlockSpec((D, D), lambda bi, si: (0, 0), pipeline_mode=pl.Buffered(1)),
        ],
        out_specs=pl.BlockSpec(blk, cur),
        out_shape=jax.ShapeDtypeStruct(x.shape, F32),
        scratch_shapes=[pltpu.VMEM((SEQ_TILE, D), BF16)]
        + [pltpu.VMEM((G_BAND, 2 * Q_GROUP), F32)] * 3
        + [pltpu.VMEM((G_BAND, 2 * Q_GROUP), BF16)] * 3,
        compiler_params=_params(2),
    )(qt, k, k, vt, vt, bias_t, x, w_o.astype(BF16))


def _group_bias(table):
    band = _band_bias(table) * LOG2E
    pad = lambda lo, hi: jnp.pad(band, ((0, 0), (0, 0), (lo, hi)), constant_values=NEG_INF)
    both = jnp.concatenate([pad(0, CHUNK), pad(CHUNK, 0)], axis=1)
    both = both.reshape(B_HEADS // 2, 2, Q_GROUP, G_BAND)
    return both.transpose(0, 3, 1, 2).reshape(B_HEADS // 2, G_BAND, 2 * Q_GROUP)


def _band_bias(table):
    n_rel = REL_MAX - REL_MIN + 1
    span = BAND + CHUNK - 1
    head = jnp.broadcast_to(table[:, n_rel - 1:], (table.shape[0], span - n_rel))
    ext = jnp.concatenate([head, table[:, ::-1]], axis=1)
    rows = [ext[:, CHUNK - 1 - q:CHUNK - 1 - q + BAND] for q in range(CHUNK)]
    return jnp.stack(rows, axis=1)


def kernel(x, p, a_w_in, a_lb_logits, a_out_norm, a_w_o, kv_norm, w_kv, b_w_q, b_rel_bias, b_w_o,
           norm_mix, norm_ffn, norm_ple, moe_w_group, moe_b_group, moe_w_expert, moe_b_expert,
           moe_w_gate, moe_w_up, moe_w_down, ple_w_proj, ple_w_gate, final_norm):
    b, s, _ = x.shape
    lower_bounds = jnp.cumsum(jax.nn.softmax(a_lb_logits.astype(F32), axis=0), axis=0)

    def moe(xi, i, final, qkv=None):
        return _moe_ple(xi, p, i, norm_ffn[i], moe_w_group[i], moe_b_group[i], moe_w_expert[i],
                        moe_b_expert[i], moe_w_gate, moe_w_up, moe_w_down, norm_ple[i],
                        ple_w_proj[i], ple_w_gate[i], final_norm, final, qkv)

    x = _mixer_a(x, norm_mix[0], a_w_in[0], lower_bounds[0], a_out_norm[0], a_w_o[0])
    x, qt, k, vt = moe(x, 0, False, (norm_mix[1], kv_norm, b_w_q[0], w_kv))

    x = _attn(x, qt, k, vt, _group_bias(b_rel_bias[0].astype(F32)), b_w_o[0])
    x = moe(x, 1, True)
    return x
```

```python
import functools

import jax
import jax.numpy as jnp
from jax import lax
from jax.experimental import pallas as pl
from jax.experimental.pallas import tpu as pltpu

F32 = jnp.float32
BF16 = jnp.bfloat16
U32 = jnp.uint32

D = 1024
CHUNK = 64
A_HEADS = 8
A_HEAD_DIM = 128
B_HEADS = 16
B_HEAD_DIM = 64
LEFT_CHUNKS = 8
BAND = (LEFT_CHUNKS + 1) * CHUNK
REL_MIN = -(CHUNK - 1)
REL_MAX = 256
ATTN_SCALE = B_HEAD_DIM ** -0.5
N_GROUPS = 4
EXPERTS_PER_GROUP = 8
N_EXPERTS = 32
TOP_K = 2
D_EXPERT = 512
MOE_BLOCK = 1024
D_PLE = 256
EPS = 1e-6
NEG_INF = -1e30
LOG2E = 1.4426950408889634

SUBLANES = 8
LANES = 128
BF16_SUBLANES = 16
V7X_VMEM_BYTES = 64 * 1024 * 1024

SEQ_TILE = 512
ROUTER_TILE = 512
COMBINE_TILE = 512
ROUTER_ROWS = -(-(N_EXPERTS + N_GROUPS) // BF16_SUBLANES) * BF16_SUBLANES
VMEM_LIMIT = V7X_VMEM_BYTES - 8 * 1024 * 1024


def _params(n_axes, vmem=VMEM_LIMIT):
    return pltpu.CompilerParams(dimension_semantics=("arbitrary",) * n_axes,
                                vmem_limit_bytes=vmem)


def _rms_scale(x):
    return lax.rsqrt(jnp.mean(x * x, axis=-1, keepdims=True) + EPS)


def _sigmoid(x):
    return 1.0 / (1.0 + jnp.exp(-x))


def _mixer_a_kernel(x_ref, g_ref, win_ref, lb_ref, onorm_ref, wo_ref, out_ref,
                    proj_scr, o_scr, state_scr, g_scr0, g_scr1, k_scr0, k_scr1,
                    qd_scr0, qd_scr1, qd_scr2, qd_scr3, kt_scr0, kt_scr1, kt_scr2, kt_scr3,
                    att_scr0, att_scr1, att_scr2, att_scr3):
    @pl.when(pl.program_id(1) == 0)
    def _():
        state_scr[...] = jnp.zeros_like(state_scr)

    x = x_ref[0]
    h = (x * _rms_scale(x) * g_ref[...]).astype(BF16)
    proj_scr[...] = jnp.dot(h, win_ref[...], preferred_element_type=F32)

    row = lax.broadcasted_iota(jnp.int32, (CHUNK, CHUNK), 0)
    col = lax.broadcasted_iota(jnp.int32, (CHUNK, CHUNK), 1)
    causal = row >= col
    tril = causal.astype(BF16)
    lb = lb_ref[...]
    onorm = onorm_ref[...]

    g_scr, k_scr = (g_scr0, g_scr1), (k_scr0, k_scr1)
    qd_scr, kt_scr = (qd_scr0, qd_scr1, qd_scr2, qd_scr3), (kt_scr0, kt_scr1, kt_scr2, kt_scr3)
    att_scr = (att_scr0, att_scr1, att_scr2, att_scr3)
    n_chunks = SEQ_TILE // CHUNK
    units = [(c, hd) for c in range(n_chunks) for hd in range(A_HEADS)]

    def rows_of(c):
        return slice(c * CHUNK, (c + 1) * CHUNK)

    def decay(c):
        f = lb + (1.0 - lb) * _sigmoid(proj_scr[rows_of(c), D:2 * D])
        logf = jnp.log(f)
        hi = logf.astype(BF16)
        lo = (logf - hi.astype(F32)).astype(BF16)
        g_scr[c % 2][...] = (jnp.dot(tril, hi, preferred_element_type=F32)
                             + jnp.dot(tril, lo, preferred_element_type=F32))
        k_scr[c % 2][...] = 1.0 - f

    def intra(n):
        c, hd = units[n]
        sl = slice(hd * A_HEAD_DIM, (hd + 1) * A_HEAD_DIM)
        gh = g_scr[c % 2][:, sl]
        g_last = gh[CHUNK - 1:CHUNK, :]
        k = k_scr[c % 2][:, sl]
        q_dec = (proj_scr[rows_of(c), sl] * jnp.exp(gh)).astype(BF16)
        k_inv = (k * jnp.exp(-gh)).astype(BF16)
        qd_scr[n % 4][...] = q_dec
        kt_scr[n % 4][...] = (k * jnp.exp(g_last - gh)).astype(BF16)
        att = lax.dot_general(q_dec, k_inv, (((1,), (1,)), ((), ())),
                              preferred_element_type=F32)
        att_scr[n % 4][...] = jnp.where(causal, att, 0.0).astype(BF16)

    def output(n):
        c, hd = units[n]
        sl = slice(hd * A_HEAD_DIM, (hd + 1) * A_HEAD_DIM)
        rows = rows_of(c)
        g_last = g_scr[c % 2][CHUNK - 1:CHUNK, sl]
        v = proj_scr[rows, 2 * D + hd * A_HEAD_DIM:2 * D + (hd + 1) * A_HEAD_DIM]
        st = state_scr[hd]
        o = (jnp.dot(att_scr[n % 4][...], v.astype(BF16), preferred_element_type=F32)
             + lax.dot_general(qd_scr[n % 4][...], st.astype(BF16), (((1,), (1,)), ((), ())),
                               preferred_element_type=F32))
        v_t = v.T.astype(BF16)
        state_scr[hd] = st * jnp.exp(g_last) + jnp.dot(v_t, kt_scr[n % 4][...],
                                                       preferred_element_type=F32)
        o = o * _rms_scale(o)
        og = proj_scr[rows, 3 * D + hd * A_HEAD_DIM:3 * D + (hd + 1) * A_HEAD_DIM]
        o = o * onorm[:, sl] * (og * _sigmoid(og))
        o_scr[rows, sl] = o.astype(BF16)

    decay(0)
    intra(0)
    intra(1)
    intra(2)
    for n, (c, hd) in enumerate(units):
        if hd == 0 and c + 1 < n_chunks:
            decay(c + 1)
        if n + 3 < len(units):
            intra(n + 3)
        output(n)
    out_ref[0] = x + jnp.dot(o_scr[...], wo_ref[...], preferred_element_type=F32)


def _mixer_a(x, g, w_in, lb, out_norm, w_o):
    b, s, _ = x.shape
    const = lambda bi, si: (0, 0)
    return pl.pallas_call(
        _mixer_a_kernel,
        name="mixer_a",
        grid=(b, s // SEQ_TILE),
        in_specs=[
            pl.BlockSpec((1, SEQ_TILE, D), lambda bi, si: (bi, si, 0)),
            pl.BlockSpec((1, D), const),
            pl.BlockSpec((D, 4 * D), const, pipeline_mode=pl.Buffered(1)),
            pl.BlockSpec((1, D), const),
            pl.BlockSpec((1, D), const),
            pl.BlockSpec((D, D), const, pipeline_mode=pl.Buffered(1)),
        ],
        out_specs=pl.BlockSpec((1, SEQ_TILE, D), lambda bi, si: (bi, si, 0)),
        out_shape=jax.ShapeDtypeStruct(x.shape, F32),
        scratch_shapes=[
            pltpu.VMEM((SEQ_TILE, 4 * D), F32),
            pltpu.VMEM((SEQ_TILE, D), BF16),
            pltpu.VMEM((A_HEADS, A_HEAD_DIM, A_HEAD_DIM), F32),
        ] + [pltpu.VMEM((CHUNK, D), F32)] * 4
          + [pltpu.VMEM((CHUNK, A_HEAD_DIM), BF16)] * 8
          + [pltpu.VMEM((CHUNK, CHUNK), BF16)] * 4,
        compiler_params=_params(2),
    )(x, g.reshape(1, D), w_in.astype(BF16), lb.reshape(1, D), out_norm.reshape(1, D),
      w_o.astype(BF16))


def _router_kernel(x_ref, g_ref, wr_ref, br_ref, ids_ref, gates_ref, counts_ref, cnt_scr,
                   before_scr):
    tm = ROUTER_TILE

    @pl.when(pl.program_id(0) == 0)
    def _():
        cnt_scr[...] = jnp.zeros_like(cnt_scr)
        tr = lax.broadcasted_iota(jnp.int32, (tm, tm), 0)
        tc = lax.broadcasted_iota(jnp.int32, (tm, tm), 1)
        before_scr[...] = (tr < tc).astype(BF16)

    x = x_ref[...]
    h = x * _rms_scale(x) * g_ref[...]
    h_hi = h.astype(BF16)
    h_lo = (h - h_hi.astype(F32)).astype(BF16)
    nt = (((1,), (1,)), ((), ()))
    both = lax.dot_general(wr_ref[...], h_hi, nt, preferred_element_type=F32)
    cross = lax.dot_general(wr_ref[0:ROUTER_ROWS], h_lo, nt, preferred_element_type=F32)
    logits = both[0:ROUTER_ROWS] + both[ROUTER_ROWS:] + cross + br_ref[...]
    el = logits[0:N_EXPERTS]
    gl = logits[N_EXPERTS:ROUTER_ROWS]
    grow = lax.broadcasted_iota(jnp.int32, gl.shape, 0)
    gl = jnp.where(grow < N_GROUPS, gl, -jnp.inf)
    gmax = jnp.max(gl, axis=0, keepdims=True)
    gsum = jnp.sum(jnp.exp(gl - gmax), axis=0, keepdims=True)
    grp_w = 1.0 / gsum
    gidx = jnp.min(jnp.where(gl == gmax, grow, N_GROUPS), axis=0, keepdims=True)

    erow = lax.broadcasted_iota(jnp.int32, el.shape, 0)
    masked = jnp.where((erow // EXPERTS_PER_GROUP) == gidx, el, -jnp.inf)
    top1 = jnp.max(masked, axis=0, keepdims=True)
    i1 = jnp.min(jnp.where(masked == top1, erow, N_EXPERTS), axis=0, keepdims=True)
    masked2 = jnp.where(erow == i1, -jnp.inf, masked)
    top2 = jnp.max(masked2, axis=0, keepdims=True)
    i2 = jnp.min(jnp.where(masked2 == top2, erow, N_EXPERTS), axis=0, keepdims=True)
    e2 = jnp.exp(top2 - top1)
    denom = 1.0 + e2
    g1 = grp_w * (1.0 / denom)
    g2 = grp_w * (e2 / denom)

    sel1 = erow == i1
    sel2 = erow == i2
    onehot = (sel1 | sel2).astype(BF16)
    prefix = jnp.dot(onehot, before_scr[...], preferred_element_type=F32) + cnt_scr[...]
    r1 = jnp.sum(jnp.where(sel1, prefix, 0.0), axis=0, keepdims=True)
    r2 = jnp.sum(jnp.where(sel2, prefix, 0.0), axis=0, keepdims=True)
    cnt_scr[...] += jnp.sum(onehot.astype(F32), axis=1, keepdims=True)

    zi = jnp.zeros((SUBLANES - 2 * TOP_K, tm), jnp.int32)
    ids_ref[0] = jnp.concatenate(
        [i1, i2, r1.astype(jnp.int32), r2.astype(jnp.int32), zi], axis=0)
    gates_ref[0] = jnp.concatenate([g1, g2, jnp.zeros((SUBLANES - TOP_K, tm), F32)], axis=0)
    counts_ref[...] = jnp.broadcast_to(cnt_scr[...], counts_ref.shape).astype(jnp.int32)


def _router(x2d, g, w_group, b_group, w_expert, b_expert):
    t = x2d.shape[0]
    nt = t // ROUTER_TILE
    pad = ROUTER_ROWS - N_EXPERTS - N_GROUPS
    wr = jnp.concatenate([w_expert.T, w_group.T, jnp.zeros((pad, D), F32)], axis=0)
    wr_hi = wr.astype(BF16)
    wr_lo = (wr - wr_hi.astype(F32)).astype(BF16)
    br = jnp.concatenate([b_expert, b_group, jnp.zeros((pad,), F32)]).reshape(ROUTER_ROWS, 1)
    const = lambda i: (0, 0)
    return pl.pallas_call(
        _router_kernel,
        name="router",
        grid=(nt,),
        in_specs=[
            pl.BlockSpec((ROUTER_TILE, D), lambda i: (i, 0)),
            pl.BlockSpec((1, D), const),
            pl.BlockSpec((2 * ROUTER_ROWS, D), const),
            pl.BlockSpec((ROUTER_ROWS, 1), const),
        ],
        out_specs=[
            pl.BlockSpec((1, SUBLANES, ROUTER_TILE), lambda i: (i, 0, 0)),
            pl.BlockSpec((1, SUBLANES, ROUTER_TILE), lambda i: (i, 0, 0)),
            pl.BlockSpec((N_EXPERTS, LANES), const),
        ],
        out_shape=[
            jax.ShapeDtypeStruct((nt, SUBLANES, ROUTER_TILE), jnp.int32),
            jax.ShapeDtypeStruct((nt, SUBLANES, ROUTER_TILE), F32),
            jax.ShapeDtypeStruct((N_EXPERTS, LANES), jnp.int32),
        ],
        scratch_shapes=[pltpu.VMEM((N_EXPERTS, 1), F32),
                        pltpu.VMEM((ROUTER_TILE, ROUTER_TILE), BF16)],
        compiler_params=_params(1),
    )(x2d, g.reshape(1, D), jnp.concatenate([wr_hi, wr_lo], axis=0), br)


SLOTS_PER_TILE = SUBLANES * TOP_K


def _issue_tile_rows(dest_ref, which, j, make_copy):
    for u in range(SUBLANES):
        for k in range(TOP_K):
            dest = dest_ref[which, j, k * SUBLANES + u]
            make_copy(j, u, k, dest).start(priority=(u * TOP_K + k) % 2)


def _issue_rows(n_rows, dest_ref, which, make_copy):
    def body(j, c):
        _issue_tile_rows(dest_ref, which, j, make_copy)
        return c

    lax.fori_loop(0, n_rows // SUBLANES, body, 0)


def _issue_rows_inline(n_rows, dest_ref, which, make_copy):
    for j in range(n_rows // SUBLANES):
        _issue_tile_rows(dest_ref, which, j, make_copy)


def _pack_bf16_pairs(x):
    half = x.shape[1] // 2
    lo = lax.bitcast_convert_type(x[:, :half].astype(BF16).astype(F32), U32)
    hi = lax.bitcast_convert_type(x[:, half:].astype(BF16).astype(F32), U32)
    return (hi & jnp.uint32(0xFFFF0000)) | (lo >> 16)


def _unpack_bf16_pairs(u):
    lo = lax.bitcast_convert_type(u << 16, F32).astype(BF16)
    hi = lax.bitcast_convert_type(u & jnp.uint32(0xFFFF0000), F32).astype(BF16)
    return lo, hi


def _tile_rows(x):
    return x.reshape(x.shape[0] // SUBLANES, SUBLANES, x.shape[1])


def _wait_rows(buf_ref, sem):
    pltpu.make_async_copy(buf_ref, buf_ref, sem).wait()


def _dispatch_kernel(zrow_ref, dest_ref, x_ref, g_ref, xs_ref, hbuf, zbuf, zsem, sem):
    @pl.when(pl.program_id(0) == 0)
    def _():
        zbuf[...] = jnp.zeros_like(zbuf)

        def zcopy(e):
            return pltpu.make_async_copy(zbuf, xs_ref.at[pl.ds(zrow_ref[e], MOE_BLOCK)], zsem)

        def zstart(e, c):
            @pl.when(zrow_ref[e] >= 0)
            def _():
                zcopy(e).start()
            return c

        def zwait(e, c):
            @pl.when(zrow_ref[e] >= 0)
            def _():
                zcopy(e).wait()
            return c

        lax.fori_loop(0, 2 * N_EXPERTS, zstart, 0)
        lax.fori_loop(0, 2 * N_EXPERTS, zwait, 0)

    i = pl.program_id(0)
    slot = i % 2
    x = x_ref[...]
    hbuf[slot] = _tile_rows(_pack_bf16_pairs(x * _rms_scale(x) * g_ref[...]))

    def row_copy(tile, sub, k, dest):
        return pltpu.make_async_copy(hbuf.at[slot, tile, pl.ds(sub, 1), :], xs_ref.at[dest],
                                     sem.at[slot])

    _issue_rows(ROUTER_TILE, dest_ref, 0, row_copy)

    def drain(which):
        for _ in range(TOP_K):
            _wait_rows(hbuf.at[which], sem.at[which])

    @pl.when(i > 0)
    def _():
        drain(1 - slot)

    @pl.when(i == pl.num_programs(0) - 1)
    def _():
        drain(slot)


def _dispatch(x2d, g, dest, zrow, n_pad):
    t = x2d.shape[0]
    nt = t // ROUTER_TILE
    return pl.pallas_call(
        _dispatch_kernel,
        name="dispatch",
        grid_spec=pltpu.PrefetchScalarGridSpec(
            num_scalar_prefetch=1,
            grid=(nt,),
            in_specs=[
                pl.BlockSpec((1, ROUTER_TILE // SUBLANES, SLOTS_PER_TILE),
                             lambda i, z: (i, 0, 0), memory_space=pltpu.SMEM),
                pl.BlockSpec((ROUTER_TILE, D), lambda i, z: (i, 0)),
                pl.BlockSpec((1, D), lambda i, z: (0, 0)),
            ],
            out_specs=pl.BlockSpec(memory_space=pl.ANY),
            scratch_shapes=[
                pltpu.VMEM((2, ROUTER_TILE // SUBLANES, SUBLANES, D // 2), U32),
                pltpu.VMEM((MOE_BLOCK, 1, D // 2), U32),
                pltpu.SemaphoreType.DMA(()),
                pltpu.SemaphoreType.DMA((2,)),
            ],
        ),
        out_shape=jax.ShapeDtypeStruct((n_pad, 1, D // 2), U32),
        compiler_params=_params(1),
    )(zrow, dest, x2d, g.reshape(1, D))


def _experts_kernel(be_ref, nxt_ref, nu_ref, xs_ref, wg_ref, wu_ref, wd_ref, y_ref,
                    wg_f, wu_f, wd_f, wg_b, wu_b, wd_b, xbuf, ybuf, w_sem, in_sem, out_sem, *, layer):
    i = pl.program_id(0)
    n_used = nu_ref[0]
    slot = i % 2

    def block_rows(ref, blk):
        return ref.at[pl.ds(pl.multiple_of(blk * MOE_BLOCK, MOE_BLOCK), MOE_BLOCK), 0]

    def fetch(blk, into):
        return pltpu.make_async_copy(block_rows(xs_ref, blk), xbuf.at[into], in_sem.at[into])

    def write_back(blk, from_):
        return pltpu.make_async_copy(ybuf.at[from_], block_rows(y_ref, blk), out_sem.at[from_])

    def weight_copies(e):
        return [pltpu.make_async_copy(src.at[layer, e], dst, w_sem.at[n])
                for n, (src, dst) in enumerate(((wg_ref, wg_f), (wu_ref, wu_f), (wd_ref, wd_f)))]

    @pl.when(i == 0)
    def _():
        for c in weight_copies(be_ref[0]):
            c.start()
        fetch(0, 0).start()

    @pl.when(i + 1 < n_used)
    def _():
        fetch(i + 1, 1 - slot).start()

    prev = be_ref[jnp.maximum(i - 1, 0)]

    @pl.when((i < n_used) & ((i == 0) | (be_ref[i] != prev)))
    def _():
        for c in weight_copies(be_ref[i]):
            c.wait()
        wg_b[...] = wg_f[...].astype(BF16)
        wu_b[...] = wu_f[...].astype(BF16)
        wd_b[...] = wd_f[...].astype(BF16)

        @pl.when(nxt_ref[i] >= 0)
        def _():
            for c in weight_copies(nxt_ref[i]):
                c.start()

    @pl.when(i >= 2)
    def _():
        write_back(i - 2, slot).wait()

    @pl.when(i < n_used)
    def _():
        fetch(i, slot).wait()
        h_lo, h_hi = _unpack_bf16_pairs(xbuf[slot])
        half = D // 2
        a = (jnp.dot(h_lo, wg_b[:half], preferred_element_type=F32)
             + jnp.dot(h_hi, wg_b[half:], preferred_element_type=F32))
        u = (jnp.dot(h_lo, wu_b[:half], preferred_element_type=F32)
             + jnp.dot(h_hi, wu_b[half:], preferred_element_type=F32))
        hid = (a * _sigmoid(a) * u).astype(BF16)
        ybuf[slot] = jnp.dot(hid, wd_b[...], preferred_element_type=F32)

    @pl.when(i >= n_used)
    def _():
        ybuf[slot] = jnp.zeros((MOE_BLOCK, D), F32)

    write_back(i, slot).start()

    @pl.when(i == pl.num_programs(0) - 1)
    def _():
        write_back(i - 1, 1 - slot).wait()
        write_back(i, slot).wait()


def _experts(xs, block_e, next_e, n_used, layer, w_gate, w_up, w_down):
    n_pad = xs.shape[0]
    n_blocks = n_pad // MOE_BLOCK
    hbm = pl.BlockSpec(memory_space=pl.ANY)
    return pl.pallas_call(
        functools.partial(_experts_kernel, layer=layer),
        name="experts",
        grid_spec=pltpu.PrefetchScalarGridSpec(
            num_scalar_prefetch=3,
            grid=(n_blocks,),
            in_specs=[hbm, hbm, hbm, hbm],
            out_specs=hbm,
            scratch_shapes=[
                pltpu.VMEM((D, D_EXPERT), F32),
                pltpu.VMEM((D, D_EXPERT), F32),
                pltpu.VMEM((D_EXPERT, D), F32),
                pltpu.VMEM((D, D_EXPERT), BF16),
                pltpu.VMEM((D, D_EXPERT), BF16),
                pltpu.VMEM((D_EXPERT, D), BF16),
                pltpu.VMEM((2, MOE_BLOCK, D // 2), U32),
                pltpu.VMEM((2, MOE_BLOCK, D), F32),
                pltpu.SemaphoreType.DMA((3,)),
                pltpu.SemaphoreType.DMA((2,)),
                pltpu.SemaphoreType.DMA((2,)),
            ],
        ),
        out_shape=jax.ShapeDtypeStruct((n_pad, 1, D), F32),
        compiler_params=_params(1),
    )(block_e, next_e, n_used, xs, w_gate, w_up, w_down)


def _combine_kernel(*refs, final, project):
    (dest_ref, dest_next_ref, x_ref, gate_ref, y_ref, p_ref, gple_ref, wp_ref, wg_ref,
     gfin_ref) = refs[:10]
    if project:
        gq_ref, gkv_ref, wq_ref, wkv_ref, out_ref, qt_ref, k_ref, vt_ref = refs[10:18]
        ybuf0, ybuf1, sem = refs[18:]
    else:
        out_ref, ybuf0, ybuf1, sem = refs[10:]
    i = pl.program_id(0)
    ybuf = (ybuf0, ybuf1)

    def row_copy(into):
        def make(tile, sub, k, dest):
            return pltpu.make_async_copy(y_ref.at[dest], ybuf[into].at[k, tile, pl.ds(sub, 1), :],
                                         sem.at[into])
        return make

    def combine(half, slot):
        rows = slice(half * COMBINE_TILE, (half + 1) * COMBINE_TILE)
        gates = gate_ref[half].T
        y0 = ybuf[slot][0].reshape(COMBINE_TILE, D)
        y1 = ybuf[slot][1].reshape(COMBINE_TILE, D)
        x = x_ref[rows, :] + gates[:, 0:1] * y0 + gates[:, 1:2] * y1
        h = (x * _rms_scale(x) * gple_ref[...]).astype(BF16)
        gate = _sigmoid(jnp.dot(h, wg_ref[...], preferred_element_type=F32))
        proj = jnp.dot(p_ref[0, rows, :].astype(BF16), wp_ref[...], preferred_element_type=F32)
        x = x + proj * gate
        if final:
            x = x * _rms_scale(x) * gfin_ref[...]
        out_ref[rows, :] = x
        if project:
            q, k, v = _qkv_rows(x, gq_ref, gkv_ref, wq_ref, wkv_ref)
            qt_ref[0, :, rows] = q.T.astype(BF16)
            k_ref[0, rows, :] = k.astype(BF16)
            vt_ref[0, :, rows] = v.T.astype(BF16)

    def wait(slot):
        for k in range(TOP_K):
            _wait_rows(ybuf[slot].at[k], sem.at[slot])

    @pl.when(i == 0)
    def _():
        _issue_rows(COMBINE_TILE, dest_ref, 0, row_copy(0))

    wait(0)
    _issue_rows_inline(COMBINE_TILE, dest_ref, 1, row_copy(1))
    combine(0, 0)
    wait(1)
    _issue_rows_inline(COMBINE_TILE, dest_next_ref, 0, row_copy(0))
    combine(1, 1)

    @pl.when(i == pl.num_programs(0) - 1)
    def _():
        wait(0)


def _qkv_rows(x, gq_ref, gkv_ref, wq_ref, wkv_ref):
    xn = x * _rms_scale(x)
    hq = (xn * gq_ref[...]).astype(BF16)
    hkv = (xn * gkv_ref[...]).astype(BF16)
    q = jnp.dot(hq, wq_ref[...], preferred_element_type=F32) * (ATTN_SCALE * LOG2E)
    kv = jnp.dot(hkv, wkv_ref[...], preferred_element_type=F32)
    return q, kv[:, :D], kv[:, D:]


def _combine_ple(x2d, dest, gates, y, p3d, layer, g_ple, w_proj, w_gate, g_final, final, qkv=None):
    assert COMBINE_TILE == ROUTER_TILE
    t = x2d.shape[0]
    step = 2 * COMBINE_TILE
    nt = t // step
    dest_blk = (2, COMBINE_TILE // SUBLANES, SLOTS_PER_TILE)
    const = lambda i: (0, 0)
    resident = functools.partial(pl.BlockSpec, index_map=const, pipeline_mode=pl.Buffered(1))
    in_specs = [
        pl.BlockSpec(dest_blk, lambda i: (i, 0, 0), memory_space=pltpu.SMEM),
        pl.BlockSpec(dest_blk, lambda i: (jnp.minimum(i + 1, nt - 1), 0, 0),
                     memory_space=pltpu.SMEM),
        pl.BlockSpec((step, D), lambda i: (i, 0)),
        pl.BlockSpec((2, SUBLANES, COMBINE_TILE), lambda i: (i, 0, 0)),
        pl.BlockSpec(memory_space=pl.ANY),
        pl.BlockSpec((1, step, D_PLE), lambda i: (layer, i, 0)),
        pl.BlockSpec((1, D), const),
        resident((D_PLE, D)),
        resident((D, D)),
        pl.BlockSpec((1, D), const),
    ]
    args = [dest, dest, x2d, gates, y, p3d, g_ple.reshape(1, D), w_proj.astype(BF16),
            w_gate.astype(BF16), g_final.reshape(1, D)]
    out_specs = [pl.BlockSpec((step, D), lambda i: (i, 0))]
    out_shape = [jax.ShapeDtypeStruct((t, D), F32)]
    if qkv is not None:
        seq, g_q, g_kv, w_q, w_kv = qkv
        per_seq = seq // step
        in_specs += [pl.BlockSpec((1, D), const), pl.BlockSpec((1, D), const),
                     resident((D, D)), resident((D, 2 * D))]
        args += [g_q.reshape(1, D), g_kv.reshape(1, D), w_q.astype(BF16), w_kv.astype(BF16)]
        feat_major = pl.BlockSpec((1, D, step), lambda i: (i // per_seq, 0, i % per_seq))
        row_major = pl.BlockSpec((1, step, D), lambda i: (i // per_seq, i % per_seq, 0))
        out_specs += [feat_major, row_major, feat_major]
        out_shape += [jax.ShapeDtypeStruct((t // seq, D, seq), BF16),
                      jax.ShapeDtypeStruct((t // seq, seq, D), BF16),
                      jax.ShapeDtypeStruct((t // seq, D, seq), BF16)]
    outs = pl.pallas_call(
        functools.partial(_combine_kernel, final=final, project=qkv is not None),
        name="combine_final" if final else "combine",
        grid=(nt,),
        in_specs=in_specs,
        out_specs=out_specs,
        out_shape=out_shape,
        scratch_shapes=[
            pltpu.VMEM((TOP_K, COMBINE_TILE // SUBLANES, SUBLANES, D), F32),
            pltpu.VMEM((TOP_K, COMBINE_TILE // SUBLANES, SUBLANES, D), F32),
            pltpu.SemaphoreType.DMA((2,)),
        ],
        compiler_params=_params(1),
    )(*args)
    return outs[0] if qkv is None else tuple(outs)


def _moe_ple(x, p, layer, norm_ffn, w_group, b_group, w_expert, b_expert, w_gate, w_up, w_down,
             norm_ple, ple_w_proj, ple_w_gate, final_norm, final, qkv=None):
    b, s, _ = x.shape
    t = b * s
    x2d = x.reshape(t, D)
    ids, gates, counts = _router(x2d, norm_ffn, w_group, b_group, w_expert, b_expert)

    counts = counts[:, 0]
    padded = (counts + MOE_BLOCK - 1) // MOE_BLOCK * MOE_BLOCK
    pad_end = jnp.cumsum(padded)
    pad_start = pad_end - padded
    n_blocks = t * TOP_K // MOE_BLOCK + N_EXPERTS
    n_pad = n_blocks * MOE_BLOCK
    e = ids[:, 0:TOP_K, :]
    r = ids[:, TOP_K:2 * TOP_K, :]
    experts = jnp.arange(N_EXPERTS, dtype=jnp.int32)
    dest = r + jnp.sum(jnp.where(e[..., None] == experts, pad_start, 0), axis=-1)
    nt = t // ROUTER_TILE
    dest = dest.reshape(nt, TOP_K, ROUTER_TILE // SUBLANES, SUBLANES).transpose(0, 2, 1, 3)
    dest = dest.reshape(nt, ROUTER_TILE // SUBLANES, SLOTS_PER_TILE).astype(jnp.int32)
    block_row = jnp.arange(n_blocks, dtype=jnp.int32) * MOE_BLOCK
    block_e = jnp.minimum(jnp.sum(pad_end[None, :] <= block_row[:, None], axis=-1),
                          N_EXPERTS - 1).astype(jnp.int32)
    n_used = (pad_end[-1:] // MOE_BLOCK).astype(jnp.int32)
    tail = (n_used[0] + jnp.arange(N_EXPERTS, dtype=jnp.int32)) * MOE_BLOCK
    zrow = jnp.concatenate([jnp.where(padded > 0, pad_end - MOE_BLOCK, -1),
                            jnp.where(tail < n_pad, tail, -1)]).astype(jnp.int32)

    xs = _dispatch(x2d, norm_ffn, dest, zrow, n_pad)
    later = jnp.where((experts[None, :] > block_e[:, None]) & (padded[None, :] > 0), experts[None, :],
                      N_EXPERTS)
    next_e = jnp.min(later, axis=-1)
    next_e = jnp.where(next_e < N_EXPERTS, next_e, -1).astype(jnp.int32)
    y = _experts(xs, block_e, next_e, n_used, layer, w_gate, w_up, w_down)
    out = _combine_ple(x2d, dest, gates, y, p.reshape(-1, t, D_PLE), layer,
                       norm_ple, ple_w_proj, ple_w_gate, final_norm, final,
                       None if qkv is None else (s,) + tuple(qkv))
    if qkv is None:
        return out.reshape(b, s, D)
    return (out[0].reshape(b, s, D),) + out[1:]


Q_GROUP = 2 * CHUNK
G_BAND = BAND + CHUNK
PAIR = 2 * B_HEAD_DIM
ONES_ROWS = BF16_SUBLANES


def _attn_kernel(qt_ref, kp_ref, kc_ref, vtp_ref, vtc_ref, bias_ref, x_ref, wo_ref, out_ref, o_scr,
                 s_scr0, s_scr1, s_scr2, p_scr0, p_scr1, p_scr2):
    drow = lax.broadcasted_iota(jnp.int32, (PAIR, Q_GROUP), 0)
    first_head = drow < B_HEAD_DIM
    s_scr = (s_scr0, s_scr1, s_scr2)
    p_scr = (p_scr0, p_scr1, p_scr2)

    def attend(first_tile):
        units = [(g, pr) for g in range(SEQ_TILE // Q_GROUP) for pr in range(B_HEADS // 2)]

        def geometry(g):
            w0 = g * Q_GROUP
            n_prev = SEQ_TILE - w0
            return w0, n_prev, G_BAND - n_prev

        def keys(g):
            return slice(geometry(g)[1], None) if first_tile else slice(None)

        def scores(unit, s_ref):
            g, pr = unit
            w0, n_prev, n_cur = geometry(g)
            feat = slice(pr * PAIR, (pr + 1) * PAIR)
            qt = qt_ref[0, feat, w0:w0 + Q_GROUP]
            zero = jnp.zeros_like(qt)
            qblk = jnp.concatenate([jnp.where(first_head, qt, zero),
                                    jnp.where(first_head, zero, qt)], axis=1)
            if first_tile:
                kb = kc_ref[0, :n_cur, feat]
            else:
                kb = jnp.concatenate([kp_ref[0, w0:, feat], kc_ref[0, :n_cur, feat]], axis=0)
            s_ref[keys(g), :] = (jnp.dot(kb, qblk, preferred_element_type=F32)
                                 + bias_ref[pr, keys(g), :])

        def weights(unit, s_ref, p_ref):
            g, _ = unit
            s = s_ref[keys(g), :]
            m = jnp.max(s, axis=0, keepdims=True)
            p_ref[keys(g), :] = jnp.exp2(s - m).astype(BF16)

        def values(unit, p_ref):
            g, pr = unit
            w0, n_prev, n_cur = geometry(g)
            feat = slice(pr * PAIR, (pr + 1) * PAIR)
            if first_tile:
                vt = vtc_ref[0, feat, :n_cur]
            else:
                vt = jnp.concatenate([vtp_ref[0, feat, w0:], vtc_ref[0, feat, :n_cur]], axis=1)
            ones = jnp.ones((ONES_ROWS, vt.shape[1]), BF16)
            ot = jnp.dot(jnp.concatenate([vt, ones], axis=0), p_ref[keys(g), :],
                         preferred_element_type=F32)
            inv = 1.0 / ot[PAIR:PAIR + 1, :]
            ot = jnp.where(first_head, ot[:PAIR, :Q_GROUP] * inv[:, :Q_GROUP],
                           ot[:PAIR, Q_GROUP:] * inv[:, Q_GROUP:])
            o_scr[w0:w0 + Q_GROUP, feat] = ot.T.astype(BF16)

        n_units = len(units)
        scores(units[0], s_scr[0])
        scores(units[1], s_scr[1])
        scores(units[2], s_scr[2])
        weights(units[0], s_scr[0], p_scr[0])
        weights(units[1], s_scr[1], p_scr[1])
        for n, unit in enumerate(units):
            if n + 3 < n_units:
                scores(units[n + 3], s_scr[n % 3])
            if n + 2 < n_units:
                weights(units[n + 2], s_scr[(n + 2) % 3], p_scr[(n + 2) % 3])
            values(unit, p_scr[n % 3])

    @pl.when(pl.program_id(1) == 0)
    def _():
        attend(True)

    @pl.when(pl.program_id(1) > 0)
    def _():
        attend(False)

    out_ref[0] = x_ref[0] + jnp.dot(o_scr[...], wo_ref[...], preferred_element_type=F32)


def _attn(x, qt, k, vt, bias_t, w_o):
    b, s, _ = x.shape
    cur = lambda bi, si: (bi, si, 0)
    prev = lambda bi, si: (bi, jnp.maximum(si - 1, 0), 0)
    cur_t = lambda bi, si: (bi, 0, si)
    prev_t = lambda bi, si: (bi, 0, jnp.maximum(si - 1, 0))
    blk = (1, SEQ_TILE, D)
    blk_t = (1, D, SEQ_TILE)
    return pl.pallas_call(
        _attn_kernel,
        name="attn",
        grid=(b, s // SEQ_TILE),
        in_specs=[
            pl.BlockSpec(blk_t, cur_t),
            pl.BlockSpec(blk, prev),
            pl.BlockSpec(blk, cur),
            pl.BlockSpec(blk_t, prev_t),
            pl.BlockSpec(blk_t, cur_t),
            pl.BlockSpec((B_HEADS // 2, G_BAND, 2 * Q_GROUP), lambda bi, si: (0, 0, 0),
                         pipeline_mode=pl.Buffered(1)),
            pl.BlockSpec(blk, cur),
            pl.BlockSpec((D, D), lambda bi, si: (0, 0), pipeline_mode=pl.Buffered(1)),
        ],
        out_specs=pl.BlockSpec(blk, cur),
        out_shape=jax.ShapeDtypeStruct(x.shape, F32),
        scratch_shapes=[pltpu.VMEM((SEQ_TILE, D), BF16)]
        + [pltpu.VMEM((G_BAND, 2 * Q_GROUP), F32)] * 3
        + [pltpu.VMEM((G_BAND, 2 * Q_GROUP), BF16)] * 3,
        compiler_params=_params(2),
    )(qt, k, k, vt, vt, bias_t, x, w_o.astype(BF16))


def _group_bias(table):
    band = _band_bias(table) * LOG2E
    pad = lambda lo, hi: jnp.pad(band, ((0, 0), (0, 0), (lo, hi)), constant_values=NEG_INF)
    both = jnp.concatenate([pad(0, CHUNK), pad(CHUNK, 0)], axis=1)
    both = both.reshape(B_HEADS // 2, 2, Q_GROUP, G_BAND)
    return both.transpose(0, 3, 1, 2).reshape(B_HEADS // 2, G_BAND, 2 * Q_GROUP)


def _band_bias(table):
    n_rel = REL_MAX - REL_MIN + 1
    span = BAND + CHUNK - 1
    head = jnp.broadcast_to(table[:, n_rel - 1:], (table.shape[0], span - n_rel))
    ext = jnp.concatenate([head, table[:, ::-1]], axis=1)
    rows = [ext[:, CHUNK - 1 - q:CHUNK - 1 - q + BAND] for q in range(CHUNK)]
    return jnp.stack(rows, axis=1)


def kernel(x, p, a_w_in, a_lb_logits, a_out_norm, a_w_o, kv_norm, w_kv, b_w_q, b_rel_bias, b_w_o,
           norm_mix, norm_ffn, norm_ple, moe_w_group, moe_b_group, moe_w_expert, moe_b_expert,
           moe_w_gate, moe_w_up, moe_w_down, ple_w_proj, ple_w_gate, final_norm):
    b, s, _ = x.shape
    lower_bounds = jnp.cumsum(jax.nn.softmax(a_lb_logits.astype(F32), axis=0), axis=0)

    def moe(xi, i, final, qkv=None):
        return _moe_ple(xi, p, i, norm_ffn[i], moe_w_group[i], moe_b_group[i], moe_w_expert[i],
                        moe_b_expert[i], moe_w_gate, moe_w_up, moe_w_down, norm_ple[i],
                        ple_w_proj[i], ple_w_gate[i], final_norm, final, qkv)

    x = _mixer_a(x, norm_mix[0], a_w_in[0], lower_bounds[0], a_out_norm[0], a_w_o[0])
    x, qt, k, vt = moe(x, 0, False, (norm_mix[1], kv_norm, b_w_q[0], w_kv))

    x = _attn(x, qt, k, vt, _group_bias(b_rel_bias[0].astype(F32)), b_w_o[0])
    x = moe(x, 1, True)
    return x
```

```python
import functools

import jax
import jax.numpy as jnp
from jax import lax
from jax.experimental import pallas as pl
from jax.experimental.pallas import tpu as pltpu

F32 = jnp.float32
BF16 = jnp.bfloat16
U32 = jnp.uint32

D = 1024
CHUNK = 64
A_HEADS = 8
A_HEAD_DIM = 128
B_HEADS = 16
B_HEAD_DIM = 64
LEFT_CHUNKS = 8
BAND = (LEFT_CHUNKS + 1) * CHUNK
REL_MIN = -(CHUNK - 1)
REL_MAX = 256
ATTN_SCALE = B_HEAD_DIM ** -0.5
N_GROUPS = 4
EXPERTS_PER_GROUP = 8
N_EXPERTS = 32
TOP_K = 2
D_EXPERT = 512
MOE_BLOCK = 512
D_PLE = 256
EPS = 1e-6
NEG_INF = -1e30
LOG2E = 1.4426950408889634

SUBLANES = 8
LANES = 128
BF16_SUBLANES = 16
V7X_VMEM_BYTES = 64 * 1024 * 1024

SEQ_TILE = 512
ROUTER_TILE = 512
COMBINE_TILE = 512
ROUTER_ROWS = -(-(N_EXPERTS + N_GROUPS) // BF16_SUBLANES) * BF16_SUBLANES
VMEM_LIMIT = V7X_VMEM_BYTES - 8 * 1024 * 1024


def _params(n_axes, vmem=VMEM_LIMIT):
    return pltpu.CompilerParams(dimension_semantics=("arbitrary",) * n_axes,
                                vmem_limit_bytes=vmem)


def _rms_scale(x):
    return lax.rsqrt(jnp.mean(x * x, axis=-1, keepdims=True) + EPS)


def _sigmoid(x):
    return 0.5 * jnp.tanh(0.5 * x) + 0.5


def _mixer_a_kernel(x_ref, g_ref, win_ref, lb_ref, onorm_ref, wo_ref, out_ref,
                    proj_scr, o_scr, state_scr, g_scr0, g_scr1, k_scr0, k_scr1,
                    qd_scr0, qd_scr1, qd_scr2, qd_scr3, kt_scr0, kt_scr1, kt_scr2, kt_scr3,
                    att_scr0, att_scr1, att_scr2, att_scr3):
    @pl.when(pl.program_id(1) == 0)
    def _():
        state_scr[...] = jnp.zeros_like(state_scr)

    x = x_ref[0]
    h = (x * _rms_scale(x) * g_ref[...]).astype(BF16)
    proj_scr[...] = jnp.dot(h, win_ref[...], preferred_element_type=F32)

    row = lax.broadcasted_iota(jnp.int32, (CHUNK, CHUNK), 0)
    col = lax.broadcasted_iota(jnp.int32, (CHUNK, CHUNK), 1)
    causal = row >= col
    tril = causal.astype(BF16)
    lb = lb_ref[...]
    onorm = onorm_ref[...]

    g_scr, k_scr = (g_scr0, g_scr1), (k_scr0, k_scr1)
    qd_scr, kt_scr = (qd_scr0, qd_scr1, qd_scr2, qd_scr3), (kt_scr0, kt_scr1, kt_scr2, kt_scr3)
    att_scr = (att_scr0, att_scr1, att_scr2, att_scr3)
    n_chunks = SEQ_TILE // CHUNK
    units = [(c, hd) for c in range(n_chunks) for hd in range(A_HEADS)]

    def rows_of(c):
        return slice(c * CHUNK, (c + 1) * CHUNK)

    def decay(c):
        f = lb + (1.0 - lb) * _sigmoid(proj_scr[rows_of(c), D:2 * D])
        logf = jnp.log(f)
        hi = logf.astype(BF16)
        lo = (logf - hi.astype(F32)).astype(BF16)
        g_scr[c % 2][...] = (jnp.dot(tril, hi, preferred_element_type=F32)
                             + jnp.dot(tril, lo, preferred_element_type=F32))
        k_scr[c % 2][...] = 1.0 - f

    def intra(n):
        c, hd = units[n]
        sl = slice(hd * A_HEAD_DIM, (hd + 1) * A_HEAD_DIM)
        gh = g_scr[c % 2][:, sl]
        g_last = gh[CHUNK - 1:CHUNK, :]
        k = k_scr[c % 2][:, sl]
        q_dec = (proj_scr[rows_of(c), sl] * jnp.exp(gh)).astype(BF16)
        k_inv = k * jnp.exp(-gh)
        qd_scr[n % 4][...] = q_dec
        kt_scr[n % 4][...] = (k_inv * jnp.exp(g_last)).astype(BF16)
        k_inv = k_inv.astype(BF16)
        att = lax.dot_general(q_dec, k_inv, (((1,), (1,)), ((), ())),
                              preferred_element_type=F32)
        att_scr[n % 4][...] = jnp.where(causal, att, 0.0).astype(BF16)

    def output(n):
        c, hd = units[n]
        sl = slice(hd * A_HEAD_DIM, (hd + 1) * A_HEAD_DIM)
        rows = rows_of(c)
        g_last = g_scr[c % 2][CHUNK - 1:CHUNK, sl]
        v = proj_scr[rows, 2 * D + hd * A_HEAD_DIM:2 * D + (hd + 1) * A_HEAD_DIM]
        st = state_scr[hd]
        o = (jnp.dot(att_scr[n % 4][...], v.astype(BF16), preferred_element_type=F32)
             + lax.dot_general(qd_scr[n % 4][...], st.astype(BF16), (((1,), (1,)), ((), ())),
                               preferred_element_type=F32))
        v_t = v.T.astype(BF16)
        state_scr[hd] = st * jnp.exp(g_last) + jnp.dot(v_t, kt_scr[n % 4][...],
                                                       preferred_element_type=F32)
        o = o * _rms_scale(o)
        og = proj_scr[rows, 3 * D + hd * A_HEAD_DIM:3 * D + (hd + 1) * A_HEAD_DIM]
        o = o * onorm[:, sl] * (og * _sigmoid(og))
        o_scr[rows, sl] = o.astype(BF16)

    decay(0)
    intra(0)
    intra(1)
    intra(2)
    for n, (c, hd) in enumerate(units):
        if hd == 0 and c + 1 < n_chunks:
            decay(c + 1)
        if n + 3 < len(units):
            intra(n + 3)
        output(n)
    out_ref[0] = x + jnp.dot(o_scr[...], wo_ref[...], preferred_element_type=F32)


def _mixer_a(x, g, w_in, lb, out_norm, w_o):
    b, s, _ = x.shape
    const = lambda bi, si: (0, 0)
    return pl.pallas_call(
        _mixer_a_kernel,
        name="mixer_a",
        grid=(b, s // SEQ_TILE),
        in_specs=[
            pl.BlockSpec((1, SEQ_TILE, D), lambda bi, si: (bi, si, 0)),
            pl.BlockSpec((1, D), const),
            pl.BlockSpec((D, 4 * D), const, pipeline_mode=pl.Buffered(1)),
            pl.BlockSpec((1, D), const),
            pl.BlockSpec((1, D), const),
            pl.BlockSpec((D, D), const, pipeline_mode=pl.Buffered(1)),
        ],
        out_specs=pl.BlockSpec((1, SEQ_TILE, D), lambda bi, si: (bi, si, 0)),
        out_shape=jax.ShapeDtypeStruct(x.shape, F32),
        scratch_shapes=[
            pltpu.VMEM((SEQ_TILE, 4 * D), F32),
            pltpu.VMEM((SEQ_TILE, D), BF16),
            pltpu.VMEM((A_HEADS, A_HEAD_DIM, A_HEAD_DIM), F32),
        ] + [pltpu.VMEM((CHUNK, D), F32)] * 4
          + [pltpu.VMEM((CHUNK, A_HEAD_DIM), BF16)] * 8
          + [pltpu.VMEM((CHUNK, CHUNK), BF16)] * 4,
        compiler_params=_params(2),
    )(x, g.reshape(1, D), w_in.astype(BF16), lb.reshape(1, D), out_norm.reshape(1, D),
      w_o.astype(BF16))


def _router_kernel(x_ref, g_ref, wr_ref, br_ref, ids_ref, gates_ref, counts_ref, cnt_scr,
                   before_scr):
    tm = ROUTER_TILE

    @pl.when(pl.program_id(0) == 0)
    def _():
        cnt_scr[...] = jnp.zeros_like(cnt_scr)
        tr = lax.broadcasted_iota(jnp.int32, (tm, tm), 0)
        tc = lax.broadcasted_iota(jnp.int32, (tm, tm), 1)
        before_scr[...] = (tr < tc).astype(BF16)

    x = x_ref[...]
    h = x * _rms_scale(x) * g_ref[...]
    h_hi = h.astype(BF16)
    h_lo = (h - h_hi.astype(F32)).astype(BF16)
    nt = (((1,), (1,)), ((), ()))
    both = lax.dot_general(wr_ref[...], h_hi, nt, preferred_element_type=F32)
    cross = lax.dot_general(wr_ref[0:ROUTER_ROWS], h_lo, nt, preferred_element_type=F32)
    logits = both[0:ROUTER_ROWS] + both[ROUTER_ROWS:] + cross + br_ref[...]
    el = logits[0:N_EXPERTS]
    gl = logits[N_EXPERTS:ROUTER_ROWS]
    grow = lax.broadcasted_iota(jnp.int32, gl.shape, 0)
    gl = jnp.where(grow < N_GROUPS, gl, -jnp.inf)
    gmax = jnp.max(gl, axis=0, keepdims=True)
    gsum = jnp.sum(jnp.exp(gl - gmax), axis=0, keepdims=True)
    grp_w = 1.0 / gsum
    gidx = jnp.min(jnp.where(gl == gmax, grow, N_GROUPS), axis=0, keepdims=True)

    erow = lax.broadcasted_iota(jnp.int32, el.shape, 0)
    masked = jnp.where((erow // EXPERTS_PER_GROUP) == gidx, el, -jnp.inf)
    top1 = jnp.max(masked, axis=0, keepdims=True)
    i1 = jnp.min(jnp.where(masked == top1, erow, N_EXPERTS), axis=0, keepdims=True)
    masked2 = jnp.where(erow == i1, -jnp.inf, masked)
    top2 = jnp.max(masked2, axis=0, keepdims=True)
    i2 = jnp.min(jnp.where(masked2 == top2, erow, N_EXPERTS), axis=0, keepdims=True)
    e2 = jnp.exp(top2 - top1)
    denom = 1.0 + e2
    g1 = grp_w * (1.0 / denom)
    g2 = grp_w * (e2 / denom)

    sel1 = erow == i1
    sel2 = erow == i2
    onehot = (sel1 | sel2).astype(BF16)
    prefix = jnp.dot(onehot, before_scr[...], preferred_element_type=F32) + cnt_scr[...]
    r1 = jnp.sum(jnp.where(sel1, prefix, 0.0), axis=0, keepdims=True)
    r2 = jnp.sum(jnp.where(sel2, prefix, 0.0), axis=0, keepdims=True)
    cnt_scr[...] += jnp.sum(onehot.astype(F32), axis=1, keepdims=True)

    zi = jnp.zeros((SUBLANES - 2 * TOP_K, tm), jnp.int32)
    ids_ref[0] = jnp.concatenate(
        [i1, i2, r1.astype(jnp.int32), r2.astype(jnp.int32), zi], axis=0)
    gates_ref[0] = jnp.concatenate([g1, g2, jnp.zeros((SUBLANES - TOP_K, tm), F32)], axis=0)
    counts_ref[...] = jnp.broadcast_to(cnt_scr[...], counts_ref.shape).astype(jnp.int32)


def _router(x2d, g, w_group, b_group, w_expert, b_expert):
    t = x2d.shape[0]
    nt = t // ROUTER_TILE
    pad = ROUTER_ROWS - N_EXPERTS - N_GROUPS
    wr = jnp.concatenate([w_expert.T, w_group.T, jnp.zeros((pad, D), F32)], axis=0)
    wr_hi = wr.astype(BF16)
    wr_lo = (wr - wr_hi.astype(F32)).astype(BF16)
    br = jnp.concatenate([b_expert, b_group, jnp.zeros((pad,), F32)]).reshape(ROUTER_ROWS, 1)
    const = lambda i: (0, 0)
    return pl.pallas_call(
        _router_kernel,
        name="router",
        grid=(nt,),
        in_specs=[
            pl.BlockSpec((ROUTER_TILE, D), lambda i: (i, 0)),
            pl.BlockSpec((1, D), const),
            pl.BlockSpec((2 * ROUTER_ROWS, D), const),
            pl.BlockSpec((ROUTER_ROWS, 1), const),
        ],
        out_specs=[
            pl.BlockSpec((1, SUBLANES, ROUTER_TILE), lambda i: (i, 0, 0)),
            pl.BlockSpec((1, SUBLANES, ROUTER_TILE), lambda i: (i, 0, 0)),
            pl.BlockSpec((N_EXPERTS, LANES), const),
        ],
        out_shape=[
            jax.ShapeDtypeStruct((nt, SUBLANES, ROUTER_TILE), jnp.int32),
            jax.ShapeDtypeStruct((nt, SUBLANES, ROUTER_TILE), F32),
            jax.ShapeDtypeStruct((N_EXPERTS, LANES), jnp.int32),
        ],
        scratch_shapes=[pltpu.VMEM((N_EXPERTS, 1), F32),
                        pltpu.VMEM((ROUTER_TILE, ROUTER_TILE), BF16)],
        compiler_params=_params(1),
    )(x2d, g.reshape(1, D), jnp.concatenate([wr_hi, wr_lo], axis=0), br)


SLOTS_PER_TILE = SUBLANES * TOP_K


def _issue_tile_rows(dest_ref, which, j, make_copy):
    for u in range(SUBLANES):
        for k in range(TOP_K):
            dest = dest_ref[which, j, k * SUBLANES + u]
            make_copy(j, u, k, dest).start(priority=(u * TOP_K + k) % 2)


def _issue_rows(n_rows, dest_ref, which, make_copy):
    def body(j, c):
        _issue_tile_rows(dest_ref, which, j, make_copy)
        return c

    lax.fori_loop(0, n_rows // SUBLANES, body, 0)


def _issue_rows_inline(n_rows, dest_ref, which, make_copy):
    for j in range(n_rows // SUBLANES):
        _issue_tile_rows(dest_ref, which, j, make_copy)


def _pack_bf16_pairs(x):
    half = x.shape[1] // 2
    lo = lax.bitcast_convert_type(x[:, :half].astype(BF16).astype(F32), U32)
    hi = lax.bitcast_convert_type(x[:, half:].astype(BF16).astype(F32), U32)
    return (hi & jnp.uint32(0xFFFF0000)) | (lo >> 16)


def _unpack_bf16_pairs(u):
    lo = lax.bitcast_convert_type(u << 16, F32).astype(BF16)
    hi = lax.bitcast_convert_type(u & jnp.uint32(0xFFFF0000), F32).astype(BF16)
    return lo, hi


def _tile_rows(x):
    return x.reshape(x.shape[0] // SUBLANES, SUBLANES, x.shape[1])


def _wait_rows(buf_ref, sem):
    pltpu.make_async_copy(buf_ref, buf_ref, sem).wait()


def _dispatch_kernel(zrow_ref, dest_ref, x_ref, g_ref, xs_ref, hbuf, zbuf, zsem, sem):
    @pl.when(pl.program_id(0) == 0)
    def _():
        zbuf[...] = jnp.zeros_like(zbuf)

        def zcopy(e):
            return pltpu.make_async_copy(zbuf, xs_ref.at[pl.ds(zrow_ref[e], MOE_BLOCK)], zsem)

        def zstart(e, c):
            @pl.when(zrow_ref[e] >= 0)
            def _():
                zcopy(e).start()
            return c

        def zwait(e, c):
            @pl.when(zrow_ref[e] >= 0)
            def _():
                zcopy(e).wait()
            return c

        lax.fori_loop(0, 2 * N_EXPERTS, zstart, 0)
        lax.fori_loop(0, 2 * N_EXPERTS, zwait, 0)

    i = pl.program_id(0)
    slot = i % 2
    x = x_ref[...]
    hbuf[slot] = _tile_rows(_pack_bf16_pairs(x * _rms_scale(x) * g_ref[...]))

    def row_copy(tile, sub, k, dest):
        return pltpu.make_async_copy(hbuf.at[slot, tile, pl.ds(sub, 1), :], xs_ref.at[dest],
                                     sem.at[slot])

    _issue_rows(ROUTER_TILE, dest_ref, 0, row_copy)

    def drain(which):
        for _ in range(TOP_K):
            _wait_rows(hbuf.at[which], sem.at[which])

    @pl.when(i > 0)
    def _():
        drain(1 - slot)

    @pl.when(i == pl.num_programs(0) - 1)
    def _():
        drain(slot)


def _dispatch(x2d, g, dest, zrow, n_pad):
    t = x2d.shape[0]
    nt = t // ROUTER_TILE
    return pl.pallas_call(
        _dispatch_kernel,
        name="dispatch",
        grid_spec=pltpu.PrefetchScalarGridSpec(
            num_scalar_prefetch=1,
            grid=(nt,),
            in_specs=[
                pl.BlockSpec((1, ROUTER_TILE // SUBLANES, SLOTS_PER_TILE),
                             lambda i, z: (i, 0, 0), memory_space=pltpu.SMEM),
                pl.BlockSpec((ROUTER_TILE, D), lambda i, z: (i, 0)),
                pl.BlockSpec((1, D), lambda i, z: (0, 0)),
            ],
            out_specs=pl.BlockSpec(memory_space=pl.ANY),
            scratch_shapes=[
                pltpu.VMEM((2, ROUTER_TILE // SUBLANES, SUBLANES, D // 2), U32),
                pltpu.VMEM((MOE_BLOCK, 1, D // 2), U32),
                pltpu.SemaphoreType.DMA(()),
                pltpu.SemaphoreType.DMA((2,)),
            ],
        ),
        out_shape=jax.ShapeDtypeStruct((n_pad, 1, D // 2), U32),
        compiler_params=_params(1),
    )(zrow, dest, x2d, g.reshape(1, D))


def _experts_kernel(be_ref, nxt_ref, nu_ref, xs_ref, wg_ref, wu_ref, wd_ref, y_ref,
                    wg_f, wu_f, wd_f, wg_b, wu_b, wd_b, xbuf, ybuf, w_sem, in_sem, out_sem, *, layer):
    i = pl.program_id(0)
    n_used = nu_ref[0]
    slot = i % 2

    def block_rows(ref, blk):
        return ref.at[pl.ds(pl.multiple_of(blk * MOE_BLOCK, MOE_BLOCK), MOE_BLOCK), 0]

    def fetch(blk, into):
        return pltpu.make_async_copy(block_rows(xs_ref, blk), xbuf.at[into], in_sem.at[into])

    def write_back(blk, from_):
        return pltpu.make_async_copy(ybuf.at[from_], block_rows(y_ref, blk), out_sem.at[from_])

    def weight_copies(e):
        return [pltpu.make_async_copy(src.at[layer, e], dst, w_sem.at[n])
                for n, (src, dst) in enumerate(((wg_ref, wg_f), (wu_ref, wu_f), (wd_ref, wd_f)))]

    @pl.when(i == 0)
    def _():
        for c in weight_copies(be_ref[0]):
            c.start()
        fetch(0, 0).start()

    @pl.when(i + 1 < n_used)
    def _():
        fetch(i + 1, 1 - slot).start()

    prev = be_ref[jnp.maximum(i - 1, 0)]

    @pl.when((i < n_used) & ((i == 0) | (be_ref[i] != prev)))
    def _():
        for c in weight_copies(be_ref[i]):
            c.wait()
        wg_b[...] = wg_f[...].astype(BF16)
        wu_b[...] = wu_f[...].astype(BF16)
        wd_b[...] = wd_f[...].astype(BF16)

        @pl.when(nxt_ref[i] >= 0)
        def _():
            for c in weight_copies(nxt_ref[i]):
                c.start()

    @pl.when(i >= 2)
    def _():
        write_back(i - 2, slot).wait()

    @pl.when(i < n_used)
    def _():
        fetch(i, slot).wait()
        h_lo, h_hi = _unpack_bf16_pairs(xbuf[slot])
        half = D // 2
        a = (jnp.dot(h_lo, wg_b[:half], preferred_element_type=F32)
             + jnp.dot(h_hi, wg_b[half:], preferred_element_type=F32))
        u = (jnp.dot(h_lo, wu_b[:half], preferred_element_type=F32)
             + jnp.dot(h_hi, wu_b[half:], preferred_element_type=F32))
        hid = (a * _sigmoid(a) * u).astype(BF16)
        ybuf[slot] = jnp.dot(hid, wd_b[...], preferred_element_type=F32)

    @pl.when(i >= n_used)
    def _():
        ybuf[slot] = jnp.zeros((MOE_BLOCK, D), F32)

    write_back(i, slot).start()

    @pl.when(i == pl.num_programs(0) - 1)
    def _():
        write_back(i - 1, 1 - slot).wait()
        write_back(i, slot).wait()


def _experts(xs, block_e, next_e, n_used, layer, w_gate, w_up, w_down):
    n_pad = xs.shape[0]
    n_blocks = n_pad // MOE_BLOCK
    hbm = pl.BlockSpec(memory_space=pl.ANY)
    return pl.pallas_call(
        functools.partial(_experts_kernel, layer=layer),
        name="experts",
        grid_spec=pltpu.PrefetchScalarGridSpec(
            num_scalar_prefetch=3,
            grid=(n_blocks,),
            in_specs=[hbm, hbm, hbm, hbm],
            out_specs=hbm,
            scratch_shapes=[
                pltpu.VMEM((D, D_EXPERT), F32),
                pltpu.VMEM((D, D_EXPERT), F32),
                pltpu.VMEM((D_EXPERT, D), F32),
                pltpu.VMEM((D, D_EXPERT), BF16),
                pltpu.VMEM((D, D_EXPERT), BF16),
                pltpu.VMEM((D_EXPERT, D), BF16),
                pltpu.VMEM((2, MOE_BLOCK, D // 2), U32),
                pltpu.VMEM((2, MOE_BLOCK, D), F32),
                pltpu.SemaphoreType.DMA((3,)),
                pltpu.SemaphoreType.DMA((2,)),
                pltpu.SemaphoreType.DMA((2,)),
            ],
        ),
        out_shape=jax.ShapeDtypeStruct((n_pad, 1, D), F32),
        compiler_params=_params(1),
    )(block_e, next_e, n_used, xs, w_gate, w_up, w_down)


def _combine_kernel(*refs, final, project):
    (dest_ref, dest_next_ref, x_ref, gate_ref, y_ref, p_ref, gple_ref, wp_ref, wg_ref,
     gfin_ref) = refs[:10]
    if project:
        gq_ref, gkv_ref, wq_ref, wkv_ref, out_ref, qt_ref, k_ref, vt_ref = refs[10:18]
        ybuf0, ybuf1, sem = refs[18:]
    else:
        out_ref, ybuf0, ybuf1, sem = refs[10:]
    i = pl.program_id(0)
    ybuf = (ybuf0, ybuf1)

    def row_copy(into):
        def make(tile, sub, k, dest):
            return pltpu.make_async_copy(y_ref.at[dest], ybuf[into].at[k, tile, pl.ds(sub, 1), :],
                                         sem.at[into])
        return make

    def combine(half, slot):
        rows = slice(half * COMBINE_TILE, (half + 1) * COMBINE_TILE)
        gates = gate_ref[half].T
        y0 = ybuf[slot][0].reshape(COMBINE_TILE, D)
        y1 = ybuf[slot][1].reshape(COMBINE_TILE, D)
        x = x_ref[rows, :] + gates[:, 0:1] * y0 + gates[:, 1:2] * y1
        h = (x * _rms_scale(x) * gple_ref[...]).astype(BF16)
        gate = _sigmoid(jnp.dot(h, wg_ref[...], preferred_element_type=F32))
        proj = jnp.dot(p_ref[0, rows, :].astype(BF16), wp_ref[...], preferred_element_type=F32)
        x = x + proj * gate
        if final:
            x = x * _rms_scale(x) * gfin_ref[...]
        out_ref[rows, :] = x
        if project:
            q, k, v = _qkv_rows(x, gq_ref, gkv_ref, wq_ref, wkv_ref)
            qt_ref[0, :, rows] = q.T.astype(BF16)
            k_ref[0, rows, :] = k.astype(BF16)
            vt_ref[0, :, rows] = v.T.astype(BF16)

    def wait(slot):
        for k in range(TOP_K):
            _wait_rows(ybuf[slot].at[k], sem.at[slot])

    @pl.when(i == 0)
    def _():
        _issue_rows(COMBINE_TILE, dest_ref, 0, row_copy(0))

    wait(0)
    _issue_rows_inline(COMBINE_TILE, dest_ref, 1, row_copy(1))
    combine(0, 0)
    wait(1)
    _issue_rows_inline(COMBINE_TILE, dest_next_ref, 0, row_copy(0))
    combine(1, 1)

    @pl.when(i == pl.num_programs(0) - 1)
    def _():
        wait(0)


def _qkv_rows(x, gq_ref, gkv_ref, wq_ref, wkv_ref):
    xn = x * _rms_scale(x)
    hq = (xn * gq_ref[...]).astype(BF16)
    hkv = (xn * gkv_ref[...]).astype(BF16)
    q = jnp.dot(hq, wq_ref[...], preferred_element_type=F32) * (ATTN_SCALE * LOG2E)
    kv = jnp.dot(hkv, wkv_ref[...], preferred_element_type=F32)
    return q, kv[:, :D], kv[:, D:]


def _combine_ple(x2d, dest, gates, y, p3d, layer, g_ple, w_proj, w_gate, g_final, final, qkv=None):
    assert COMBINE_TILE == ROUTER_TILE
    t = x2d.shape[0]
    step = 2 * COMBINE_TILE
    nt = t // step
    dest_blk = (2, COMBINE_TILE // SUBLANES, SLOTS_PER_TILE)
    const = lambda i: (0, 0)
    resident = functools.partial(pl.BlockSpec, index_map=const, pipeline_mode=pl.Buffered(1))
    in_specs = [
        pl.BlockSpec(dest_blk, lambda i: (i, 0, 0), memory_space=pltpu.SMEM),
        pl.BlockSpec(dest_blk, lambda i: (jnp.minimum(i + 1, nt - 1), 0, 0),
                     memory_space=pltpu.SMEM),
        pl.BlockSpec((step, D), lambda i: (i, 0)),
        pl.BlockSpec((2, SUBLANES, COMBINE_TILE), lambda i: (i, 0, 0)),
        pl.BlockSpec(memory_space=pl.ANY),
        pl.BlockSpec((1, step, D_PLE), lambda i: (layer, i, 0)),
        pl.BlockSpec((1, D), const),
        resident((D_PLE, D)),
        resident((D, D)),
        pl.BlockSpec((1, D), const),
    ]
    args = [dest, dest, x2d, gates, y, p3d, g_ple.reshape(1, D), w_proj.astype(BF16),
            w_gate.astype(BF16), g_final.reshape(1, D)]
    out_specs = [pl.BlockSpec((step, D), lambda i: (i, 0))]
    out_shape = [jax.ShapeDtypeStruct((t, D), F32)]
    if qkv is not None:
        seq, g_q, g_kv, w_q, w_kv = qkv
        per_seq = seq // step
        in_specs += [pl.BlockSpec((1, D), const), pl.BlockSpec((1, D), const),
                     resident((D, D)), resident((D, 2 * D))]
        args += [g_q.reshape(1, D), g_kv.reshape(1, D), w_q.astype(BF16), w_kv.astype(BF16)]
        feat_major = pl.BlockSpec((1, D, step), lambda i: (i // per_seq, 0, i % per_seq))
        row_major = pl.BlockSpec((1, step, D), lambda i: (i // per_seq, i % per_seq, 0))
        out_specs += [feat_major, row_major, feat_major]
        out_shape += [jax.ShapeDtypeStruct((t // seq, D, seq), BF16),
                      jax.ShapeDtypeStruct((t // seq, seq, D), BF16),
                      jax.ShapeDtypeStruct((t // seq, D, seq), BF16)]
    outs = pl.pallas_call(
        functools.partial(_combine_kernel, final=final, project=qkv is not None),
        name="combine_final" if final else "combine",
        grid=(nt,),
        in_specs=in_specs,
        out_specs=out_specs,
        out_shape=out_shape,
        scratch_shapes=[
            pltpu.VMEM((TOP_K, COMBINE_TILE // SUBLANES, SUBLANES, D), F32),
            pltpu.VMEM((TOP_K, COMBINE_TILE // SUBLANES, SUBLANES, D), F32),
            pltpu.SemaphoreType.DMA((2,)),
        ],
        compiler_params=_params(1),
    )(*args)
    return outs[0] if qkv is None else tuple(outs)


def _moe_ple(x, p, layer, norm_ffn, w_group, b_group, w_expert, b_expert, w_gate, w_up, w_down,
             norm_ple, ple_w_proj, ple_w_gate, final_norm, final, qkv=None):
    b, s, _ = x.shape
    t = b * s
    x2d = x.reshape(t, D)
    ids, gates, counts = _router(x2d, norm_ffn, w_group, b_group, w_expert, b_expert)

    counts = counts[:, 0]
    padded = (counts + MOE_BLOCK - 1) // MOE_BLOCK * MOE_BLOCK
    pad_end = jnp.cumsum(padded)
    pad_start = pad_end - padded
    n_blocks = t * TOP_K // MOE_BLOCK + N_EXPERTS
    n_pad = n_blocks * MOE_BLOCK
    e = ids[:, 0:TOP_K, :]
    r = ids[:, TOP_K:2 * TOP_K, :]
    experts = jnp.arange(N_EXPERTS, dtype=jnp.int32)
    dest = r + jnp.sum(jnp.where(e[..., None] == experts, pad_start, 0), axis=-1)
    nt = t // ROUTER_TILE
    dest = dest.reshape(nt, TOP_K, ROUTER_TILE // SUBLANES, SUBLANES).transpose(0, 2, 1, 3)
    dest = dest.reshape(nt, ROUTER_TILE // SUBLANES, SLOTS_PER_TILE).astype(jnp.int32)
    block_row = jnp.arange(n_blocks, dtype=jnp.int32) * MOE_BLOCK
    block_e = jnp.minimum(jnp.sum(pad_end[None, :] <= block_row[:, None], axis=-1),
                          N_EXPERTS - 1).astype(jnp.int32)
    n_used = (pad_end[-1:] // MOE_BLOCK).astype(jnp.int32)
    tail = (n_used[0] + jnp.arange(N_EXPERTS, dtype=jnp.int32)) * MOE_BLOCK
    zrow = jnp.concatenate([jnp.where(padded > 0, pad_end - MOE_BLOCK, -1),
                            jnp.where(tail < n_pad, tail, -1)]).astype(jnp.int32)

    xs = _dispatch(x2d, norm_ffn, dest, zrow, n_pad)
    later = jnp.where((experts[None, :] > block_e[:, None]) & (padded[None, :] > 0), experts[None, :],
                      N_EXPERTS)
    next_e = jnp.min(later, axis=-1)
    next_e = jnp.where(next_e < N_EXPERTS, next_e, -1).astype(jnp.int32)
    y = _experts(xs, block_e, next_e, n_used, layer, w_gate, w_up, w_down)
    out = _combine_ple(x2d, dest, gates, y, p.reshape(-1, t, D_PLE), layer,
                       norm_ple, ple_w_proj, ple_w_gate, final_norm, final,
                       None if qkv is None else (s,) + tuple(qkv))
    if qkv is None:
        return out.reshape(b, s, D)
    return (out[0].reshape(b, s, D),) + out[1:]


Q_GROUP = 2 * CHUNK
G_BAND = BAND + CHUNK
PAIR = 2 * B_HEAD_DIM
ONES_ROWS = BF16_SUBLANES


def _attn_kernel(qt_ref, kp_ref, kc_ref, vtp_ref, vtc_ref, bias_ref, x_ref, wo_ref, out_ref, o_scr,
                 s_scr0, s_scr1, s_scr2, p_scr0, p_scr1, p_scr2):
    drow = lax.broadcasted_iota(jnp.int32, (PAIR, Q_GROUP), 0)
    first_head = drow < B_HEAD_DIM
    s_scr = (s_scr0, s_scr1, s_scr2)
    p_scr = (p_scr0, p_scr1, p_scr2)

    def attend(first_tile):
        units = [(g, pr) for g in range(SEQ_TILE // Q_GROUP) for pr in range(B_HEADS // 2)]

        def geometry(g):
            w0 = g * Q_GROUP
            n_prev = SEQ_TILE - w0
            return w0, n_prev, G_BAND - n_prev

        def keys(g):
            return slice(geometry(g)[1], None) if first_tile else slice(None)

        def scores(unit, s_ref):
            g, pr = unit
            w0, n_prev, n_cur = geometry(g)
            feat = slice(pr * PAIR, (pr + 1) * PAIR)
            qt = qt_ref[0, feat, w0:w0 + Q_GROUP]
            zero = jnp.zeros_like(qt)
            qblk = jnp.concatenate([jnp.where(first_head, qt, zero),
                                    jnp.where(first_head, zero, qt)], axis=1)
            if first_tile:
                kb = kc_ref[0, :n_cur, feat]
            else:
                kb = jnp.concatenate([kp_ref[0, w0:, feat], kc_ref[0, :n_cur, feat]], axis=0)
            s_ref[keys(g), :] = (jnp.dot(kb, qblk, preferred_element_type=F32)
                                 + bias_ref[pr, keys(g), :])

        def weights(unit, s_ref, p_ref):
            g, _ = unit
            s = s_ref[keys(g), :]
            m = jnp.max(s, axis=0, keepdims=True)
            p_ref[keys(g), :] = jnp.exp2(s - m).astype(BF16)

        def values(unit, p_ref):
            g, pr = unit
            w0, n_prev, n_cur = geometry(g)
            feat = slice(pr * PAIR, (pr + 1) * PAIR)
            if first_tile:
                vt = vtc_ref[0, feat, :n_cur]
            else:
                vt = jnp.concatenate([vtp_ref[0, feat, w0:], vtc_ref[0, feat, :n_cur]], axis=1)
            ones = jnp.ones((ONES_ROWS, vt.shape[1]), BF16)
            ot = jnp.dot(jnp.concatenate([vt, ones], axis=0), p_ref[keys(g), :],
                         preferred_element_type=F32)
            inv = 1.0 / ot[PAIR:PAIR + 1, :]
            ot = jnp.where(first_head, ot[:PAIR, :Q_GROUP] * inv[:, :Q_GROUP],
                           ot[:PAIR, Q_GROUP:] * inv[:, Q_GROUP:])
            o_scr[w0:w0 + Q_GROUP, feat] = ot.T.astype(BF16)

        n_units = len(units)
        scores(units[0], s_scr[0])
        scores(units[1], s_scr[1])
        scores(units[2], s_scr[2])
        weights(units[0], s_scr[0], p_scr[0])
        weights(units[1], s_scr[1], p_scr[1])
        for n, unit in enumerate(units):
            if n + 3 < n_units:
                scores(units[n + 3], s_scr[n % 3])
            if n + 2 < n_units:
                weights(units[n + 2], s_scr[(n + 2) % 3], p_scr[(n + 2) % 3])
            values(unit, p_scr[n % 3])

    @pl.when(pl.program_id(1) == 0)
    def _():
        attend(True)

    @pl.when(pl.program_id(1) > 0)
    def _():
        attend(False)

    out_ref[0] = x_ref[0] + jnp.dot(o_scr[...], wo_ref[...], preferred_element_type=F32)


def _attn(x, qt, k, vt, bias_t, w_o):
    b, s, _ = x.shape
    cur = lambda bi, si: (bi, si, 0)
    prev = lambda bi, si: (bi, jnp.maximum(si - 1, 0), 0)
    cur_t = lambda bi, si: (bi, 0, si)
    prev_t = lambda bi, si: (bi, 0, jnp.maximum(si - 1, 0))
    blk = (1, SEQ_TILE, D)
    blk_t = (1, D, SEQ_TILE)
    return pl.pallas_call(
        _attn_kernel,
        name="attn",
        grid=(b, s // SEQ_TILE),
        in_specs=[
            pl.BlockSpec(blk_t, cur_t),
            pl.BlockSpec(blk, prev),
            pl.BlockSpec(blk, cur),
            pl.BlockSpec(blk_t, prev_t),
            pl.BlockSpec(blk_t, cur_t),
            pl.BlockSpec((B_HEADS // 2, G_BAND, 2 * Q_GROUP), lambda bi, si: (0, 0, 0),
                         pipeline_mode=pl.Buffered(1)),
            pl.BlockSpec(blk, cur),
            pl.BlockSpec((D, D), lambda bi, si: (0, 0), pipeline_mode=pl.Buffered(1)),
        ],
        out_specs=pl.BlockSpec(blk, cur),
        out_shape=jax.ShapeDtypeStruct(x.shape, F32),
        scratch_shapes=[pltpu.VMEM((SEQ_TILE, D), BF16)]
        + [pltpu.VMEM((G_BAND, 2 * Q_GROUP), F32)] * 3
        + [pltpu.VMEM((G_BAND, 2 * Q_GROUP), BF16)] * 3,
        compiler_params=_params(2),
    )(qt, k, k, vt, vt, bias_t, x, w_o.astype(BF16))


def _group_bias(table):
    band = _band_bias(table) * LOG2E
    pad = lambda lo, hi: jnp.pad(band, ((0, 0), (0, 0), (lo, hi)), constant_values=NEG_INF)
    both = jnp.concatenate([pad(0, CHUNK), pad(CHUNK, 0)], axis=1)
    both = both.reshape(B_HEADS // 2, 2, Q_GROUP, G_BAND)
    return both.transpose(0, 3, 1, 2).reshape(B_HEADS // 2, G_BAND, 2 * Q_GROUP)


def _band_bias(table):
    n_rel = REL_MAX - REL_MIN + 1
    span = BAND + CHUNK - 1
    head = jnp.broadcast_to(table[:, n_rel - 1:], (table.shape[0], span - n_rel))
    ext = jnp.concatenate([head, table[:, ::-1]], axis=1)
    rows = [ext[:, CHUNK - 1 - q:CHUNK - 1 - q + BAND] for q in range(CHUNK)]
    return jnp.stack(rows, axis=1)


def kernel(x, p, a_w_in, a_lb_logits, a_out_norm, a_w_o, kv_norm, w_kv, b_w_q, b_rel_bias, b_w_o,
           norm_mix, norm_ffn, norm_ple, moe_w_group, moe_b_group, moe_w_expert, moe_b_expert,
           moe_w_gate, moe_w_up, moe_w_down, ple_w_proj, ple_w_gate, final_norm):
    b, s, _ = x.shape
    lower_bounds = jnp.cumsum(jax.nn.softmax(a_lb_logits.astype(F32), axis=0), axis=0)

    def moe(xi, i, final, qkv=None):
        return _moe_ple(xi, p, i, norm_ffn[i], moe_w_group[i], moe_b_group[i], moe_w_expert[i],
                        moe_b_expert[i], moe_w_gate, moe_w_up, moe_w_down, norm_ple[i],
                        ple_w_proj[i], ple_w_gate[i], final_norm, final, qkv)

    x = _mixer_a(x, norm_mix[0], a_w_in[0], lower_bounds[0], a_out_norm[0], a_w_o[0])
    x, qt, k, vt = moe(x, 0, False, (norm_mix[1], kv_norm, b_w_q[0], w_kv))

    x = _attn(x, qt, k, vt, _group_bias(b_rel_bias[0].astype(F32)), b_w_o[0])
    x = moe(x, 1, True)
    return x
```

```python
import functools

import jax
import jax.numpy as jnp
from jax import lax
from jax.experimental import pallas as pl
from jax.experimental.pallas import tpu as pltpu

F32 = jnp.float32
BF16 = jnp.bfloat16
U32 = jnp.uint32

D = 1024
CHUNK = 64
A_HEADS = 8
A_HEAD_DIM = 128
B_HEADS = 16
B_HEAD_DIM = 64
LEFT_CHUNKS = 8
BAND = (LEFT_CHUNKS + 1) * CHUNK
REL_MIN = -(CHUNK - 1)
REL_MAX = 256
ATTN_SCALE = B_HEAD_DIM ** -0.5
N_GROUPS = 4
EXPERTS_PER_GROUP = 8
N_EXPERTS = 32
TOP_K = 2
D_EXPERT = 512
MOE_BLOCK = 512
D_PLE = 256
EPS = 1e-6
NEG_INF = -1e30
LOG2E = 1.4426950408889634

SUBLANES = 8
LANES = 128
BF16_SUBLANES = 16
V7X_VMEM_BYTES = 64 * 1024 * 1024

SEQ_TILE = 512
ROUTER_TILE = 512
COMBINE_TILE = 512
ROUTER_ROWS = -(-(N_EXPERTS + N_GROUPS) // BF16_SUBLANES) * BF16_SUBLANES
VMEM_LIMIT = V7X_VMEM_BYTES - 8 * 1024 * 1024


def _params(n_axes, vmem=VMEM_LIMIT):
    return pltpu.CompilerParams(dimension_semantics=("arbitrary",) * n_axes,
                                vmem_limit_bytes=vmem)


def _rms_scale(x):
    return lax.rsqrt(jnp.mean(x * x, axis=-1, keepdims=True) + EPS)


def _sigmoid(x):
    return 0.5 * jnp.tanh(0.5 * x) + 0.5


def _mixer_a_kernel(x_ref, g_ref, win_ref, lb_ref, onorm_ref, wo_ref, out_ref,
                    proj_scr, o_scr, state_scr, g_scr0, g_scr1, k_scr0, k_scr1,
                    qd_scr0, qd_scr1, qd_scr2, qd_scr3, kt_scr0, kt_scr1, kt_scr2, kt_scr3,
                    att_scr0, att_scr1, att_scr2, att_scr3):
    @pl.when(pl.program_id(1) == 0)
    def _():
        state_scr[...] = jnp.zeros_like(state_scr)

    x = x_ref[0]
    h = (x * _rms_scale(x) * g_ref[...]).astype(BF16)
    proj_scr[...] = jnp.dot(h, win_ref[...], preferred_element_type=F32)

    row = lax.broadcasted_iota(jnp.int32, (CHUNK, CHUNK), 0)
    col = lax.broadcasted_iota(jnp.int32, (CHUNK, CHUNK), 1)
    causal = row >= col
    tril = causal.astype(BF16)
    lb = lb_ref[...]
    onorm = onorm_ref[...]

    g_scr, k_scr = (g_scr0, g_scr1), (k_scr0, k_scr1)
    qd_scr, kt_scr = (qd_scr0, qd_scr1, qd_scr2, qd_scr3), (kt_scr0, kt_scr1, kt_scr2, kt_scr3)
    att_scr = (att_scr0, att_scr1, att_scr2, att_scr3)
    n_chunks = SEQ_TILE // CHUNK
    units = [(c, hd) for c in range(n_chunks) for hd in range(A_HEADS)]

    def rows_of(c):
        return slice(c * CHUNK, (c + 1) * CHUNK)

    def decay(c):
        f = lb + (1.0 - lb) * _sigmoid(proj_scr[rows_of(c), D:2 * D])
        logf = jnp.log(f)
        hi = logf.astype(BF16)
        lo = (logf - hi.astype(F32)).astype(BF16)
        g_scr[c % 2][...] = (jnp.dot(tril, hi, preferred_element_type=F32)
                             + jnp.dot(tril, lo, preferred_element_type=F32))
        k_scr[c % 2][...] = 1.0 - f

    def intra(n):
        c, hd = units[n]
        sl = slice(hd * A_HEAD_DIM, (hd + 1) * A_HEAD_DIM)
        gh = g_scr[c % 2][:, sl]
        g_last = gh[CHUNK - 1:CHUNK, :]
        k = k_scr[c % 2][:, sl]
        q_dec = (proj_scr[rows_of(c), sl] * jnp.exp(gh)).astype(BF16)
        k_inv = k * jnp.exp(-gh)
        qd_scr[n % 4][...] = q_dec
        kt_scr[n % 4][...] = (k_inv * jnp.exp(g_last)).astype(BF16)
        k_inv = k_inv.astype(BF16)
        att = lax.dot_general(q_dec, k_inv, (((1,), (1,)), ((), ())),
                              preferred_element_type=F32)
        att_scr[n % 4][...] = jnp.where(causal, att, 0.0).astype(BF16)

    def output(n):
        c, hd = units[n]
        sl = slice(hd * A_HEAD_DIM, (hd + 1) * A_HEAD_DIM)
        rows = rows_of(c)
        g_last = g_scr[c % 2][CHUNK - 1:CHUNK, sl]
        v = proj_scr[rows, 2 * D + hd * A_HEAD_DIM:2 * D + (hd + 1) * A_HEAD_DIM]
        st = state_scr[hd]
        o = (jnp.dot(att_scr[n % 4][...], v.astype(BF16), preferred_element_type=F32)
             + lax.dot_general(qd_scr[n % 4][...], st.astype(BF16), (((1,), (1,)), ((), ())),
                               preferred_element_type=F32))
        v_t = v.T.astype(BF16)
        state_scr[hd] = st * jnp.exp(g_last) + jnp.dot(v_t, kt_scr[n % 4][...],
                                                       preferred_element_type=F32)
        o = o * _rms_scale(o)
        og = proj_scr[rows, 3 * D + hd * A_HEAD_DIM:3 * D + (hd + 1) * A_HEAD_DIM]
        o = o * onorm[:, sl] * (og * _sigmoid(og))
        o_scr[rows, sl] = o.astype(BF16)

    decay(0)
    intra(0)
    intra(1)
    intra(2)
    for n, (c, hd) in enumerate(units):
        if hd == 0 and c + 1 < n_chunks:
            decay(c + 1)
        if n + 3 < len(units):
            intra(n + 3)
        output(n)
    out_ref[0] = x + jnp.dot(o_scr[...], wo_ref[...], preferred_element_type=F32)


def _mixer_a(x, g, w_in, lb, out_norm, w_o):
    b, s, _ = x.shape
    const = lambda bi, si: (0, 0)
    return pl.pallas_call(
        _mixer_a_kernel,
        name="mixer_a",
        grid=(b, s // SEQ_TILE),
        in_specs=[
            pl.BlockSpec((1, SEQ_TILE, D), lambda bi, si: (bi, si, 0)),
            pl.BlockSpec((1, D), const),
            pl.BlockSpec((D, 4 * D), const, pipeline_mode=pl.Buffered(1)),
            pl.BlockSpec((1, D), const),
            pl.BlockSpec((1, D), const),
            pl.BlockSpec((D, D), const, pipeline_mode=pl.Buffered(1)),
        ],
        out_specs=pl.BlockSpec((1, SEQ_TILE, D), lambda bi, si: (bi, si, 0)),
        out_shape=jax.ShapeDtypeStruct(x.shape, F32),
        scratch_shapes=[
            pltpu.VMEM((SEQ_TILE, 4 * D), F32),
            pltpu.VMEM((SEQ_TILE, D), BF16),
            pltpu.VMEM((A_HEADS, A_HEAD_DIM, A_HEAD_DIM), F32),
        ] + [pltpu.VMEM((CHUNK, D), F32)] * 4
          + [pltpu.VMEM((CHUNK, A_HEAD_DIM), BF16)] * 8
          + [pltpu.VMEM((CHUNK, CHUNK), BF16)] * 4,
        compiler_params=_params(2),
    )(x, g.reshape(1, D), w_in.astype(BF16), lb.reshape(1, D), out_norm.reshape(1, D),
      w_o.astype(BF16))


def _router_kernel(x_ref, g_ref, wr_ref, br_ref, ids_ref, gates_ref, counts_ref, cnt_scr,
                   before_scr):
    tm = ROUTER_TILE

    @pl.when(pl.program_id(0) == 0)
    def _():
        cnt_scr[...] = jnp.zeros_like(cnt_scr)
        tr = lax.broadcasted_iota(jnp.int32, (tm, tm), 0)
        tc = lax.broadcasted_iota(jnp.int32, (tm, tm), 1)
        before_scr[...] = (tr < tc).astype(BF16)

    x = x_ref[...]
    h = x * _rms_scale(x) * g_ref[...]
    h_hi = h.astype(BF16)
    h_lo = (h - h_hi.astype(F32)).astype(BF16)
    nt = (((1,), (1,)), ((), ()))
    both = lax.dot_general(wr_ref[...], h_hi, nt, preferred_element_type=F32)
    cross = lax.dot_general(wr_ref[0:ROUTER_ROWS], h_lo, nt, preferred_element_type=F32)
    logits = both[0:ROUTER_ROWS] + both[ROUTER_ROWS:] + cross + br_ref[...]
    el = logits[0:N_EXPERTS]
    gl = logits[N_EXPERTS:ROUTER_ROWS]
    grow = lax.broadcasted_iota(jnp.int32, gl.shape, 0)
    gl = jnp.where(grow < N_GROUPS, gl, -jnp.inf)
    gmax = jnp.max(gl, axis=0, keepdims=True)
    gsum = jnp.sum(jnp.exp(gl - gmax), axis=0, keepdims=True)
    grp_w = 1.0 / gsum
    gidx = jnp.min(jnp.where(gl == gmax, grow, N_GROUPS), axis=0, keepdims=True)

    erow = lax.broadcasted_iota(jnp.int32, el.shape, 0)
    masked = jnp.where((erow // EXPERTS_PER_GROUP) == gidx, el, -jnp.inf)
    top1 = jnp.max(masked, axis=0, keepdims=True)
    i1 = jnp.min(jnp.where(masked == top1, erow, N_EXPERTS), axis=0, keepdims=True)
    masked2 = jnp.where(erow == i1, -jnp.inf, masked)
    top2 = jnp.max(masked2, axis=0, keepdims=True)
    i2 = jnp.min(jnp.where(masked2 == top2, erow, N_EXPERTS), axis=0, keepdims=True)
    e2 = jnp.exp(top2 - top1)
    denom = 1.0 + e2
    g1 = grp_w * (1.0 / denom)
    g2 = grp_w * (e2 / denom)

    sel1 = erow == i1
    sel2 = erow == i2
    onehot = (sel1 | sel2).astype(BF16)
    prefix = jnp.dot(onehot, before_scr[...], preferred_element_type=F32) + cnt_scr[...]
    r1 = jnp.sum(jnp.where(sel1, prefix, 0.0), axis=0, keepdims=True)
    r2 = jnp.sum(jnp.where(sel2, prefix, 0.0), axis=0, keepdims=True)
    cnt_scr[...] += jnp.sum(onehot.astype(F32), axis=1, keepdims=True)

    zi = jnp.zeros((SUBLANES - 2 * TOP_K, tm), jnp.int32)
    ids_ref[0] = jnp.concatenate(
        [i1, i2, r1.astype(jnp.int32), r2.astype(jnp.int32), zi], axis=0)
    gates_ref[0] = jnp.concatenate([g1, g2, jnp.zeros((SUBLANES - TOP_K, tm), F32)], axis=0)
    counts_ref[...] = jnp.broadcast_to(cnt_scr[...], counts_ref.shape).astype(jnp.int32)


def _router(x2d, g, w_group, b_group, w_expert, b_expert):
    t = x2d.shape[0]
    nt = t // ROUTER_TILE
    pad = ROUTER_ROWS - N_EXPERTS - N_GROUPS
    wr = jnp.concatenate([w_expert.T, w_group.T, jnp.zeros((pad, D), F32)], axis=0)
    wr_hi = wr.astype(BF16)
    wr_lo = (wr - wr_hi.astype(F32)).astype(BF16)
    br = jnp.concatenate([b_expert, b_group, jnp.zeros((pad,), F32)]).reshape(ROUTER_ROWS, 1)
    const = lambda i: (0, 0)
    return pl.pallas_call(
        _router_kernel,
        name="router",
        grid=(nt,),
        in_specs=[
            pl.BlockSpec((ROUTER_TILE, D), lambda i: (i, 0)),
            pl.BlockSpec((1, D), const),
            pl.BlockSpec((2 * ROUTER_ROWS, D), const),
            pl.BlockSpec((ROUTER_ROWS, 1), const),
        ],
        out_specs=[
            pl.BlockSpec((1, SUBLANES, ROUTER_TILE), lambda i: (i, 0, 0)),
            pl.BlockSpec((1, SUBLANES, ROUTER_TILE), lambda i: (i, 0, 0)),
            pl.BlockSpec((N_EXPERTS, LANES), const),
        ],
        out_shape=[
            jax.ShapeDtypeStruct((nt, SUBLANES, ROUTER_TILE), jnp.int32),
            jax.ShapeDtypeStruct((nt, SUBLANES, ROUTER_TILE), F32),
            jax.ShapeDtypeStruct((N_EXPERTS, LANES), jnp.int32),
        ],
        scratch_shapes=[pltpu.VMEM((N_EXPERTS, 1), F32),
                        pltpu.VMEM((ROUTER_TILE, ROUTER_TILE), BF16)],
        compiler_params=_params(1),
    )(x2d, g.reshape(1, D), jnp.concatenate([wr_hi, wr_lo], axis=0), br)


SLOTS_PER_TILE = SUBLANES * TOP_K


def _issue_tile_rows(dest_ref, which, j, make_copy):
    for u in range(SUBLANES):
        for k in range(TOP_K):
            dest = dest_ref[which, j, k * SUBLANES + u]
            make_copy(j, u, k, dest).start(priority=(u * TOP_K + k) % 2)


def _issue_rows(n_rows, dest_ref, which, make_copy):
    def body(j, c):
        _issue_tile_rows(dest_ref, which, j, make_copy)
        return c

    lax.fori_loop(0, n_rows // SUBLANES, body, 0)


def _issue_rows_inline(n_rows, dest_ref, which, make_copy):
    for j in range(n_rows // SUBLANES):
        _issue_tile_rows(dest_ref, which, j, make_copy)


def _pack_bf16_pairs(x):
    half = x.shape[1] // 2
    lo = lax.bitcast_convert_type(x[:, :half].astype(BF16).astype(F32), U32)
    hi = lax.bitcast_convert_type(x[:, half:].astype(BF16).astype(F32), U32)
    return (hi & jnp.uint32(0xFFFF0000)) | (lo >> 16)


def _unpack_bf16_pairs(u):
    lo = lax.bitcast_convert_type(u << 16, F32).astype(BF16)
    hi = lax.bitcast_convert_type(u & jnp.uint32(0xFFFF0000), F32).astype(BF16)
    return lo, hi


def _tile_rows(x):
    return x.reshape(x.shape[0] // SUBLANES, SUBLANES, x.shape[1])


def _wait_rows(buf_ref, sem):
    pltpu.make_async_copy(buf_ref, buf_ref, sem).wait()


def _dispatch_kernel(zrow_ref, dest_ref, x_ref, g_ref, xs_ref, hbuf, zbuf, zsem, sem):
    @pl.when(pl.program_id(0) == 0)
    def _():
        zbuf[...] = jnp.zeros_like(zbuf)

        def zcopy(e):
            return pltpu.make_async_copy(zbuf, xs_ref.at[pl.ds(zrow_ref[e], MOE_BLOCK)], zsem)

        def zstart(e, c):
            @pl.when(zrow_ref[e] >= 0)
            def _():
                zcopy(e).start()
            return c

        def zwait(e, c):
            @pl.when(zrow_ref[e] >= 0)
            def _():
                zcopy(e).wait()
            return c

        lax.fori_loop(0, 2 * N_EXPERTS, zstart, 0)
        lax.fori_loop(0, 2 * N_EXPERTS, zwait, 0)

    i = pl.program_id(0)
    slot = i % 2
    x = x_ref[...]
    hbuf[slot] = _tile_rows(_pack_bf16_pairs(x * _rms_scale(x) * g_ref[...]))

    def row_copy(tile, sub, k, dest):
        return pltpu.make_async_copy(hbuf.at[slot, tile, pl.ds(sub, 1), :], xs_ref.at[dest],
                                     sem.at[slot])

    _issue_rows(ROUTER_TILE, dest_ref, 0, row_copy)

    def drain(which):
        for _ in range(TOP_K):
            _wait_rows(hbuf.at[which], sem.at[which])

    @pl.when(i > 0)
    def _():
        drain(1 - slot)

    @pl.when(i == pl.num_programs(0) - 1)
    def _():
        drain(slot)


def _dispatch(x2d, g, dest, zrow, n_pad):
    t = x2d.shape[0]
    nt = t // ROUTER_TILE
    return pl.pallas_call(
        _dispatch_kernel,
        name="dispatch",
        grid_spec=pltpu.PrefetchScalarGridSpec(
            num_scalar_prefetch=1,
            grid=(nt,),
            in_specs=[
                pl.BlockSpec((1, ROUTER_TILE // SUBLANES, SLOTS_PER_TILE),
                             lambda i, z: (i, 0, 0), memory_space=pltpu.SMEM),
                pl.BlockSpec((ROUTER_TILE, D), lambda i, z: (i, 0)),
                pl.BlockSpec((1, D), lambda i, z: (0, 0)),
            ],
            out_specs=pl.BlockSpec(memory_space=pl.ANY),
            scratch_shapes=[
                pltpu.VMEM((2, ROUTER_TILE // SUBLANES, SUBLANES, D // 2), U32),
                pltpu.VMEM((MOE_BLOCK, 1, D // 2), U32),
                pltpu.SemaphoreType.DMA(()),
                pltpu.SemaphoreType.DMA((2,)),
            ],
        ),
        out_shape=jax.ShapeDtypeStruct((n_pad, 1, D // 2), U32),
        compiler_params=_params(1),
    )(zrow, dest, x2d, g.reshape(1, D))


def _experts_kernel(be_ref, nxt_ref, nu_ref, xs_ref, wg_ref, wu_ref, wd_ref, y_ref,
                    wg_f, wu_f, wd_f, wg_b, wu_b, wd_b, xbuf, ybuf, w_sem, in_sem, out_sem, *, layer):
    i = pl.program_id(0)
    n_used = nu_ref[0]
    slot = i % 2

    def block_rows(ref, blk):
        return ref.at[pl.ds(pl.multiple_of(blk * MOE_BLOCK, MOE_BLOCK), MOE_BLOCK), 0]

    def fetch(blk, into):
        return pltpu.make_async_copy(block_rows(xs_ref, blk), xbuf.at[into], in_sem.at[into])

    def write_back(blk, from_):
        return pltpu.make_async_copy(ybuf.at[from_], block_rows(y_ref, blk), out_sem.at[from_])

    def weight_copies(e):
        return [pltpu.make_async_copy(src.at[layer, e], dst, w_sem.at[n])
                for n, (src, dst) in enumerate(((wg_ref, wg_f), (wu_ref, wu_f), (wd_ref, wd_f)))]

    @pl.when(i == 0)
    def _():
        for c in weight_copies(be_ref[0]):
            c.start()
        fetch(0, 0).start()

    @pl.when(i + 1 < n_used)
    def _():
        fetch(i + 1, 1 - slot).start()

    prev = be_ref[jnp.maximum(i - 1, 0)]

    @pl.when((i < n_used) & ((i == 0) | (be_ref[i] != prev)))
    def _():
        for c in weight_copies(be_ref[i]):
            c.wait()
        wg_b[...] = wg_f[...].astype(BF16)
        wu_b[...] = wu_f[...].astype(BF16)
        wd_b[...] = wd_f[...].astype(BF16)

        @pl.when(nxt_ref[i] >= 0)
        def _():
            for c in weight_copies(nxt_ref[i]):
                c.start()

    @pl.when(i >= 2)
    def _():
        write_back(i - 2, slot).wait()

    @pl.when(i < n_used)
    def _():
        fetch(i, slot).wait()
        h_lo, h_hi = _unpack_bf16_pairs(xbuf[slot])
        half = D // 2
        a = (jnp.dot(h_lo, wg_b[:half], preferred_element_type=F32)
             + jnp.dot(h_hi, wg_b[half:], preferred_element_type=F32))
        u = (jnp.dot(h_lo, wu_b[:half], preferred_element_type=F32)
             + jnp.dot(h_hi, wu_b[half:], preferred_element_type=F32))
        hid = (a * _sigmoid(a) * u).astype(BF16)
        ybuf[slot] = jnp.dot(hid, wd_b[...], preferred_element_type=F32)

    @pl.when(i >= n_used)
    def _():
        ybuf[slot] = jnp.zeros((MOE_BLOCK, D), F32)

    write_back(i, slot).start()

    @pl.when(i == pl.num_programs(0) - 1)
    def _():
        write_back(i - 1, 1 - slot).wait()
        write_back(i, slot).wait()


def _experts(xs, block_e, next_e, n_used, layer, w_gate, w_up, w_down):
    n_pad = xs.shape[0]
    n_blocks = n_pad // MOE_BLOCK
    hbm = pl.BlockSpec(memory_space=pl.ANY)
    return pl.pallas_call(
        functools.partial(_experts_kernel, layer=layer),
        name="experts",
        grid_spec=pltpu.PrefetchScalarGridSpec(
            num_scalar_prefetch=3,
            grid=(n_blocks,),
            in_specs=[hbm, hbm, hbm, hbm],
            out_specs=hbm,
            scratch_shapes=[
                pltpu.VMEM((D, D_EXPERT), F32),
                pltpu.VMEM((D, D_EXPERT), F32),
                pltpu.VMEM((D_EXPERT, D), F32),
                pltpu.VMEM((D, D_EXPERT), BF16),
                pltpu.VMEM((D, D_EXPERT), BF16),
                pltpu.VMEM((D_EXPERT, D), BF16),
                pltpu.VMEM((2, MOE_BLOCK, D // 2), U32),
                pltpu.VMEM((2, MOE_BLOCK, D), F32),
                pltpu.SemaphoreType.DMA((3,)),
                pltpu.SemaphoreType.DMA((2,)),
                pltpu.SemaphoreType.DMA((2,)),
            ],
        ),
        out_shape=jax.ShapeDtypeStruct((n_pad, 1, D), F32),
        compiler_params=_params(1),
    )(block_e, next_e, n_used, xs, w_gate, w_up, w_down)


def _combine_kernel(*refs, final, project):
    (dest_ref, dest_next_ref, x_ref, gate_ref, y_ref, p_ref, gple_ref, wp_ref, wg_ref,
     gfin_ref) = refs[:10]
    if project:
        gq_ref, gkv_ref, wq_ref, wkv_ref, out_ref, qt_ref, k_ref, vt_ref = refs[10:18]
        ybuf0, ybuf1, sem = refs[18:]
    else:
        out_ref, ybuf0, ybuf1, sem = refs[10:]
    i = pl.program_id(0)
    ybuf = (ybuf0, ybuf1)

    def row_copy(into):
        def make(tile, sub, k, dest):
            return pltpu.make_async_copy(y_ref.at[dest], ybuf[into].at[k, tile, pl.ds(sub, 1), :],
                                         sem.at[into])
        return make

    def combine(half, slot):
        rows = slice(half * COMBINE_TILE, (half + 1) * COMBINE_TILE)
        gates = gate_ref[half].T
        y0 = ybuf[slot][0].reshape(COMBINE_TILE, D)
        y1 = ybuf[slot][1].reshape(COMBINE_TILE, D)
        x = x_ref[rows, :] + gates[:, 0:1] * y0 + gates[:, 1:2] * y1
        h = (x * _rms_scale(x) * gple_ref[...]).astype(BF16)
        gate = _sigmoid(jnp.dot(h, wg_ref[...], preferred_element_type=F32))
        proj = jnp.dot(p_ref[0, rows, :].astype(BF16), wp_ref[...], preferred_element_type=F32)
        x = x + proj * gate
        if final:
            x = x * _rms_scale(x) * gfin_ref[...]
        out_ref[rows, :] = x
        if project:
            q, k, v = _qkv_rows(x, gq_ref, gkv_ref, wq_ref, wkv_ref)
            qt_ref[0, :, rows] = q.T.astype(BF16)
            k_ref[0, rows, :] = k.astype(BF16)
            vt_ref[0, :, rows] = v.T.astype(BF16)

    def wait(slot):
        for k in range(TOP_K):
            _wait_rows(ybuf[slot].at[k], sem.at[slot])

    @pl.when(i == 0)
    def _():
        _issue_rows(COMBINE_TILE, dest_ref, 0, row_copy(0))

    wait(0)
    _issue_rows_inline(COMBINE_TILE, dest_ref, 1, row_copy(1))
    combine(0, 0)
    wait(1)
    _issue_rows_inline(COMBINE_TILE, dest_next_ref, 0, row_copy(0))
    combine(1, 1)

    @pl.when(i == pl.num_programs(0) - 1)
    def _():
        wait(0)


def _qkv_rows(x, gq_ref, gkv_ref, wq_ref, wkv_ref):
    xn = x * _rms_scale(x)
    hq = (xn * gq_ref[...]).astype(BF16)
    hkv = (xn * gkv_ref[...]).astype(BF16)
    q = jnp.dot(hq, wq_ref[...], preferred_element_type=F32) * (ATTN_SCALE * LOG2E)
    kv = jnp.dot(hkv, wkv_ref[...], preferred_element_type=F32)
    return q, kv[:, :D], kv[:, D:]


def _combine_ple(x2d, dest, gates, y, p3d, layer, g_ple, w_proj, w_gate, g_final, final, qkv=None):
    assert COMBINE_TILE == ROUTER_TILE
    t = x2d.shape[0]
    step = 2 * COMBINE_TILE
    nt = t // step
    dest_blk = (2, COMBINE_TILE // SUBLANES, SLOTS_PER_TILE)
    const = lambda i: (0, 0)
    resident = functools.partial(pl.BlockSpec, index_map=const, pipeline_mode=pl.Buffered(1))
    in_specs = [
        pl.BlockSpec(dest_blk, lambda i: (i, 0, 0), memory_space=pltpu.SMEM),
        pl.BlockSpec(dest_blk, lambda i: (jnp.minimum(i + 1, nt - 1), 0, 0),
                     memory_space=pltpu.SMEM),
        pl.BlockSpec((step, D), lambda i: (i, 0)),
        pl.BlockSpec((2, SUBLANES, COMBINE_TILE), lambda i: (i, 0, 0)),
        pl.BlockSpec(memory_space=pl.ANY),
        pl.BlockSpec((1, step, D_PLE), lambda i: (layer, i, 0)),
        pl.BlockSpec((1, D), const),
        resident((D_PLE, D)),
        resident((D, D)),
        pl.BlockSpec((1, D), const),
    ]
    args = [dest, dest, x2d, gates, y, p3d, g_ple.reshape(1, D), w_proj.astype(BF16),
            w_gate.astype(BF16), g_final.reshape(1, D)]
    out_specs = [pl.BlockSpec((step, D), lambda i: (i, 0))]
    out_shape = [jax.ShapeDtypeStruct((t, D), F32)]
    if qkv is not None:
        seq, g_q, g_kv, w_q, w_kv = qkv
        per_seq = seq // step
        in_specs += [pl.BlockSpec((1, D), const), pl.BlockSpec((1, D), const),
                     resident((D, D)), resident((D, 2 * D))]
        args += [g_q.reshape(1, D), g_kv.reshape(1, D), w_q.astype(BF16), w_kv.astype(BF16)]
        feat_major = pl.BlockSpec((1, D, step), lambda i: (i // per_seq, 0, i % per_seq))
        row_major = pl.BlockSpec((1, step, D), lambda i: (i // per_seq, i % per_seq, 0))
        out_specs += [feat_major, row_major, feat_major]
        out_shape += [jax.ShapeDtypeStruct((t // seq, D, seq), BF16),
                      jax.ShapeDtypeStruct((t // seq, seq, D), BF16),
                      jax.ShapeDtypeStruct((t // seq, D, seq), BF16)]
    outs = pl.pallas_call(
        functools.partial(_combine_kernel, final=final, project=qkv is not None),
        name="combine_final" if final else "combine",
        grid=(nt,),
        in_specs=in_specs,
        out_specs=out_specs,
        out_shape=out_shape,
        scratch_shapes=[
            pltpu.VMEM((TOP_K, COMBINE_TILE // SUBLANES, SUBLANES, D), F32),
            pltpu.VMEM((TOP_K, COMBINE_TILE // SUBLANES, SUBLANES, D), F32),
            pltpu.SemaphoreType.DMA((2,)),
        ],
        compiler_params=_params(1),
    )(*args)
    return outs[0] if qkv is None else tuple(outs)


def _moe_ple(x, p, layer, norm_ffn, w_group, b_group, w_expert, b_expert, w_gate, w_up, w_down,
             norm_ple, ple_w_proj, ple_w_gate, final_norm, final, qkv=None):
    b, s, _ = x.shape
    t = b * s
    x2d = x.reshape(t, D)
    ids, gates, counts = _router(x2d, norm_ffn, w_group, b_group, w_expert, b_expert)

    counts = counts[:, 0]
    padded = (counts + MOE_BLOCK - 1) // MOE_BLOCK * MOE_BLOCK
    pad_end = jnp.cumsum(padded)
    pad_start = pad_end - padded
    n_blocks = t * TOP_K // MOE_BLOCK + N_EXPERTS
    n_pad = n_blocks * MOE_BLOCK
    e = ids[:, 0:TOP_K, :]
    r = ids[:, TOP_K:2 * TOP_K, :]
    experts = jnp.arange(N_EXPERTS, dtype=jnp.int32)
    dest = r + jnp.sum(jnp.where(e[..., None] == experts, pad_start, 0), axis=-1)
    nt = t // ROUTER_TILE
    dest = dest.reshape(nt, TOP_K, ROUTER_TILE // SUBLANES, SUBLANES).transpose(0, 2, 1, 3)
    dest = dest.reshape(nt, ROUTER_TILE // SUBLANES, SLOTS_PER_TILE).astype(jnp.int32)
    block_row = jnp.arange(n_blocks, dtype=jnp.int32) * MOE_BLOCK
    block_e = jnp.minimum(jnp.sum(pad_end[None, :] <= block_row[:, None], axis=-1),
                          N_EXPERTS - 1).astype(jnp.int32)
    n_used = (pad_end[-1:] // MOE_BLOCK).astype(jnp.int32)
    tail = (n_used[0] + jnp.arange(N_EXPERTS, dtype=jnp.int32)) * MOE_BLOCK
    zrow = jnp.concatenate([jnp.where(padded > 0, pad_end - MOE_BLOCK, -1),
                            jnp.where(tail < n_pad, tail, -1)]).astype(jnp.int32)

    xs = _dispatch(x2d, norm_ffn, dest, zrow, n_pad)
    later = jnp.where((experts[None, :] > block_e[:, None]) & (padded[None, :] > 0), experts[None, :],
                      N_EXPERTS)
    next_e = jnp.min(later, axis=-1)
    next_e = jnp.where(next_e < N_EXPERTS, next_e, -1).astype(jnp.int32)
    y = _experts(xs, block_e, next_e, n_used, layer, w_gate, w_up, w_down)
    out = _combine_ple(x2d, dest, gates, y, p.reshape(-1, t, D_PLE), layer,
                       norm_ple, ple_w_proj, ple_w_gate, final_norm, final,
                       None if qkv is None else (s,) + tuple(qkv))
    if qkv is None:
        return out.reshape(b, s, D)
    return (out[0].reshape(b, s, D),) + out[1:]


Q_GROUP = 2 * CHUNK
G_BAND = BAND + CHUNK
PAIR = 2 * B_HEAD_DIM
ONES_ROWS = BF16_SUBLANES


def _attn_kernel(qt_ref, kp_ref, kc_ref, vtp_ref, vtc_ref, bias_ref, x_ref, wo_ref, out_ref, o_scr,
                 s_scr0, s_scr1, s_scr2, s_scr3, p_scr0, p_scr1, p_scr2, p_scr3):
    drow = lax.broadcasted_iota(jnp.int32, (PAIR, Q_GROUP), 0)
    first_head = drow < B_HEAD_DIM
    s_scr = (s_scr0, s_scr1, s_scr2, s_scr3)
    p_scr = (p_scr0, p_scr1, p_scr2, p_scr3)

    def attend(first_tile):
        units = [(g, pr) for g in range(SEQ_TILE // Q_GROUP) for pr in range(B_HEADS // 2)]

        def geometry(g):
            w0 = g * Q_GROUP
            n_prev = SEQ_TILE - w0
            return w0, n_prev, G_BAND - n_prev

        def keys(g):
            return slice(geometry(g)[1], None) if first_tile else slice(None)

        def scores(unit, s_ref):
            g, pr = unit
            w0, n_prev, n_cur = geometry(g)
            feat = slice(pr * PAIR, (pr + 1) * PAIR)
            qt = qt_ref[0, feat, w0:w0 + Q_GROUP]
            zero = jnp.zeros_like(qt)
            qblk = jnp.concatenate([jnp.where(first_head, qt, zero),
                                    jnp.where(first_head, zero, qt)], axis=1)
            if first_tile:
                kb = kc_ref[0, :n_cur, feat]
            else:
                kb = jnp.concatenate([kp_ref[0, w0:, feat], kc_ref[0, :n_cur, feat]], axis=0)
            s_ref[keys(g), :] = (jnp.dot(kb, qblk, preferred_element_type=F32)
                                 + bias_ref[pr, keys(g), :])

        def weights(unit, s_ref, p_ref):
            g, _ = unit
            s = s_ref[keys(g), :]
            m = jnp.max(s, axis=0, keepdims=True)
            p_ref[keys(g), :] = jnp.exp2(s - m).astype(BF16)

        def values(unit, p_ref):
            g, pr = unit
            w0, n_prev, n_cur = geometry(g)
            feat = slice(pr * PAIR, (pr + 1) * PAIR)
            if first_tile:
                vt = vtc_ref[0, feat, :n_cur]
            else:
                vt = jnp.concatenate([vtp_ref[0, feat, w0:], vtc_ref[0, feat, :n_cur]], axis=1)
            ones = jnp.ones((ONES_ROWS, vt.shape[1]), BF16)
            ot = jnp.dot(jnp.concatenate([vt, ones], axis=0), p_ref[keys(g), :],
                         preferred_element_type=F32)
            inv = 1.0 / ot[PAIR:PAIR + 1, :]
            ot = jnp.where(first_head, ot[:PAIR, :Q_GROUP] * inv[:, :Q_GROUP],
                           ot[:PAIR, Q_GROUP:] * inv[:, Q_GROUP:])
            o_scr[w0:w0 + Q_GROUP, feat] = ot.T.astype(BF16)

        n_units = len(units)
        depth = len(s_scr)
        for n in range(depth):
            scores(units[n], s_scr[n])
        for n in range(depth - 1):
            weights(units[n], s_scr[n], p_scr[n])
        for n, unit in enumerate(units):
            if n + depth < n_units:
                scores(units[n + depth], s_scr[n % depth])
            if n + depth - 1 < n_units:
                weights(units[n + depth - 1], s_scr[(n + depth - 1) % depth],
                        p_scr[(n + depth - 1) % depth])
            values(unit, p_scr[n % depth])

    @pl.when(pl.program_id(1) == 0)
    def _():
        attend(True)

    @pl.when(pl.program_id(1) > 0)
    def _():
        attend(False)

    out_ref[0] = x_ref[0] + jnp.dot(o_scr[...], wo_ref[...], preferred_element_type=F32)


def _attn(x, qt, k, vt, bias_t, w_o):
    b, s, _ = x.shape
    cur = lambda bi, si: (bi, si, 0)
    prev = lambda bi, si: (bi, jnp.maximum(si - 1, 0), 0)
    cur_t = lambda bi, si: (bi, 0, si)
    prev_t = lambda bi, si: (bi, 0, jnp.maximum(si - 1, 0))
    blk = (1, SEQ_TILE, D)
    blk_t = (1, D, SEQ_TILE)
    return pl.pallas_call(
        _attn_kernel,
        name="attn",
        grid=(b, s // SEQ_TILE),
        in_specs=[
            pl.BlockSpec(blk_t, cur_t),
            pl.BlockSpec(blk, prev),
            pl.BlockSpec(blk, cur),
            pl.BlockSpec(blk_t, prev_t),
            pl.BlockSpec(blk_t, cur_t),
            pl.BlockSpec((B_HEADS // 2, G_BAND, 2 * Q_GROUP), lambda bi, si: (0, 0, 0),
                         pipeline_mode=pl.Buffered(1)),
            pl.BlockSpec(blk, cur),
            pl.BlockSpec((D, D), lambda bi, si: (0, 0), pipeline_mode=pl.Buffered(1)),
        ],
        out_specs=pl.BlockSpec(blk, cur),
        out_shape=jax.ShapeDtypeStruct(x.shape, F32),
        scratch_shapes=[pltpu.VMEM((SEQ_TILE, D), BF16)]
        + [pltpu.VMEM((G_BAND, 2 * Q_GROUP), F32)] * 4
        + [pltpu.VMEM((G_BAND, 2 * Q_GROUP), BF16)] * 4,
        compiler_params=_params(2),
    )(qt, k, k, vt, vt, bias_t, x, w_o.astype(BF16))


def _group_bias(table):
    band = _band_bias(table) * LOG2E
    pad = lambda lo, hi: jnp.pad(band, ((0, 0), (0, 0), (lo, hi)), constant_values=NEG_INF)
    both = jnp.concatenate([pad(0, CHUNK), pad(CHUNK, 0)], axis=1)
    both = both.reshape(B_HEADS // 2, 2, Q_GROUP, G_BAND)
    return both.transpose(0, 3, 1, 2).reshape(B_HEADS // 2, G_BAND, 2 * Q_GROUP)


def _band_bias(table):
    n_rel = REL_MAX - REL_MIN + 1
    span = BAND + CHUNK - 1
    head = jnp.broadcast_to(table[:, n_rel - 1:], (table.shape[0], span - n_rel))
    ext = jnp.concatenate([head, table[:, ::-1]], axis=1)
    rows = [ext[:, CHUNK - 1 - q:CHUNK - 1 - q + BAND] for q in range(CHUNK)]
    return jnp.stack(rows, axis=1)


def kernel(x, p, a_w_in, a_lb_logits, a_out_norm, a_w_o, kv_norm, w_kv, b_w_q, b_rel_bias, b_w_o,
           norm_mix, norm_ffn, norm_ple, moe_w_group, moe_b_group, moe_w_expert, moe_b_expert,
           moe_w_gate, moe_w_up, moe_w_down, ple_w_proj, ple_w_gate, final_norm):
    b, s, _ = x.shape
    lower_bounds = jnp.cumsum(jax.nn.softmax(a_lb_logits.astype(F32), axis=0), axis=0)

    def moe(xi, i, final, qkv=None):
        return _moe_ple(xi, p, i, norm_ffn[i], moe_w_group[i], moe_b_group[i], moe_w_expert[i],
                        moe_b_expert[i], moe_w_gate, moe_w_up, moe_w_down, norm_ple[i],
                        ple_w_proj[i], ple_w_gate[i], final_norm, final, qkv)

    x = _mixer_a(x, norm_mix[0], a_w_in[0], lower_bounds[0], a_out_norm[0], a_w_o[0])
    x, qt, k, vt = moe(x, 0, False, (norm_mix[1], kv_norm, b_w_q[0], w_kv))

    x = _attn(x, qt, k, vt, _group_bias(b_rel_bias[0].astype(F32)), b_w_o[0])
    x = moe(x, 1, True)
    return x
```

```python
import functools

import jax
import jax.numpy as jnp
from jax import lax
from jax.experimental import pallas as pl
from jax.experimental.pallas import tpu as pltpu

F32 = jnp.float32
BF16 = jnp.bfloat16
U32 = jnp.uint32

D = 1024
CHUNK = 64
A_HEADS = 8
A_HEAD_DIM = 128
B_HEADS = 16
B_HEAD_DIM = 64
LEFT_CHUNKS = 8
BAND = (LEFT_CHUNKS + 1) * CHUNK
REL_MIN = -(CHUNK - 1)
REL_MAX = 256
ATTN_SCALE = B_HEAD_DIM ** -0.5
N_GROUPS = 4
EXPERTS_PER_GROUP = 8
N_EXPERTS = 32
TOP_K = 2
D_EXPERT = 512
MOE_BLOCK = 512
D_PLE = 256
EPS = 1e-6
NEG_INF = -1e30
LOG2E = 1.4426950408889634

SUBLANES = 8
LANES = 128
BF16_SUBLANES = 16
V7X_VMEM_BYTES = 64 * 1024 * 1024

SEQ_TILE = 512
ROUTER_TILE = 512
COMBINE_TILE = 512
MIXER_DEPTH = 6
ATTN_DEPTH = 6
ROUTER_ROWS = -(-(N_EXPERTS + N_GROUPS) // BF16_SUBLANES) * BF16_SUBLANES
VMEM_LIMIT = V7X_VMEM_BYTES - 8 * 1024 * 1024


def _params(n_axes, vmem=VMEM_LIMIT):
    return pltpu.CompilerParams(dimension_semantics=("arbitrary",) * n_axes,
                                vmem_limit_bytes=vmem)


def _rms_scale(x):
    return lax.rsqrt(jnp.mean(x * x, axis=-1, keepdims=True) + EPS)


def _sigmoid(x):
    return 0.5 * jnp.tanh(0.5 * x) + 0.5


def _mixer_a_kernel(x_ref, g_ref, win_ref, lb_ref, onorm_ref, wo_ref, out_ref,
                    proj_scr, o_scr, state_scr, g_scr0, g_scr1, k_scr0, k_scr1, *stage_scr):
    @pl.when(pl.program_id(1) == 0)
    def _():
        state_scr[...] = jnp.zeros_like(state_scr)

    x = x_ref[0]
    h = (x * _rms_scale(x) * g_ref[...]).astype(BF16)
    proj_scr[...] = jnp.dot(h, win_ref[...], preferred_element_type=F32)

    row = lax.broadcasted_iota(jnp.int32, (CHUNK, CHUNK), 0)
    col = lax.broadcasted_iota(jnp.int32, (CHUNK, CHUNK), 1)
    causal = row >= col
    tril = causal.astype(BF16)
    lb = lb_ref[...]
    onorm = onorm_ref[...]

    g_scr, k_scr = (g_scr0, g_scr1), (k_scr0, k_scr1)
    depth = MIXER_DEPTH
    qd_scr, kt_scr, att_scr = (stage_scr[0:depth], stage_scr[depth:2 * depth],
                               stage_scr[2 * depth:3 * depth])
    n_chunks = SEQ_TILE // CHUNK
    units = [(c, hd) for c in range(n_chunks) for hd in range(A_HEADS)]

    def rows_of(c):
        return slice(c * CHUNK, (c + 1) * CHUNK)

    def decay(c):
        f = lb + (1.0 - lb) * _sigmoid(proj_scr[rows_of(c), D:2 * D])
        logf = jnp.log(f)
        hi = logf.astype(BF16)
        lo = (logf - hi.astype(F32)).astype(BF16)
        g_scr[c % 2][...] = (jnp.dot(tril, hi, preferred_element_type=F32)
                             + jnp.dot(tril, lo, preferred_element_type=F32))
        k_scr[c % 2][...] = 1.0 - f

    def intra(n):
        c, hd = units[n]
        sl = slice(hd * A_HEAD_DIM, (hd + 1) * A_HEAD_DIM)
        gh = g_scr[c % 2][:, sl]
        g_last = gh[CHUNK - 1:CHUNK, :]
        k = k_scr[c % 2][:, sl]
        q_dec = (proj_scr[rows_of(c), sl] * jnp.exp(gh)).astype(BF16)
        k_inv = k * jnp.exp(-gh)
        qd_scr[n % depth][...] = q_dec
        kt_scr[n % depth][...] = (k_inv * jnp.exp(g_last)).astype(BF16)
        k_inv = k_inv.astype(BF16)
        att = lax.dot_general(q_dec, k_inv, (((1,), (1,)), ((), ())),
                              preferred_element_type=F32)
        att_scr[n % depth][...] = jnp.where(causal, att, 0.0).astype(BF16)

    def output(n):
        c, hd = units[n]
        sl = slice(hd * A_HEAD_DIM, (hd + 1) * A_HEAD_DIM)
        rows = rows_of(c)
        g_last = g_scr[c % 2][CHUNK - 1:CHUNK, sl]
        v = proj_scr[rows, 2 * D + hd * A_HEAD_DIM:2 * D + (hd + 1) * A_HEAD_DIM]
        st = state_scr[hd]
        o = (jnp.dot(att_scr[n % depth][...], v.astype(BF16), preferred_element_type=F32)
             + lax.dot_general(qd_scr[n % depth][...], st.astype(BF16), (((1,), (1,)), ((), ())),
                               preferred_element_type=F32))
        v_t = v.T.astype(BF16)
        state_scr[hd] = st * jnp.exp(g_last) + jnp.dot(v_t, kt_scr[n % depth][...],
                                                       preferred_element_type=F32)
        o = o * _rms_scale(o)
        og = proj_scr[rows, 3 * D + hd * A_HEAD_DIM:3 * D + (hd + 1) * A_HEAD_DIM]
        o = o * onorm[:, sl] * (og * _sigmoid(og))
        o_scr[rows, sl] = o.astype(BF16)

    decay(0)
    for n in range(depth - 1):
        intra(n)
    for n, (c, hd) in enumerate(units):
        if hd == 0 and c + 1 < n_chunks:
            decay(c + 1)
        if n + depth - 1 < len(units):
            intra(n + depth - 1)
        output(n)
    out_ref[0] = x + jnp.dot(o_scr[...], wo_ref[...], preferred_element_type=F32)


def _mixer_a(x, g, w_in, lb, out_norm, w_o):
    b, s, _ = x.shape
    const = lambda bi, si: (0, 0)
    return pl.pallas_call(
        _mixer_a_kernel,
        name="mixer_a",
        grid=(b, s // SEQ_TILE),
        in_specs=[
            pl.BlockSpec((1, SEQ_TILE, D), lambda bi, si: (bi, si, 0)),
            pl.BlockSpec((1, D), const),
            pl.BlockSpec((D, 4 * D), const, pipeline_mode=pl.Buffered(1)),
            pl.BlockSpec((1, D), const),
            pl.BlockSpec((1, D), const),
            pl.BlockSpec((D, D), const, pipeline_mode=pl.Buffered(1)),
        ],
        out_specs=pl.BlockSpec((1, SEQ_TILE, D), lambda bi, si: (bi, si, 0)),
        out_shape=jax.ShapeDtypeStruct(x.shape, F32),
        scratch_shapes=[
            pltpu.VMEM((SEQ_TILE, 4 * D), F32),
            pltpu.VMEM((SEQ_TILE, D), BF16),
            pltpu.VMEM((A_HEADS, A_HEAD_DIM, A_HEAD_DIM), F32),
        ] + [pltpu.VMEM((CHUNK, D), F32)] * 4
          + [pltpu.VMEM((CHUNK, A_HEAD_DIM), BF16)] * (2 * MIXER_DEPTH)
          + [pltpu.VMEM((CHUNK, CHUNK), BF16)] * MIXER_DEPTH,
        compiler_params=_params(2),
    )(x, g.reshape(1, D), w_in.astype(BF16), lb.reshape(1, D), out_norm.reshape(1, D),
      w_o.astype(BF16))


def _router_kernel(x_ref, g_ref, wr_ref, br_ref, ids_ref, gates_ref, counts_ref, cnt_scr,
                   before_scr):
    tm = ROUTER_TILE

    @pl.when(pl.program_id(0) == 0)
    def _():
        cnt_scr[...] = jnp.zeros_like(cnt_scr)
        tr = lax.broadcasted_iota(jnp.int32, (tm, tm), 0)
        tc = lax.broadcasted_iota(jnp.int32, (tm, tm), 1)
        before_scr[...] = (tr < tc).astype(BF16)

    x = x_ref[...]
    h = x * _rms_scale(x) * g_ref[...]
    h_hi = h.astype(BF16)
    h_lo = (h - h_hi.astype(F32)).astype(BF16)
    nt = (((1,), (1,)), ((), ()))
    both = lax.dot_general(wr_ref[...], h_hi, nt, preferred_element_type=F32)
    cross = lax.dot_general(wr_ref[0:ROUTER_ROWS], h_lo, nt, preferred_element_type=F32)
    logits = both[0:ROUTER_ROWS] + both[ROUTER_ROWS:] + cross + br_ref[...]
    el = logits[0:N_EXPERTS]
    gl = logits[N_EXPERTS:ROUTER_ROWS]
    grow = lax.broadcasted_iota(jnp.int32, gl.shape, 0)
    gl = jnp.where(grow < N_GROUPS, gl, -jnp.inf)
    gmax = jnp.max(gl, axis=0, keepdims=True)
    gsum = jnp.sum(jnp.exp(gl - gmax), axis=0, keepdims=True)
    grp_w = 1.0 / gsum
    gidx = jnp.min(jnp.where(gl == gmax, grow, N_GROUPS), axis=0, keepdims=True)

    erow = lax.broadcasted_iota(jnp.int32, el.shape, 0)
    masked = jnp.where((erow // EXPERTS_PER_GROUP) == gidx, el, -jnp.inf)
    top1 = jnp.max(masked, axis=0, keepdims=True)
    i1 = jnp.min(jnp.where(masked == top1, erow, N_EXPERTS), axis=0, keepdims=True)
    masked2 = jnp.where(erow == i1, -jnp.inf, masked)
    top2 = jnp.max(masked2, axis=0, keepdims=True)
    i2 = jnp.min(jnp.where(masked2 == top2, erow, N_EXPERTS), axis=0, keepdims=True)
    e2 = jnp.exp(top2 - top1)
    denom = 1.0 + e2
    g1 = grp_w * (1.0 / denom)
    g2 = grp_w * (e2 / denom)

    sel1 = erow == i1
    sel2 = erow == i2
    onehot = (sel1 | sel2).astype(BF16)
    prefix = jnp.dot(onehot, before_scr[...], preferred_element_type=F32) + cnt_scr[...]
    r1 = jnp.sum(jnp.where(sel1, prefix, 0.0), axis=0, keepdims=True)
    r2 = jnp.sum(jnp.where(sel2, prefix, 0.0), axis=0, keepdims=True)
    cnt_scr[...] += jnp.sum(onehot.astype(F32), axis=1, keepdims=True)

    zi = jnp.zeros((SUBLANES - 2 * TOP_K, tm), jnp.int32)
    ids_ref[0] = jnp.concatenate(
        [i1, i2, r1.astype(jnp.int32), r2.astype(jnp.int32), zi], axis=0)
    gates_ref[0] = jnp.concatenate([g1, g2, jnp.zeros((SUBLANES - TOP_K, tm), F32)], axis=0)
    counts_ref[...] = jnp.broadcast_to(cnt_scr[...], counts_ref.shape).astype(jnp.int32)


def _router(x2d, g, w_group, b_group, w_expert, b_expert):
    t = x2d.shape[0]
    nt = t // ROUTER_TILE
    pad = ROUTER_ROWS - N_EXPERTS - N_GROUPS
    wr = jnp.concatenate([w_expert.T, w_group.T, jnp.zeros((pad, D), F32)], axis=0)
    wr_hi = wr.astype(BF16)
    wr_lo = (wr - wr_hi.astype(F32)).astype(BF16)
    br = jnp.concatenate([b_expert, b_group, jnp.zeros((pad,), F32)]).reshape(ROUTER_ROWS, 1)
    const = lambda i: (0, 0)
    return pl.pallas_call(
        _router_kernel,
        name="router",
        grid=(nt,),
        in_specs=[
            pl.BlockSpec((ROUTER_TILE, D), lambda i: (i, 0)),
            pl.BlockSpec((1, D), const),
            pl.BlockSpec((2 * ROUTER_ROWS, D), const),
            pl.BlockSpec((ROUTER_ROWS, 1), const),
        ],
        out_specs=[
            pl.BlockSpec((1, SUBLANES, ROUTER_TILE), lambda i: (i, 0, 0)),
            pl.BlockSpec((1, SUBLANES, ROUTER_TILE), lambda i: (i, 0, 0)),
            pl.BlockSpec((N_EXPERTS, LANES), const),
        ],
        out_shape=[
            jax.ShapeDtypeStruct((nt, SUBLANES, ROUTER_TILE), jnp.int32),
            jax.ShapeDtypeStruct((nt, SUBLANES, ROUTER_TILE), F32),
            jax.ShapeDtypeStruct((N_EXPERTS, LANES), jnp.int32),
        ],
        scratch_shapes=[pltpu.VMEM((N_EXPERTS, 1), F32),
                        pltpu.VMEM((ROUTER_TILE, ROUTER_TILE), BF16)],
        compiler_params=_params(1),
    )(x2d, g.reshape(1, D), jnp.concatenate([wr_hi, wr_lo], axis=0), br)


SLOTS_PER_TILE = SUBLANES * TOP_K


def _issue_tile_rows(dest_ref, which, j, make_copy):
    for u in range(SUBLANES):
        for k in range(TOP_K):
            dest = dest_ref[which, j, k * SUBLANES + u]
            make_copy(j, u, k, dest).start(priority=(u * TOP_K + k) % 2)


def _issue_rows(n_rows, dest_ref, which, make_copy):
    def body(j, c):
        _issue_tile_rows(dest_ref, which, j, make_copy)
        return c

    lax.fori_loop(0, n_rows // SUBLANES, body, 0)


def _issue_rows_inline(n_rows, dest_ref, which, make_copy):
    for j in range(n_rows // SUBLANES):
        _issue_tile_rows(dest_ref, which, j, make_copy)


def _pack_bf16_pairs(x):
    half = x.shape[1] // 2
    lo = lax.bitcast_convert_type(x[:, :half].astype(BF16).astype(F32), U32)
    hi = lax.bitcast_convert_type(x[:, half:].astype(BF16).astype(F32), U32)
    return (hi & jnp.uint32(0xFFFF0000)) | (lo >> 16)


def _unpack_bf16_pairs(u):
    lo = lax.bitcast_convert_type(u << 16, F32).astype(BF16)
    hi = lax.bitcast_convert_type(u & jnp.uint32(0xFFFF0000), F32).astype(BF16)
    return lo, hi


def _tile_rows(x):
    return x.reshape(x.shape[0] // SUBLANES, SUBLANES, x.shape[1])


def _wait_rows(buf_ref, sem):
    pltpu.make_async_copy(buf_ref, buf_ref, sem).wait()


def _dispatch_kernel(zrow_ref, dest_ref, x_ref, g_ref, xs_ref, hbuf, zbuf, zsem, sem):
    @pl.when(pl.program_id(0) == 0)
    def _():
        zbuf[...] = jnp.zeros_like(zbuf)

        def zcopy(e):
            return pltpu.make_async_copy(zbuf, xs_ref.at[pl.ds(zrow_ref[e], MOE_BLOCK)], zsem)

        def zstart(e, c):
            @pl.when(zrow_ref[e] >= 0)
            def _():
                zcopy(e).start()
            return c

        def zwait(e, c):
            @pl.when(zrow_ref[e] >= 0)
            def _():
                zcopy(e).wait()
            return c

        lax.fori_loop(0, 2 * N_EXPERTS, zstart, 0)
        lax.fori_loop(0, 2 * N_EXPERTS, zwait, 0)

    i = pl.program_id(0)
    slot = i % 2
    x = x_ref[...]
    hbuf[slot] = _tile_rows(_pack_bf16_pairs(x * _rms_scale(x) * g_ref[...]))

    def row_copy(tile, sub, k, dest):
        return pltpu.make_async_copy(hbuf.at[slot, tile, pl.ds(sub, 1), :], xs_ref.at[dest],
                                     sem.at[slot])

    _issue_rows(ROUTER_TILE, dest_ref, 0, row_copy)

    def drain(which):
        for _ in range(TOP_K):
            _wait_rows(hbuf.at[which], sem.at[which])

    @pl.when(i > 0)
    def _():
        drain(1 - slot)

    @pl.when(i == pl.num_programs(0) - 1)
    def _():
        drain(slot)


def _dispatch(x2d, g, dest, zrow, n_pad):
    t = x2d.shape[0]
    nt = t // ROUTER_TILE
    return pl.pallas_call(
        _dispatch_kernel,
        name="dispatch",
        grid_spec=pltpu.PrefetchScalarGridSpec(
            num_scalar_prefetch=1,
            grid=(nt,),
            in_specs=[
                pl.BlockSpec((1, ROUTER_TILE // SUBLANES, SLOTS_PER_TILE),
                             lambda i, z: (i, 0, 0), memory_space=pltpu.SMEM),
                pl.BlockSpec((ROUTER_TILE, D), lambda i, z: (i, 0)),
                pl.BlockSpec((1, D), lambda i, z: (0, 0)),
            ],
            out_specs=pl.BlockSpec(memory_space=pl.ANY),
            scratch_shapes=[
                pltpu.VMEM((2, ROUTER_TILE // SUBLANES, SUBLANES, D // 2), U32),
                pltpu.VMEM((MOE_BLOCK, 1, D // 2), U32),
                pltpu.SemaphoreType.DMA(()),
                pltpu.SemaphoreType.DMA((2,)),
            ],
        ),
        out_shape=jax.ShapeDtypeStruct((n_pad, 1, D // 2), U32),
        compiler_params=_params(1),
    )(zrow, dest, x2d, g.reshape(1, D))


def _experts_kernel(be_ref, nxt_ref, nu_ref, xs_ref, wg_ref, wu_ref, wd_ref, y_ref,
                    wg_f, wu_f, wd_f, wg_b, wu_b, wd_b, xbuf, ybuf, w_sem, in_sem, out_sem, *, layer):
    i = pl.program_id(0)
    n_used = nu_ref[0]
    slot = i % 2

    def block_rows(ref, blk):
        return ref.at[pl.ds(pl.multiple_of(blk * MOE_BLOCK, MOE_BLOCK), MOE_BLOCK), 0]

    def fetch(blk, into):
        return pltpu.make_async_copy(block_rows(xs_ref, blk), xbuf.at[into], in_sem.at[into])

    def write_back(blk, from_):
        return pltpu.make_async_copy(ybuf.at[from_], block_rows(y_ref, blk), out_sem.at[from_])

    def weight_copies(e):
        return [pltpu.make_async_copy(src.at[layer, e], dst, w_sem.at[n])
                for n, (src, dst) in enumerate(((wg_ref, wg_f), (wu_ref, wu_f), (wd_ref, wd_f)))]

    @pl.when(i == 0)
    def _():
        for c in weight_copies(be_ref[0]):
            c.start()
        fetch(0, 0).start()

    @pl.when(i + 1 < n_used)
    def _():
        fetch(i + 1, 1 - slot).start()

    prev = be_ref[jnp.maximum(i - 1, 0)]

    @pl.when((i < n_used) & ((i == 0) | (be_ref[i] != prev)))
    def _():
        for c in weight_copies(be_ref[i]):
            c.wait()
        wg_b[...] = wg_f[...].astype(BF16)
        wu_b[...] = wu_f[...].astype(BF16)
        wd_b[...] = wd_f[...].astype(BF16)

        @pl.when(nxt_ref[i] >= 0)
        def _():
            for c in weight_copies(nxt_ref[i]):
                c.start()

    @pl.when(i >= 2)
    def _():
        write_back(i - 2, slot).wait()

    @pl.when(i < n_used)
    def _():
        fetch(i, slot).wait()
        h_lo, h_hi = _unpack_bf16_pairs(xbuf[slot])
        half = D // 2
        a = (jnp.dot(h_lo, wg_b[:half], preferred_element_type=F32)
             + jnp.dot(h_hi, wg_b[half:], preferred_element_type=F32))
        u = (jnp.dot(h_lo, wu_b[:half], preferred_element_type=F32)
             + jnp.dot(h_hi, wu_b[half:], preferred_element_type=F32))
        hid = (a * _sigmoid(a) * u).astype(BF16)
        ybuf[slot] = jnp.dot(hid, wd_b[...], preferred_element_type=F32)

    @pl.when(i >= n_used)
    def _():
        ybuf[slot] = jnp.zeros((MOE_BLOCK, D), F32)

    write_back(i, slot).start()

    @pl.when(i == pl.num_programs(0) - 1)
    def _():
        write_back(i - 1, 1 - slot).wait()
        write_back(i, slot).wait()


def _experts(xs, block_e, next_e, n_used, layer, w_gate, w_up, w_down):
    n_pad = xs.shape[0]
    n_blocks = n_pad // MOE_BLOCK
    hbm = pl.BlockSpec(memory_space=pl.ANY)
    return pl.pallas_call(
        functools.partial(_experts_kernel, layer=layer),
        name="experts",
        grid_spec=pltpu.PrefetchScalarGridSpec(
            num_scalar_prefetch=3,
            grid=(n_blocks,),
            in_specs=[hbm, hbm, hbm, hbm],
            out_specs=hbm,
            scratch_shapes=[
                pltpu.VMEM((D, D_EXPERT), F32),
                pltpu.VMEM((D, D_EXPERT), F32),
                pltpu.VMEM((D_EXPERT, D), F32),
                pltpu.VMEM((D, D_EXPERT), BF16),
                pltpu.VMEM((D, D_EXPERT), BF16),
                pltpu.VMEM((D_EXPERT, D), BF16),
                pltpu.VMEM((2, MOE_BLOCK, D // 2), U32),
                pltpu.VMEM((2, MOE_BLOCK, D), F32),
                pltpu.SemaphoreType.DMA((3,)),
                pltpu.SemaphoreType.DMA((2,)),
                pltpu.SemaphoreType.DMA((2,)),
            ],
        ),
        out_shape=jax.ShapeDtypeStruct((n_pad, 1, D), F32),
        compiler_params=_params(1),
    )(block_e, next_e, n_used, xs, w_gate, w_up, w_down)


def _combine_kernel(*refs, final, project):
    (dest_ref, dest_next_ref, x_ref, gate_ref, y_ref, p_ref, gple_ref, wp_ref, wg_ref,
     gfin_ref) = refs[:10]
    if project:
        gq_ref, gkv_ref, wq_ref, wkv_ref, out_ref, qt_ref, k_ref, vt_ref = refs[10:18]
        ybuf0, ybuf1, sem = refs[18:]
    else:
        out_ref, ybuf0, ybuf1, sem = refs[10:]
    i = pl.program_id(0)
    ybuf = (ybuf0, ybuf1)

    def row_copy(into):
        def make(tile, sub, k, dest):
            return pltpu.make_async_copy(y_ref.at[dest], ybuf[into].at[k, tile, pl.ds(sub, 1), :],
                                         sem.at[into])
        return make

    def combine(half, slot):
        rows = slice(half * COMBINE_TILE, (half + 1) * COMBINE_TILE)
        gates = gate_ref[half].T
        y0 = ybuf[slot][0].reshape(COMBINE_TILE, D)
        y1 = ybuf[slot][1].reshape(COMBINE_TILE, D)
        x = x_ref[rows, :] + gates[:, 0:1] * y0 + gates[:, 1:2] * y1
        h = (x * _rms_scale(x) * gple_ref[...]).astype(BF16)
        gate = _sigmoid(jnp.dot(h, wg_ref[...], preferred_element_type=F32))
        proj = jnp.dot(p_ref[0, rows, :].astype(BF16), wp_ref[...], preferred_element_type=F32)
        x = x + proj * gate
        if final:
            x = x * _rms_scale(x) * gfin_ref[...]
        out_ref[rows, :] = x
        if project:
            q, k, v = _qkv_rows(x, gq_ref, gkv_ref, wq_ref, wkv_ref)
            qt_ref[0, :, rows] = q.T.astype(BF16)
            k_ref[0, rows, :] = k.astype(BF16)
            vt_ref[0, :, rows] = v.T.astype(BF16)

    def wait(slot):
        for k in range(TOP_K):
            _wait_rows(ybuf[slot].at[k], sem.at[slot])

    @pl.when(i == 0)
    def _():
        _issue_rows(COMBINE_TILE, dest_ref, 0, row_copy(0))

    wait(0)
    _issue_rows_inline(COMBINE_TILE, dest_ref, 1, row_copy(1))
    combine(0, 0)
    wait(1)
    _issue_rows_inline(COMBINE_TILE, dest_next_ref, 0, row_copy(0))
    combine(1, 1)

    @pl.when(i == pl.num_programs(0) - 1)
    def _():
        wait(0)


def _qkv_rows(x, gq_ref, gkv_ref, wq_ref, wkv_ref):
    xn = x * _rms_scale(x)
    hq = (xn * gq_ref[...]).astype(BF16)
    hkv = (xn * gkv_ref[...]).astype(BF16)
    q = jnp.dot(hq, wq_ref[...], preferred_element_type=F32) * (ATTN_SCALE * LOG2E)
    kv = jnp.dot(hkv, wkv_ref[...], preferred_element_type=F32)
    return q, kv[:, :D], kv[:, D:]


def _combine_ple(x2d, dest, gates, y, p3d, layer, g_ple, w_proj, w_gate, g_final, final, qkv=None):
    assert COMBINE_TILE == ROUTER_TILE
    t = x2d.shape[0]
    step = 2 * COMBINE_TILE
    nt = t // step
    dest_blk = (2, COMBINE_TILE // SUBLANES, SLOTS_PER_TILE)
    const = lambda i: (0, 0)
    resident = functools.partial(pl.BlockSpec, index_map=const, pipeline_mode=pl.Buffered(1))
    in_specs = [
        pl.BlockSpec(dest_blk, lambda i: (i, 0, 0), memory_space=pltpu.SMEM),
        pl.BlockSpec(dest_blk, lambda i: (jnp.minimum(i + 1, nt - 1), 0, 0),
                     memory_space=pltpu.SMEM),
        pl.BlockSpec((step, D), lambda i: (i, 0)),
        pl.BlockSpec((2, SUBLANES, COMBINE_TILE), lambda i: (i, 0, 0)),
        pl.BlockSpec(memory_space=pl.ANY),
        pl.BlockSpec((1, step, D_PLE), lambda i: (layer, i, 0)),
        pl.BlockSpec((1, D), const),
        resident((D_PLE, D)),
        resident((D, D)),
        pl.BlockSpec((1, D), const),
    ]
    args = [dest, dest, x2d, gates, y, p3d, g_ple.reshape(1, D), w_proj.astype(BF16),
            w_gate.astype(BF16), g_final.reshape(1, D)]
    out_specs = [pl.BlockSpec((step, D), lambda i: (i, 0))]
    out_shape = [jax.ShapeDtypeStruct((t, D), F32)]
    if qkv is not None:
        seq, g_q, g_kv, w_q, w_kv = qkv
        per_seq = seq // step
        in_specs += [pl.BlockSpec((1, D), const), pl.BlockSpec((1, D), const),
                     resident((D, D)), resident((D, 2 * D))]
        args += [g_q.reshape(1, D), g_kv.reshape(1, D), w_q.astype(BF16), w_kv.astype(BF16)]
        feat_major = pl.BlockSpec((1, D, step), lambda i: (i // per_seq, 0, i % per_seq))
        row_major = pl.BlockSpec((1, step, D), lambda i: (i // per_seq, i % per_seq, 0))
        out_specs += [feat_major, row_major, feat_major]
        out_shape += [jax.ShapeDtypeStruct((t // seq, D, seq), BF16),
                      jax.ShapeDtypeStruct((t // seq, seq, D), BF16),
                      jax.ShapeDtypeStruct((t // seq, D, seq), BF16)]
    outs = pl.pallas_call(
        functools.partial(_combine_kernel, final=final, project=qkv is not None),
        name="combine_final" if final else "combine",
        grid=(nt,),
        in_specs=in_specs,
        out_specs=out_specs,
        out_shape=out_shape,
        scratch_shapes=[
            pltpu.VMEM((TOP_K, COMBINE_TILE // SUBLANES, SUBLANES, D), F32),
            pltpu.VMEM((TOP_K, COMBINE_TILE // SUBLANES, SUBLANES, D), F32),
            pltpu.SemaphoreType.DMA((2,)),
        ],
        compiler_params=_params(1),
    )(*args)
    return outs[0] if qkv is None else tuple(outs)


def _moe_ple(x, p, layer, norm_ffn, w_group, b_group, w_expert, b_expert, w_gate, w_up, w_down,
             norm_ple, ple_w_proj, ple_w_gate, final_norm, final, qkv=None):
    b, s, _ = x.shape
    t = b * s
    x2d = x.reshape(t, D)
    ids, gates, counts = _router(x2d, norm_ffn, w_group, b_group, w_expert, b_expert)

    counts = counts[:, 0]
    padded = (counts + MOE_BLOCK - 1) // MOE_BLOCK * MOE_BLOCK
    pad_end = jnp.cumsum(padded)
    pad_start = pad_end - padded
    n_blocks = t * TOP_K // MOE_BLOCK + N_EXPERTS
    n_pad = n_blocks * MOE_BLOCK
    e = ids[:, 0:TOP_K, :]
    r = ids[:, TOP_K:2 * TOP_K, :]
    experts = jnp.arange(N_EXPERTS, dtype=jnp.int32)
    dest = r + jnp.sum(jnp.where(e[..., None] == experts, pad_start, 0), axis=-1)
    nt = t // ROUTER_TILE
    dest = dest.reshape(nt, TOP_K, ROUTER_TILE // SUBLANES, SUBLANES).transpose(0, 2, 1, 3)
    dest = dest.reshape(nt, ROUTER_TILE // SUBLANES, SLOTS_PER_TILE).astype(jnp.int32)
    block_row = jnp.arange(n_blocks, dtype=jnp.int32) * MOE_BLOCK
    block_e = jnp.minimum(jnp.sum(pad_end[None, :] <= block_row[:, None], axis=-1),
                          N_EXPERTS - 1).astype(jnp.int32)
    n_used = (pad_end[-1:] // MOE_BLOCK).astype(jnp.int32)
    tail = (n_used[0] + jnp.arange(N_EXPERTS, dtype=jnp.int32)) * MOE_BLOCK
    zrow = jnp.concatenate([jnp.where(padded > 0, pad_end - MOE_BLOCK, -1),
                            jnp.where(tail < n_pad, tail, -1)]).astype(jnp.int32)

    xs = _dispatch(x2d, norm_ffn, dest, zrow, n_pad)
    later = jnp.where((experts[None, :] > block_e[:, None]) & (padded[None, :] > 0), experts[None, :],
                      N_EXPERTS)
    next_e = jnp.min(later, axis=-1)
    next_e = jnp.where(next_e < N_EXPERTS, next_e, -1).astype(jnp.int32)
    y = _experts(xs, block_e, next_e, n_used, layer, w_gate, w_up, w_down)
    out = _combine_ple(x2d, dest, gates, y, p.reshape(-1, t, D_PLE), layer,
                       norm_ple, ple_w_proj, ple_w_gate, final_norm, final,
                       None if qkv is None else (s,) + tuple(qkv))
    if qkv is None:
        return out.reshape(b, s, D)
    return (out[0].reshape(b, s, D),) + out[1:]


Q_GROUP = 2 * CHUNK
G_BAND = BAND + CHUNK
PAIR = 2 * B_HEAD_DIM
ONES_ROWS = BF16_SUBLANES


def _attn_kernel(qt_ref, kp_ref, kc_ref, vtp_ref, vtc_ref, bias_ref, x_ref, wo_ref, out_ref, o_scr,
                 *stage_scr):
    drow = lax.broadcasted_iota(jnp.int32, (PAIR, Q_GROUP), 0)
    first_head = drow < B_HEAD_DIM
    s_scr = stage_scr[:ATTN_DEPTH]
    p_scr = stage_scr[ATTN_DEPTH:]

    def attend(first_tile):
        units = [(g, pr) for g in range(SEQ_TILE // Q_GROUP) for pr in range(B_HEADS // 2)]

        def geometry(g):
            w0 = g * Q_GROUP
            n_prev = SEQ_TILE - w0
            return w0, n_prev, G_BAND - n_prev

        def keys(g):
            return slice(geometry(g)[1], None) if first_tile else slice(None)

        def scores(unit, s_ref):
            g, pr = unit
            w0, n_prev, n_cur = geometry(g)
            feat = slice(pr * PAIR, (pr + 1) * PAIR)
            qt = qt_ref[0, feat, w0:w0 + Q_GROUP]
            zero = jnp.zeros_like(qt)
            qblk = jnp.concatenate([jnp.where(first_head, qt, zero),
                                    jnp.where(first_head, zero, qt)], axis=1)
            if first_tile:
                kb = kc_ref[0, :n_cur, feat]
            else:
                kb = jnp.concatenate([kp_ref[0, w0:, feat], kc_ref[0, :n_cur, feat]], axis=0)
            s_ref[keys(g), :] = (jnp.dot(kb, qblk, preferred_element_type=F32)
                                 + bias_ref[pr, keys(g), :])

        def weights(unit, s_ref, p_ref):
            g, _ = unit
            s = s_ref[keys(g), :]
            m = jnp.max(s, axis=0, keepdims=True)
            p_ref[keys(g), :] = jnp.exp2(s - m).astype(BF16)

        def values(unit, p_ref):
            g, pr = unit
            w0, n_prev, n_cur = geometry(g)
            feat = slice(pr * PAIR, (pr + 1) * PAIR)
            if first_tile:
                vt = vtc_ref[0, feat, :n_cur]
            else:
                vt = jnp.concatenate([vtp_ref[0, feat, w0:], vtc_ref[0, feat, :n_cur]], axis=1)
            ones = jnp.ones((ONES_ROWS, vt.shape[1]), BF16)
            ot = jnp.dot(jnp.concatenate([vt, ones], axis=0), p_ref[keys(g), :],
                         preferred_element_type=F32)
            inv = 1.0 / ot[PAIR:PAIR + 1, :]
            ot = jnp.where(first_head, ot[:PAIR, :Q_GROUP] * inv[:, :Q_GROUP],
                           ot[:PAIR, Q_GROUP:] * inv[:, Q_GROUP:])
            o_scr[w0:w0 + Q_GROUP, feat] = ot.T.astype(BF16)

        n_units = len(units)
        depth = len(s_scr)
        for n in range(depth):
            scores(units[n], s_scr[n])
        for n in range(depth - 1):
            weights(units[n], s_scr[n], p_scr[n])
        for n, unit in enumerate(units):
            if n + depth < n_units:
                scores(units[n + depth], s_scr[n % depth])
            if n + depth - 1 < n_units:
                weights(units[n + depth - 1], s_scr[(n + depth - 1) % depth],
                        p_scr[(n + depth - 1) % depth])
            values(unit, p_scr[n % depth])

    @pl.when(pl.program_id(1) == 0)
    def _():
        attend(True)

    @pl.when(pl.program_id(1) > 0)
    def _():
        attend(False)

    out_ref[0] = x_ref[0] + jnp.dot(o_scr[...], wo_ref[...], preferred_element_type=F32)


def _attn(x, qt, k, vt, bias_t, w_o):
    b, s, _ = x.shape
    cur = lambda bi, si: (bi, si, 0)
    prev = lambda bi, si: (bi, jnp.maximum(si - 1, 0), 0)
    cur_t = lambda bi, si: (bi, 0, si)
    prev_t = lambda bi, si: (bi, 0, jnp.maximum(si - 1, 0))
    blk = (1, SEQ_TILE, D)
    blk_t = (1, D, SEQ_TILE)
    return pl.pallas_call(
        _attn_kernel,
        name="attn",
        grid=(b, s // SEQ_TILE),
        in_specs=[
            pl.BlockSpec(blk_t, cur_t),
            pl.BlockSpec(blk, prev),
            pl.BlockSpec(blk, cur),
            pl.BlockSpec(blk_t, prev_t),
            pl.BlockSpec(blk_t, cur_t),
            pl.BlockSpec((B_HEADS // 2, G_BAND, 2 * Q_GROUP), lambda bi, si: (0, 0, 0),
                         pipeline_mode=pl.Buffered(1)),
            pl.BlockSpec(blk, cur),
            pl.BlockSpec((D, D), lambda bi, si: (0, 0), pipeline_mode=pl.Buffered(1)),
        ],
        out_specs=pl.BlockSpec(blk, cur),
        out_shape=jax.ShapeDtypeStruct(x.shape, F32),
        scratch_shapes=[pltpu.VMEM((SEQ_TILE, D), BF16)]
        + [pltpu.VMEM((G_BAND, 2 * Q_GROUP), F32)] * ATTN_DEPTH
        + [pltpu.VMEM((G_BAND, 2 * Q_GROUP), BF16)] * ATTN_DEPTH,
        compiler_params=_params(2),
    )(qt, k, k, vt, vt, bias_t, x, w_o.astype(BF16))


def _group_bias(table):
    band = _band_bias(table) * LOG2E
    pad = lambda lo, hi: jnp.pad(band, ((0, 0), (0, 0), (lo, hi)), constant_values=NEG_INF)
    both = jnp.concatenate([pad(0, CHUNK), pad(CHUNK, 0)], axis=1)
    both = both.reshape(B_HEADS // 2, 2, Q_GROUP, G_BAND)
    return both.transpose(0, 3, 1, 2).reshape(B_HEADS // 2, G_BAND, 2 * Q_GROUP)


def _band_bias(table):
    n_rel = REL_MAX - REL_MIN + 1
    span = BAND + CHUNK - 1
    head = jnp.broadcast_to(table[:, n_rel - 1:], (table.shape[0], span - n_rel))
    ext = jnp.concatenate([head, table[:, ::-1]], axis=1)
    rows = [ext[:, CHUNK - 1 - q:CHUNK - 1 - q + BAND] for q in range(CHUNK)]
    return jnp.stack(rows, axis=1)


def kernel(x, p, a_w_in, a_lb_logits, a_out_norm, a_w_o, kv_norm, w_kv, b_w_q, b_rel_bias, b_w_o,
           norm_mix, norm_ffn, norm_ple, moe_w_group, moe_b_group, moe_w_expert, moe_b_expert,
           moe_w_gate, moe_w_up, moe_w_down, ple_w_proj, ple_w_gate, final_norm):
    b, s, _ = x.shape
    lower_bounds = jnp.cumsum(jax.nn.softmax(a_lb_logits.astype(F32), axis=0), axis=0)

    def moe(xi, i, final, qkv=None):
        return _moe_ple(xi, p, i, norm_ffn[i], moe_w_group[i], moe_b_group[i], moe_w_expert[i],
                        moe_b_expert[i], moe_w_gate, moe_w_up, moe_w_down, norm_ple[i],
                        ple_w_proj[i], ple_w_gate[i], final_norm, final, qkv)

    x = _mixer_a(x, norm_mix[0], a_w_in[0], lower_bounds[0], a_out_norm[0], a_w_o[0])
    x, qt, k, vt = moe(x, 0, False, (norm_mix[1], kv_norm, b_w_q[0], w_kv))

    x = _attn(x, qt, k, vt, _group_bias(b_rel_bias[0].astype(F32)), b_w_o[0])
    x = moe(x, 1, True)
    return x
```

```python
import functools

import jax
import jax.numpy as jnp
from jax import lax
from jax.experimental import pallas as pl
from jax.experimental.pallas import tpu as pltpu

F32 = jnp.float32
BF16 = jnp.bfloat16
U32 = jnp.uint32

D = 1024
CHUNK = 64
A_HEADS = 8
A_HEAD_DIM = 128
B_HEADS = 16
B_HEAD_DIM = 64
LEFT_CHUNKS = 8
BAND = (LEFT_CHUNKS + 1) * CHUNK
REL_MIN = -(CHUNK - 1)
REL_MAX = 256
ATTN_SCALE = B_HEAD_DIM ** -0.5
N_GROUPS = 4
EXPERTS_PER_GROUP = 8
N_EXPERTS = 32
TOP_K = 2
D_EXPERT = 512
MOE_BLOCK = 512
D_PLE = 256
EPS = 1e-6
NEG_INF = -1e30
LOG2E = 1.4426950408889634

SUBLANES = 8
LANES = 128
BF16_SUBLANES = 16
V7X_VMEM_BYTES = 64 * 1024 * 1024

SEQ_TILE = 512
ROUTER_TILE = 512
COMBINE_TILE = 512
MIXER_DEPTH = 4
ATTN_DEPTH = 4
ROUTER_ROWS = -(-(N_EXPERTS + N_GROUPS) // BF16_SUBLANES) * BF16_SUBLANES
VMEM_LIMIT = V7X_VMEM_BYTES - 8 * 1024 * 1024


def _params(n_axes, vmem=VMEM_LIMIT):
    return pltpu.CompilerParams(dimension_semantics=("arbitrary",) * n_axes,
                                vmem_limit_bytes=vmem)


def _rms_scale(x):
    return lax.rsqrt(jnp.mean(x * x, axis=-1, keepdims=True) + EPS)


def _sigmoid(x):
    return 0.5 * jnp.tanh(0.5 * x) + 0.5


def _mixer_a_kernel(x_ref, g_ref, win_ref, lb_ref, onorm_ref, wo_ref, out_ref,
                    proj_scr, o_scr, state_scr, g_scr0, g_scr1, k_scr0, k_scr1, *stage_scr):
    @pl.when(pl.program_id(1) == 0)
    def _():
        state_scr[...] = jnp.zeros_like(state_scr)

    x = x_ref[0]
    h = (x * _rms_scale(x) * g_ref[...]).astype(BF16)
    proj_scr[...] = jnp.dot(h, win_ref[...], preferred_element_type=F32)

    row = lax.broadcasted_iota(jnp.int32, (CHUNK, CHUNK), 0)
    col = lax.broadcasted_iota(jnp.int32, (CHUNK, CHUNK), 1)
    causal = row >= col
    tril = causal.astype(BF16)
    lb = lb_ref[...]
    onorm = onorm_ref[...]

    g_scr, k_scr = (g_scr0, g_scr1), (k_scr0, k_scr1)
    depth = MIXER_DEPTH
    qd_scr, kt_scr, att_scr = (stage_scr[0:depth], stage_scr[depth:2 * depth],
                               stage_scr[2 * depth:3 * depth])
    n_chunks = SEQ_TILE // CHUNK
    units = [(c, hd) for c in range(n_chunks) for hd in range(A_HEADS)]

    def rows_of(c):
        return slice(c * CHUNK, (c + 1) * CHUNK)

    def decay(c):
        f = lb + (1.0 - lb) * _sigmoid(proj_scr[rows_of(c), D:2 * D])
        logf = jnp.log(f)
        hi = logf.astype(BF16)
        lo = (logf - hi.astype(F32)).astype(BF16)
        g_scr[c % 2][...] = (jnp.dot(tril, hi, preferred_element_type=F32)
                             + jnp.dot(tril, lo, preferred_element_type=F32))
        k_scr[c % 2][...] = 1.0 - f

    def intra(n):
        c, hd = units[n]
        sl = slice(hd * A_HEAD_DIM, (hd + 1) * A_HEAD_DIM)
        gh = g_scr[c % 2][:, sl]
        g_last = gh[CHUNK - 1:CHUNK, :]
        k = k_scr[c % 2][:, sl]
        q_dec = (proj_scr[rows_of(c), sl] * jnp.exp(gh)).astype(BF16)
        k_inv = k * jnp.exp(-gh)
        qd_scr[n % depth][...] = q_dec
        kt_scr[n % depth][...] = (k_inv * jnp.exp(g_last)).astype(BF16)
        k_inv = k_inv.astype(BF16)
        att = lax.dot_general(q_dec, k_inv, (((1,), (1,)), ((), ())),
                              preferred_element_type=F32)
        att_scr[n % depth][...] = jnp.where(causal, att, 0.0).astype(BF16)

    def output(n):
        c, hd = units[n]
        sl = slice(hd * A_HEAD_DIM, (hd + 1) * A_HEAD_DIM)
        rows = rows_of(c)
        g_last = g_scr[c % 2][CHUNK - 1:CHUNK, sl]
        v = proj_scr[rows, 2 * D + hd * A_HEAD_DIM:2 * D + (hd + 1) * A_HEAD_DIM]
        st = state_scr[hd]
        o = (jnp.dot(att_scr[n % depth][...], v.astype(BF16), preferred_element_type=F32)
             + lax.dot_general(qd_scr[n % depth][...], st.astype(BF16), (((1,), (1,)), ((), ())),
                               preferred_element_type=F32))
        v_t = v.T.astype(BF16)
        state_scr[hd] = st * jnp.exp(g_last) + jnp.dot(v_t, kt_scr[n % depth][...],
                                                       preferred_element_type=F32)
        o = o * _rms_scale(o)
        og = proj_scr[rows, 3 * D + hd * A_HEAD_DIM:3 * D + (hd + 1) * A_HEAD_DIM]
        o = o * onorm[:, sl] * (og * _sigmoid(og))
        o_scr[rows, sl] = o.astype(BF16)

    decay(0)
    for n in range(depth - 1):
        intra(n)
    for n, (c, hd) in enumerate(units):
        if hd == 0 and c + 1 < n_chunks:
            decay(c + 1)
        if n + depth - 1 < len(units):
            intra(n + depth - 1)
        output(n)
    out_ref[0] = x + jnp.dot(o_scr[...], wo_ref[...], preferred_element_type=F32)


def _mixer_a(x, g, w_in, lb, out_norm, w_o):
    b, s, _ = x.shape
    const = lambda bi, si: (0, 0)
    return pl.pallas_call(
        _mixer_a_kernel,
        name="mixer_a",
        grid=(b, s // SEQ_TILE),
        in_specs=[
            pl.BlockSpec((1, SEQ_TILE, D), lambda bi, si: (bi, si, 0)),
            pl.BlockSpec((1, D), const),
            pl.BlockSpec((D, 4 * D), const, pipeline_mode=pl.Buffered(1)),
            pl.BlockSpec((1, D), const),
            pl.BlockSpec((1, D), const),
            pl.BlockSpec((D, D), const, pipeline_mode=pl.Buffered(1)),
        ],
        out_specs=pl.BlockSpec((1, SEQ_TILE, D), lambda bi, si: (bi, si, 0)),
        out_shape=jax.ShapeDtypeStruct(x.shape, F32),
        scratch_shapes=[
            pltpu.VMEM((SEQ_TILE, 4 * D), F32),
            pltpu.VMEM((SEQ_TILE, D), BF16),
            pltpu.VMEM((A_HEADS, A_HEAD_DIM, A_HEAD_DIM), F32),
        ] + [pltpu.VMEM((CHUNK, D), F32)] * 4
          + [pltpu.VMEM((CHUNK, A_HEAD_DIM), BF16)] * (2 * MIXER_DEPTH)
          + [pltpu.VMEM((CHUNK, CHUNK), BF16)] * MIXER_DEPTH,
        compiler_params=_params(2),
    )(x, g.reshape(1, D), w_in.astype(BF16), lb.reshape(1, D), out_norm.reshape(1, D),
      w_o.astype(BF16))


def _router_kernel(x_ref, g_ref, wr_ref, br_ref, ids_ref, gates_ref, counts_ref, cnt_scr,
                   before_scr):
    tm = ROUTER_TILE

    @pl.when(pl.program_id(0) == 0)
    def _():
        cnt_scr[...] = jnp.zeros_like(cnt_scr)
        tr = lax.broadcasted_iota(jnp.int32, (tm, tm), 0)
        tc = lax.broadcasted_iota(jnp.int32, (tm, tm), 1)
        before_scr[...] = (tr < tc).astype(BF16)

    x = x_ref[...]
    h = x * _rms_scale(x) * g_ref[...]
    h_hi = h.astype(BF16)
    h_lo = (h - h_hi.astype(F32)).astype(BF16)
    nt = (((1,), (1,)), ((), ()))
    both = lax.dot_general(wr_ref[...], h_hi, nt, preferred_element_type=F32)
    cross = lax.dot_general(wr_ref[0:ROUTER_ROWS], h_lo, nt, preferred_element_type=F32)
    logits = both[0:ROUTER_ROWS] + both[ROUTER_ROWS:] + cross + br_ref[...]
    el = logits[0:N_EXPERTS]
    gl = logits[N_EXPERTS:ROUTER_ROWS]
    grow = lax.broadcasted_iota(jnp.int32, gl.shape, 0)
    gl = jnp.where(grow < N_GROUPS, gl, -jnp.inf)
    gmax = jnp.max(gl, axis=0, keepdims=True)
    gsum = jnp.sum(jnp.exp(gl - gmax), axis=0, keepdims=True)
    grp_w = 1.0 / gsum
    gidx = jnp.min(jnp.where(gl == gmax, grow, N_GROUPS), axis=0, keepdims=True)

    erow = lax.broadcasted_iota(jnp.int32, el.shape, 0)
    masked = jnp.where((erow // EXPERTS_PER_GROUP) == gidx, el, -jnp.inf)
    top1 = jnp.max(masked, axis=0, keepdims=True)
    i1 = jnp.min(jnp.where(masked == top1, erow, N_EXPERTS), axis=0, keepdims=True)
    masked2 = jnp.where(erow == i1, -jnp.inf, masked)
    top2 = jnp.max(masked2, axis=0, keepdims=True)
    i2 = jnp.min(jnp.where(masked2 == top2, erow, N_EXPERTS), axis=0, keepdims=True)
    e2 = jnp.exp(top2 - top1)
    denom = 1.0 + e2
    g1 = grp_w * (1.0 / denom)
    g2 = grp_w * (e2 / denom)

    sel1 = erow == i1
    sel2 = erow == i2
    onehot = (sel1 | sel2).astype(BF16)
    prefix = jnp.dot(onehot, before_scr[...], preferred_element_type=F32) + cnt_scr[...]
    r1 = jnp.sum(jnp.where(sel1, prefix, 0.0), axis=0, keepdims=True)
    r2 = jnp.sum(jnp.where(sel2, prefix, 0.0), axis=0, keepdims=True)
    cnt_scr[...] += jnp.sum(onehot.astype(F32), axis=1, keepdims=True)

    zi = jnp.zeros((SUBLANES - 2 * TOP_K, tm), jnp.int32)
    ids_ref[0] = jnp.concatenate(
        [i1, i2, r1.astype(jnp.int32), r2.astype(jnp.int32), zi], axis=0)
    gates_ref[0] = jnp.concatenate([g1, g2, jnp.zeros((SUBLANES - TOP_K, tm), F32)], axis=0)
    counts_ref[...] = jnp.broadcast_to(cnt_scr[...], counts_ref.shape).astype(jnp.int32)


def _router(x2d, g, w_group, b_group, w_expert, b_expert):
    t = x2d.shape[0]
    nt = t // ROUTER_TILE
    pad = ROUTER_ROWS - N_EXPERTS - N_GROUPS
    wr = jnp.concatenate([w_expert.T, w_group.T, jnp.zeros((pad, D), F32)], axis=0)
    wr_hi = wr.astype(BF16)
    wr_lo = (wr - wr_hi.astype(F32)).astype(BF16)
    br = jnp.concatenate([b_expert, b_group, jnp.zeros((pad,), F32)]).reshape(ROUTER_ROWS, 1)
    const = lambda i: (0, 0)
    return pl.pallas_call(
        _router_kernel,
        name="router",
        grid=(nt,),
        in_specs=[
            pl.BlockSpec((ROUTER_TILE, D), lambda i: (i, 0)),
            pl.BlockSpec((1, D), const),
            pl.BlockSpec((2 * ROUTER_ROWS, D), const),
            pl.BlockSpec((ROUTER_ROWS, 1), const),
        ],
        out_specs=[
            pl.BlockSpec((1, SUBLANES, ROUTER_TILE), lambda i: (i, 0, 0)),
            pl.BlockSpec((1, SUBLANES, ROUTER_TILE), lambda i: (i, 0, 0)),
            pl.BlockSpec((N_EXPERTS, LANES), const),
        ],
        out_shape=[
            jax.ShapeDtypeStruct((nt, SUBLANES, ROUTER_TILE), jnp.int32),
            jax.ShapeDtypeStruct((nt, SUBLANES, ROUTER_TILE), F32),
            jax.ShapeDtypeStruct((N_EXPERTS, LANES), jnp.int32),
        ],
        scratch_shapes=[pltpu.VMEM((N_EXPERTS, 1), F32),
                        pltpu.VMEM((ROUTER_TILE, ROUTER_TILE), BF16)],
        compiler_params=_params(1),
    )(x2d, g.reshape(1, D), jnp.concatenate([wr_hi, wr_lo], axis=0), br)


SLOTS_PER_TILE = SUBLANES * TOP_K


def _issue_tile_rows(dest_ref, which, j, make_copy):
    for u in range(SUBLANES):
        for k in range(TOP_K):
            dest = dest_ref[which, j, k * SUBLANES + u]
            make_copy(j, u, k, dest).start(priority=(u * TOP_K + k) % 2)


def _issue_rows(n_rows, dest_ref, which, make_copy):
    def body(j, c):
        _issue_tile_rows(dest_ref, which, j, make_copy)
        return c

    lax.fori_loop(0, n_rows // SUBLANES, body, 0)


def _issue_rows_inline(n_rows, dest_ref, which, make_copy):
    for j in range(n_rows // SUBLANES):
        _issue_tile_rows(dest_ref, which, j, make_copy)


def _pack_bf16_pairs(x):
    half = x.shape[1] // 2
    lo = lax.bitcast_convert_type(x[:, :half].astype(BF16).astype(F32), U32)
    hi = lax.bitcast_convert_type(x[:, half:].astype(BF16).astype(F32), U32)
    return (hi & jnp.uint32(0xFFFF0000)) | (lo >> 16)


def _unpack_bf16_pairs(u):
    lo = lax.bitcast_convert_type(u << 16, F32).astype(BF16)
    hi = lax.bitcast_convert_type(u & jnp.uint32(0xFFFF0000), F32).astype(BF16)
    return lo, hi


def _tile_rows(x):
    return x.reshape(x.shape[0] // SUBLANES, SUBLANES, x.shape[1])


def _wait_rows(buf_ref, sem):
    pltpu.make_async_copy(buf_ref, buf_ref, sem).wait()


def _dispatch_kernel(zrow_ref, dest_ref, x_ref, g_ref, xs_ref, hbuf, zbuf, zsem, sem):
    @pl.when(pl.program_id(0) == 0)
    def _():
        zbuf[...] = jnp.zeros_like(zbuf)

        def zcopy(e):
            return pltpu.make_async_copy(zbuf, xs_ref.at[pl.ds(zrow_ref[e], MOE_BLOCK)], zsem)

        def zstart(e, c):
            @pl.when(zrow_ref[e] >= 0)
            def _():
                zcopy(e).start()
            return c

        def zwait(e, c):
            @pl.when(zrow_ref[e] >= 0)
            def _():
                zcopy(e).wait()
            return c

        lax.fori_loop(0, 2 * N_EXPERTS, zstart, 0)
        lax.fori_loop(0, 2 * N_EXPERTS, zwait, 0)

    i = pl.program_id(0)
    slot = i % 2
    x = x_ref[...]
    hbuf[slot] = _tile_rows(_pack_bf16_pairs(x * _rms_scale(x) * g_ref[...]))

    def row_copy(tile, sub, k, dest):
        return pltpu.make_async_copy(hbuf.at[slot, tile, pl.ds(sub, 1), :], xs_ref.at[dest],
                                     sem.at[slot])

    _issue_rows(ROUTER_TILE, dest_ref, 0, row_copy)

    def drain(which):
        for _ in range(TOP_K):
            _wait_rows(hbuf.at[which], sem.at[which])

    @pl.when(i > 0)
    def _():
        drain(1 - slot)

    @pl.when(i == pl.num_programs(0) - 1)
    def _():
        drain(slot)


def _dispatch(x2d, g, dest, zrow, n_pad):
    t = x2d.shape[0]
    nt = t // ROUTER_TILE
    return pl.pallas_call(
        _dispatch_kernel,
        name="dispatch",
        grid_spec=pltpu.PrefetchScalarGridSpec(
            num_scalar_prefetch=1,
            grid=(nt,),
            in_specs=[
                pl.BlockSpec((1, ROUTER_TILE // SUBLANES, SLOTS_PER_TILE),
                             lambda i, z: (i, 0, 0), memory_space=pltpu.SMEM),
                pl.BlockSpec((ROUTER_TILE, D), lambda i, z: (i, 0)),
                pl.BlockSpec((1, D), lambda i, z: (0, 0)),
            ],
            out_specs=pl.BlockSpec(memory_space=pl.ANY),
            scratch_shapes=[
                pltpu.VMEM((2, ROUTER_TILE // SUBLANES, SUBLANES, D // 2), U32),
                pltpu.VMEM((MOE_BLOCK, 1, D // 2), U32),
                pltpu.SemaphoreType.DMA(()),
                pltpu.SemaphoreType.DMA((2,)),
            ],
        ),
        out_shape=jax.ShapeDtypeStruct((n_pad, 1, D // 2), U32),
        compiler_params=_params(1),
    )(zrow, dest, x2d, g.reshape(1, D))


def _experts_kernel(be_ref, nxt_ref, nu_ref, xs_ref, wg_ref, wu_ref, wd_ref, y_ref,
                    wg_f, wu_f, wd_f, wg_b, wu_b, wd_b, xbuf, ybuf, w_sem, in_sem, out_sem, *, layer):
    i = pl.program_id(0)
    n_used = nu_ref[0]
    slot = i % 2

    def block_rows(ref, blk):
        return ref.at[pl.ds(pl.multiple_of(blk * MOE_BLOCK, MOE_BLOCK), MOE_BLOCK), 0]

    def fetch(blk, into):
        return pltpu.make_async_copy(block_rows(xs_ref, blk), xbuf.at[into], in_sem.at[into])

    def write_back(blk, from_):
        return pltpu.make_async_copy(ybuf.at[from_], block_rows(y_ref, blk), out_sem.at[from_])

    def weight_copies(e):
        return [pltpu.make_async_copy(src.at[layer, e], dst, w_sem.at[n])
                for n, (src, dst) in enumerate(((wg_ref, wg_f), (wu_ref, wu_f), (wd_ref, wd_f)))]

    @pl.when(i == 0)
    def _():
        for c in weight_copies(be_ref[0]):
            c.start()
        fetch(0, 0).start()

    @pl.when(i + 1 < n_used)
    def _():
        fetch(i + 1, 1 - slot).start()

    prev = be_ref[jnp.maximum(i - 1, 0)]

    @pl.when((i < n_used) & ((i == 0) | (be_ref[i] != prev)))
    def _():
        for c in weight_copies(be_ref[i]):
            c.wait()
        wg_b[...] = wg_f[...].astype(BF16)
        wu_b[...] = wu_f[...].astype(BF16)
        wd_b[...] = wd_f[...].astype(BF16)

        @pl.when(nxt_ref[i] >= 0)
        def _():
            for c in weight_copies(nxt_ref[i]):
                c.start()

    @pl.when(i >= 2)
    def _():
        write_back(i - 2, slot).wait()

    @pl.when(i < n_used)
    def _():
        fetch(i, slot).wait()
        h_lo, h_hi = _unpack_bf16_pairs(xbuf[slot])
        half = D // 2
        a = (jnp.dot(h_lo, wg_b[:half], preferred_element_type=F32)
             + jnp.dot(h_hi, wg_b[half:], preferred_element_type=F32))
        u = (jnp.dot(h_lo, wu_b[:half], preferred_element_type=F32)
             + jnp.dot(h_hi, wu_b[half:], preferred_element_type=F32))
        hid = (a * _sigmoid(a) * u).astype(BF16)
        ybuf[slot] = jnp.dot(hid, wd_b[...], preferred_element_type=F32)

    @pl.when(i >= n_used)
    def _():
        ybuf[slot] = jnp.zeros((MOE_BLOCK, D), F32)

    write_back(i, slot).start()

    @pl.when(i == pl.num_programs(0) - 1)
    def _():
        write_back(i - 1, 1 - slot).wait()
        write_back(i, slot).wait()


def _experts(xs, block_e, next_e, n_used, layer, w_gate, w_up, w_down):
    n_pad = xs.shape[0]
    n_blocks = n_pad // MOE_BLOCK
    hbm = pl.BlockSpec(memory_space=pl.ANY)
    return pl.pallas_call(
        functools.partial(_experts_kernel, layer=layer),
        name="experts",
        grid_spec=pltpu.PrefetchScalarGridSpec(
            num_scalar_prefetch=3,
            grid=(n_blocks,),
            in_specs=[hbm, hbm, hbm, hbm],
            out_specs=hbm,
            scratch_shapes=[
                pltpu.VMEM((D, D_EXPERT), F32),
                pltpu.VMEM((D, D_EXPERT), F32),
                pltpu.VMEM((D_EXPERT, D), F32),
                pltpu.VMEM((D, D_EXPERT), BF16),
                pltpu.VMEM((D, D_EXPERT), BF16),
                pltpu.VMEM((D_EXPERT, D), BF16),
                pltpu.VMEM((2, MOE_BLOCK, D // 2), U32),
                pltpu.VMEM((2, MOE_BLOCK, D), F32),
                pltpu.SemaphoreType.DMA((3,)),
                pltpu.SemaphoreType.DMA((2,)),
                pltpu.SemaphoreType.DMA((2,)),
            ],
        ),
        out_shape=jax.ShapeDtypeStruct((n_pad, 1, D), F32),
        compiler_params=_params(1),
    )(block_e, next_e, n_used, xs, w_gate, w_up, w_down)


def _combine_kernel(*refs, final, project):
    (dest_ref, dest_next_ref, x_ref, gate_ref, y_ref, p_ref, gple_ref, wp_ref, wg_ref,
     gfin_ref) = refs[:10]
    if project:
        gq_ref, gkv_ref, wq_ref, wkv_ref, out_ref, qt_ref, k_ref, vt_ref = refs[10:18]
        ybuf0, ybuf1, sem = refs[18:]
    else:
        out_ref, ybuf0, ybuf1, sem = refs[10:]
    i = pl.program_id(0)
    ybuf = (ybuf0, ybuf1)

    def row_copy(into):
        def make(tile, sub, k, dest):
            return pltpu.make_async_copy(y_ref.at[dest], ybuf[into].at[k, tile, pl.ds(sub, 1), :],
                                         sem.at[into])
        return make

    def combine(half, slot):
        rows = slice(half * COMBINE_TILE, (half + 1) * COMBINE_TILE)
        gates = gate_ref[half].T
        y0 = ybuf[slot][0].reshape(COMBINE_TILE, D)
        y1 = ybuf[slot][1].reshape(COMBINE_TILE, D)
        x = x_ref[rows, :] + gates[:, 0:1] * y0 + gates[:, 1:2] * y1
        h = (x * _rms_scale(x) * gple_ref[...]).astype(BF16)
        gate = _sigmoid(jnp.dot(h, wg_ref[...], preferred_element_type=F32))
        proj = jnp.dot(p_ref[0, rows, :].astype(BF16), wp_ref[...], preferred_element_type=F32)
        x = x + proj * gate
        if final:
            x = x * _rms_scale(x) * gfin_ref[...]
        out_ref[rows, :] = x
        if project:
            q, k, v = _qkv_rows(x, gq_ref, gkv_ref, wq_ref, wkv_ref)
            qt_ref[0, :, rows] = q.T.astype(BF16)
            k_ref[0, rows, :] = k.astype(BF16)
            vt_ref[0, :, rows] = v.T.astype(BF16)

    def wait(slot):
        for k in range(TOP_K):
            _wait_rows(ybuf[slot].at[k], sem.at[slot])

    @pl.when(i == 0)
    def _():
        _issue_rows(COMBINE_TILE, dest_ref, 0, row_copy(0))

    wait(0)
    _issue_rows_inline(COMBINE_TILE, dest_ref, 1, row_copy(1))
    combine(0, 0)
    wait(1)
    _issue_rows_inline(COMBINE_TILE, dest_next_ref, 0, row_copy(0))
    combine(1, 1)

    @pl.when(i == pl.num_programs(0) - 1)
    def _():
        wait(0)


def _qkv_rows(x, gq_ref, gkv_ref, wq_ref, wkv_ref):
    xn = x * _rms_scale(x)
    hq = (xn * gq_ref[...]).astype(BF16)
    hkv = (xn * gkv_ref[...]).astype(BF16)
    q = jnp.dot(hq, wq_ref[...], preferred_element_type=F32) * (ATTN_SCALE * LOG2E)
    kv = jnp.dot(hkv, wkv_ref[...], preferred_element_type=F32)
    return q, kv[:, :D], kv[:, D:]


def _combine_ple(x2d, dest, gates, y, p3d, layer, g_ple, w_proj, w_gate, g_final, final, qkv=None):
    assert COMBINE_TILE == ROUTER_TILE
    t = x2d.shape[0]
    step = 2 * COMBINE_TILE
    nt = t // step
    dest_blk = (2, COMBINE_TILE // SUBLANES, SLOTS_PER_TILE)
    const = lambda i: (0, 0)
    resident = functools.partial(pl.BlockSpec, index_map=const, pipeline_mode=pl.Buffered(1))
    in_specs = [
        pl.BlockSpec(dest_blk, lambda i: (i, 0, 0), memory_space=pltpu.SMEM),
        pl.BlockSpec(dest_blk, lambda i: (jnp.minimum(i + 1, nt - 1), 0, 0),
                     memory_space=pltpu.SMEM),
        pl.BlockSpec((step, D), lambda i: (i, 0)),
        pl.BlockSpec((2, SUBLANES, COMBINE_TILE), lambda i: (i, 0, 0)),
        pl.BlockSpec(memory_space=pl.ANY),
        pl.BlockSpec((1, step, D_PLE), lambda i: (layer, i, 0)),
        pl.BlockSpec((1, D), const),
        resident((D_PLE, D)),
        resident((D, D)),
        pl.BlockSpec((1, D), const),
    ]
    args = [dest, dest, x2d, gates, y, p3d, g_ple.reshape(1, D), w_proj.astype(BF16),
            w_gate.astype(BF16), g_final.reshape(1, D)]
    out_specs = [pl.BlockSpec((step, D), lambda i: (i, 0))]
    out_shape = [jax.ShapeDtypeStruct((t, D), F32)]
    if qkv is not None:
        seq, g_q, g_kv, w_q, w_kv = qkv
        per_seq = seq // step
        in_specs += [pl.BlockSpec((1, D), const), pl.BlockSpec((1, D), const),
                     resident((D, D)), resident((D, 2 * D))]
        args += [g_q.reshape(1, D), g_kv.reshape(1, D), w_q.astype(BF16), w_kv.astype(BF16)]
        feat_major = pl.BlockSpec((1, D, step), lambda i: (i // per_seq, 0, i % per_seq))
        row_major = pl.BlockSpec((1, step, D), lambda i: (i // per_seq, i % per_seq, 0))
        out_specs += [feat_major, row_major, feat_major]
        out_shape += [jax.ShapeDtypeStruct((t // seq, D, seq), BF16),
                      jax.ShapeDtypeStruct((t // seq, seq, D), BF16),
                      jax.ShapeDtypeStruct((t // seq, D, seq), BF16)]
    outs = pl.pallas_call(
        functools.partial(_combine_kernel, final=final, project=qkv is not None),
        name="combine_final" if final else "combine",
        grid=(nt,),
        in_specs=in_specs,
        out_specs=out_specs,
        out_shape=out_shape,
        scratch_shapes=[
            pltpu.VMEM((TOP_K, COMBINE_TILE // SUBLANES, SUBLANES, D), F32),
            pltpu.VMEM((TOP_K, COMBINE_TILE // SUBLANES, SUBLANES, D), F32),
            pltpu.SemaphoreType.DMA((2,)),
        ],
        compiler_params=_params(1),
    )(*args)
    return outs[0] if qkv is None else tuple(outs)


def _moe_ple(x, p, layer, norm_ffn, w_group, b_group, w_expert, b_expert, w_gate, w_up, w_down,
             norm_ple, ple_w_proj, ple_w_gate, final_norm, final, qkv=None):
    b, s, _ = x.shape
    t = b * s
    x2d = x.reshape(t, D)
    ids, gates, counts = _router(x2d, norm_ffn, w_group, b_group, w_expert, b_expert)

    counts = counts[:, 0]
    padded = (counts + MOE_BLOCK - 1) // MOE_BLOCK * MOE_BLOCK
    pad_end = jnp.cumsum(padded)
    pad_start = pad_end - padded
    n_blocks = t * TOP_K // MOE_BLOCK + N_EXPERTS
    n_pad = n_blocks * MOE_BLOCK
    e = ids[:, 0:TOP_K, :]
    r = ids[:, TOP_K:2 * TOP_K, :]
    experts = jnp.arange(N_EXPERTS, dtype=jnp.int32)
    dest = r + jnp.sum(jnp.where(e[..., None] == experts, pad_start, 0), axis=-1)
    nt = t // ROUTER_TILE
    dest = dest.reshape(nt, TOP_K, ROUTER_TILE // SUBLANES, SUBLANES).transpose(0, 2, 1, 3)
    dest = dest.reshape(nt, ROUTER_TILE // SUBLANES, SLOTS_PER_TILE).astype(jnp.int32)
    block_row = jnp.arange(n_blocks, dtype=jnp.int32) * MOE_BLOCK
    block_e = jnp.minimum(jnp.sum(pad_end[None, :] <= block_row[:, None], axis=-1),
                          N_EXPERTS - 1).astype(jnp.int32)
    n_used = (pad_end[-1:] // MOE_BLOCK).astype(jnp.int32)
    tail = (n_used[0] + jnp.arange(N_EXPERTS, dtype=jnp.int32)) * MOE_BLOCK
    zrow = jnp.concatenate([jnp.where(padded > 0, pad_end - MOE_BLOCK, -1),
                            jnp.where(tail < n_pad, tail, -1)]).astype(jnp.int32)

    xs = _dispatch(x2d, norm_ffn, dest, zrow, n_pad)
    later = jnp.where((experts[None, :] > block_e[:, None]) & (padded[None, :] > 0), experts[None, :],
                      N_EXPERTS)
    next_e = jnp.min(later, axis=-1)
    next_e = jnp.where(next_e < N_EXPERTS, next_e, -1).astype(jnp.int32)
    y = _experts(xs, block_e, next_e, n_used, layer, w_gate, w_up, w_down)
    out = _combine_ple(x2d, dest, gates, y, p.reshape(-1, t, D_PLE), layer,
                       norm_ple, ple_w_proj, ple_w_gate, final_norm, final,
                       None if qkv is None else (s,) + tuple(qkv))
    if qkv is None:
        return out.reshape(b, s, D)
    return (out[0].reshape(b, s, D),) + out[1:]


Q_GROUP = 2 * CHUNK
G_BAND = BAND + CHUNK
PAIR = 2 * B_HEAD_DIM
ONES_ROWS = BF16_SUBLANES


def _attn_kernel(qt_ref, kp_ref, kc_ref, vtp_ref, vtc_ref, bias_ref, x_ref, wo_ref, out_ref, o_scr,
                 *stage_scr):
    drow = lax.broadcasted_iota(jnp.int32, (PAIR, Q_GROUP), 0)
    first_head = drow < B_HEAD_DIM
    s_scr = stage_scr[:ATTN_DEPTH]
    p_scr = stage_scr[ATTN_DEPTH:]

    def attend(first_tile):
        units = [(g, pr) for g in range(SEQ_TILE // Q_GROUP) for pr in range(B_HEADS // 2)]

        def geometry(g):
            w0 = g * Q_GROUP
            n_prev = SEQ_TILE - w0
            return w0, n_prev, G_BAND - n_prev

        def keys(g):
            return slice(geometry(g)[1], None) if first_tile else slice(None)

        def scores(unit, s_ref):
            g, pr = unit
            w0, n_prev, n_cur = geometry(g)
            feat = slice(pr * PAIR, (pr + 1) * PAIR)
            qt = qt_ref[0, feat, w0:w0 + Q_GROUP]
            zero = jnp.zeros_like(qt)
            qblk = jnp.concatenate([jnp.where(first_head, qt, zero),
                                    jnp.where(first_head, zero, qt)], axis=1)
            if first_tile:
                kb = kc_ref[0, :n_cur, feat]
            else:
                kb = jnp.concatenate([kp_ref[0, w0:, feat], kc_ref[0, :n_cur, feat]], axis=0)
            s_ref[keys(g), :] = (jnp.dot(kb, qblk, preferred_element_type=F32)
                                 + bias_ref[pr, keys(g), :])

        def weights(unit, s_ref, p_ref):
            g, _ = unit
            s = s_ref[keys(g), :]
            m = jnp.max(s, axis=0, keepdims=True)
            p_ref[keys(g), :] = jnp.exp2(s - m).astype(BF16)

        def values(unit, p_ref):
            g, pr = unit
            w0, n_prev, n_cur = geometry(g)
            feat = slice(pr * PAIR, (pr + 1) * PAIR)
            if first_tile:
                vt = vtc_ref[0, feat, :n_cur]
            else:
                vt = jnp.concatenate([vtp_ref[0, feat, w0:], vtc_ref[0, feat, :n_cur]], axis=1)
            ones = jnp.ones((ONES_ROWS, vt.shape[1]), BF16)
            ot = jnp.dot(jnp.concatenate([vt, ones], axis=0), p_ref[keys(g), :],
                         preferred_element_type=F32)
            inv = 1.0 / ot[PAIR:PAIR + 1, :]
            ot = jnp.where(first_head, ot[:PAIR, :Q_GROUP] * inv[:, :Q_GROUP],
                           ot[:PAIR, Q_GROUP:] * inv[:, Q_GROUP:])
            o_scr[w0:w0 + Q_GROUP, feat] = ot.T.astype(BF16)

        n_units = len(units)
        depth = len(s_scr)
        for n in range(depth):
            scores(units[n], s_scr[n])
        for n in range(depth - 1):
            weights(units[n], s_scr[n], p_scr[n])
        for n, unit in enumerate(units):
            if n + depth < n_units:
                scores(units[n + depth], s_scr[n % depth])
            if n + depth - 1 < n_units:
                weights(units[n + depth - 1], s_scr[(n + depth - 1) % depth],
                        p_scr[(n + depth - 1) % depth])
            values(unit, p_scr[n % depth])

    @pl.when(pl.program_id(1) == 0)
    def _():
        attend(True)

    @pl.when(pl.program_id(1) > 0)
    def _():
        attend(False)

    out_ref[0] = x_ref[0] + jnp.dot(o_scr[...], wo_ref[...], preferred_element_type=F32)


def _attn(x, qt, k, vt, bias_t, w_o):
    b, s, _ = x.shape
    cur = lambda bi, si: (bi, si, 0)
    prev = lambda bi, si: (bi, jnp.maximum(si - 1, 0), 0)
    cur_t = lambda bi, si: (bi, 0, si)
    prev_t = lambda bi, si: (bi, 0, jnp.maximum(si - 1, 0))
    blk = (1, SEQ_TILE, D)
    blk_t = (1, D, SEQ_TILE)
    return pl.pallas_call(
        _attn_kernel,
        name="attn",
        grid=(b, s // SEQ_TILE),
        in_specs=[
            pl.BlockSpec(blk_t, cur_t),
            pl.BlockSpec(blk, prev),
            pl.BlockSpec(blk, cur),
            pl.BlockSpec(blk_t, prev_t),
            pl.BlockSpec(blk_t, cur_t),
            pl.BlockSpec((B_HEADS // 2, G_BAND, 2 * Q_GROUP), lambda bi, si: (0, 0, 0),
                         pipeline_mode=pl.Buffered(1)),
            pl.BlockSpec(blk, cur),
            pl.BlockSpec((D, D), lambda bi, si: (0, 0), pipeline_mode=pl.Buffered(1)),
        ],
        out_specs=pl.BlockSpec(blk, cur),
        out_shape=jax.ShapeDtypeStruct(x.shape, F32),
        scratch_shapes=[pltpu.VMEM((SEQ_TILE, D), BF16)]
        + [pltpu.VMEM((G_BAND, 2 * Q_GROUP), F32)] * ATTN_DEPTH
        + [pltpu.VMEM((G_BAND, 2 * Q_GROUP), BF16)] * ATTN_DEPTH,
        compiler_params=_params(2),
    )(qt, k, k, vt, vt, bias_t, x, w_o.astype(BF16))


def _group_bias(table):
    band = _band_bias(table) * LOG2E
    pad = lambda lo, hi: jnp.pad(band, ((0, 0), (0, 0), (lo, hi)), constant_values=NEG_INF)
    both = jnp.concatenate([pad(0, CHUNK), pad(CHUNK, 0)], axis=1)
    both = both.reshape(B_HEADS // 2, 2, Q_GROUP, G_BAND)
    return both.transpose(0, 3, 1, 2).reshape(B_HEADS // 2, G_BAND, 2 * Q_GROUP)


def _band_bias(table):
    n_rel = REL_MAX - REL_MIN + 1
    span = BAND + CHUNK - 1
    head = jnp.broadcast_to(table[:, n_rel - 1:], (table.shape[0], span - n_rel))
    ext = jnp.concatenate([head, table[:, ::-1]], axis=1)
    rows = [ext[:, CHUNK - 1 - q:CHUNK - 1 - q + BAND] for q in range(CHUNK)]
    return jnp.stack(rows, axis=1)


def kernel(x, p, a_w_in, a_lb_logits, a_out_norm, a_w_o, kv_norm, w_kv, b_w_q, b_rel_bias, b_w_o,
           norm_mix, norm_ffn, norm_ple, moe_w_group, moe_b_group, moe_w_expert, moe_b_expert,
           moe_w_gate, moe_w_up, moe_w_down, ple_w_proj, ple_w_gate, final_norm):
    b, s, _ = x.shape
    lower_bounds = jnp.cumsum(jax.nn.softmax(a_lb_logits.astype(F32), axis=0), axis=0)

    def moe(xi, i, final, qkv=None):
        return _moe_ple(xi, p, i, norm_ffn[i], moe_w_group[i], moe_b_group[i], moe_w_expert[i],
                        moe_b_expert[i], moe_w_gate, moe_w_up, moe_w_down, norm_ple[i],
                        ple_w_proj[i], ple_w_gate[i], final_norm, final, qkv)

    x = _mixer_a(x, norm_mix[0], a_w_in[0], lower_bounds[0], a_out_norm[0], a_w_o[0])
    x, qt, k, vt = moe(x, 0, False, (norm_mix[1], kv_norm, b_w_q[0], w_kv))

    x = _attn(x, qt, k, vt, _group_bias(b_rel_bias[0].astype(F32)), b_w_o[0])
    x = moe(x, 1, True)
    return x
```
